```python
import math
import jax, jax.numpy as jnp
from jax import lax
import numpy as np


D_MODEL = 1024
BATCH = 32
SEQ = 2048
DEPTH = 2

CHUNK = 64
QBLOCK = 128
N_HEADS = 8
D_QLAT = 256
D_LAT = 128
D_VHEAD = 64
IDX_HEADS = 8
IDX_DIM = 32
DSA_TOPK_MAX = 256
SSM_GROUPS = 16
SSM_GROUP_DIM = 16
SSM_STATE = 64
W_SSM = SSM_GROUPS * SSM_GROUP_DIM
POOL_WINDOWS = (2, 4, 8, 16)
POOL_GROUPS = 4
POOL_GROUP_DIM = 64
W_POOL = POOL_GROUPS * POOL_GROUP_DIM
W_ATTN = N_HEADS * D_VHEAD
N_BRANCH = 3
D_FF = 2048
CONV_WIDTH = 3
RMS_EPS = 1e-6
NEG_INF = -1e30
ATTN_SCALE = D_LAT ** -0.5
IDX_SCALE = IDX_DIM ** -0.5
IDX_HEAD_SCALE = IDX_HEADS ** -0.5
IN_SPLITS = (D_QLAT, D_LAT, IDX_DIM, IDX_HEADS, W_SSM, W_POOL, N_BRANCH * D_MODEL)
N_IN = 4008

kernel_name = 'chunk_causal_hybrid_dsa_s5_pool_block'


def _rmsnorm(x, g):
    xf = x.astype(jnp.float32)
    y = xf * lax.rsqrt(jnp.mean(xf * xf, axis=-1, keepdims=True) + RMS_EPS)
    return (y * g.astype(jnp.float32)).astype(x.dtype)


def _alibi_slopes(n):
    return jnp.exp2(-8.0 * jnp.arange(1, n + 1, dtype=jnp.float32) / n)


def _dsa_branch(cq, ckv, kidx, widx, g_cq, w_uq, w_qi, g_ckv, w_uv):
    Bsz, S, _ = cq.shape
    cq = _rmsnorm(cq, g_cq)
    ckv = _rmsnorm(ckv, g_ckv)
    q = jnp.einsum('bsr,rhd->bshd', cq, w_uq)
    qi = jnp.einsum('bsr,rhd->bshd', cq, w_qi)
    top_k = min(DSA_TOPK_MAX, S // 4)
    slopes = _alibi_slopes(N_HEADS)
    nb = S // QBLOCK
    key_pos = jnp.arange(S, dtype=jnp.int32)

    def to_blocks(a):
        return jnp.moveaxis(a.reshape((Bsz, nb, QBLOCK) + a.shape[2:]), 1, 0)

    def attend(blk):
        q_b, qi_b, w_b, t_b = blk
        limit = (t_b // CHUNK + 1) * CHUNK
        admissible = key_pos[None, :] < limit[:, None]
        idx_logits = jnp.einsum('bqhd,bsd->bqhs', qi_b, kidx).astype(jnp.float32) * IDX_SCALE
        score = jnp.einsum('bqhs,bqh->bqs', jax.nn.relu(idx_logits), w_b.astype(jnp.float32)) * IDX_HEAD_SCALE
        score = jnp.where(admissible[None], score, NEG_INF)
        _, sel = lax.top_k(score, top_k)
        kv_sel = jax.vmap(lambda kv, i: kv[i])(ckv, sel)
        logits = jnp.einsum('bqhd,bqkd->bqhk', q_b, kv_sel).astype(jnp.float32) * ATTN_SCALE
        dist = jnp.abs(t_b[None, :, None] - sel).astype(jnp.float32)
        logits = logits - slopes[None, None, :, None] * dist[:, :, None, :]
        valid = sel < limit[None, :, None]
        logits = jnp.where(valid[:, :, None, :], logits, NEG_INF)
        probs = jax.nn.softmax(logits, axis=-1).astype(kv_sel.dtype)
        return jnp.einsum('bqhk,bqkd->bqhd', probs, kv_sel)

    pos = jnp.arange(S, dtype=jnp.int32).reshape(nb, QBLOCK)
    o = lax.map(attend, (to_blocks(q), to_blocks(qi), to_blocks(widx), pos))
    o = jnp.moveaxis(o, 0, 1).reshape(Bsz, S, N_HEADS, D_LAT)
    return jnp.einsum('bshd,hde->bshe', o, w_uv).reshape(Bsz, S, W_ATTN)


def _ssm_combine(left, right):
    a_l, b_l = left
    a_r, b_r = right
    return (a_r * a_l, a_r * b_l + b_r)


def _s5_branch(u, a_re, a_im, b_re, b_im, c_re, c_im, d_skip, log_step, w_glu, b_glu):
    Bsz, S, _ = u.shape
    f32 = jnp.float32
    uf = u.astype(f32).reshape(Bsz, S, SSM_GROUPS, SSM_GROUP_DIM)
    lam = lax.complex(a_re.astype(f32), a_im.astype(f32))
    step = jnp.exp(log_step.astype(f32))[:, None]
    lam_bar = jnp.exp(lam * step)
    b_mat = lax.complex(b_re.astype(f32), b_im.astype(f32))
    b_bar = ((lam_bar - 1.0) / lam)[:, :, None] * b_mat
    bu = jnp.einsum('bsgp,gnp->bsgn', uf, b_bar)
    a_seq = jnp.broadcast_to(lam_bar, (1, S, SSM_GROUPS, SSM_STATE))
    _, states = lax.associative_scan(_ssm_combine, (a_seq, bu), axis=1)
    c_mat = lax.complex(c_re.astype(f32), c_im.astype(f32))
    y = jnp.real(jnp.einsum('bsgn,gpn->bsgp', states, c_mat))
    y = y + d_skip.astype(f32).reshape(SSM_GROUPS, SSM_GROUP_DIM) * uf
    z = jax.nn.gelu(y.reshape(Bsz, S, W_SSM))
    out = z * jax.nn.sigmoid(z @ w_glu.astype(f32) + b_glu.astype(f32))
    return out.astype(u.dtype)


def _pool_branch(u, w_pool, pool_scale):
    Bsz, S, _ = u.shape
    f32 = jnp.float32
    uf = u.astype(f32).reshape(Bsz, S, POOL_GROUPS, POOL_GROUP_DIM)
    csum = jnp.pad(jnp.cumsum(uf, axis=1), ((0, 0), (1, 0), (0, 0), (0, 0)))
    t = jnp.arange(S)
    means = []
    for g, win in enumerate(POOL_WINDOWS):
        cs_g = csum[:, :, g]
        lo = jnp.maximum(t + 1 - win, 0)
        count = jnp.minimum(t + 1, win).astype(f32)[None, :, None]
        means.append((cs_g[:, 1:] - cs_g[:, lo]) / count)
    pooled = jnp.stack(means, axis=2) - uf
    y = jnp.einsum('bsgc,gcd->bsgd', pooled, w_pool.astype(f32))
    y = y * pool_scale.astype(f32).reshape(POOL_GROUPS, POOL_GROUP_DIM)
    return y.reshape(Bsz, S, W_POOL).astype(u.dtype)


def _token_mixer(h, w_in, g_cq, w_uq, w_qi, g_ckv, w_uv, a_re, a_im, b_re, b_im, c_re, c_im,
                 d_skip, log_step, w_glu, b_glu, w_pool, pool_scale, p_a, p_b, p_c, w_out):
    Bsz, S, _ = h.shape
    z = h @ w_in
    splits = [int(v) for v in np.cumsum(IN_SPLITS)[:-1]]
    cq, ckv, kidx, widx, u_ssm, u_pool, gate_logits = jnp.split(z, splits, axis=-1)
    y_attn = _dsa_branch(cq, ckv, kidx, widx, g_cq, w_uq, w_qi, g_ckv, w_uv)
    y_ssm = _s5_branch(u_ssm, a_re, a_im, b_re, b_im, c_re, c_im, d_skip, log_step, w_glu, b_glu)
    y_pool = _pool_branch(u_pool, w_pool, pool_scale)
    gates = jax.nn.sigmoid(gate_logits).reshape(Bsz, S, N_BRANCH, D_MODEL)
    merged = (gates[:, :, 0] * (y_attn @ p_a)
              + gates[:, :, 1] * (y_ssm @ p_b)
              + gates[:, :, 2] * (y_pool @ p_c))
    return merged @ w_out


def _conv_gated_ffn(h, w_up, conv_w, conv_b, w_down):
    S = h.shape[1]
    u = h @ w_up
    up = jnp.pad(u, ((0, 0), (CONV_WIDTH - 1, 0), (0, 0)))
    acc = conv_b + conv_w[0] * up[:, 0:S]
    for k in range(1, CONV_WIDTH):
        acc = acc + conv_w[k] * up[:, k:k + S]
    g, v = jnp.split(acc, 2, axis=-1)
    return (jax.nn.gelu(g) * v) @ w_down


def setup_inputs(seed: int = 0) -> dict:
    key = jax.random.key(seed)
    ks = iter(jax.random.split(key, 40))
    f32 = jnp.float32
    L = DEPTH

    def nrm(shape, scale):
        return jax.random.normal(next(ks), shape, f32) * scale

    def gain(shape):
        return 1.0 + nrm(shape, 0.02)

    n_idx = jnp.arange(SSM_STATE, dtype=f32)
    return {
        'x': nrm((BATCH, SEQ, D_MODEL), 1.0),
        'c': nrm((BATCH, D_MODEL), 1.0),
        'mod_w': nrm((L, D_MODEL, 6 * D_MODEL), 0.5 * D_MODEL ** -0.5),
        'mod_b': nrm((L, 6 * D_MODEL), 0.02),
        'mix_pre_g': gain((L, D_MODEL)),
        'mix_post_g': gain((L, D_MODEL)),
        'ffn_pre_g': gain((L, D_MODEL)),
        'ffn_post_g': gain((L, D_MODEL)),
        'w_in': nrm((L, D_MODEL, N_IN), D_MODEL ** -0.5),
        'g_cq': gain((L, D_QLAT)),
        'w_uq': nrm((L, D_QLAT, N_HEADS, D_LAT), D_QLAT ** -0.5),
        'w_qi': nrm((L, D_QLAT, IDX_HEADS, IDX_DIM), D_QLAT ** -0.5),
        'g_ckv': gain((L, D_LAT)),
        'w_uv': nrm((L, N_HEADS, D_LAT, D_VHEAD), D_LAT ** -0.5),
        'a_re': -0.5 + nrm((L, SSM_GROUPS, SSM_STATE), 0.01),
        'a_im': math.pi * n_idx + nrm((L, SSM_GROUPS, SSM_STATE), 0.01),
        'b_re': nrm((L, SSM_GROUPS, SSM_STATE, SSM_GROUP_DIM), (2 * SSM_GROUP_DIM) ** -0.5),
        'b_im': nrm((L, SSM_GROUPS, SSM_STATE, SSM_GROUP_DIM), (2 * SSM_GROUP_DIM) ** -0.5),
        'c_re': nrm((L, SSM_GROUPS, SSM_GROUP_DIM, SSM_STATE), (2 * SSM_STATE) ** -0.5 * 4.0),
        'c_im': nrm((L, SSM_GROUPS, SSM_GROUP_DIM, SSM_STATE), (2 * SSM_STATE) ** -0.5 * 4.0),
        'd_skip': nrm((L, W_SSM), 1.0),
        'log_step': jax.random.uniform(next(ks), (L, SSM_GROUPS), f32, math.log(1e-3), math.log(1e-1)),
        'w_glu': nrm((L, W_SSM, W_SSM), W_SSM ** -0.5),
        'b_glu': nrm((L, W_SSM), 0.01),
        'w_pool': nrm((L, POOL_GROUPS, POOL_GROUP_DIM, POOL_GROUP_DIM), POOL_GROUP_DIM ** -0.5),
        'pool_scale': 1.0 + nrm((L, W_POOL), 0.1),
        'p_a': nrm((L, W_ATTN, D_MODEL), W_ATTN ** -0.5),
        'p_b': nrm((L, W_SSM, D_MODEL), W_SSM ** -0.5),
        'p_c': nrm((L, W_POOL, D_MODEL), W_POOL ** -0.5),
        'w_out': nrm((L, D_MODEL, D_MODEL), D_MODEL ** -0.5),
        'w_up': nrm((L, D_MODEL, 2 * D_FF), D_MODEL ** -0.5),
        'conv_w': nrm((L, CONV_WIDTH, 2 * D_FF), CONV_WIDTH ** -0.5),
        'conv_b': nrm((L, 2 * D_FF), 0.01),
        'w_down': nrm((L, D_FF, D_MODEL), D_FF ** -0.5),
    }


def reference(x, c, mod_w, mod_b, mix_pre_g, mix_post_g, ffn_pre_g, ffn_post_g, w_in,
              g_cq, w_uq, w_qi, g_ckv, w_uv, a_re, a_im, b_re, b_im, c_re, c_im, d_skip,
              log_step, w_glu, b_glu, w_pool, pool_scale, p_a, p_b, p_c, w_out,
              w_up, conv_w, conv_b, w_down):
    cond = jax.nn.silu(c)
    for l in range(DEPTH):
        mod = (cond @ mod_w[l] + mod_b[l])[:, None, :]
        sh_m, sc_m, gt_m, sh_f, sc_f, gt_f = jnp.split(mod, 6, axis=-1)
        h = _rmsnorm(x, mix_pre_g[l]) * (1.0 + sc_m) + sh_m
        y = _token_mixer(h, w_in[l], g_cq[l], w_uq[l], w_qi[l], g_ckv[l], w_uv[l],
                         a_re[l], a_im[l], b_re[l], b_im[l], c_re[l], c_im[l], d_skip[l],
                         log_step[l], w_glu[l], b_glu[l], w_pool[l], pool_scale[l],
                         p_a[l], p_b[l], p_c[l], w_out[l])
        x = x + gt_m * _rmsnorm(y, mix_post_g[l])
        h = _rmsnorm(x, ffn_pre_g[l]) * (1.0 + sc_f) + sh_f
        y = _conv_gated_ffn(h, w_up[l], conv_w[l], conv_b[l], w_down[l])
        x = x + gt_f * _rmsnorm(y, ffn_post_g[l])
    return x
```

```python
import functools
import math

import jax
import jax.numpy as jnp
from jax import lax
from jax.experimental import pallas as pl
from jax.experimental.pallas import tpu as pltpu

F32 = jnp.float32
_MXU_DTYPE = jnp.bfloat16

CHUNK = 64
N_HEADS = 8
D_QLAT = 256
D_LAT = 128
D_VHEAD = 64
IDX_HEADS = 8
IDX_DIM = 32
DSA_TOPK_MAX = 256
SSM_GROUPS = 16
SSM_GROUP_DIM = 16
SSM_STATE = 64
W_SSM = SSM_GROUPS * SSM_GROUP_DIM
N_STATE = SSM_GROUPS * SSM_STATE
POOL_WINDOWS = (2, 4, 8, 16)
POOL_GROUPS = 4
POOL_GROUP_DIM = 64
W_POOL = POOL_GROUPS * POOL_GROUP_DIM
W_ATTN = N_HEADS * D_VHEAD
N_BRANCH = 3
CONV_WIDTH = 3
RMS_EPS = 1e-6
NEG_INF = -1e30
ATTN_SCALE = D_LAT ** -0.5
IDX_SCALE = IDX_DIM ** -0.5
IDX_HEAD_SCALE = IDX_HEADS ** -0.5
IN_SPLITS = (D_QLAT, D_LAT, IDX_DIM, IDX_HEADS, W_SSM, W_POOL)

LANES = 128
SUBLANES = 8
MISC_W = LANES
W_SMALL = D_QLAT + D_LAT + MISC_W + W_SSM + W_POOL
VMEM_LIMIT = 56 * 1024 * 1024

QB = 256
POOL_HALO = 16
CONV_HALO = 8


def _mm(a, b):
    return jnp.dot(a.astype(_MXU_DTYPE), b.astype(_MXU_DTYPE), preferred_element_type=F32)


def _rms(x, g):
    return x * lax.rsqrt(jnp.mean(x * x, axis=-1, keepdims=True) + RMS_EPS) * g


def _gelu(x):
    return 0.5 * x * (1.0 + jnp.tanh(math.sqrt(2.0 / math.pi) * (x + 0.044715 * (x * x * x))))


def _sigmoid(x):
    return 1.0 / (1.0 + jnp.exp(-x))


def _params(*sem):
    return pltpu.CompilerParams(dimension_semantics=sem, vmem_limit_bytes=VMEM_LIMIT)


def _mod_kernel(c_ref, w_ref, b_ref, o_ref):
    c = c_ref[...]
    cond = c * _sigmoid(c)
    o_ref[0] = _mm(cond, w_ref[0]) + b_ref[0]


def _mod_call(c, mod_w, mod_b):
    L, D, D6 = mod_w.shape
    B = c.shape[0]
    nt = D6 // D
    return pl.pallas_call(
        _mod_kernel,
        grid=(L, nt),
        in_specs=[pl.BlockSpec((B, D), lambda l, n: (0, 0)),
                  pl.BlockSpec((1, D, D), lambda l, n: (l, 0, n)),
                  pl.BlockSpec((1, 1, D), lambda l, n: (l, 0, n))],
        out_specs=pl.BlockSpec((1, B, D), lambda l, n: (l, 0, n)),
        out_shape=jax.ShapeDtypeStruct((L, B, D6), F32),
        compiler_params=_params("arbitrary", "arbitrary"),
        name="adaln_mod",
    )(c, mod_w.astype(_MXU_DTYPE), mod_b.reshape(L, 1, D6))


def _inproj_kernel(x_ref, g_ref, sc_ref, sh_ref, w_ref, gcq_ref, gckv_ref,
                   cq_o, ckv_o, misc_o, ussm_o, upool_o):
    h = _rms(x_ref[0], g_ref[...]) * (1.0 + sc_ref[0]) + sh_ref[0]
    z = _mm(h, w_ref[...])
    o0, o1, o2, o3 = D_QLAT, D_QLAT + D_LAT, D_QLAT + D_LAT + MISC_W, D_QLAT + D_LAT + MISC_W + W_SSM
    cq_o[0] = _rms(z[:, :o0], gcq_ref[...]).astype(cq_o.dtype)
    ckv_o[0] = _rms(z[:, o0:o1], gckv_ref[...]).astype(ckv_o.dtype)
    misc_o[0] = z[:, o1:o2]
    ussm_o[...] = z[:, o2:o3]
    upool_o[0] = z[:, o3:]


def _inproj_call(x, g_pre, sc, sh, w_small, g_cq, g_ckv, ts):
    B, S, D = x.shape
    row = lambda b, i: (b, i, 0)
    per_b = lambda b, i: (b, 0, 0)
    const2 = lambda b, i: (0, 0)
    return pl.pallas_call(
        _inproj_kernel,
        grid=(B, S // ts),
        in_specs=[pl.BlockSpec((1, ts, D), row),
                  pl.BlockSpec((1, D), const2),
                  pl.BlockSpec((1, 1, D), per_b),
                  pl.BlockSpec((1, 1, D), per_b),
                  pl.BlockSpec((D, W_SMALL), const2),
                  pl.BlockSpec((1, D_QLAT), const2),
                  pl.BlockSpec((1, D_LAT), const2)],
        out_specs=[pl.BlockSpec((1, ts, D_QLAT), row),
                   pl.BlockSpec((1, ts, D_LAT), row),
                   pl.BlockSpec((1, ts, MISC_W), row),
                   pl.BlockSpec((ts, W_SSM), lambda b, i: (i, b)),
                   pl.BlockSpec((1, ts, W_POOL), row)],
        out_shape=[jax.ShapeDtypeStruct((B, S, D_QLAT), _MXU_DTYPE),
                   jax.ShapeDtypeStruct((B, S, D_LAT), _MXU_DTYPE),
                   jax.ShapeDtypeStruct((B, S, MISC_W), F32),
                   jax.ShapeDtypeStruct((S, B * W_SSM), F32),
                   jax.ShapeDtypeStruct((B, S, W_POOL), F32)],
        compiler_params=_params("arbitrary", "arbitrary"),
        name="in_projection",
    )(x, g_pre, sc, sh, w_small, g_cq, g_ckv)


def _sortable(i):
    return i ^ ((i >> 31) & jnp.int32(0x7FFFFFFF))


def _attn_kernel(slope_ref, cq_ref, ckv_ref, misc_ref, wqi_ref, wuq_ref, wuv_ref, o_ref,
                 ckvt_sc, qi_sc, score_sc, yt_sc, *, topk):
    j = pl.program_id(1)
    nkt = ckvt_sc.shape[0]
    q0 = pl.multiple_of(j * QB, QB)

    @pl.when(j == 0)
    def _():
        for kt in range(nkt):
            ckvt_sc[kt] = ckv_ref[0, kt * QB:(kt + 1) * QB, :].astype(F32).T.astype(ckvt_sc.dtype)

    cqt = cq_ref[0].astype(F32).T.astype(_MXU_DTYPE)
    misct = misc_ref[0, pl.ds(q0, QB), :].T
    qi_sc[...] = _mm(wqi_ref[...], cqt).astype(qi_sc.dtype)

    kl = lax.broadcasted_iota(jnp.int32, (QB, QB), 0)
    ql = lax.broadcasted_iota(jnp.int32, (QB, QB), 1)
    diag_ok = kl < (ql // CHUNK + 1) * CHUNK
    d0 = (ql - kl).astype(F32)

    def score_tile(kt):
        k0 = pl.multiple_of(kt * QB, QB)
        kmat = misc_ref[0, pl.ds(k0, QB), :].astype(_MXU_DTYPE)
        acc = jnp.zeros((QB, QB), F32)
        for h in range(IDX_HEADS):
            lg = jnp.dot(kmat, qi_sc[h * LANES:(h + 1) * LANES, :], preferred_element_type=F32)
            wq = misct[IDX_DIM + h:IDX_DIM + h + 1, :] * (IDX_SCALE * IDX_HEAD_SCALE)
            acc = acc + jnp.maximum(lg, 0.0) * wq
        return acc

    def score_body(kt, carry):
        cmin, cmax = carry
        s = score_tile(kt)
        score_sc[pl.ds(pl.multiple_of(kt * QB, QB), QB), :] = s
        return (jnp.minimum(cmin, jnp.min(s, axis=0, keepdims=True)),
                jnp.maximum(cmax, jnp.max(s, axis=0, keepdims=True)))

    big = jnp.full((1, QB), 3.0e38, F32)
    cmin, cmax = lax.fori_loop(0, j, score_body, (big, -big))
    s = score_tile(j)
    cmin = jnp.minimum(cmin, jnp.min(jnp.where(diag_ok, s, 3.0e38), axis=0, keepdims=True))
    cmax = jnp.maximum(cmax, jnp.max(jnp.where(diag_ok, s, -3.0e38), axis=0, keepdims=True))
    score_sc[pl.ds(q0, QB), :] = jnp.where(diag_ok, s, -jnp.inf)

    def count(pred, thr):
        def body(kt, acc):
            t = score_sc[pl.ds(pl.multiple_of(kt * QB, QB), QB), :]
            ind = jnp.where(pred(t, thr), 1.0, 0.0)
            return acc + jnp.sum(ind.reshape(QB // SUBLANES, SUBLANES, QB), axis=0)
        acc = lax.fori_loop(0, j + 1, body, jnp.zeros((SUBLANES, QB), F32))
        return jnp.sum(acc, axis=0, keepdims=True)

    ge = lambda t, thr: t >= thr
    gt = lambda t, thr: t > thr

    def key_to_f32(k):
        return pltpu.bitcast(_sortable(k), F32)

    @pl.when((j + 1) * QB > topk)
    def _():
        kf = jnp.float32(topk)
        lo0 = _sortable(pltpu.bitcast(cmin, jnp.int32))
        hi0 = _sortable(pltpu.bitcast(cmax, jnp.int32)) + 1

        def cond(c):
            lo, hi, _ = c
            return jnp.max((hi > lo + 1).astype(jnp.int32)) > 0

        def body(c):
            lo, hi, done = c
            mid = (lo >> 1) + (hi >> 1) + (lo & hi & 1)
            cnt = count(ge, key_to_f32(mid))
            is_ge = cnt >= kf
            is_eq = cnt == kf
            lo = jnp.where(is_ge, mid, lo)
            hi = jnp.where(is_eq, mid + 1, jnp.where(is_ge, hi, mid))
            return lo, hi, jnp.where(is_eq, 1, done)

        lo, _, done = lax.while_loop(cond, body, (lo0, hi0, jnp.zeros((1, QB), jnp.int32)))
        thr = key_to_f32(lo)
        ties = jnp.max(1 - done) > 0

        @pl.when(jnp.logical_not(ties))
        def _():
            def body(kt, _):
                r = pl.ds(pl.multiple_of(kt * QB, QB), QB)
                score_sc[r, :] = jnp.where(score_sc[r, :] >= thr, 0.0, NEG_INF)
                return 0
            lax.fori_loop(0, j + 1, body, 0)

        @pl.when(ties)
        def _():
            need = jnp.where(done > 0, 3.0e38, kf - count(gt, thr))
            tri = (lax.broadcasted_iota(jnp.int32, (QB, QB), 0)
                   >= lax.broadcasted_iota(jnp.int32, (QB, QB), 1)).astype(_MXU_DTYPE)

            def body(kt, seen):
                r = pl.ds(pl.multiple_of(kt * QB, QB), QB)
                t = score_sc[r, :]
                eq = jnp.where(t == thr, 1.0, 0.0)
                rank = seen + jnp.dot(tri, eq.astype(_MXU_DTYPE), preferred_element_type=F32)
                sel = (t > thr) | ((t == thr) & (rank <= need))
                score_sc[r, :] = jnp.where(sel, 0.0, NEG_INF)
                return seen + jnp.sum(eq, axis=0, keepdims=True)
            lax.fori_loop(0, j + 1, body, jnp.zeros((1, QB), F32))

    @pl.when((j + 1) * QB <= topk)
    def _():
        def body(kt, _):
            r = pl.ds(pl.multiple_of(kt * QB, QB), QB)
            score_sc[r, :] = jnp.where(score_sc[r, :] > -jnp.inf, 0.0, NEG_INF)
            return 0
        lax.fori_loop(0, j + 1, body, 0)

    def head_body(h, _):
        slope = slope_ref[h]
        qt = (_mm(wuq_ref[h], cqt) * ATTN_SCALE).astype(_MXU_DTYPE)

        def tile(kt, carry, dist):
            m, l, acc = carry
            k0 = pl.multiple_of(kt * QB, QB)
            lg = jnp.dot(ckv_ref[0, pl.ds(k0, QB), :], qt, preferred_element_type=F32)
            lg = lg - slope * dist + score_sc[pl.ds(k0, QB), :]
            m_new = jnp.maximum(m, jnp.max(lg, axis=0, keepdims=True))
            alpha = jnp.exp(m - m_new)
            p = jnp.exp(lg - m_new)
            l = alpha * l + jnp.sum(p, axis=0, keepdims=True)
            acc = alpha * acc + jnp.dot(ckvt_sc[kt], p.astype(_MXU_DTYPE), preferred_element_type=F32)
            return m_new, l, acc

        def off_diag(kt, carry):
            return tile(kt, carry, d0 + (q0 - kt * QB).astype(F32))

        init = (jnp.full((1, QB), -3.0e38, F32), jnp.zeros((1, QB), F32), jnp.zeros((D_LAT, QB), F32))
        carry = lax.fori_loop(0, j, off_diag, init)
        _, l, acc = tile(j, carry, jnp.abs(d0))
        o = acc / l
        yt_sc[pl.ds(pl.multiple_of(h * D_VHEAD, D_VHEAD), D_VHEAD), :] = _mm(wuv_ref[h], o)
        return 0

    lax.fori_loop(0, N_HEADS, head_body, 0)
    o_ref[0] = yt_sc[...].T.astype(o_ref.dtype)


def _attn_call(cqn, ckvn, misc, wqi, wuq, wuv, topk):
    B, S, _ = cqn.shape
    nkt = S // QB
    slopes = jnp.exp2(-8.0 * jnp.arange(1, N_HEADS + 1, dtype=F32) / N_HEADS)
    per_b = lambda b, j: (b, 0, 0)
    const2 = lambda b, j: (0, 0)
    const3 = lambda b, j: (0, 0, 0)
    return pl.pallas_call(
        functools.partial(_attn_kernel, topk=topk),
        grid=(B, nkt),
        in_specs=[pl.BlockSpec(memory_space=pltpu.SMEM),
                  pl.BlockSpec((1, QB, D_QLAT), lambda b, j: (b, j, 0)),
                  pl.BlockSpec((1, S, D_LAT), per_b),
                  pl.BlockSpec((1, S, MISC_W), per_b),
                  pl.BlockSpec((IDX_HEADS * LANES, D_QLAT), const2),
                  pl.BlockSpec((N_HEADS, D_LAT, D_QLAT), const3),
                  pl.BlockSpec((N_HEADS, D_VHEAD, D_LAT), const3)],
        out_specs=pl.BlockSpec((1, QB, W_ATTN), lambda b, j: (b, j, 0)),
        out_shape=jax.ShapeDtypeStruct((B, S, W_ATTN), _MXU_DTYPE),
        scratch_shapes=[pltpu.VMEM((nkt, D_LAT, QB), _MXU_DTYPE),
                        pltpu.VMEM((IDX_HEADS * LANES, QB), _MXU_DTYPE),
                        pltpu.VMEM((S, QB), F32),
                        pltpu.VMEM((W_ATTN, QB), F32)],
        compiler_params=_params("arbitrary", "arbitrary"),
        name="dsa_attention",
    )(slopes, cqn, ckvn, misc, wqi, wuq, wuv)


def _ssm_kernel(u_ref, bd_ref, lr_ref, li_ref, cd_ref, d_ref, wg_ref, bg_ref, o_ref, x_sc, bu_sc, *, nb, cw):
    tc = u_ref.shape[0] // nb

    @pl.when(pl.program_id(0) == 0)
    def _():
        x_sc[...] = jnp.zeros_like(x_sc)

    u = u_ref[...]
    bu_sc[...] = _mm(u, bd_ref[...])
    for c in range(N_STATE // cw):
        re = slice(c * cw, (c + 1) * cw)
        im = slice(N_STATE + c * cw, N_STATE + (c + 1) * cw)
        lr = lr_ref[:, re]
        li = li_ref[:, re]

        def step(t, carry, re=re, im=im, lr=lr, li=li):
            xr, xi = carry
            rows = pl.ds(pl.multiple_of(t * nb, nb), nb)
            nr = lr * xr - li * xi + bu_sc[rows, re]
            ni = lr * xi + li * xr + bu_sc[rows, im]
            bu_sc[rows, re] = nr
            bu_sc[rows, im] = ni
            return nr, ni

        xr, xi = lax.fori_loop(0, tc, step, (x_sc[:, re], x_sc[:, im]))
        x_sc[:, re] = xr
        x_sc[:, im] = xi
    y = _mm(bu_sc[...], cd_ref[...]) + d_ref[...] * u
    z = _gelu(y)
    o_ref[...] = z * _sigmoid(_mm(z, wg_ref[...]) + bg_ref[...])


def _ssm_call(u_tm, bd, lr, li, cd, dskip, w_glu, b_glu, nb, tc):
    rows = u_tm.shape[0]
    const2 = lambda i: (0, 0)
    return pl.pallas_call(
        functools.partial(_ssm_kernel, nb=nb, cw=256),
        grid=(rows // (tc * nb),),
        in_specs=[pl.BlockSpec((tc * nb, W_SSM), lambda i: (i, 0)),
                  pl.BlockSpec((W_SSM, 2 * N_STATE), const2),
                  pl.BlockSpec((1, N_STATE), const2),
                  pl.BlockSpec((1, N_STATE), const2),
                  pl.BlockSpec((2 * N_STATE, W_SSM), const2),
                  pl.BlockSpec((1, W_SSM), const2),
                  pl.BlockSpec((W_SSM, W_SSM), const2),
                  pl.BlockSpec((1, W_SSM), const2)],
        out_specs=pl.BlockSpec((tc * nb, W_SSM), lambda i: (i, 0)),
        out_shape=jax.ShapeDtypeStruct((rows, W_SSM), F32),
        scratch_shapes=[pltpu.VMEM((nb, 2 * N_STATE), F32),
                        pltpu.VMEM((tc * nb, 2 * N_STATE), F32)],
        compiler_params=_params("arbitrary"),
        name="s5_ssm",
    )(u_tm, bd, lr, li, cd, dskip, w_glu, b_glu)


def _ssm_weights(a_re, a_im, b_re, b_im, c_re, c_im, log_step):
    step = jnp.exp(log_step)[:, None]
    er = jnp.exp(a_re * step)
    ang = a_im * step
    lr, li = er * jnp.cos(ang), er * jnp.sin(ang)
    den = a_re * a_re + a_im * a_im
    fr = ((lr - 1.0) * a_re + li * a_im) / den
    fi = (li * a_re - (lr - 1.0) * a_im) / den
    br = fr[:, :, None] * b_re - fi[:, :, None] * b_im
    bi = fr[:, :, None] * b_im + fi[:, :, None] * b_re
    eye = jnp.eye(SSM_GROUPS, dtype=F32)
    pack_b = lambda m: jnp.einsum('gnp,gh->gphn', m, eye).reshape(W_SSM, N_STATE)
    pack_c = lambda m: jnp.einsum('gpn,gh->gnhp', m, eye).reshape(N_STATE, W_SSM)
    bd = jnp.concatenate([pack_b(br), pack_b(bi)], axis=1)
    cd = jnp.concatenate([pack_c(c_re), pack_c(-c_im)], axis=0)
    return (bd.astype(_MXU_DTYPE), lr.reshape(1, N_STATE), li.reshape(1, N_STATE), cd.astype(_MXU_DTYPE))


def _merge_kernel(x_ref, gpre_ref, sc_ref, sh_ref, gt_ref, gpost_ref, wg_ref, ya_ref, ys_ref,
                  up_ref, uh_ref, wp_ref, ps_ref, pa_ref, pb_ref, pc_ref, wo_ref, o_ref, halo_sc):
    i = pl.program_id(1)
    ts = x_ref.shape[1]
    x = x_ref[0]
    h = _rms(x, gpre_ref[...]) * (1.0 + sc_ref[0]) + sh_ref[0]

    u = up_ref[0]
    halo_sc[:POOL_HALO, :] = jnp.where(i > 0, uh_ref[0], 0.0)
    halo_sc[POOL_HALO:, :] = u
    lane = lax.broadcasted_iota(jnp.int32, (ts, W_POOL), 1)
    tpos = (i * ts + lax.broadcasted_iota(jnp.int32, (ts, W_POOL), 0) + 1).astype(F32)
    run = u
    pooled = jnp.zeros_like(u)
    prev = 1
    for g, win in enumerate(POOL_WINDOWS):
        for k in range(prev, win):
            run = run + halo_sc[POOL_HALO - k:POOL_HALO - k + ts, :]
        prev = win
        in_group = (lane >= g * POOL_GROUP_DIM) & (lane < (g + 1) * POOL_GROUP_DIM)
        pooled = jnp.where(in_group, run / jnp.minimum(tpos, float(win)), pooled)
    y_pool = _mm(pooled - u, wp_ref[...]) * ps_ref[...]

    d = x.shape[-1]
    merged = (_sigmoid(_mm(h, wg_ref[:, :d])) * _mm(ya_ref[0], pa_ref[...])
              + _sigmoid(_mm(h, wg_ref[:, d:2 * d])) * _mm(ys_ref[...], pb_ref[...])
              + _sigmoid(_mm(h, wg_ref[:, 2 * d:])) * _mm(y_pool, pc_ref[...]))
    y = _mm(merged, wo_ref[...])
    o_ref[0] = x + gt_ref[0] * _rms(y, gpost_ref[...])


def _merge_call(x, g_pre, sc, sh, gt, g_post, w_gate, y_attn, y_ssm_tm, u_pool, w_pool_bd, pool_scale,
                p_a, p_b, p_c, w_out, ts):
    B, S, D = x.shape
    row = lambda b, i: (b, i, 0)
    per_b = lambda b, i: (b, 0, 0)
    const2 = lambda b, i: (0, 0)
    hb = ts // POOL_HALO
    return pl.pallas_call(
        _merge_kernel,
        grid=(B, S // ts),
        in_specs=[pl.BlockSpec((1, ts, D), row),
                  pl.BlockSpec((1, D), const2),
                  pl.BlockSpec((1, 1, D), per_b),
                  pl.BlockSpec((1, 1, D), per_b),
                  pl.BlockSpec((1, 1, D), per_b),
                  pl.BlockSpec((1, D), const2),
                  pl.BlockSpec((D, N_BRANCH * D), const2),
                  pl.BlockSpec((1, ts, W_ATTN), row),
                  pl.BlockSpec((ts, W_SSM), lambda b, i: (i, b)),
                  pl.BlockSpec((1, ts, W_POOL), row),
                  pl.BlockSpec((1, POOL_HALO, W_POOL), lambda b, i: (b, jnp.maximum(i * hb - 1, 0), 0)),
                  pl.BlockSpec((W_POOL, W_POOL), const2),
                  pl.BlockSpec((1, W_POOL), const2),
                  pl.BlockSpec((W_ATTN, D), const2),
                  pl.BlockSpec((W_SSM, D), const2),
                  pl.BlockSpec((W_POOL, D), const2),
                  pl.BlockSpec((D, D), const2)],
        out_specs=pl.BlockSpec((1, ts, D), row),
        out_shape=jax.ShapeDtypeStruct((B, S, D), F32),
        scratch_shapes=[pltpu.VMEM((POOL_HALO + ts, W_POOL), F32)],
        compiler_params=_params("arbitrary", "arbitrary"),
        name="mixer_merge",
    )(x, g_pre, sc, sh, gt, g_post, w_gate, y_attn, y_ssm_tm, u_pool, u_pool, w_pool_bd, pool_scale,
      p_a, p_b, p_c, w_out)


def _ffn_kernel(x_ref, xh_ref, gpre_ref, sc_ref, sh_ref, gt_ref, gpost_ref, wup_ref, cw_ref, cb_ref,
                wdn_ref, o_ref, h_sc, ug_sc, uv_sc, *, cols):
    i = pl.program_id(1)
    ts = x_ref.shape[1]
    dff = wdn_ref.shape[0]
    x = x_ref[0]
    adaln = lambda v: _rms(v, gpre_ref[...]) * (1.0 + sc_ref[0]) + sh_ref[0]
    h_sc[:CONV_HALO, :] = adaln(xh_ref[0]).astype(h_sc.dtype)
    h_sc[CONV_HALO:, :] = adaln(x).astype(h_sc.dtype)
    keep = jnp.where(i > 0, 1.0, 0.0)
    y = jnp.zeros((ts, x.shape[-1]), F32)
    for c in range(dff // cols):
        halves = []
        for sc_buf, off in ((ug_sc, 0), (uv_sc, dff)):
            cs = slice(off + c * cols, off + (c + 1) * cols)
            up = jnp.dot(h_sc[...], wup_ref[:, cs], preferred_element_type=F32)
            sc_buf[:CONV_HALO, :] = up[:CONV_HALO] * keep
            sc_buf[CONV_HALO:, :] = up[CONV_HALO:]
            acc = cb_ref[:, cs]
            for k in range(CONV_WIDTH):
                r0 = CONV_HALO - (CONV_WIDTH - 1) + k
                acc = acc + cw_ref[k:k + 1, cs] * sc_buf[r0:r0 + ts, :]
            halves.append(acc)
        y = y + _mm(_gelu(halves[0]) * halves[1], wdn_ref[c * cols:(c + 1) * cols, :])
    o_ref[0] = x + gt_ref[0] * _rms(y, gpost_ref[...])


def _ffn_call(x, g_pre, sc, sh, gt, g_post, w_up, conv_w, conv_b, w_down, ts, cols=512):
    B, S, D = x.shape
    dff = w_down.shape[0]
    row = lambda b, i: (b, i, 0)
    per_b = lambda b, i: (b, 0, 0)
    const2 = lambda b, i: (0, 0)
    hb = ts // CONV_HALO
    return pl.pallas_call(
        functools.partial(_ffn_kernel, cols=cols),
        grid=(B, S // ts),
        in_specs=[pl.BlockSpec((1, ts, D), row),
                  pl.BlockSpec((1, CONV_HALO, D), lambda b, i: (b, jnp.maximum(i * hb - 1, 0), 0)),
                  pl.BlockSpec((1, D), const2),
                  pl.BlockSpec((1, 1, D), per_b),
                  pl.BlockSpec((1, 1, D), per_b),
                  pl.BlockSpec((1, 1, D), per_b),
                  pl.BlockSpec((1, D), const2),
                  pl.BlockSpec((D, 2 * dff), const2),
                  pl.BlockSpec((CONV_WIDTH, 2 * dff), const2),
                  pl.BlockSpec((1, 2 * dff), const2),
                  pl.BlockSpec((dff, D), const2)],
        out_specs=pl.BlockSpec((1, ts, D), row),
        out_shape=jax.ShapeDtypeStruct((B, S, D), F32),
        scratch_shapes=[pltpu.VMEM((CONV_HALO + ts, D), _MXU_DTYPE),
                        pltpu.VMEM((CONV_HALO + ts, cols), F32),
                        pltpu.VMEM((CONV_HALO + ts, cols), F32)],
        compiler_params=_params("arbitrary", "arbitrary"),
        name="conv_gated_ffn",
    )(x, x, g_pre, sc, sh, gt, g_post, w_up, conv_w, conv_b, w_down)


def _pick(n, pref):
    t = min(n, pref)
    assert n % t == 0, (n, t)
    return t


def kernel(x, c, mod_w, mod_b, mix_pre_g, mix_post_g, ffn_pre_g, ffn_post_g, w_in, g_cq, w_uq, w_qi, g_ckv, w_uv, a_re, a_im, b_re, b_im, c_re, c_im, d_skip, log_step, w_glu, b_glu, w_pool, pool_scale, p_a, p_b, p_c, w_out, w_up, conv_w, conv_b, w_down):
    B, S, D = x.shape
    depth = mod_w.shape[0]
    assert S % QB == 0 and QB % CHUNK == 0
    topk = min(DSA_TOPK_MAX, S // 4)
    ts = _pick(S, 512)
    ts_tail = _pick(S, 256)
    tc = _pick(S, 32)
    cast = lambda w: w.astype(_MXU_DTYPE)
    row = lambda v: v.reshape(1, -1)

    mod = _mod_call(c, mod_w, mod_b)
    cuts = [0]
    for wdt in IN_SPLITS:
        cuts.append(cuts[-1] + wdt)
    eye_p = jnp.eye(POOL_GROUPS, dtype=F32)
    for l in range(depth):
        sh_m, sc_m, gt_m, sh_f, sc_f, gt_f = [mod[l][:, None, k * D:(k + 1) * D] for k in range(6)]
        wl = w_in[l]
        pad = jnp.zeros((D, MISC_W - IDX_DIM - IDX_HEADS), F32)
        w_small = cast(jnp.concatenate(
            [wl[:, cuts[0]:cuts[2]], wl[:, cuts[2]:cuts[4]], pad, wl[:, cuts[4]:cuts[6]]], axis=1))
        w_gate = cast(wl[:, cuts[6]:])
        cqn, ckvn, misc, u_ssm, u_pool = _inproj_call(
            x, row(mix_pre_g[l]), sc_m, sh_m, w_small, row(g_cq[l]), row(g_ckv[l]), ts)

        wqi = jnp.transpose(w_qi[l], (1, 2, 0))
        wqi = jnp.pad(wqi, ((0, 0), (0, LANES - IDX_DIM), (0, 0))).reshape(IDX_HEADS * LANES, D_QLAT)
        y_attn = _attn_call(cqn, ckvn, misc, cast(wqi), cast(jnp.transpose(w_uq[l], (1, 2, 0))),
                            cast(jnp.transpose(w_uv[l], (0, 2, 1))), topk)

        bd, lr, li, cd = _ssm_weights(a_re[l], a_im[l], b_re[l], b_im[l], c_re[l], c_im[l], log_step[l])
        y_ssm = _ssm_call(u_ssm.reshape(S * B, W_SSM), bd, lr, li, cd, row(d_skip[l]), cast(w_glu[l]),
                          row(b_glu[l]), B, tc).reshape(S, B * W_SSM)

        w_pool_bd = cast(jnp.einsum('gcd,gh->gchd', w_pool[l], eye_p).reshape(W_POOL, W_POOL))
        x = _merge_call(x, row(mix_pre_g[l]), sc_m, sh_m, gt_m, row(mix_post_g[l]), w_gate, y_attn, y_ssm,
                        u_pool, w_pool_bd, row(pool_scale[l]), cast(p_a[l]), cast(p_b[l]), cast(p_c[l]),
                        cast(w_out[l]), ts_tail)
        x = _ffn_call(x, row(ffn_pre_g[l]), sc_f, sh_f, gt_f, row(ffn_post_g[l]), cast(w_up[l]), conv_w[l],
                      row(conv_b[l]), cast(w_down[l]), ts_tail)
    return x
```

```python
import functools
import math

import jax
import jax.numpy as jnp
from jax import lax
from jax.experimental import pallas as pl
from jax.experimental.pallas import tpu as pltpu

F32 = jnp.float32
_MXU_DTYPE = jnp.bfloat16

CHUNK = 64
N_HEADS = 8
D_QLAT = 256
D_LAT = 128
D_VHEAD = 64
IDX_HEADS = 8
IDX_DIM = 32
DSA_TOPK_MAX = 256
SSM_GROUPS = 16
SSM_GROUP_DIM = 16
SSM_STATE = 64
W_SSM = SSM_GROUPS * SSM_GROUP_DIM
N_STATE = SSM_GROUPS * SSM_STATE
POOL_WINDOWS = (2, 4, 8, 16)
POOL_GROUPS = 4
POOL_GROUP_DIM = 64
W_POOL = POOL_GROUPS * POOL_GROUP_DIM
W_ATTN = N_HEADS * D_VHEAD
N_BRANCH = 3
CONV_WIDTH = 3
RMS_EPS = 1e-6
NEG_INF = -1e30
ATTN_SCALE = D_LAT ** -0.5
IDX_SCALE = IDX_DIM ** -0.5
IDX_HEAD_SCALE = IDX_HEADS ** -0.5
IN_SPLITS = (D_QLAT, D_LAT, IDX_DIM, IDX_HEADS, W_SSM, W_POOL)

LANES = 128
SUBLANES = 8
MISC_W = LANES
W_SMALL = D_QLAT + D_LAT + MISC_W + W_SSM + W_POOL
VMEM_LIMIT = 56 * 1024 * 1024

QB = 256
BISECT_PASSES = 4
POOL_HALO = 16
CONV_HALO = 8


def _mm(a, b):
    return jnp.dot(a.astype(_MXU_DTYPE), b.astype(_MXU_DTYPE), preferred_element_type=F32)


def _rms(x, g):
    return x * lax.rsqrt(jnp.mean(x * x, axis=-1, keepdims=True) + RMS_EPS) * g


def _gelu(x):
    return 0.5 * x * (1.0 + jnp.tanh(math.sqrt(2.0 / math.pi) * (x + 0.044715 * (x * x * x))))


def _sigmoid(x):
    return 1.0 / (1.0 + jnp.exp(-x))


def _params(*sem):
    return pltpu.CompilerParams(dimension_semantics=sem, vmem_limit_bytes=VMEM_LIMIT)


def _mod_kernel(c_ref, w_ref, b_ref, o_ref):
    c = c_ref[...]
    cond = c * _sigmoid(c)
    o_ref[0] = _mm(cond, w_ref[0]) + b_ref[0]


def _mod_call(c, mod_w, mod_b):
    L, D, D6 = mod_w.shape
    B = c.shape[0]
    nt = D6 // D
    return pl.pallas_call(
        _mod_kernel,
        grid=(L, nt),
        in_specs=[pl.BlockSpec((B, D), lambda l, n: (0, 0)),
                  pl.BlockSpec((1, D, D), lambda l, n: (l, 0, n)),
                  pl.BlockSpec((1, 1, D), lambda l, n: (l, 0, n))],
        out_specs=pl.BlockSpec((1, B, D), lambda l, n: (l, 0, n)),
        out_shape=jax.ShapeDtypeStruct((L, B, D6), F32),
        compiler_params=_params("arbitrary", "arbitrary"),
        name="adaln_mod",
    )(c, mod_w.astype(_MXU_DTYPE), mod_b.reshape(L, 1, D6))


def _inproj_kernel(x_ref, g_ref, sc_ref, sh_ref, w_ref, gcq_ref, gckv_ref,
                   cq_o, ckv_o, misc_o, ussm_o, upool_o):
    h = _rms(x_ref[0], g_ref[...]) * (1.0 + sc_ref[0]) + sh_ref[0]
    z = _mm(h, w_ref[...])
    o0, o1, o2, o3 = D_QLAT, D_QLAT + D_LAT, D_QLAT + D_LAT + MISC_W, D_QLAT + D_LAT + MISC_W + W_SSM
    cq_o[0] = _rms(z[:, :o0], gcq_ref[...]).astype(cq_o.dtype)
    ckv_o[0] = _rms(z[:, o0:o1], gckv_ref[...]).astype(ckv_o.dtype)
    misc_o[0] = z[:, o1:o2]
    ussm_o[...] = z[:, o2:o3]
    upool_o[0] = z[:, o3:]


def _inproj_call(x, g_pre, sc, sh, w_small, g_cq, g_ckv, ts):
    B, S, D = x.shape
    row = lambda b, i: (b, i, 0)
    per_b = lambda b, i: (b, 0, 0)
    const2 = lambda b, i: (0, 0)
    return pl.pallas_call(
        _inproj_kernel,
        grid=(B, S // ts),
        in_specs=[pl.BlockSpec((1, ts, D), row),
                  pl.BlockSpec((1, D), const2),
                  pl.BlockSpec((1, 1, D), per_b),
                  pl.BlockSpec((1, 1, D), per_b),
                  pl.BlockSpec((D, W_SMALL), const2),
                  pl.BlockSpec((1, D_QLAT), const2),
                  pl.BlockSpec((1, D_LAT), const2)],
        out_specs=[pl.BlockSpec((1, ts, D_QLAT), row),
                   pl.BlockSpec((1, ts, D_LAT), row),
                   pl.BlockSpec((1, ts, MISC_W), row),
                   pl.BlockSpec((ts, W_SSM), lambda b, i: (i, b)),
                   pl.BlockSpec((1, ts, W_POOL), row)],
        out_shape=[jax.ShapeDtypeStruct((B, S, D_QLAT), _MXU_DTYPE),
                   jax.ShapeDtypeStruct((B, S, D_LAT), _MXU_DTYPE),
                   jax.ShapeDtypeStruct((B, S, MISC_W), F32),
                   jax.ShapeDtypeStruct((S, B * W_SSM), F32),
                   jax.ShapeDtypeStruct((B, S, W_POOL), F32)],
        compiler_params=_params("arbitrary", "arbitrary"),
        name="in_projection",
    )(x, g_pre, sc, sh, w_small, g_cq, g_ckv)


def _sortable(i):
    return i ^ ((i >> 31) & jnp.int32(0x7FFFFFFF))


def _attn_kernel(slope_ref, cq_ref, ckv_ref, misc_ref, wqi_ref, wuq_ref, wuv_ref, o_ref,
                 kaug_sc, ckvt_sc, qi_sc, qt_sc, score_sc, lg_sc, m_sc, l_sc, acc_sc, yt_sc, *, topk):
    j = pl.program_id(1)
    nkt = ckvt_sc.shape[0]
    q0 = pl.multiple_of(j * QB, QB)

    @pl.when(j == 0)
    def _():
        lane = lax.broadcasted_iota(jnp.int32, (QB, D_LAT), 1)
        s_in = lax.broadcasted_iota(jnp.int32, (QB, D_LAT), 0).astype(F32)
        for kt in range(nkt):
            kv = ckv_ref[0, kt * QB:(kt + 1) * QB, :]
            ckvt_sc[kt] = kv.astype(F32).T.astype(ckvt_sc.dtype)
            kaug_sc[kt * QB:(kt + 1) * QB, :D_LAT] = kv
            pos = jnp.where(lane == 0, s_in, jnp.where(lane == 1, float(kt), 0.0))
            kaug_sc[kt * QB:(kt + 1) * QB, D_LAT:] = pos.astype(kaug_sc.dtype)

    cqt = cq_ref[0].astype(F32).T.astype(_MXU_DTYPE)
    misct = misc_ref[0, pl.ds(q0, QB), :].T
    qi_sc[...] = _mm(wqi_ref[...], cqt).astype(qi_sc.dtype)
    sub = lax.broadcasted_iota(jnp.int32, (D_LAT, QB), 0)
    for h in range(N_HEADS):
        qt_sc[h, :D_LAT, :] = (_mm(wuq_ref[h], cqt) * ATTN_SCALE).astype(qt_sc.dtype)
        slope = slope_ref[h]
        qt_sc[h, D_LAT:, :] = jnp.where(sub == 0, slope, jnp.where(sub == 1, slope * QB, 0.0)).astype(qt_sc.dtype)

    kl = lax.broadcasted_iota(jnp.int32, (QB, QB), 0)
    ql = lax.broadcasted_iota(jnp.int32, (QB, QB), 1)
    diag_ok = kl < (ql // CHUNK + 1) * CHUNK
    ahead = 2.0 * jnp.maximum(kl - ql, 0).astype(F32)

    def score_tile(kt):
        k0 = pl.multiple_of(kt * QB, QB)
        kmat = misc_ref[0, pl.ds(k0, QB), :].astype(_MXU_DTYPE)
        acc = jnp.zeros((QB, QB), F32)
        for h in range(IDX_HEADS):
            lg = jnp.dot(kmat, qi_sc[h * LANES:(h + 1) * LANES, :], preferred_element_type=F32)
            wq = misct[IDX_DIM + h:IDX_DIM + h + 1, :] * (IDX_SCALE * IDX_HEAD_SCALE)
            acc = acc + jnp.maximum(lg, 0.0) * wq
        return acc

    def score_body(kt, carry):
        cmin, cmax = carry
        s = score_tile(kt)
        score_sc[pl.ds(pl.multiple_of(kt * QB, QB), QB), :] = s
        return (jnp.minimum(cmin, jnp.min(s, axis=0, keepdims=True)),
                jnp.maximum(cmax, jnp.max(s, axis=0, keepdims=True)))

    big = jnp.full((1, QB), 3.0e38, F32)
    cmin, cmax = lax.fori_loop(0, j, score_body, (big, -big))
    s = score_tile(j)
    cmin = jnp.minimum(cmin, jnp.min(jnp.where(diag_ok, s, 3.0e38), axis=0, keepdims=True))
    cmax = jnp.maximum(cmax, jnp.max(jnp.where(diag_ok, s, -3.0e38), axis=0, keepdims=True))
    score_sc[pl.ds(q0, QB), :] = jnp.where(diag_ok, s, -jnp.inf)

    def count(pred, thr):
        def body(kt, acc):
            t = score_sc[pl.ds(pl.multiple_of(kt * QB, QB), QB), :]
            ind = jnp.where(pred(t, thr), 1.0, 0.0)
            return acc + jnp.sum(ind.reshape(QB // SUBLANES, SUBLANES, QB), axis=0)
        acc = lax.fori_loop(0, j + 1, body, jnp.zeros((SUBLANES, QB), F32))
        return jnp.sum(acc, axis=0, keepdims=True)

    ge = lambda t, thr: t >= thr
    gt = lambda t, thr: t > thr

    def key_to_f32(k):
        return pltpu.bitcast(_sortable(k), F32)

    @pl.when((j + 1) * QB > topk)
    def _():
        kf = jnp.float32(topk)
        lo0 = _sortable(pltpu.bitcast(cmin, jnp.int32))
        hi0 = _sortable(pltpu.bitcast(cmax, jnp.int32)) + 1

        def any_lane(flag):
            f = jnp.where(flag, 1.0, 0.0)
            parts = [f[:, k * LANES:(k + 1) * LANES] for k in range(QB // LANES)]
            return jnp.max(functools.reduce(jnp.maximum, parts)) > 0.0

        def cond(c):
            lo, hi, _ = c
            return any_lane(hi > lo + 1)

        def one_pass(c):
            lo, hi, done = c
            mid = (lo >> 1) + (hi >> 1) + (lo & hi & 1)
            cnt = count(ge, key_to_f32(mid))
            is_ge = cnt >= kf
            is_eq = cnt == kf
            lo = jnp.where(is_ge, mid, lo)
            hi = jnp.where(is_eq, mid + 1, jnp.where(is_ge, hi, mid))
            return lo, hi, jnp.where(is_eq, 1, done)

        def body(c):
            for _ in range(BISECT_PASSES):
                c = one_pass(c)
            return c

        lo, _, done = lax.while_loop(cond, body, (lo0, hi0, jnp.zeros((1, QB), jnp.int32)))
        thr = key_to_f32(lo)
        ties = any_lane(done == 0)

        @pl.when(jnp.logical_not(ties))
        def _():
            def body(kt, _):
                r = pl.ds(pl.multiple_of(kt * QB, QB), QB)
                score_sc[r, :] = jnp.where(score_sc[r, :] >= thr, 0.0, NEG_INF)
                return 0
            lax.fori_loop(0, j + 1, body, 0)

        @pl.when(ties)
        def _():
            need = jnp.where(done > 0, 3.0e38, kf - count(gt, thr))
            tri = (lax.broadcasted_iota(jnp.int32, (QB, QB), 0)
                   >= lax.broadcasted_iota(jnp.int32, (QB, QB), 1)).astype(_MXU_DTYPE)

            def body(kt, seen):
                r = pl.ds(pl.multiple_of(kt * QB, QB), QB)
                t = score_sc[r, :]
                eq = jnp.where(t == thr, 1.0, 0.0)
                rank = seen + jnp.dot(tri, eq.astype(_MXU_DTYPE), preferred_element_type=F32)
                sel = (t > thr) | ((t == thr) & (rank <= need))
                score_sc[r, :] = jnp.where(sel, 0.0, NEG_INF)
                return seen + jnp.sum(eq, axis=0, keepdims=True)
            lax.fori_loop(0, j + 1, body, jnp.zeros((1, QB), F32))

    @pl.when((j + 1) * QB <= topk)
    def _():
        def body(kt, _):
            r = pl.ds(pl.multiple_of(kt * QB, QB), QB)
            score_sc[r, :] = jnp.where(score_sc[r, :] > -jnp.inf, 0.0, NEG_INF)
            return 0
        lax.fori_loop(0, j + 1, body, 0)

    m_sc[...] = jnp.full(m_sc.shape, -3.0e38, F32)
    l_sc[...] = jnp.zeros(l_sc.shape, F32)
    acc_sc[...] = jnp.zeros(acc_sc.shape, F32)

    def attend(kt, ahead_of_query):
        k0 = pl.multiple_of(kt * QB, QB)
        keys = kaug_sc[pl.ds(k0, QB), :]
        vals_t = ckvt_sc[kt]
        mask = score_sc[pl.ds(k0, QB), :]
        def logits(h):
            lg = jnp.dot(keys, qt_sc[h], preferred_element_type=F32) + mask
            if ahead_of_query is not None:
                lg = lg - slope_ref[h] * ahead_of_query
            lg_sc[h] = lg
            return jnp.max(lg, axis=0, keepdims=True)

        tile_max = logits(0)
        for h in range(N_HEADS):
            next_max = logits(h + 1) if h + 1 < N_HEADS else None
            m = m_sc[h]
            m_new = jnp.maximum(m, tile_max)
            alpha = jnp.exp(m - m_new)
            m_sc[h] = m_new
            p = jnp.exp(lg_sc[h] - m_new)
            l_sc[h] = alpha * l_sc[h] + jnp.sum(p, axis=0, keepdims=True)
            acc_sc[h] = alpha * acc_sc[h] + jnp.dot(vals_t, p.astype(_MXU_DTYPE), preferred_element_type=F32)
            tile_max = next_max

    def off_diag(kt, _):
        attend(kt, None)
        return 0

    lax.fori_loop(0, j, off_diag, 0)
    attend(j, ahead)
    for h in range(N_HEADS):
        yt_sc[h * D_VHEAD:(h + 1) * D_VHEAD, :] = _mm(wuv_ref[h], acc_sc[h] / l_sc[h])
    o_ref[0] = yt_sc[...].T.astype(o_ref.dtype)


def _attn_call(cqn, ckvn, misc, wqi, wuq, wuv, topk):
    B, S, _ = cqn.shape
    nkt = S // QB
    slopes = jnp.exp2(-8.0 * jnp.arange(1, N_HEADS + 1, dtype=F32) / N_HEADS)
    per_b = lambda b, j: (b, 0, 0)
    const2 = lambda b, j: (0, 0)
    const3 = lambda b, j: (0, 0, 0)
    return pl.pallas_call(
        functools.partial(_attn_kernel, topk=topk),
        grid=(B, nkt),
        in_specs=[pl.BlockSpec(memory_space=pltpu.SMEM),
                  pl.BlockSpec((1, QB, D_QLAT), lambda b, j: (b, j, 0)),
                  pl.BlockSpec((1, S, D_LAT), per_b),
                  pl.BlockSpec((1, S, MISC_W), per_b),
                  pl.BlockSpec((IDX_HEADS * LANES, D_QLAT), const2),
                  pl.BlockSpec((N_HEADS, D_LAT, D_QLAT), const3),
                  pl.BlockSpec((N_HEADS, D_VHEAD, D_LAT), const3)],
        out_specs=pl.BlockSpec((1, QB, W_ATTN), lambda b, j: (b, j, 0)),
        out_shape=jax.ShapeDtypeStruct((B, S, W_ATTN), _MXU_DTYPE),
        scratch_shapes=[pltpu.VMEM((S, 2 * D_LAT), _MXU_DTYPE),
                        pltpu.VMEM((nkt, D_LAT, QB), _MXU_DTYPE),
                        pltpu.VMEM((IDX_HEADS * LANES, QB), _MXU_DTYPE),
                        pltpu.VMEM((N_HEADS, 2 * D_LAT, QB), _MXU_DTYPE),
                        pltpu.VMEM((S, QB), F32),
                        pltpu.VMEM((N_HEADS, QB, QB), F32),
                        pltpu.VMEM((N_HEADS, 1, QB), F32),
                        pltpu.VMEM((N_HEADS, 1, QB), F32),
                        pltpu.VMEM((N_HEADS, D_LAT, QB), F32),
                        pltpu.VMEM((W_ATTN, QB), F32)],
        compiler_params=_params("arbitrary", "arbitrary"),
        name="dsa_attention",
    )(slopes, cqn, ckvn, misc, wqi, wuq, wuv)


def _ssm_kernel(u_ref, bd_ref, lr_ref, li_ref, cd_ref, d_ref, wg_ref, bg_ref, o_ref, x_sc, bu_sc, *, nb, cw):
    tc = u_ref.shape[0] // nb

    @pl.when(pl.program_id(0) == 0)
    def _():
        x_sc[...] = jnp.zeros_like(x_sc)

    u = u_ref[...]
    bu_sc[...] = _mm(u, bd_ref[...])
    for c in range(N_STATE // cw):
        re = slice(c * cw, (c + 1) * cw)
        im = slice(N_STATE + c * cw, N_STATE + (c + 1) * cw)
        lr = lr_ref[:, re]
        li = li_ref[:, re]

        def step(t, carry, re=re, im=im, lr=lr, li=li):
            xr, xi = carry
            rows = pl.ds(pl.multiple_of(t * nb, nb), nb)
            nr = lr * xr - li * xi + bu_sc[rows, re]
            ni = lr * xi + li * xr + bu_sc[rows, im]
            bu_sc[rows, re] = nr
            bu_sc[rows, im] = ni
            return nr, ni

        xr, xi = lax.fori_loop(0, tc, step, (x_sc[:, re], x_sc[:, im]))
        x_sc[:, re] = xr
        x_sc[:, im] = xi
    y = _mm(bu_sc[...], cd_ref[...]) + d_ref[...] * u
    z = _gelu(y)
    o_ref[...] = z * _sigmoid(_mm(z, wg_ref[...]) + bg_ref[...])


def _ssm_call(u_tm, bd, lr, li, cd, dskip, w_glu, b_glu, nb, tc):
    rows = u_tm.shape[0]
    const2 = lambda i: (0, 0)
    return pl.pallas_call(
        functools.partial(_ssm_kernel, nb=nb, cw=256),
        grid=(rows // (tc * nb),),
        in_specs=[pl.BlockSpec((tc * nb, W_SSM), lambda i: (i, 0)),
                  pl.BlockSpec((W_SSM, 2 * N_STATE), const2),
                  pl.BlockSpec((1, N_STATE), const2),
                  pl.BlockSpec((1, N_STATE), const2),
                  pl.BlockSpec((2 * N_STATE, W_SSM), const2),
                  pl.BlockSpec((1, W_SSM), const2),
                  pl.BlockSpec((W_SSM, W_SSM), const2),
                  pl.BlockSpec((1, W_SSM), const2)],
        out_specs=pl.BlockSpec((tc * nb, W_SSM), lambda i: (i, 0)),
        out_shape=jax.ShapeDtypeStruct((rows, W_SSM), F32),
        scratch_shapes=[pltpu.VMEM((nb, 2 * N_STATE), F32),
                        pltpu.VMEM((tc * nb, 2 * N_STATE), F32)],
        compiler_params=_params("arbitrary"),
        name="s5_ssm",
    )(u_tm, bd, lr, li, cd, dskip, w_glu, b_glu)


def _ssm_weights(a_re, a_im, b_re, b_im, c_re, c_im, log_step):
    step = jnp.exp(log_step)[:, None]
    er = jnp.exp(a_re * step)
    ang = a_im * step
    lr, li = er * jnp.cos(ang), er * jnp.sin(ang)
    den = a_re * a_re + a_im * a_im
    fr = ((lr - 1.0) * a_re + li * a_im) / den
    fi = (li * a_re - (lr - 1.0) * a_im) / den
    br = fr[:, :, None] * b_re - fi[:, :, None] * b_im
    bi = fr[:, :, None] * b_im + fi[:, :, None] * b_re
    eye = jnp.eye(SSM_GROUPS, dtype=F32)
    pack_b = lambda m: jnp.einsum('gnp,gh->gphn', m, eye).reshape(W_SSM, N_STATE)
    pack_c = lambda m: jnp.einsum('gpn,gh->gnhp', m, eye).reshape(N_STATE, W_SSM)
    bd = jnp.concatenate([pack_b(br), pack_b(bi)], axis=1)
    cd = jnp.concatenate([pack_c(c_re), pack_c(-c_im)], axis=0)
    return (bd.astype(_MXU_DTYPE), lr.reshape(1, N_STATE), li.reshape(1, N_STATE), cd.astype(_MXU_DTYPE))


def _merge_kernel(x_ref, gpre_ref, sc_ref, sh_ref, gt_ref, gpost_ref, wg_ref, ya_ref, ys_ref,
                  up_ref, uh_ref, wp_ref, ps_ref, pa_ref, pb_ref, pc_ref, wo_ref, o_ref, halo_sc):
    i = pl.program_id(1)
    ts = x_ref.shape[1]
    x = x_ref[0]
    h = _rms(x, gpre_ref[...]) * (1.0 + sc_ref[0]) + sh_ref[0]

    u = up_ref[0]
    halo_sc[:POOL_HALO, :] = jnp.where(i > 0, uh_ref[0], 0.0)
    halo_sc[POOL_HALO:, :] = u
    lane = lax.broadcasted_iota(jnp.int32, (ts, W_POOL), 1)
    tpos = (i * ts + lax.broadcasted_iota(jnp.int32, (ts, W_POOL), 0) + 1).astype(F32)
    run = u
    pooled = jnp.zeros_like(u)
    prev = 1
    for g, win in enumerate(POOL_WINDOWS):
        for k in range(prev, win):
            run = run + halo_sc[POOL_HALO - k:POOL_HALO - k + ts, :]
        prev = win
        in_group = (lane >= g * POOL_GROUP_DIM) & (lane < (g + 1) * POOL_GROUP_DIM)
        pooled = jnp.where(in_group, run / jnp.minimum(tpos, float(win)), pooled)
    y_pool = _mm(pooled - u, wp_ref[...]) * ps_ref[...]

    d = x.shape[-1]
    merged = (_sigmoid(_mm(h, wg_ref[:, :d])) * _mm(ya_ref[0], pa_ref[...])
              + _sigmoid(_mm(h, wg_ref[:, d:2 * d])) * _mm(ys_ref[...], pb_ref[...])
              + _sigmoid(_mm(h, wg_ref[:, 2 * d:])) * _mm(y_pool, pc_ref[...]))
    y = _mm(merged, wo_ref[...])
    o_ref[0] = x + gt_ref[0] * _rms(y, gpost_ref[...])


def _merge_call(x, g_pre, sc, sh, gt, g_post, w_gate, y_attn, y_ssm_tm, u_pool, w_pool_bd, pool_scale,
                p_a, p_b, p_c, w_out, ts):
    B, S, D = x.shape
    row = lambda b, i: (b, i, 0)
    per_b = lambda b, i: (b, 0, 0)
    const2 = lambda b, i: (0, 0)
    hb = ts // POOL_HALO
    return pl.pallas_call(
        _merge_kernel,
        grid=(B, S // ts),
        in_specs=[pl.BlockSpec((1, ts, D), row),
                  pl.BlockSpec((1, D), const2),
                  pl.BlockSpec((1, 1, D), per_b),
                  pl.BlockSpec((1, 1, D), per_b),
                  pl.BlockSpec((1, 1, D), per_b),
                  pl.BlockSpec((1, D), const2),
                  pl.BlockSpec((D, N_BRANCH * D), const2),
                  pl.BlockSpec((1, ts, W_ATTN), row),
                  pl.BlockSpec((ts, W_SSM), lambda b, i: (i, b)),
                  pl.BlockSpec((1, ts, W_POOL), row),
                  pl.BlockSpec((1, POOL_HALO, W_POOL), lambda b, i: (b, jnp.maximum(i * hb - 1, 0), 0)),
                  pl.BlockSpec((W_POOL, W_POOL), const2),
                  pl.BlockSpec((1, W_POOL), const2),
                  pl.BlockSpec((W_ATTN, D), const2),
                  pl.BlockSpec((W_SSM, D), const2),
                  pl.BlockSpec((W_POOL, D), const2),
                  pl.BlockSpec((D, D), const2)],
        out_specs=pl.BlockSpec((1, ts, D), row),
        out_shape=jax.ShapeDtypeStruct((B, S, D), F32),
        scratch_shapes=[pltpu.VMEM((POOL_HALO + ts, W_POOL), F32)],
        compiler_params=_params("arbitrary", "arbitrary"),
        name="mixer_merge",
    )(x, g_pre, sc, sh, gt, g_post, w_gate, y_attn, y_ssm_tm, u_pool, u_pool, w_pool_bd, pool_scale,
      p_a, p_b, p_c, w_out)


def _ffn_kernel(x_ref, xh_ref, gpre_ref, sc_ref, sh_ref, gt_ref, gpost_ref, wup_ref, cw_ref, cb_ref,
                wdn_ref, o_ref, h_sc, ug_sc, uv_sc, *, cols):
    i = pl.program_id(1)
    ts = x_ref.shape[1]
    dff = wdn_ref.shape[0]
    x = x_ref[0]
    adaln = lambda v: _rms(v, gpre_ref[...]) * (1.0 + sc_ref[0]) + sh_ref[0]
    h_sc[:CONV_HALO, :] = adaln(xh_ref[0]).astype(h_sc.dtype)
    h_sc[CONV_HALO:, :] = adaln(x).astype(h_sc.dtype)
    keep = jnp.where(i > 0, 1.0, 0.0)
    y = jnp.zeros((ts, x.shape[-1]), F32)
    for c in range(dff // cols):
        halves = []
        for sc_buf, off in ((ug_sc, 0), (uv_sc, dff)):
            cs = slice(off + c * cols, off + (c + 1) * cols)
            up = jnp.dot(h_sc[...], wup_ref[:, cs], preferred_element_type=F32)
            sc_buf[:CONV_HALO, :] = up[:CONV_HALO] * keep
            sc_buf[CONV_HALO:, :] = up[CONV_HALO:]
            acc = cb_ref[:, cs]
            for k in range(CONV_WIDTH):
                r0 = CONV_HALO - (CONV_WIDTH - 1) + k
                acc = acc + cw_ref[k:k + 1, cs] * sc_buf[r0:r0 + ts, :]
            halves.append(acc)
        y = y + _mm(_gelu(halves[0]) * halves[1], wdn_ref[c * cols:(c + 1) * cols, :])
    o_ref[0] = x + gt_ref[0] * _rms(y, gpost_ref[...])


def _ffn_call(x, g_pre, sc, sh, gt, g_post, w_up, conv_w, conv_b, w_down, ts, cols=512):
    B, S, D = x.shape
    dff = w_down.shape[0]
    row = lambda b, i: (b, i, 0)
    per_b = lambda b, i: (b, 0, 0)
    const2 = lambda b, i: (0, 0)
    hb = ts // CONV_HALO
    return pl.pallas_call(
        functools.partial(_ffn_kernel, cols=cols),
        grid=(B, S // ts),
        in_specs=[pl.BlockSpec((1, ts, D), row),
                  pl.BlockSpec((1, CONV_HALO, D), lambda b, i: (b, jnp.maximum(i * hb - 1, 0), 0)),
                  pl.BlockSpec((1, D), const2),
                  pl.BlockSpec((1, 1, D), per_b),
                  pl.BlockSpec((1, 1, D), per_b),
                  pl.BlockSpec((1, 1, D), per_b),
                  pl.BlockSpec((1, D), const2),
                  pl.BlockSpec((D, 2 * dff), const2),
                  pl.BlockSpec((CONV_WIDTH, 2 * dff), const2),
                  pl.BlockSpec((1, 2 * dff), const2),
                  pl.BlockSpec((dff, D), const2)],
        out_specs=pl.BlockSpec((1, ts, D), row),
        out_shape=jax.ShapeDtypeStruct((B, S, D), F32),
        scratch_shapes=[pltpu.VMEM((CONV_HALO + ts, D), _MXU_DTYPE),
                        pltpu.VMEM((CONV_HALO + ts, cols), F32),
                        pltpu.VMEM((CONV_HALO + ts, cols), F32)],
        compiler_params=_params("arbitrary", "arbitrary"),
        name="conv_gated_ffn",
    )(x, x, g_pre, sc, sh, gt, g_post, w_up, conv_w, conv_b, w_down)


def _pick(n, pref):
    t = min(n, pref)
    assert n % t == 0, (n, t)
    return t


def kernel(x, c, mod_w, mod_b, mix_pre_g, mix_post_g, ffn_pre_g, ffn_post_g, w_in, g_cq, w_uq, w_qi, g_ckv, w_uv, a_re, a_im, b_re, b_im, c_re, c_im, d_skip, log_step, w_glu, b_glu, w_pool, pool_scale, p_a, p_b, p_c, w_out, w_up, conv_w, conv_b, w_down):
    B, S, D = x.shape
    depth = mod_w.shape[0]
    assert S % QB == 0 and QB % CHUNK == 0
    topk = min(DSA_TOPK_MAX, S // 4)
    ts = _pick(S, 512)
    ts_tail = _pick(S, 256)
    tc = _pick(S, 32)
    cast = lambda w: w.astype(_MXU_DTYPE)
    row = lambda v: v.reshape(1, -1)

    mod = _mod_call(c, mod_w, mod_b)
    cuts = [0]
    for wdt in IN_SPLITS:
        cuts.append(cuts[-1] + wdt)
    eye_p = jnp.eye(POOL_GROUPS, dtype=F32)
    for l in range(depth):
        sh_m, sc_m, gt_m, sh_f, sc_f, gt_f = [mod[l][:, None, k * D:(k + 1) * D] for k in range(6)]
        wl = w_in[l]
        pad = jnp.zeros((D, MISC_W - IDX_DIM - IDX_HEADS), F32)
        w_small = cast(jnp.concatenate(
            [wl[:, cuts[0]:cuts[2]], wl[:, cuts[2]:cuts[4]], pad, wl[:, cuts[4]:cuts[6]]], axis=1))
        w_gate = cast(wl[:, cuts[6]:])
        cqn, ckvn, misc, u_ssm, u_pool = _inproj_call(
            x, row(mix_pre_g[l]), sc_m, sh_m, w_small, row(g_cq[l]), row(g_ckv[l]), ts)

        wqi = jnp.transpose(w_qi[l], (1, 2, 0))
        wqi = jnp.pad(wqi, ((0, 0), (0, LANES - IDX_DIM), (0, 0))).reshape(IDX_HEADS * LANES, D_QLAT)
        y_attn = _attn_call(cqn, ckvn, misc, cast(wqi), cast(jnp.transpose(w_uq[l], (1, 2, 0))),
                            cast(jnp.transpose(w_uv[l], (0, 2, 1))), topk)

        bd, lr, li, cd = _ssm_weights(a_re[l], a_im[l], b_re[l], b_im[l], c_re[l], c_im[l], log_step[l])
        y_ssm = _ssm_call(u_ssm.reshape(S * B, W_SSM), bd, lr, li, cd, row(d_skip[l]), cast(w_glu[l]),
                          row(b_glu[l]), B, tc).reshape(S, B * W_SSM)

        w_pool_bd = cast(jnp.einsum('gcd,gh->gchd', w_pool[l], eye_p).reshape(W_POOL, W_POOL))
        x = _merge_call(x, row(mix_pre_g[l]), sc_m, sh_m, gt_m, row(mix_post_g[l]), w_gate, y_attn, y_ssm,
                        u_pool, w_pool_bd, row(pool_scale[l]), cast(p_a[l]), cast(p_b[l]), cast(p_c[l]),
                        cast(w_out[l]), ts_tail)
        x = _ffn_call(x, row(ffn_pre_g[l]), sc_f, sh_f, gt_f, row(ffn_post_g[l]), cast(w_up[l]), conv_w[l],
                      row(conv_b[l]), cast(w_down[l]), ts_tail)
    return x
```

```python
import functools
import math

import jax
import jax.numpy as jnp
from jax import lax
from jax.experimental import pallas as pl
from jax.experimental.pallas import tpu as pltpu

F32 = jnp.float32
_MXU_DTYPE = jnp.bfloat16

CHUNK = 64
N_HEADS = 8
D_QLAT = 256
D_LAT = 128
D_VHEAD = 64
IDX_HEADS = 8
IDX_DIM = 32
DSA_TOPK_MAX = 256
SSM_GROUPS = 16
SSM_GROUP_DIM = 16
SSM_STATE = 64
W_SSM = SSM_GROUPS * SSM_GROUP_DIM
N_STATE = SSM_GROUPS * SSM_STATE
POOL_WINDOWS = (2, 4, 8, 16)
POOL_GROUPS = 4
POOL_GROUP_DIM = 64
W_POOL = POOL_GROUPS * POOL_GROUP_DIM
W_ATTN = N_HEADS * D_VHEAD
N_BRANCH = 3
CONV_WIDTH = 3
RMS_EPS = 1e-6
NEG_INF = -1e30
ATTN_SCALE = D_LAT ** -0.5
IDX_SCALE = IDX_DIM ** -0.5
IDX_HEAD_SCALE = IDX_HEADS ** -0.5
IN_SPLITS = (D_QLAT, D_LAT, IDX_DIM, IDX_HEADS, W_SSM, W_POOL)

LANES = 128
SUBLANES = 8
MISC_W = LANES
W_SMALL = D_QLAT + D_LAT + MISC_W + W_SSM + W_POOL
VMEM_LIMIT = 56 * 1024 * 1024

QB = 256
BISECT_PASSES = 4
SLOPE_PARTS = 3
LOG2E = math.log2(math.e)
COUNT_ROWS = 4 * SUBLANES
POOL_HALO = 16
CONV_HALO = 16


def _mm(a, b):
    return jnp.dot(a.astype(_MXU_DTYPE), b.astype(_MXU_DTYPE), preferred_element_type=F32)


def _rms(x, g):
    return x * lax.rsqrt(jnp.mean(x * x, axis=-1, keepdims=True) + RMS_EPS) * g


def _gelu(x):
    return 0.5 * x * (1.0 + jnp.tanh(math.sqrt(2.0 / math.pi) * (x + 0.044715 * (x * x * x))))


def _sigmoid(x):
    return 1.0 / (1.0 + jnp.exp(-x))


def _params(*sem):
    return pltpu.CompilerParams(dimension_semantics=sem, vmem_limit_bytes=VMEM_LIMIT)


def _mod_kernel(c_ref, w_ref, b_ref, o_ref):
    c = c_ref[...]
    cond = c * _sigmoid(c)
    o_ref[0] = _mm(cond, w_ref[0]) + b_ref[0]


def _mod_call(c, mod_w, mod_b):
    L, D, D6 = mod_w.shape
    B = c.shape[0]
    nt = D6 // D
    return pl.pallas_call(
        _mod_kernel,
        grid=(L, nt),
        in_specs=[pl.BlockSpec((B, D), lambda l, n: (0, 0)),
                  pl.BlockSpec((1, D, D), lambda l, n: (l, 0, n)),
                  pl.BlockSpec((1, 1, D), lambda l, n: (l, 0, n))],
        out_specs=pl.BlockSpec((1, B, D), lambda l, n: (l, 0, n)),
        out_shape=jax.ShapeDtypeStruct((L, B, D6), F32),
        compiler_params=_params("arbitrary", "arbitrary"),
        name="adaln_mod",
    )(c, mod_w.astype(_MXU_DTYPE), mod_b.reshape(L, 1, D6))


def _inproj_kernel(x_ref, g_ref, sc_ref, sh_ref, w_ref, gcq_ref, gckv_ref,
                   cq_o, ckv_o, misc_o, ussm_o, upool_o):
    h = _rms(x_ref[0], g_ref[...]) * (1.0 + sc_ref[0]) + sh_ref[0]
    z = _mm(h, w_ref[...])
    o0, o1, o2, o3 = D_QLAT, D_QLAT + D_LAT, D_QLAT + D_LAT + MISC_W, D_QLAT + D_LAT + MISC_W + W_SSM
    cq_o[0] = _rms(z[:, :o0], gcq_ref[...]).astype(cq_o.dtype)
    ckv_o[0] = _rms(z[:, o0:o1], gckv_ref[...]).astype(ckv_o.dtype)
    misc_o[0] = z[:, o1:o2]
    ussm_o[...] = z[:, o2:o3]
    upool_o[0] = z[:, o3:]


def _inproj_call(x, g_pre, sc, sh, w_small, g_cq, g_ckv, ts):
    B, S, D = x.shape
    row = lambda b, i: (b, i, 0)
    per_b = lambda b, i: (b, 0, 0)
    const2 = lambda b, i: (0, 0)
    return pl.pallas_call(
        _inproj_kernel,
        grid=(B, S // ts),
        in_specs=[pl.BlockSpec((1, ts, D), row),
                  pl.BlockSpec((1, D), const2),
                  pl.BlockSpec((1, 1, D), per_b),
                  pl.BlockSpec((1, 1, D), per_b),
                  pl.BlockSpec((D, W_SMALL), const2),
                  pl.BlockSpec((1, D_QLAT), const2),
                  pl.BlockSpec((1, D_LAT), const2)],
        out_specs=[pl.BlockSpec((1, ts, D_QLAT), row),
                   pl.BlockSpec((1, ts, D_LAT), row),
                   pl.BlockSpec((1, ts, MISC_W), row),
                   pl.BlockSpec((ts, W_SSM), lambda b, i: (i, b)),
                   pl.BlockSpec((1, ts, W_POOL), row)],
        out_shape=[jax.ShapeDtypeStruct((B, S, D_QLAT), _MXU_DTYPE),
                   jax.ShapeDtypeStruct((B, S, D_LAT), _MXU_DTYPE),
                   jax.ShapeDtypeStruct((B, S, MISC_W), F32),
                   jax.ShapeDtypeStruct((S, B * W_SSM), F32),
                   jax.ShapeDtypeStruct((B, S, W_POOL), F32)],
        compiler_params=_params("arbitrary", "arbitrary"),
        name="in_projection",
    )(x, g_pre, sc, sh, w_small, g_cq, g_ckv)


def _sortable(i):
    return i ^ ((i >> 31) & jnp.int32(0x7FFFFFFF))


def _attn_kernel(slope_ref, cq_ref, ckv_ref, misc_ref, wqi_ref, wuq_ref, wuv_ref, o_ref,
                 kaug_sc, ckvt_sc, qi_sc, qt_sc, score_sc, lg_sc, m_sc, l_sc, acc_sc, yt_sc, *, topk):
    j = pl.program_id(1)
    nkt = ckvt_sc.shape[0]
    q0 = pl.multiple_of(j * QB, QB)

    @pl.when(j == 0)
    def _():
        lane = lax.broadcasted_iota(jnp.int32, (QB, D_LAT), 1)
        s_in = lax.broadcasted_iota(jnp.int32, (QB, D_LAT), 0).astype(F32)
        for kt in range(nkt):
            kv = ckv_ref[0, kt * QB:(kt + 1) * QB, :]
            ckvt_sc[kt] = kv.astype(F32).T.astype(ckvt_sc.dtype)
            kaug_sc[kt * QB:(kt + 1) * QB, :D_LAT] = kv
            pos = jnp.where(lane < SLOPE_PARTS, s_in, jnp.where(lane < 2 * SLOPE_PARTS, float(kt), 0.0))
            kaug_sc[kt * QB:(kt + 1) * QB, D_LAT:] = pos.astype(kaug_sc.dtype)

    cqt = cq_ref[0].astype(F32).T.astype(_MXU_DTYPE)
    misct = misc_ref[0, pl.ds(q0, QB), :].T
    qi_sc[...] = _mm(wqi_ref[...], cqt).astype(qi_sc.dtype)
    sub = lax.broadcasted_iota(jnp.int32, (D_LAT, QB), 0)
    for h in range(N_HEADS):
        qt_sc[h, :D_LAT, :] = (_mm(wuq_ref[h], cqt) * (ATTN_SCALE * LOG2E)).astype(qt_sc.dtype)
        rows = jnp.zeros((D_LAT, QB), F32)
        for part in range(SLOPE_PARTS):
            rows = jnp.where(sub == part, slope_ref[h, part], rows)
            rows = jnp.where(sub == SLOPE_PARTS + part, slope_ref[h, part] * QB, rows)
        qt_sc[h, D_LAT:, :] = rows.astype(qt_sc.dtype)

    kl = lax.broadcasted_iota(jnp.int32, (QB, QB), 0)
    ql = lax.broadcasted_iota(jnp.int32, (QB, QB), 1)
    diag_ok = kl < (ql // CHUNK + 1) * CHUNK
    ahead = 2.0 * jnp.maximum(kl - ql, 0).astype(F32)

    def score_tile(kt):
        k0 = pl.multiple_of(kt * QB, QB)
        kmat = misc_ref[0, pl.ds(k0, QB), :].astype(_MXU_DTYPE)
        acc = jnp.zeros((QB, QB), F32)
        for h in range(IDX_HEADS):
            lg = jnp.dot(kmat, qi_sc[h * LANES:(h + 1) * LANES, :], preferred_element_type=F32)
            wq = misct[IDX_DIM + h:IDX_DIM + h + 1, :] * (IDX_SCALE * IDX_HEAD_SCALE)
            acc = acc + jnp.maximum(lg, 0.0) * wq
        return acc

    def score_body(kt, carry):
        cmin, cmax = carry
        s = score_tile(kt)
        score_sc[pl.ds(pl.multiple_of(kt * QB, QB), QB), :] = s
        return (jnp.minimum(cmin, jnp.min(s, axis=0, keepdims=True)),
                jnp.maximum(cmax, jnp.max(s, axis=0, keepdims=True)))

    big = jnp.full((1, QB), 3.0e38, F32)
    cmin, cmax = lax.fori_loop(0, j, score_body, (big, -big))
    s = score_tile(j)
    cmin = jnp.minimum(cmin, jnp.min(jnp.where(diag_ok, s, 3.0e38), axis=0, keepdims=True))
    cmax = jnp.maximum(cmax, jnp.max(jnp.where(diag_ok, s, -3.0e38), axis=0, keepdims=True))
    score_sc[pl.ds(q0, QB), :] = jnp.where(diag_ok, s, -jnp.inf)

    def count(pred, thr):
        def body(kt, acc):
            t = score_sc[pl.ds(pl.multiple_of(kt * QB, QB), QB), :]
            ind = jnp.where(pred(t, thr), 1.0, 0.0)
            return acc + jnp.sum(ind.reshape(QB // COUNT_ROWS, COUNT_ROWS, QB), axis=0)
        acc = lax.fori_loop(0, j + 1, body, jnp.zeros((COUNT_ROWS, QB), F32))
        return jnp.sum(acc, axis=0, keepdims=True)

    ge = lambda t, thr: t >= thr
    gt = lambda t, thr: t > thr

    def key_to_f32(k):
        return pltpu.bitcast(_sortable(k), F32)

    @pl.when((j + 1) * QB > topk)
    def _():
        kf = jnp.float32(topk)
        lo0 = _sortable(pltpu.bitcast(cmin, jnp.int32))
        hi0 = _sortable(pltpu.bitcast(cmax, jnp.int32)) + 1

        def any_lane(flag):
            f = jnp.where(flag, 1.0, 0.0)
            parts = [f[:, k * LANES:(k + 1) * LANES] for k in range(QB // LANES)]
            return jnp.max(functools.reduce(jnp.maximum, parts)) > 0.0

        def cond(c):
            lo, hi, _ = c
            return any_lane(hi > lo + 1)

        def one_pass(c):
            lo, hi, done = c
            mid = (lo >> 1) + (hi >> 1) + (lo & hi & 1)
            cnt = count(ge, key_to_f32(mid))
            is_ge = cnt >= kf
            is_eq = cnt == kf
            lo = jnp.where(is_ge, mid, lo)
            hi = jnp.where(is_eq, mid + 1, jnp.where(is_ge, hi, mid))
            return lo, hi, jnp.where(is_eq, 1, done)

        def body(c):
            for _ in range(BISECT_PASSES):
                c = one_pass(c)
            return c

        lo, _, done = lax.while_loop(cond, body, (lo0, hi0, jnp.zeros((1, QB), jnp.int32)))
        thr = key_to_f32(lo)
        ties = any_lane(done == 0)

        @pl.when(jnp.logical_not(ties))
        def _():
            def body(kt, _):
                r = pl.ds(pl.multiple_of(kt * QB, QB), QB)
                score_sc[r, :] = jnp.where(score_sc[r, :] >= thr, 0.0, NEG_INF)
                return 0
            lax.fori_loop(0, j + 1, body, 0)

        @pl.when(ties)
        def _():
            need = jnp.where(done > 0, 3.0e38, kf - count(gt, thr))
            tri = (lax.broadcasted_iota(jnp.int32, (QB, QB), 0)
                   >= lax.broadcasted_iota(jnp.int32, (QB, QB), 1)).astype(_MXU_DTYPE)

            def body(kt, seen):
                r = pl.ds(pl.multiple_of(kt * QB, QB), QB)
                t = score_sc[r, :]
                eq = jnp.where(t == thr, 1.0, 0.0)
                rank = seen + jnp.dot(tri, eq.astype(_MXU_DTYPE), preferred_element_type=F32)
                sel = (t > thr) | ((t == thr) & (rank <= need))
                score_sc[r, :] = jnp.where(sel, 0.0, NEG_INF)
                return seen + jnp.sum(eq, axis=0, keepdims=True)
            lax.fori_loop(0, j + 1, body, jnp.zeros((1, QB), F32))

    @pl.when((j + 1) * QB <= topk)
    def _():
        def body(kt, _):
            r = pl.ds(pl.multiple_of(kt * QB, QB), QB)
            score_sc[r, :] = jnp.where(score_sc[r, :] > -jnp.inf, 0.0, NEG_INF)
            return 0
        lax.fori_loop(0, j + 1, body, 0)

    m_sc[...] = jnp.full(m_sc.shape, -3.0e38, F32)
    l_sc[...] = jnp.zeros(l_sc.shape, F32)
    acc_sc[...] = jnp.zeros(acc_sc.shape, F32)

    def attend(kt, ahead_of_query):
        k0 = pl.multiple_of(kt * QB, QB)
        keys = kaug_sc[pl.ds(k0, QB), :]
        vals_t = ckvt_sc[kt]
        mask = score_sc[pl.ds(k0, QB), :]
        tile_max = []
        for h in range(N_HEADS):
            lg = jnp.dot(keys, qt_sc[h], preferred_element_type=F32) + mask
            if ahead_of_query is not None:
                lg = lg - slope_ref[h, SLOPE_PARTS] * ahead_of_query
            lg_sc[h] = lg
            tile_max.append(jnp.max(lg, axis=0, keepdims=True))
        for h in range(N_HEADS):
            m = m_sc[h]
            m_new = jnp.maximum(m, tile_max[h])
            alpha = jnp.exp2(m - m_new)
            m_sc[h] = m_new
            p = jnp.exp2(lg_sc[h] - m_new)
            l_sc[h] = alpha * l_sc[h] + jnp.sum(p, axis=0, keepdims=True)
            acc_sc[h] = alpha * acc_sc[h] + jnp.dot(vals_t, p.astype(_MXU_DTYPE), preferred_element_type=F32)

    def off_diag(kt, _):
        attend(kt, None)
        return 0

    lax.fori_loop(0, j, off_diag, 0)
    attend(j, ahead)
    for h in range(N_HEADS):
        yt_sc[h * D_VHEAD:(h + 1) * D_VHEAD, :] = _mm(wuv_ref[h], acc_sc[h] / l_sc[h])
    o_ref[0] = yt_sc[...].T.astype(o_ref.dtype)


def _attn_call(cqn, ckvn, misc, wqi, wuq, wuv, topk):
    B, S, _ = cqn.shape
    nkt = S // QB
    slope = jnp.exp2(-8.0 * jnp.arange(1, N_HEADS + 1, dtype=F32) / N_HEADS) * LOG2E
    parts, rest = [], slope
    for _ in range(SLOPE_PARTS):
        piece = rest.astype(_MXU_DTYPE).astype(F32)
        parts.append(piece)
        rest = rest - piece
    slopes = jnp.stack(parts + [slope], axis=1)
    per_b = lambda b, j: (b, 0, 0)
    const2 = lambda b, j: (0, 0)
    const3 = lambda b, j: (0, 0, 0)
    return pl.pallas_call(
        functools.partial(_attn_kernel, topk=topk),
        grid=(B, nkt),
        in_specs=[pl.BlockSpec(memory_space=pltpu.SMEM),
                  pl.BlockSpec((1, QB, D_QLAT), lambda b, j: (b, j, 0)),
                  pl.BlockSpec((1, S, D_LAT), per_b),
                  pl.BlockSpec((1, S, MISC_W), per_b),
                  pl.BlockSpec((IDX_HEADS * LANES, D_QLAT), const2),
                  pl.BlockSpec((N_HEADS, D_LAT, D_QLAT), const3),
                  pl.BlockSpec((N_HEADS, D_VHEAD, D_LAT), const3)],
        out_specs=pl.BlockSpec((1, QB, W_ATTN), lambda b, j: (b, j, 0)),
        out_shape=jax.ShapeDtypeStruct((B, S, W_ATTN), _MXU_DTYPE),
        scratch_shapes=[pltpu.VMEM((S, 2 * D_LAT), _MXU_DTYPE),
                        pltpu.VMEM((nkt, D_LAT, QB), _MXU_DTYPE),
                        pltpu.VMEM((IDX_HEADS * LANES, QB), _MXU_DTYPE),
                        pltpu.VMEM((N_HEADS, 2 * D_LAT, QB), _MXU_DTYPE),
                        pltpu.VMEM((S, QB), F32),
                        pltpu.VMEM((N_HEADS, QB, QB), F32),
                        pltpu.VMEM((N_HEADS, 1, QB), F32),
                        pltpu.VMEM((N_HEADS, 1, QB), F32),
                        pltpu.VMEM((N_HEADS, D_LAT, QB), F32),
                        pltpu.VMEM((W_ATTN, QB), F32)],
        compiler_params=_params("arbitrary", "arbitrary"),
        name="dsa_attention",
    )(slopes, cqn, ckvn, misc, wqi, wuq, wuv)


def _ssm_kernel(u_ref, bd_ref, lr_ref, li_ref, cd_ref, d_ref, wg_ref, bg_ref, o_ref, x_sc, bu_sc, *, nb, cw):
    tc = u_ref.shape[0] // nb

    @pl.when(pl.program_id(0) == 0)
    def _():
        x_sc[...] = jnp.zeros_like(x_sc)

    u = u_ref[...]
    bu_sc[...] = _mm(u, bd_ref[...])
    for c in range(N_STATE // cw):
        re = slice(c * cw, (c + 1) * cw)
        im = slice(N_STATE + c * cw, N_STATE + (c + 1) * cw)
        lr = lr_ref[:, re]
        li = li_ref[:, re]

        def step(t, carry, re=re, im=im, lr=lr, li=li):
            xr, xi = carry
            rows = pl.ds(pl.multiple_of(t * nb, nb), nb)
            nr = lr * xr - li * xi + bu_sc[rows, re]
            ni = lr * xi + li * xr + bu_sc[rows, im]
            bu_sc[rows, re] = nr
            bu_sc[rows, im] = ni
            return nr, ni

        xr, xi = lax.fori_loop(0, tc, step, (x_sc[:, re], x_sc[:, im]))
        x_sc[:, re] = xr
        x_sc[:, im] = xi
    y = _mm(bu_sc[...], cd_ref[...]) + d_ref[...] * u
    z = _gelu(y)
    o_ref[...] = z * _sigmoid(_mm(z, wg_ref[...]) + bg_ref[...])


def _ssm_call(u_tm, bd, lr, li, cd, dskip, w_glu, b_glu, nb, tc):
    rows = u_tm.shape[0]
    const2 = lambda i: (0, 0)
    return pl.pallas_call(
        functools.partial(_ssm_kernel, nb=nb, cw=256),
        grid=(rows // (tc * nb),),
        in_specs=[pl.BlockSpec((tc * nb, W_SSM), lambda i: (i, 0)),
                  pl.BlockSpec((W_SSM, 2 * N_STATE), const2),
                  pl.BlockSpec((1, N_STATE), const2),
                  pl.BlockSpec((1, N_STATE), const2),
                  pl.BlockSpec((2 * N_STATE, W_SSM), const2),
                  pl.BlockSpec((1, W_SSM), const2),
                  pl.BlockSpec((W_SSM, W_SSM), const2),
                  pl.BlockSpec((1, W_SSM), const2)],
        out_specs=pl.BlockSpec((tc * nb, W_SSM), lambda i: (i, 0)),
        out_shape=jax.ShapeDtypeStruct((rows, W_SSM), F32),
        scratch_shapes=[pltpu.VMEM((nb, 2 * N_STATE), F32),
                        pltpu.VMEM((tc * nb, 2 * N_STATE), F32)],
        compiler_params=_params("arbitrary"),
        name="s5_ssm",
    )(u_tm, bd, lr, li, cd, dskip, w_glu, b_glu)


def _ssm_weights(a_re, a_im, b_re, b_im, c_re, c_im, log_step):
    step = jnp.exp(log_step)[:, None]
    er = jnp.exp(a_re * step)
    ang = a_im * step
    lr, li = er * jnp.cos(ang), er * jnp.sin(ang)
    den = a_re * a_re + a_im * a_im
    fr = ((lr - 1.0) * a_re + li * a_im) / den
    fi = (li * a_re - (lr - 1.0) * a_im) / den
    br = fr[:, :, None] * b_re - fi[:, :, None] * b_im
    bi = fr[:, :, None] * b_im + fi[:, :, None] * b_re
    eye = jnp.eye(SSM_GROUPS, dtype=F32)
    pack_b = lambda m: jnp.einsum('gnp,gh->gphn', m, eye).reshape(W_SSM, N_STATE)
    pack_c = lambda m: jnp.einsum('gpn,gh->gnhp', m, eye).reshape(N_STATE, W_SSM)
    bd = jnp.concatenate([pack_b(br), pack_b(bi)], axis=1)
    cd = jnp.concatenate([pack_c(c_re), pack_c(-c_im)], axis=0)
    return (bd.astype(_MXU_DTYPE), lr.reshape(1, N_STATE), li.reshape(1, N_STATE), cd.astype(_MXU_DTYPE))


def _merge_kernel(x_ref, gpre_ref, sc_ref, sh_ref, gt_ref, gpost_ref, wg_ref, ya_ref, ys_ref,
                  up_ref, uh_ref, wp_ref, ps_ref, pa_ref, pb_ref, pc_ref, wo_ref, o_ref, halo_sc):
    i = pl.program_id(1)
    ts = x_ref.shape[1]
    x = x_ref[0]
    h = _rms(x, gpre_ref[...]) * (1.0 + sc_ref[0]) + sh_ref[0]

    u = up_ref[0]
    halo_sc[:POOL_HALO, :] = jnp.where(i > 0, uh_ref[0], 0.0)
    halo_sc[POOL_HALO:, :] = u
    lane = lax.broadcasted_iota(jnp.int32, (ts, W_POOL), 1)
    tpos = (i * ts + lax.broadcasted_iota(jnp.int32, (ts, W_POOL), 0) + 1).astype(F32)
    run = u
    pooled = jnp.zeros_like(u)
    prev = 1
    for g, win in enumerate(POOL_WINDOWS):
        for k in range(prev, win):
            run = run + halo_sc[POOL_HALO - k:POOL_HALO - k + ts, :]
        prev = win
        in_group = (lane >= g * POOL_GROUP_DIM) & (lane < (g + 1) * POOL_GROUP_DIM)
        pooled = jnp.where(in_group, run / jnp.minimum(tpos, float(win)), pooled)
    y_pool = _mm(pooled - u, wp_ref[...]) * ps_ref[...]

    d = x.shape[-1]
    merged = (_sigmoid(_mm(h, wg_ref[:, :d])) * _mm(ya_ref[0], pa_ref[...])
              + _sigmoid(_mm(h, wg_ref[:, d:2 * d])) * _mm(ys_ref[...], pb_ref[...])
              + _sigmoid(_mm(h, wg_ref[:, 2 * d:])) * _mm(y_pool, pc_ref[...]))
    y = _mm(merged, wo_ref[...])
    o_ref[0] = x + gt_ref[0] * _rms(y, gpost_ref[...])


def _merge_call(x, g_pre, sc, sh, gt, g_post, w_gate, y_attn, y_ssm_tm, u_pool, w_pool_bd, pool_scale,
                p_a, p_b, p_c, w_out, ts):
    B, S, D = x.shape
    row = lambda b, i: (b, i, 0)
    per_b = lambda b, i: (b, 0, 0)
    const2 = lambda b, i: (0, 0)
    hb = ts // POOL_HALO
    return pl.pallas_call(
        _merge_kernel,
        grid=(B, S // ts),
        in_specs=[pl.BlockSpec((1, ts, D), row),
                  pl.BlockSpec((1, D), const2),
                  pl.BlockSpec((1, 1, D), per_b),
                  pl.BlockSpec((1, 1, D), per_b),
                  pl.BlockSpec((1, 1, D), per_b),
                  pl.BlockSpec((1, D), const2),
                  pl.BlockSpec((D, N_BRANCH * D), const2),
                  pl.BlockSpec((1, ts, W_ATTN), row),
                  pl.BlockSpec((ts, W_SSM), lambda b, i: (i, b)),
                  pl.BlockSpec((1, ts, W_POOL), row),
                  pl.BlockSpec((1, POOL_HALO, W_POOL), lambda b, i: (b, jnp.maximum(i * hb - 1, 0), 0)),
                  pl.BlockSpec((W_POOL, W_POOL), const2),
                  pl.BlockSpec((1, W_POOL), const2),
                  pl.BlockSpec((W_ATTN, D), const2),
                  pl.BlockSpec((W_SSM, D), const2),
                  pl.BlockSpec((W_POOL, D), const2),
                  pl.BlockSpec((D, D), const2)],
        out_specs=pl.BlockSpec((1, ts, D), row),
        out_shape=jax.ShapeDtypeStruct((B, S, D), F32),
        scratch_shapes=[pltpu.VMEM((POOL_HALO + ts, W_POOL), F32)],
        compiler_params=_params("arbitrary", "arbitrary"),
        name="mixer_merge",
    )(x, g_pre, sc, sh, gt, g_post, w_gate, y_attn, y_ssm_tm, u_pool, u_pool, w_pool_bd, pool_scale,
      p_a, p_b, p_c, w_out)


def _ffn_kernel(x_ref, xh_ref, gpre_ref, sc_ref, sh_ref, gt_ref, gpost_ref, wup_ref, cw_ref, cb_ref,
                wdn_ref, o_ref, h_sc, u_sc, a_sc, *, cols, down_cols, ahead):
    i = pl.program_id(1)
    ts = x_ref.shape[1]
    dff = wdn_ref.shape[0]
    x = x_ref[0]
    adaln = lambda v: _rms(v, gpre_ref[...]) * (1.0 + sc_ref[0]) + sh_ref[0]
    h_sc[:CONV_HALO, :] = adaln(xh_ref[0]).astype(h_sc.dtype)
    h_sc[CONV_HALO:, :] = adaln(x).astype(h_sc.dtype)
    keep = jnp.where(i > 0, 1.0, 0.0)
    nchunk = dff // cols
    per_down = down_cols // cols
    col = lambda c, half: slice(half * dff + c * cols, half * dff + (c + 1) * cols)

    def up_project(c):
        for half in range(2):
            up = jnp.dot(h_sc[...], wup_ref[:, col(c, half)], preferred_element_type=F32)
            u_sc[half, :CONV_HALO, col(c, 0)] = up[:CONV_HALO] * keep
            u_sc[half, CONV_HALO:, col(c, 0)] = up[CONV_HALO:]

    def conv(c, half):
        acc = cb_ref[:, col(c, half)]
        for k in range(CONV_WIDTH):
            r0 = CONV_HALO - (CONV_WIDTH - 1) + k
            acc = acc + cw_ref[k:k + 1, col(c, half)] * u_sc[half, r0:r0 + ts, col(c, 0)]
        return acc

    y = jnp.zeros((ts, x.shape[-1]), F32)
    for c in range(min(ahead, nchunk)):
        up_project(c)
    for c in range(nchunk):
        if c + ahead < nchunk:
            up_project(c + ahead)
        a_sc[:, c * cols:(c + 1) * cols] = (_gelu(conv(c, 0)) * conv(c, 1)).astype(a_sc.dtype)
        if (c + 1) % per_down == 0:
            rows = slice((c + 1) * cols - down_cols, (c + 1) * cols)
            y = y + jnp.dot(a_sc[:, rows], wdn_ref[rows, :], preferred_element_type=F32)
    o_ref[0] = x + gt_ref[0] * _rms(y, gpost_ref[...])


def _ffn_call(x, g_pre, sc, sh, gt, g_post, w_up, conv_w, conv_b, w_down, ts, cols=256, down_cols=512, ahead=3):
    B, S, D = x.shape
    dff = w_down.shape[0]
    row = lambda b, i: (b, i, 0)
    per_b = lambda b, i: (b, 0, 0)
    const2 = lambda b, i: (0, 0)
    hb = ts // CONV_HALO
    return pl.pallas_call(
        functools.partial(_ffn_kernel, cols=cols, down_cols=down_cols, ahead=ahead),
        grid=(B, S // ts),
        in_specs=[pl.BlockSpec((1, ts, D), row),
                  pl.BlockSpec((1, CONV_HALO, D), lambda b, i: (b, jnp.maximum(i * hb - 1, 0), 0)),
                  pl.BlockSpec((1, D), const2),
                  pl.BlockSpec((1, 1, D), per_b),
                  pl.BlockSpec((1, 1, D), per_b),
                  pl.BlockSpec((1, 1, D), per_b),
                  pl.BlockSpec((1, D), const2),
                  pl.BlockSpec((D, 2 * dff), const2),
                  pl.BlockSpec((CONV_WIDTH, 2 * dff), const2),
                  pl.BlockSpec((1, 2 * dff), const2),
                  pl.BlockSpec((dff, D), const2)],
        out_specs=pl.BlockSpec((1, ts, D), row),
        out_shape=jax.ShapeDtypeStruct((B, S, D), F32),
        scratch_shapes=[pltpu.VMEM((CONV_HALO + ts, D), _MXU_DTYPE),
                        pltpu.VMEM((2, CONV_HALO + ts, dff), F32),
                        pltpu.VMEM((ts, dff), _MXU_DTYPE)],
        compiler_params=_params("arbitrary", "arbitrary"),
        name="conv_gated_ffn",
    )(x, x, g_pre, sc, sh, gt, g_post, w_up, conv_w, conv_b, w_down)


def _pick(n, pref):
    t = min(n, pref)
    assert n % t == 0, (n, t)
    return t


def kernel(x, c, mod_w, mod_b, mix_pre_g, mix_post_g, ffn_pre_g, ffn_post_g, w_in, g_cq, w_uq, w_qi, g_ckv, w_uv, a_re, a_im, b_re, b_im, c_re, c_im, d_skip, log_step, w_glu, b_glu, w_pool, pool_scale, p_a, p_b, p_c, w_out, w_up, conv_w, conv_b, w_down):
    B, S, D = x.shape
    depth = mod_w.shape[0]
    assert S % QB == 0 and QB % CHUNK == 0
    topk = min(DSA_TOPK_MAX, S // 4)
    ts = _pick(S, 512)
    ts_tail = _pick(S, 256)
    tc = _pick(S, 32)
    cast = lambda w: w.astype(_MXU_DTYPE)
    row = lambda v: v.reshape(1, -1)

    mod = _mod_call(c, mod_w, mod_b)
    cuts = [0]
    for wdt in IN_SPLITS:
        cuts.append(cuts[-1] + wdt)
    eye_p = jnp.eye(POOL_GROUPS, dtype=F32)
    for l in range(depth):
        sh_m, sc_m, gt_m, sh_f, sc_f, gt_f = [mod[l][:, None, k * D:(k + 1) * D] for k in range(6)]
        wl = w_in[l]
        pad = jnp.zeros((D, MISC_W - IDX_DIM - IDX_HEADS), F32)
        w_small = cast(jnp.concatenate(
            [wl[:, cuts[0]:cuts[2]], wl[:, cuts[2]:cuts[4]], pad, wl[:, cuts[4]:cuts[6]]], axis=1))
        w_gate = cast(wl[:, cuts[6]:])
        cqn, ckvn, misc, u_ssm, u_pool = _inproj_call(
            x, row(mix_pre_g[l]), sc_m, sh_m, w_small, row(g_cq[l]), row(g_ckv[l]), ts)

        wqi = jnp.transpose(w_qi[l], (1, 2, 0))
        wqi = jnp.pad(wqi, ((0, 0), (0, LANES - IDX_DIM), (0, 0))).reshape(IDX_HEADS * LANES, D_QLAT)
        y_attn = _attn_call(cqn, ckvn, misc, cast(wqi), cast(jnp.transpose(w_uq[l], (1, 2, 0))),
                            cast(jnp.transpose(w_uv[l], (0, 2, 1))), topk)

        bd, lr, li, cd = _ssm_weights(a_re[l], a_im[l], b_re[l], b_im[l], c_re[l], c_im[l], log_step[l])
        y_ssm = _ssm_call(u_ssm.reshape(S * B, W_SSM), bd, lr, li, cd, row(d_skip[l]), cast(w_glu[l]),
                          row(b_glu[l]), B, tc).reshape(S, B * W_SSM)

        w_pool_bd = cast(jnp.einsum('gcd,gh->gchd', w_pool[l], eye_p).reshape(W_POOL, W_POOL))
        x = _merge_call(x, row(mix_pre_g[l]), sc_m, sh_m, gt_m, row(mix_post_g[l]), w_gate, y_attn, y_ssm,
                        u_pool, w_pool_bd, row(pool_scale[l]), cast(p_a[l]), cast(p_b[l]), cast(p_c[l]),
                        cast(w_out[l]), ts_tail)
        x = _ffn_call(x, row(ffn_pre_g[l]), sc_f, sh_f, gt_f, row(ffn_post_g[l]), cast(w_up[l]), conv_w[l],
                      row(conv_b[l]), cast(w_down[l]), ts_tail)
    return x
```

```python
import functools
import math

import jax
import jax.numpy as jnp
from jax import lax
from jax.experimental import pallas as pl
from jax.experimental.pallas import tpu as pltpu

F32 = jnp.float32
_MXU_DTYPE = jnp.bfloat16

CHUNK = 64
N_HEADS = 8
D_QLAT = 256
D_LAT = 128
D_VHEAD = 64
IDX_HEADS = 8
IDX_DIM = 32
DSA_TOPK_MAX = 256
SSM_GROUPS = 16
SSM_GROUP_DIM = 16
SSM_STATE = 64
W_SSM = SSM_GROUPS * SSM_GROUP_DIM
N_STATE = SSM_GROUPS * SSM_STATE
POOL_WINDOWS = (2, 4, 8, 16)
POOL_GROUPS = 4
POOL_GROUP_DIM = 64
W_POOL = POOL_GROUPS * POOL_GROUP_DIM
W_ATTN = N_HEADS * D_VHEAD
N_BRANCH = 3
CONV_WIDTH = 3
RMS_EPS = 1e-6
NEG_INF = -1e30
ATTN_SCALE = D_LAT ** -0.5
IDX_SCALE = IDX_DIM ** -0.5
IDX_HEAD_SCALE = IDX_HEADS ** -0.5
IN_SPLITS = (D_QLAT, D_LAT, IDX_DIM, IDX_HEADS, W_SSM, W_POOL)

LANES = 128
SUBLANES = 8
MISC_W = LANES
W_SMALL = D_QLAT + D_LAT + MISC_W + W_SSM + W_POOL
VMEM_LIMIT = 56 * 1024 * 1024

QB = 256
BISECT_PASSES = 13
SLOPE_PARTS = 3
LOG2E = math.log2(math.e)
COUNT_ROWS = 4 * SUBLANES
WALK_ROWS = 2 * SUBLANES
POOL_HALO = 16
CONV_HALO = 16


def _mm(a, b):
    return jnp.dot(a.astype(_MXU_DTYPE), b.astype(_MXU_DTYPE), preferred_element_type=F32)


def _rms(x, g):
    return x * lax.rsqrt(jnp.mean(x * x, axis=-1, keepdims=True) + RMS_EPS) * g


def _gelu(x):
    return 0.5 * x * (1.0 + jnp.tanh(math.sqrt(2.0 / math.pi) * (x + 0.044715 * (x * x * x))))


def _sigmoid(x):
    return 1.0 / (1.0 + jnp.exp(-x))


def _params(*sem):
    return pltpu.CompilerParams(dimension_semantics=sem, vmem_limit_bytes=VMEM_LIMIT)


def _mod_kernel(c_ref, w_ref, b_ref, o_ref):
    c = c_ref[...]
    cond = c * _sigmoid(c)
    o_ref[0] = _mm(cond, w_ref[0]) + b_ref[0]


def _mod_call(c, mod_w, mod_b):
    L, D, D6 = mod_w.shape
    B = c.shape[0]
    nt = D6 // D
    return pl.pallas_call(
        _mod_kernel,
        grid=(L, nt),
        in_specs=[pl.BlockSpec((B, D), lambda l, n: (0, 0)),
                  pl.BlockSpec((1, D, D), lambda l, n: (l, 0, n)),
                  pl.BlockSpec((1, 1, D), lambda l, n: (l, 0, n))],
        out_specs=pl.BlockSpec((1, B, D), lambda l, n: (l, 0, n)),
        out_shape=jax.ShapeDtypeStruct((L, B, D6), F32),
        compiler_params=_params("arbitrary", "arbitrary"),
        name="adaln_mod",
    )(c, mod_w.astype(_MXU_DTYPE), mod_b.reshape(L, 1, D6))


def _inproj_kernel(x_ref, g_ref, sc_ref, sh_ref, w_ref, gcq_ref, gckv_ref,
                   cq_o, ckv_o, misc_o, ussm_o, upool_o):
    h = _rms(x_ref[0], g_ref[...]) * (1.0 + sc_ref[0]) + sh_ref[0]
    z = _mm(h, w_ref[...])
    o0, o1, o2, o3 = D_QLAT, D_QLAT + D_LAT, D_QLAT + D_LAT + MISC_W, D_QLAT + D_LAT + MISC_W + W_SSM
    cq_o[0] = _rms(z[:, :o0], gcq_ref[...]).astype(cq_o.dtype)
    ckv_o[0] = _rms(z[:, o0:o1], gckv_ref[...]).astype(ckv_o.dtype)
    misc_o[0] = z[:, o1:o2]
    ussm_o[...] = z[:, o2:o3]
    upool_o[0] = z[:, o3:]


def _inproj_call(x, g_pre, sc, sh, w_small, g_cq, g_ckv, ts):
    B, S, D = x.shape
    row = lambda b, i: (b, i, 0)
    per_b = lambda b, i: (b, 0, 0)
    const2 = lambda b, i: (0, 0)
    return pl.pallas_call(
        _inproj_kernel,
        grid=(B, S // ts),
        in_specs=[pl.BlockSpec((1, ts, D), row),
                  pl.BlockSpec((1, D), const2),
                  pl.BlockSpec((1, 1, D), per_b),
                  pl.BlockSpec((1, 1, D), per_b),
                  pl.BlockSpec((D, W_SMALL), const2),
                  pl.BlockSpec((1, D_QLAT), const2),
                  pl.BlockSpec((1, D_LAT), const2)],
        out_specs=[pl.BlockSpec((1, ts, D_QLAT), row),
                   pl.BlockSpec((1, ts, D_LAT), row),
                   pl.BlockSpec((1, ts, MISC_W), row),
                   pl.BlockSpec((ts, W_SSM), lambda b, i: (i, b)),
                   pl.BlockSpec((1, ts, W_POOL), row)],
        out_shape=[jax.ShapeDtypeStruct((B, S, D_QLAT), _MXU_DTYPE),
                   jax.ShapeDtypeStruct((B, S, D_LAT), _MXU_DTYPE),
                   jax.ShapeDtypeStruct((B, S, MISC_W), F32),
                   jax.ShapeDtypeStruct((S, B * W_SSM), F32),
                   jax.ShapeDtypeStruct((B, S, W_POOL), F32)],
        compiler_params=_params("arbitrary", "arbitrary"),
        name="in_projection",
    )(x, g_pre, sc, sh, w_small, g_cq, g_ckv)


def _sortable(i):
    return i ^ ((i >> 31) & jnp.int32(0x7FFFFFFF))


def _attn_kernel(slope_ref, cq_ref, ckv_ref, misc_ref, wqi_ref, wuq_ref, wuv_ref, o_ref,
                 kaug_sc, ckvt_sc, qi_sc, qt_sc, score_sc, lg_sc, m_sc, l_sc, acc_sc, yt_sc, *, topk):
    j = pl.program_id(1)
    nkt = ckvt_sc.shape[0]
    q0 = pl.multiple_of(j * QB, QB)

    @pl.when((pl.program_id(0) == 0) & (j == 0))
    def _():
        lane = lax.broadcasted_iota(jnp.int32, (QB, D_LAT), 1)
        s_in = lax.broadcasted_iota(jnp.int32, (QB, D_LAT), 0).astype(F32)
        for kt in range(nkt):
            pos = jnp.where(lane < SLOPE_PARTS, s_in, jnp.where(lane < 2 * SLOPE_PARTS, float(kt), 0.0))
            kaug_sc[kt * QB:(kt + 1) * QB, D_LAT:] = pos.astype(kaug_sc.dtype)
        sub = lax.broadcasted_iota(jnp.int32, (D_LAT, QB), 0)
        for h in range(N_HEADS):
            rows = jnp.zeros((D_LAT, QB), F32)
            for part in range(SLOPE_PARTS):
                rows = jnp.where(sub == part, slope_ref[h, part], rows)
                rows = jnp.where(sub == SLOPE_PARTS + part, slope_ref[h, part] * QB, rows)
            qt_sc[h, D_LAT:, :] = rows.astype(qt_sc.dtype)

    @pl.when(j == 0)
    def _():
        for kt in range(nkt):
            kv = ckv_ref[0, kt * QB:(kt + 1) * QB, :]
            ckvt_sc[kt] = kv.astype(F32).T.astype(ckvt_sc.dtype)
            kaug_sc[kt * QB:(kt + 1) * QB, :D_LAT] = kv

    cqt = cq_ref[0].astype(F32).T.astype(_MXU_DTYPE)
    misct = misc_ref[0, pl.ds(q0, QB), :].T
    qi_sc[...] = _mm(wqi_ref[...], cqt).astype(qi_sc.dtype)
    for h in range(N_HEADS):
        qt_sc[h, :D_LAT, :] = (_mm(wuq_ref[h], cqt) * (ATTN_SCALE * LOG2E)).astype(qt_sc.dtype)

    kl = lax.broadcasted_iota(jnp.int32, (QB, QB), 0)
    ql = lax.broadcasted_iota(jnp.int32, (QB, QB), 1)
    diag_ok = kl < (ql // CHUNK + 1) * CHUNK
    ahead = 2.0 * jnp.maximum(kl - ql, 0).astype(F32)

    def score_tile(kt):
        k0 = pl.multiple_of(kt * QB, QB)
        kmat = misc_ref[0, pl.ds(k0, QB), :].astype(_MXU_DTYPE)
        acc = jnp.zeros((QB, QB), F32)
        for h in range(IDX_HEADS):
            lg = jnp.dot(kmat, qi_sc[h * LANES:(h + 1) * LANES, :], preferred_element_type=F32)
            wq = misct[IDX_DIM + h:IDX_DIM + h + 1, :] * (IDX_SCALE * IDX_HEAD_SCALE)
            acc = acc + jnp.maximum(lg, 0.0) * wq
        return acc

    def score_body(kt, carry):
        cmin, cmax = carry
        s = score_tile(kt)
        score_sc[pl.ds(pl.multiple_of(kt * QB, QB), QB), :] = s
        return (jnp.minimum(cmin, jnp.min(s, axis=0, keepdims=True)),
                jnp.maximum(cmax, jnp.max(s, axis=0, keepdims=True)))

    big = jnp.full((1, QB), 3.0e38, F32)
    cmin, cmax = lax.fori_loop(0, j, score_body, (big, -big))
    s = score_tile(j)
    cmin = jnp.minimum(cmin, jnp.min(jnp.where(diag_ok, s, 3.0e38), axis=0, keepdims=True))
    cmax = jnp.maximum(cmax, jnp.max(jnp.where(diag_ok, s, -3.0e38), axis=0, keepdims=True))
    score_sc[pl.ds(q0, QB), :] = jnp.where(diag_ok, s, -jnp.inf)

    def rows_reduce(x, op, rows):
        return op(x.reshape(QB // rows, rows, QB), axis=0)

    def count(pred, thr):
        def body(kt, acc):
            t = score_sc[pl.ds(pl.multiple_of(kt * QB, QB), QB), :]
            return acc + rows_reduce(jnp.where(pred(t, thr), 1.0, 0.0), jnp.sum, COUNT_ROWS)
        acc = lax.fori_loop(0, j + 1, body, jnp.zeros((COUNT_ROWS, QB), F32))
        return jnp.sum(acc, axis=0, keepdims=True)

    def count_and_next(v):
        def body(kt, carry):
            acc, below = carry
            t = score_sc[pl.ds(pl.multiple_of(kt * QB, QB), QB), :]
            hit = t >= v
            return (acc + rows_reduce(jnp.where(hit, 1.0, 0.0), jnp.sum, WALK_ROWS),
                    jnp.maximum(below, rows_reduce(jnp.where(hit, -jnp.inf, t), jnp.max, WALK_ROWS)))
        init = (jnp.zeros((WALK_ROWS, QB), F32), jnp.full((WALK_ROWS, QB), -jnp.inf, F32))
        acc, below = lax.fori_loop(0, j + 1, body, init)
        return jnp.sum(acc, axis=0, keepdims=True), jnp.max(below, axis=0, keepdims=True)

    ge = lambda t, thr: t >= thr
    gt = lambda t, thr: t > thr

    @pl.when((j + 1) * QB > topk)
    def _():
        kf = jnp.float32(topk)

        def any_lane(flag):
            f = jnp.where(flag, 1.0, 0.0)
            parts = [f[:, k * LANES:(k + 1) * LANES] for k in range(QB // LANES)]
            return jnp.max(functools.reduce(jnp.maximum, parts)) > 0.0

        above_max = pltpu.bitcast(_sortable(_sortable(pltpu.bitcast(cmax, jnp.int32)) + 1), F32)

        def halve(_, c):
            lo, hi = c
            mid = 0.5 * lo + 0.5 * hi
            keep_low = count(ge, mid) >= kf
            return jnp.where(keep_low, mid, lo), jnp.where(keep_low, hi, mid)

        _, hi = lax.fori_loop(0, BISECT_PASSES, halve, (cmin, above_max))

        def unresolved(c):
            return any_lane(c[1] == 0)

        def step(c):
            v, done, thr, cnt_ge = c
            cnt, below = count_and_next(v)
            hit = (done == 0) & ((cnt >= kf) | (v <= cmin))
            thr = jnp.where(hit, v, thr)
            cnt_ge = jnp.where(hit, cnt, cnt_ge)
            done = jnp.where(hit, 1, done)
            return jnp.where(done > 0, v, jnp.maximum(below, cmin)), done, thr, cnt_ge

        zero = jnp.zeros((1, QB), F32)
        _, _, thr, cnt_ge = lax.while_loop(unresolved, step, (hi, jnp.zeros((1, QB), jnp.int32), zero, zero))
        ties = any_lane(cnt_ge > kf)

        @pl.when(jnp.logical_not(ties))
        def _():
            def body(kt, _):
                r = pl.ds(pl.multiple_of(kt * QB, QB), QB)
                score_sc[r, :] = jnp.where(score_sc[r, :] >= thr, 0.0, NEG_INF)
                return 0
            lax.fori_loop(0, j + 1, body, 0)

        @pl.when(ties)
        def _():
            need = kf - count(gt, thr)
            tri = (lax.broadcasted_iota(jnp.int32, (QB, QB), 0)
                   >= lax.broadcasted_iota(jnp.int32, (QB, QB), 1)).astype(_MXU_DTYPE)

            def body(kt, seen):
                r = pl.ds(pl.multiple_of(kt * QB, QB), QB)
                t = score_sc[r, :]
                eq = jnp.where(t == thr, 1.0, 0.0)
                rank = seen + jnp.dot(tri, eq.astype(_MXU_DTYPE), preferred_element_type=F32)
                sel = (t > thr) | ((t == thr) & (rank <= need))
                score_sc[r, :] = jnp.where(sel, 0.0, NEG_INF)
                return seen + jnp.sum(eq, axis=0, keepdims=True)
            lax.fori_loop(0, j + 1, body, jnp.zeros((1, QB), F32))

    @pl.when((j + 1) * QB <= topk)
    def _():
        def body(kt, _):
            r = pl.ds(pl.multiple_of(kt * QB, QB), QB)
            score_sc[r, :] = jnp.where(score_sc[r, :] > -jnp.inf, 0.0, NEG_INF)
            return 0
        lax.fori_loop(0, j + 1, body, 0)

    m_sc[...] = jnp.full(m_sc.shape, -3.0e38, F32)
    l_sc[...] = jnp.zeros(l_sc.shape, F32)
    acc_sc[...] = jnp.zeros(acc_sc.shape, F32)

    def attend(kt, ahead_of_query):
        k0 = pl.multiple_of(kt * QB, QB)
        keys = kaug_sc[pl.ds(k0, QB), :]
        vals_t = ckvt_sc[kt]
        mask = score_sc[pl.ds(k0, QB), :]
        tile_max = []
        for h in range(N_HEADS):
            lg = jnp.dot(keys, qt_sc[h], preferred_element_type=F32) + mask
            if ahead_of_query is not None:
                lg = lg - slope_ref[h, SLOPE_PARTS] * ahead_of_query
            lg_sc[h] = lg
            tile_max.append(jnp.max(lg, axis=0, keepdims=True))
        for h in range(N_HEADS):
            m = m_sc[h]
            m_new = jnp.maximum(m, tile_max[h])
            alpha = jnp.exp2(m - m_new)
            m_sc[h] = m_new
            p = jnp.exp2(lg_sc[h] - m_new)
            l_sc[h] = alpha * l_sc[h] + jnp.sum(p, axis=0, keepdims=True)
            acc_sc[h] = alpha * acc_sc[h] + jnp.dot(vals_t, p.astype(_MXU_DTYPE), preferred_element_type=F32)

    def off_diag(kt, _):
        attend(kt, None)
        return 0

    lax.fori_loop(0, j, off_diag, 0)
    attend(j, ahead)
    for h in range(N_HEADS):
        yt_sc[h * D_VHEAD:(h + 1) * D_VHEAD, :] = _mm(wuv_ref[h], acc_sc[h] / l_sc[h])
    o_ref[0] = yt_sc[...].T.astype(o_ref.dtype)


def _attn_call(cqn, ckvn, misc, wqi, wuq, wuv, topk):
    B, S, _ = cqn.shape
    nkt = S // QB
    slope = jnp.exp2(-8.0 * jnp.arange(1, N_HEADS + 1, dtype=F32) / N_HEADS) * LOG2E
    parts, rest = [], slope
    for _ in range(SLOPE_PARTS):
        piece = rest.astype(_MXU_DTYPE).astype(F32)
        parts.append(piece)
        rest = rest - piece
    slopes = jnp.stack(parts + [slope], axis=1)
    per_b = lambda b, j: (b, 0, 0)
    const2 = lambda b, j: (0, 0)
    const3 = lambda b, j: (0, 0, 0)
    return pl.pallas_call(
        functools.partial(_attn_kernel, topk=topk),
        grid=(B, nkt),
        in_specs=[pl.BlockSpec(memory_space=pltpu.SMEM),
                  pl.BlockSpec((1, QB, D_QLAT), lambda b, j: (b, j, 0)),
                  pl.BlockSpec((1, S, D_LAT), per_b),
                  pl.BlockSpec((1, S, MISC_W), per_b),
                  pl.BlockSpec((IDX_HEADS * LANES, D_QLAT), const2),
                  pl.BlockSpec((N_HEADS, D_LAT, D_QLAT), const3),
                  pl.BlockSpec((N_HEADS, D_VHEAD, D_LAT), const3)],
        out_specs=pl.BlockSpec((1, QB, W_ATTN), lambda b, j: (b, j, 0)),
        out_shape=jax.ShapeDtypeStruct((B, S, W_ATTN), _MXU_DTYPE),
        scratch_shapes=[pltpu.VMEM((S, 2 * D_LAT), _MXU_DTYPE),
                        pltpu.VMEM((nkt, D_LAT, QB), _MXU_DTYPE),
                        pltpu.VMEM((IDX_HEADS * LANES, QB), _MXU_DTYPE),
                        pltpu.VMEM((N_HEADS, 2 * D_LAT, QB), _MXU_DTYPE),
                        pltpu.VMEM((S, QB), F32),
                        pltpu.VMEM((N_HEADS, QB, QB), F32),
                        pltpu.VMEM((N_HEADS, 1, QB), F32),
                        pltpu.VMEM((N_HEADS, 1, QB), F32),
                        pltpu.VMEM((N_HEADS, D_LAT, QB), F32),
                        pltpu.VMEM((W_ATTN, QB), F32)],
        compiler_params=_params("arbitrary", "arbitrary"),
        name="dsa_attention",
    )(slopes, cqn, ckvn, misc, wqi, wuq, wuv)


def _ssm_kernel(u_ref, bd_ref, lr_ref, li_ref, cd_ref, d_ref, wg_ref, bg_ref, o_ref, x_sc, bu_sc, *, nb, cw):
    tc = u_ref.shape[0] // nb

    @pl.when(pl.program_id(0) == 0)
    def _():
        x_sc[...] = jnp.zeros_like(x_sc)

    u = u_ref[...]
    bu_sc[...] = _mm(u, bd_ref[...])
    for c in range(N_STATE // cw):
        re = slice(c * cw, (c + 1) * cw)
        im = slice(N_STATE + c * cw, N_STATE + (c + 1) * cw)
        lr = lr_ref[:, re]
        li = li_ref[:, re]

        def step(t, carry, re=re, im=im, lr=lr, li=li):
            xr, xi = carry
            rows = pl.ds(pl.multiple_of(t * nb, nb), nb)
            nr = lr * xr - li * xi + bu_sc[rows, re]
            ni = lr * xi + li * xr + bu_sc[rows, im]
            bu_sc[rows, re] = nr
            bu_sc[rows, im] = ni
            return nr, ni

        xr, xi = lax.fori_loop(0, tc, step, (x_sc[:, re], x_sc[:, im]))
        x_sc[:, re] = xr
        x_sc[:, im] = xi
    y = _mm(bu_sc[...], cd_ref[...]) + d_ref[...] * u
    z = _gelu(y)
    o_ref[...] = z * _sigmoid(_mm(z, wg_ref[...]) + bg_ref[...])


def _ssm_call(u_tm, bd, lr, li, cd, dskip, w_glu, b_glu, nb, tc):
    rows = u_tm.shape[0]
    const2 = lambda i: (0, 0)
    return pl.pallas_call(
        functools.partial(_ssm_kernel, nb=nb, cw=256),
        grid=(rows // (tc * nb),),
        in_specs=[pl.BlockSpec((tc * nb, W_SSM), lambda i: (i, 0)),
                  pl.BlockSpec((W_SSM, 2 * N_STATE), const2),
                  pl.BlockSpec((1, N_STATE), const2),
                  pl.BlockSpec((1, N_STATE), const2),
                  pl.BlockSpec((2 * N_STATE, W_SSM), const2),
                  pl.BlockSpec((1, W_SSM), const2),
                  pl.BlockSpec((W_SSM, W_SSM), const2),
                  pl.BlockSpec((1, W_SSM), const2)],
        out_specs=pl.BlockSpec((tc * nb, W_SSM), lambda i: (i, 0)),
        out_shape=jax.ShapeDtypeStruct((rows, W_SSM), F32),
        scratch_shapes=[pltpu.VMEM((nb, 2 * N_STATE), F32),
                        pltpu.VMEM((tc * nb, 2 * N_STATE), F32)],
        compiler_params=_params("arbitrary"),
        name="s5_ssm",
    )(u_tm, bd, lr, li, cd, dskip, w_glu, b_glu)


def _ssm_weights(a_re, a_im, b_re, b_im, c_re, c_im, log_step):
    step = jnp.exp(log_step)[:, None]
    er = jnp.exp(a_re * step)
    ang = a_im * step
    lr, li = er * jnp.cos(ang), er * jnp.sin(ang)
    den = a_re * a_re + a_im * a_im
    fr = ((lr - 1.0) * a_re + li * a_im) / den
    fi = (li * a_re - (lr - 1.0) * a_im) / den
    br = fr[:, :, None] * b_re - fi[:, :, None] * b_im
    bi = fr[:, :, None] * b_im + fi[:, :, None] * b_re
    eye = jnp.eye(SSM_GROUPS, dtype=F32)
    pack_b = lambda m: jnp.einsum('gnp,gh->gphn', m, eye).reshape(W_SSM, N_STATE)
    pack_c = lambda m: jnp.einsum('gpn,gh->gnhp', m, eye).reshape(N_STATE, W_SSM)
    bd = jnp.concatenate([pack_b(br), pack_b(bi)], axis=1)
    cd = jnp.concatenate([pack_c(c_re), pack_c(-c_im)], axis=0)
    return (bd.astype(_MXU_DTYPE), lr.reshape(1, N_STATE), li.reshape(1, N_STATE), cd.astype(_MXU_DTYPE))


def _merge_kernel(x_ref, gpre_ref, sc_ref, sh_ref, gt_ref, gpost_ref, wg_ref, ya_ref, ys_ref,
                  up_ref, uh_ref, wp_ref, ps_ref, pa_ref, pb_ref, pc_ref, wo_ref, o_ref, halo_sc):
    i = pl.program_id(1)
    ts = x_ref.shape[1]
    x = x_ref[0]
    h = _rms(x, gpre_ref[...]) * (1.0 + sc_ref[0]) + sh_ref[0]

    u = up_ref[0]
    halo_sc[:POOL_HALO, :] = jnp.where(i > 0, uh_ref[0], 0.0)
    halo_sc[POOL_HALO:, :] = u
    lane = lax.broadcasted_iota(jnp.int32, (ts, W_POOL), 1)
    tpos = (i * ts + lax.broadcasted_iota(jnp.int32, (ts, W_POOL), 0) + 1).astype(F32)
    run = u
    pooled = jnp.zeros_like(u)
    prev = 1
    for g, win in enumerate(POOL_WINDOWS):
        for k in range(prev, win):
            run = run + halo_sc[POOL_HALO - k:POOL_HALO - k + ts, :]
        prev = win
        in_group = (lane >= g * POOL_GROUP_DIM) & (lane < (g + 1) * POOL_GROUP_DIM)
        pooled = jnp.where(in_group, run / jnp.minimum(tpos, float(win)), pooled)
    y_pool = _mm(pooled - u, wp_ref[...]) * ps_ref[...]

    d = x.shape[-1]
    merged = (_sigmoid(_mm(h, wg_ref[:, :d])) * _mm(ya_ref[0], pa_ref[...])
              + _sigmoid(_mm(h, wg_ref[:, d:2 * d])) * _mm(ys_ref[...], pb_ref[...])
              + _sigmoid(_mm(h, wg_ref[:, 2 * d:])) * _mm(y_pool, pc_ref[...]))
    y = _mm(merged, wo_ref[...])
    o_ref[0] = x + gt_ref[0] * _rms(y, gpost_ref[...])


def _merge_call(x, g_pre, sc, sh, gt, g_post, w_gate, y_attn, y_ssm_tm, u_pool, w_pool_bd, pool_scale,
                p_a, p_b, p_c, w_out, ts):
    B, S, D = x.shape
    row = lambda b, i: (b, i, 0)
    per_b = lambda b, i: (b, 0, 0)
    const2 = lambda b, i: (0, 0)
    hb = ts // POOL_HALO
    return pl.pallas_call(
        _merge_kernel,
        grid=(B, S // ts),
        in_specs=[pl.BlockSpec((1, ts, D), row),
                  pl.BlockSpec((1, D), const2),
                  pl.BlockSpec((1, 1, D), per_b),
                  pl.BlockSpec((1, 1, D), per_b),
                  pl.BlockSpec((1, 1, D), per_b),
                  pl.BlockSpec((1, D), const2),
                  pl.BlockSpec((D, N_BRANCH * D), const2),
                  pl.BlockSpec((1, ts, W_ATTN), row),
                  pl.BlockSpec((ts, W_SSM), lambda b, i: (i, b)),
                  pl.BlockSpec((1, ts, W_POOL), row),
                  pl.BlockSpec((1, POOL_HALO, W_POOL), lambda b, i: (b, jnp.maximum(i * hb - 1, 0), 0)),
                  pl.BlockSpec((W_POOL, W_POOL), const2),
                  pl.BlockSpec((1, W_POOL), const2),
                  pl.BlockSpec((W_ATTN, D), const2),
                  pl.BlockSpec((W_SSM, D), const2),
                  pl.BlockSpec((W_POOL, D), const2),
                  pl.BlockSpec((D, D), const2)],
        out_specs=pl.BlockSpec((1, ts, D), row),
        out_shape=jax.ShapeDtypeStruct((B, S, D), F32),
        scratch_shapes=[pltpu.VMEM((POOL_HALO + ts, W_POOL), F32)],
        compiler_params=_params("arbitrary", "arbitrary"),
        name="mixer_merge",
    )(x, g_pre, sc, sh, gt, g_post, w_gate, y_attn, y_ssm_tm, u_pool, u_pool, w_pool_bd, pool_scale,
      p_a, p_b, p_c, w_out)


def _ffn_kernel(x_ref, xh_ref, gpre_ref, sc_ref, sh_ref, gt_ref, gpost_ref, wup_ref, cw_ref, cb_ref,
                wdn_ref, o_ref, h_sc, u_sc, a_sc, *, cols, down_cols, ahead):
    i = pl.program_id(1)
    ts = x_ref.shape[1]
    dff = wdn_ref.shape[0]
    x = x_ref[0]
    adaln = lambda v: _rms(v, gpre_ref[...]) * (1.0 + sc_ref[0]) + sh_ref[0]
    h_sc[:CONV_HALO, :] = adaln(xh_ref[0]).astype(h_sc.dtype)
    h_sc[CONV_HALO:, :] = adaln(x).astype(h_sc.dtype)
    keep = jnp.where(i > 0, 1.0, 0.0)
    nchunk = dff // cols
    per_down = down_cols // cols
    col = lambda c, half: slice(half * dff + c * cols, half * dff + (c + 1) * cols)

    def up_project(c):
        for half in range(2):
            up = jnp.dot(h_sc[...], wup_ref[:, col(c, half)], preferred_element_type=F32)
            u_sc[half, :CONV_HALO, col(c, 0)] = up[:CONV_HALO] * keep
            u_sc[half, CONV_HALO:, col(c, 0)] = up[CONV_HALO:]

    def conv(c, half):
        acc = cb_ref[:, col(c, half)]
        for k in range(CONV_WIDTH):
            r0 = CONV_HALO - (CONV_WIDTH - 1) + k
            acc = acc + cw_ref[k:k + 1, col(c, half)] * u_sc[half, r0:r0 + ts, col(c, 0)]
        return acc

    y = jnp.zeros((ts, x.shape[-1]), F32)
    for c in range(min(ahead, nchunk)):
        up_project(c)
    for c in range(nchunk):
        if c + ahead < nchunk:
            up_project(c + ahead)
        a_sc[:, c * cols:(c + 1) * cols] = (_gelu(conv(c, 0)) * conv(c, 1)).astype(a_sc.dtype)
        if (c + 1) % per_down == 0:
            rows = slice((c + 1) * cols - down_cols, (c + 1) * cols)
            y = y + jnp.dot(a_sc[:, rows], wdn_ref[rows, :], preferred_element_type=F32)
    o_ref[0] = x + gt_ref[0] * _rms(y, gpost_ref[...])


def _ffn_call(x, g_pre, sc, sh, gt, g_post, w_up, conv_w, conv_b, w_down, ts, cols=256, down_cols=512, ahead=3):
    B, S, D = x.shape
    dff = w_down.shape[0]
    row = lambda b, i: (b, i, 0)
    per_b = lambda b, i: (b, 0, 0)
    const2 = lambda b, i: (0, 0)
    hb = ts // CONV_HALO
    return pl.pallas_call(
        functools.partial(_ffn_kernel, cols=cols, down_cols=down_cols, ahead=ahead),
        grid=(B, S // ts),
        in_specs=[pl.BlockSpec((1, ts, D), row),
                  pl.BlockSpec((1, CONV_HALO, D), lambda b, i: (b, jnp.maximum(i * hb - 1, 0), 0)),
                  pl.BlockSpec((1, D), const2),
                  pl.BlockSpec((1, 1, D), per_b),
                  pl.BlockSpec((1, 1, D), per_b),
                  pl.BlockSpec((1, 1, D), per_b),
                  pl.BlockSpec((1, D), const2),
                  pl.BlockSpec((D, 2 * dff), const2),
                  pl.BlockSpec((CONV_WIDTH, 2 * dff), const2),
                  pl.BlockSpec((1, 2 * dff), const2),
                  pl.BlockSpec((dff, D), const2)],
        out_specs=pl.BlockSpec((1, ts, D), row),
        out_shape=jax.ShapeDtypeStruct((B, S, D), F32),
        scratch_shapes=[pltpu.VMEM((CONV_HALO + ts, D), _MXU_DTYPE),
                        pltpu.VMEM((2, CONV_HALO + ts, dff), F32),
                        pltpu.VMEM((ts, dff), _MXU_DTYPE)],
        compiler_params=_params("arbitrary", "arbitrary"),
        name="conv_gated_ffn",
    )(x, x, g_pre, sc, sh, gt, g_post, w_up, conv_w, conv_b, w_down)


def _pick(n, pref):
    t = min(n, pref)
    assert n % t == 0, (n, t)
    return t


def kernel(x, c, mod_w, mod_b, mix_pre_g, mix_post_g, ffn_pre_g, ffn_post_g, w_in, g_cq, w_uq, w_qi, g_ckv, w_uv, a_re, a_im, b_re, b_im, c_re, c_im, d_skip, log_step, w_glu, b_glu, w_pool, pool_scale, p_a, p_b, p_c, w_out, w_up, conv_w, conv_b, w_down):
    B, S, D = x.shape
    depth = mod_w.shape[0]
    assert S % QB == 0 and QB % CHUNK == 0
    topk = min(DSA_TOPK_MAX, S // 4)
    ts = _pick(S, 512)
    ts_tail = _pick(S, 256)
    tc = _pick(S, 32)
    cast = lambda w: w.astype(_MXU_DTYPE)
    row = lambda v: v.reshape(1, -1)

    mod = _mod_call(c, mod_w, mod_b)
    cuts = [0]
    for wdt in IN_SPLITS:
        cuts.append(cuts[-1] + wdt)
    eye_p = jnp.eye(POOL_GROUPS, dtype=F32)
    for l in range(depth):
        sh_m, sc_m, gt_m, sh_f, sc_f, gt_f = [mod[l][:, None, k * D:(k + 1) * D] for k in range(6)]
        wl = w_in[l]
        pad = jnp.zeros((D, MISC_W - IDX_DIM - IDX_HEADS), F32)
        w_small = cast(jnp.concatenate(
            [wl[:, cuts[0]:cuts[2]], wl[:, cuts[2]:cuts[4]], pad, wl[:, cuts[4]:cuts[6]]], axis=1))
        w_gate = cast(wl[:, cuts[6]:])
        cqn, ckvn, misc, u_ssm, u_pool = _inproj_call(
            x, row(mix_pre_g[l]), sc_m, sh_m, w_small, row(g_cq[l]), row(g_ckv[l]), ts)

        wqi = jnp.transpose(w_qi[l], (1, 2, 0))
        wqi = jnp.pad(wqi, ((0, 0), (0, LANES - IDX_DIM), (0, 0))).reshape(IDX_HEADS * LANES, D_QLAT)
        y_attn = _attn_call(cqn, ckvn, misc, cast(wqi), cast(jnp.transpose(w_uq[l], (1, 2, 0))),
                            cast(jnp.transpose(w_uv[l], (0, 2, 1))), topk)

        bd, lr, li, cd = _ssm_weights(a_re[l], a_im[l], b_re[l], b_im[l], c_re[l], c_im[l], log_step[l])
        y_ssm = _ssm_call(u_ssm.reshape(S * B, W_SSM), bd, lr, li, cd, row(d_skip[l]), cast(w_glu[l]),
                          row(b_glu[l]), B, tc).reshape(S, B * W_SSM)

        w_pool_bd = cast(jnp.einsum('gcd,gh->gchd', w_pool[l], eye_p).reshape(W_POOL, W_POOL))
        x = _merge_call(x, row(mix_pre_g[l]), sc_m, sh_m, gt_m, row(mix_post_g[l]), w_gate, y_attn, y_ssm,
                        u_pool, w_pool_bd, row(pool_scale[l]), cast(p_a[l]), cast(p_b[l]), cast(p_c[l]),
                        cast(w_out[l]), ts_tail)
        x = _ffn_call(x, row(ffn_pre_g[l]), sc_f, sh_f, gt_f, row(ffn_post_g[l]), cast(w_up[l]), conv_w[l],
                      row(conv_b[l]), cast(w_down[l]), ts_tail)
    return x
```

```python
import functools
import math

import jax
import jax.numpy as jnp
from jax import lax
from jax.experimental import pallas as pl
from jax.experimental.pallas import tpu as pltpu

F32 = jnp.float32
_MXU_DTYPE = jnp.bfloat16

CHUNK = 64
N_HEADS = 8
D_QLAT = 256
D_LAT = 128
D_VHEAD = 64
IDX_HEADS = 8
IDX_DIM = 32
DSA_TOPK_MAX = 256
SSM_GROUPS = 16
SSM_GROUP_DIM = 16
SSM_STATE = 64
W_SSM = SSM_GROUPS * SSM_GROUP_DIM
N_STATE = SSM_GROUPS * SSM_STATE
POOL_WINDOWS = (2, 4, 8, 16)
POOL_GROUPS = 4
POOL_GROUP_DIM = 64
W_POOL = POOL_GROUPS * POOL_GROUP_DIM
W_ATTN = N_HEADS * D_VHEAD
N_BRANCH = 3
CONV_WIDTH = 3
RMS_EPS = 1e-6
NEG_INF = -1e30
ATTN_SCALE = D_LAT ** -0.5
IDX_SCALE = IDX_DIM ** -0.5
IDX_HEAD_SCALE = IDX_HEADS ** -0.5
IN_SPLITS = (D_QLAT, D_LAT, IDX_DIM, IDX_HEADS, W_SSM, W_POOL)

LANES = 128
SUBLANES = 8
MISC_W = LANES
W_SMALL = D_QLAT + D_LAT + MISC_W + W_SSM + W_POOL
VMEM_LIMIT = 56 * 1024 * 1024

QB = 256
BISECT_PASSES = 13
SLOPE_PARTS = 3
DENOM_ROWS = 16
LOG2E = math.log2(math.e)
COUNT_ROWS = 4 * SUBLANES
WALK_ROWS = 2 * SUBLANES
POOL_HALO = 16
CONV_HALO = 16


def _mm(a, b):
    return jnp.dot(a.astype(_MXU_DTYPE), b.astype(_MXU_DTYPE), preferred_element_type=F32)


def _rms(x, g):
    return x * lax.rsqrt(jnp.mean(x * x, axis=-1, keepdims=True) + RMS_EPS) * g


def _gelu(x):
    return 0.5 * x * (1.0 + jnp.tanh(math.sqrt(2.0 / math.pi) * (x + 0.044715 * (x * x * x))))


def _sigmoid(x):
    return 1.0 / (1.0 + jnp.exp(-x))


def _params(*sem):
    return pltpu.CompilerParams(dimension_semantics=sem, vmem_limit_bytes=VMEM_LIMIT)


def _mod_kernel(c_ref, w_ref, b_ref, o_ref):
    c = c_ref[...]
    cond = c * _sigmoid(c)
    o_ref[0] = _mm(cond, w_ref[0]) + b_ref[0]


def _mod_call(c, mod_w, mod_b):
    L, D, D6 = mod_w.shape
    B = c.shape[0]
    nt = D6 // D
    return pl.pallas_call(
        _mod_kernel,
        grid=(L, nt),
        in_specs=[pl.BlockSpec((B, D), lambda l, n: (0, 0)),
                  pl.BlockSpec((1, D, D), lambda l, n: (l, 0, n)),
                  pl.BlockSpec((1, 1, D), lambda l, n: (l, 0, n))],
        out_specs=pl.BlockSpec((1, B, D), lambda l, n: (l, 0, n)),
        out_shape=jax.ShapeDtypeStruct((L, B, D6), F32),
        compiler_params=_params("arbitrary", "arbitrary"),
        name="adaln_mod",
    )(c, mod_w.astype(_MXU_DTYPE), mod_b.reshape(L, 1, D6))


def _inproj_kernel(x_ref, g_ref, sc_ref, sh_ref, w_ref, gcq_ref, gckv_ref,
                   cq_o, ckv_o, misc_o, ussm_o, upool_o):
    h = _rms(x_ref[0], g_ref[...]) * (1.0 + sc_ref[0]) + sh_ref[0]
    z = _mm(h, w_ref[...])
    o0, o1, o2, o3 = D_QLAT, D_QLAT + D_LAT, D_QLAT + D_LAT + MISC_W, D_QLAT + D_LAT + MISC_W + W_SSM
    cq_o[0] = _rms(z[:, :o0], gcq_ref[...]).astype(cq_o.dtype)
    ckv_o[0] = _rms(z[:, o0:o1], gckv_ref[...]).astype(ckv_o.dtype)
    misc_o[0] = z[:, o1:o2]
    ussm_o[...] = z[:, o2:o3]
    upool_o[0] = z[:, o3:]


def _inproj_call(x, g_pre, sc, sh, w_small, g_cq, g_ckv, ts):
    B, S, D = x.shape
    row = lambda b, i: (b, i, 0)
    per_b = lambda b, i: (b, 0, 0)
    const2 = lambda b, i: (0, 0)
    return pl.pallas_call(
        _inproj_kernel,
        grid=(B, S // ts),
        in_specs=[pl.BlockSpec((1, ts, D), row),
                  pl.BlockSpec((1, D), const2),
                  pl.BlockSpec((1, 1, D), per_b),
                  pl.BlockSpec((1, 1, D), per_b),
                  pl.BlockSpec((D, W_SMALL), const2),
                  pl.BlockSpec((1, D_QLAT), const2),
                  pl.BlockSpec((1, D_LAT), const2)],
        out_specs=[pl.BlockSpec((1, ts, D_QLAT), row),
                   pl.BlockSpec((1, ts, D_LAT), row),
                   pl.BlockSpec((1, ts, MISC_W), row),
                   pl.BlockSpec((ts, W_SSM), lambda b, i: (i, b)),
                   pl.BlockSpec((1, ts, W_POOL), row)],
        out_shape=[jax.ShapeDtypeStruct((B, S, D_QLAT), _MXU_DTYPE),
                   jax.ShapeDtypeStruct((B, S, D_LAT), _MXU_DTYPE),
                   jax.ShapeDtypeStruct((B, S, MISC_W), F32),
                   jax.ShapeDtypeStruct((S, B * W_SSM), F32),
                   jax.ShapeDtypeStruct((B, S, W_POOL), F32)],
        compiler_params=_params("arbitrary", "arbitrary"),
        name="in_projection",
    )(x, g_pre, sc, sh, w_small, g_cq, g_ckv)


def _sortable(i):
    return i ^ ((i >> 31) & jnp.int32(0x7FFFFFFF))


def _attn_kernel(slope_ref, cq_ref, ckv_ref, misc_ref, wqi_ref, wuq_ref, wuv_ref, o_ref,
                 kaug_sc, ckvt_sc, qi_sc, qt_sc, score_sc, lg_sc, tmax_sc, m_sc, acc_sc, yt_sc, *, topk):
    j = pl.program_id(1)
    nkt = ckvt_sc.shape[0]
    q0 = pl.multiple_of(j * QB, QB)

    @pl.when((pl.program_id(0) == 0) & (j == 0))
    def _():
        lane = lax.broadcasted_iota(jnp.int32, (QB, D_LAT), 1)
        s_in = lax.broadcasted_iota(jnp.int32, (QB, D_LAT), 0).astype(F32)
        for kt in range(nkt):
            pos = jnp.where(lane < SLOPE_PARTS, s_in, jnp.where(lane < 2 * SLOPE_PARTS, float(kt), 0.0))
            kaug_sc[kt * QB:(kt + 1) * QB, D_LAT:] = pos.astype(kaug_sc.dtype)
        sub = lax.broadcasted_iota(jnp.int32, (D_LAT, QB), 0)
        for h in range(N_HEADS):
            rows = jnp.zeros((D_LAT, QB), F32)
            for part in range(SLOPE_PARTS):
                rows = jnp.where(sub == part, slope_ref[h, part], rows)
                rows = jnp.where(sub == SLOPE_PARTS + part, slope_ref[h, part] * QB, rows)
            qt_sc[h, D_LAT:, :] = rows.astype(qt_sc.dtype)
        ones_row = jnp.where(lax.broadcasted_iota(jnp.int32, (DENOM_ROWS, QB), 0) == 0, 1.0, 0.0)
        for kt in range(nkt):
            ckvt_sc[kt, D_LAT:, :] = ones_row.astype(ckvt_sc.dtype)

    @pl.when(j == 0)
    def _():
        for kt in range(nkt):
            kv = ckv_ref[0, kt * QB:(kt + 1) * QB, :]
            ckvt_sc[kt, :D_LAT, :] = kv.astype(F32).T.astype(ckvt_sc.dtype)
            kaug_sc[kt * QB:(kt + 1) * QB, :D_LAT] = kv

    cqt = cq_ref[0].astype(F32).T.astype(_MXU_DTYPE)
    misct = misc_ref[0, pl.ds(q0, QB), :].T
    qi_sc[...] = _mm(wqi_ref[...], cqt).astype(qi_sc.dtype)
    for h in range(N_HEADS):
        qt_sc[h, :D_LAT, :] = (_mm(wuq_ref[h], cqt) * (ATTN_SCALE * LOG2E)).astype(qt_sc.dtype)

    kl = lax.broadcasted_iota(jnp.int32, (QB, QB), 0)
    ql = lax.broadcasted_iota(jnp.int32, (QB, QB), 1)
    diag_ok = kl < (ql // CHUNK + 1) * CHUNK
    ahead = 2.0 * jnp.maximum(kl - ql, 0).astype(F32)

    def score_tile(kt):
        k0 = pl.multiple_of(kt * QB, QB)
        kmat = misc_ref[0, pl.ds(k0, QB), :].astype(_MXU_DTYPE)
        acc = jnp.zeros((QB, QB), F32)
        for h in range(IDX_HEADS):
            lg = jnp.dot(kmat, qi_sc[h * LANES:(h + 1) * LANES, :], preferred_element_type=F32)
            wq = misct[IDX_DIM + h:IDX_DIM + h + 1, :] * (IDX_SCALE * IDX_HEAD_SCALE)
            acc = acc + jnp.maximum(lg, 0.0) * wq
        return acc

    def score_body(kt, carry):
        cmin, cmax = carry
        s = score_tile(kt)
        score_sc[pl.ds(pl.multiple_of(kt * QB, QB), QB), :] = s
        return (jnp.minimum(cmin, jnp.min(s, axis=0, keepdims=True)),
                jnp.maximum(cmax, jnp.max(s, axis=0, keepdims=True)))

    big = jnp.full((1, QB), 3.0e38, F32)
    cmin, cmax = lax.fori_loop(0, j, score_body, (big, -big))
    s = score_tile(j)
    cmin = jnp.minimum(cmin, jnp.min(jnp.where(diag_ok, s, 3.0e38), axis=0, keepdims=True))
    cmax = jnp.maximum(cmax, jnp.max(jnp.where(diag_ok, s, -3.0e38), axis=0, keepdims=True))
    score_sc[pl.ds(q0, QB), :] = jnp.where(diag_ok, s, -jnp.inf)

    def rows_reduce(x, op, rows):
        return op(x.reshape(QB // rows, rows, QB), axis=0)

    def count(pred, thr):
        def body(kt, acc):
            t = score_sc[pl.ds(pl.multiple_of(kt * QB, QB), QB), :]
            return acc + rows_reduce(jnp.where(pred(t, thr), 1.0, 0.0), jnp.sum, COUNT_ROWS)
        acc = lax.fori_loop(0, j + 1, body, jnp.zeros((COUNT_ROWS, QB), F32))
        return jnp.sum(acc, axis=0, keepdims=True)

    def count_and_next(v):
        def body(kt, carry):
            acc, below = carry
            t = score_sc[pl.ds(pl.multiple_of(kt * QB, QB), QB), :]
            hit = t >= v
            return (acc + rows_reduce(jnp.where(hit, 1.0, 0.0), jnp.sum, WALK_ROWS),
                    jnp.maximum(below, rows_reduce(jnp.where(hit, -jnp.inf, t), jnp.max, WALK_ROWS)))
        init = (jnp.zeros((WALK_ROWS, QB), F32), jnp.full((WALK_ROWS, QB), -jnp.inf, F32))
        acc, below = lax.fori_loop(0, j + 1, body, init)
        return jnp.sum(acc, axis=0, keepdims=True), jnp.max(below, axis=0, keepdims=True)

    ge = lambda t, thr: t >= thr
    gt = lambda t, thr: t > thr

    @pl.when((j + 1) * QB > topk)
    def _():
        kf = jnp.float32(topk)

        def any_lane(flag):
            f = jnp.where(flag, 1.0, 0.0)
            parts = [f[:, k * LANES:(k + 1) * LANES] for k in range(QB // LANES)]
            return jnp.max(functools.reduce(jnp.maximum, parts)) > 0.0

        above_max = pltpu.bitcast(_sortable(_sortable(pltpu.bitcast(cmax, jnp.int32)) + 1), F32)

        def halve(_, c):
            lo, hi = c
            mid = 0.5 * lo + 0.5 * hi
            keep_low = count(ge, mid) >= kf
            return jnp.where(keep_low, mid, lo), jnp.where(keep_low, hi, mid)

        _, hi = lax.fori_loop(0, BISECT_PASSES, halve, (cmin, above_max))

        def unresolved(c):
            return any_lane(c[1] == 0)

        def step(c):
            v, done, thr, cnt_ge = c
            cnt, below = count_and_next(v)
            hit = (done == 0) & ((cnt >= kf) | (v <= cmin))
            thr = jnp.where(hit, v, thr)
            cnt_ge = jnp.where(hit, cnt, cnt_ge)
            done = jnp.where(hit, 1, done)
            return jnp.where(done > 0, v, jnp.maximum(below, cmin)), done, thr, cnt_ge

        zero = jnp.zeros((1, QB), F32)
        _, _, thr, cnt_ge = lax.while_loop(unresolved, step, (hi, jnp.zeros((1, QB), jnp.int32), zero, zero))
        ties = any_lane(cnt_ge > kf)

        @pl.when(jnp.logical_not(ties))
        def _():
            def body(kt, _):
                r = pl.ds(pl.multiple_of(kt * QB, QB), QB)
                score_sc[r, :] = jnp.where(score_sc[r, :] >= thr, 0.0, NEG_INF)
                return 0
            lax.fori_loop(0, j + 1, body, 0)

        @pl.when(ties)
        def _():
            need = kf - count(gt, thr)
            tri = (lax.broadcasted_iota(jnp.int32, (QB, QB), 0)
                   >= lax.broadcasted_iota(jnp.int32, (QB, QB), 1)).astype(_MXU_DTYPE)

            def body(kt, seen):
                r = pl.ds(pl.multiple_of(kt * QB, QB), QB)
                t = score_sc[r, :]
                eq = jnp.where(t == thr, 1.0, 0.0)
                rank = seen + jnp.dot(tri, eq.astype(_MXU_DTYPE), preferred_element_type=F32)
                sel = (t > thr) | ((t == thr) & (rank <= need))
                score_sc[r, :] = jnp.where(sel, 0.0, NEG_INF)
                return seen + jnp.sum(eq, axis=0, keepdims=True)
            lax.fori_loop(0, j + 1, body, jnp.zeros((1, QB), F32))

    @pl.when((j + 1) * QB <= topk)
    def _():
        def body(kt, _):
            r = pl.ds(pl.multiple_of(kt * QB, QB), QB)
            score_sc[r, :] = jnp.where(score_sc[r, :] > -jnp.inf, 0.0, NEG_INF)
            return 0
        lax.fori_loop(0, j + 1, body, 0)

    m_sc[...] = jnp.full(m_sc.shape, -3.0e38, F32)
    acc_sc[...] = jnp.zeros(acc_sc.shape, F32)

    def qk_stage(kt, h, ahead_of_query):
        k0 = pl.multiple_of(kt * QB, QB)
        lg = score_sc[pl.ds(k0, QB), :] + jnp.dot(kaug_sc[pl.ds(k0, QB), :], qt_sc[h], preferred_element_type=F32)
        if ahead_of_query is not None:
            lg = lg - slope_ref[h, SLOPE_PARTS] * ahead_of_query
        lg_sc[h] = lg
        tmax_sc[h] = jnp.max(lg, axis=0, keepdims=True)

    def pv_stage(kt, h):
        m = m_sc[h]
        m_new = jnp.maximum(m, tmax_sc[h])
        alpha = jnp.exp2(m - m_new)
        m_sc[h] = m_new
        p = jnp.exp2(lg_sc[h] - m_new)
        acc_sc[h] = alpha * acc_sc[h] + jnp.dot(ckvt_sc[kt], p.astype(_MXU_DTYPE), preferred_element_type=F32)

    @pl.when(j == 0)
    def _():
        for h in range(N_HEADS):
            qk_stage(j, h, ahead)

    @pl.when(j > 0)
    def _():
        for h in range(N_HEADS):
            qk_stage(0, h, None)

        def steady(kt, _):
            for h in range(N_HEADS):
                pv_stage(kt, h)
                qk_stage(kt + 1, h, None)
            return 0

        lax.fori_loop(0, j - 1, steady, 0)
        for h in range(N_HEADS):
            pv_stage(j - 1, h)
            qk_stage(j, h, ahead)

    for h in range(N_HEADS):
        pv_stage(j, h)
    for h in range(N_HEADS):
        out = acc_sc[h, :D_LAT, :] / acc_sc[h, D_LAT:D_LAT + 1, :]
        yt_sc[h * D_VHEAD:(h + 1) * D_VHEAD, :] = _mm(wuv_ref[h], out)
    o_ref[0] = yt_sc[...].T.astype(o_ref.dtype)


def _attn_call(cqn, ckvn, misc, wqi, wuq, wuv, topk):
    B, S, _ = cqn.shape
    nkt = S // QB
    slope = jnp.exp2(-8.0 * jnp.arange(1, N_HEADS + 1, dtype=F32) / N_HEADS) * LOG2E
    parts, rest = [], slope
    for _ in range(SLOPE_PARTS):
        piece = rest.astype(_MXU_DTYPE).astype(F32)
        parts.append(piece)
        rest = rest - piece
    slopes = jnp.stack(parts + [slope], axis=1)
    per_b = lambda b, j: (b, 0, 0)
    const2 = lambda b, j: (0, 0)
    const3 = lambda b, j: (0, 0, 0)
    return pl.pallas_call(
        functools.partial(_attn_kernel, topk=topk),
        grid=(B, nkt),
        in_specs=[pl.BlockSpec(memory_space=pltpu.SMEM),
                  pl.BlockSpec((1, QB, D_QLAT), lambda b, j: (b, j, 0)),
                  pl.BlockSpec((1, S, D_LAT), per_b),
                  pl.BlockSpec((1, S, MISC_W), per_b),
                  pl.BlockSpec((IDX_HEADS * LANES, D_QLAT), const2),
                  pl.BlockSpec((N_HEADS, D_LAT, D_QLAT), const3),
                  pl.BlockSpec((N_HEADS, D_VHEAD, D_LAT), const3)],
        out_specs=pl.BlockSpec((1, QB, W_ATTN), lambda b, j: (b, j, 0)),
        out_shape=jax.ShapeDtypeStruct((B, S, W_ATTN), _MXU_DTYPE),
        scratch_shapes=[pltpu.VMEM((S, 2 * D_LAT), _MXU_DTYPE),
                        pltpu.VMEM((nkt, D_LAT + DENOM_ROWS, QB), _MXU_DTYPE),
                        pltpu.VMEM((IDX_HEADS * LANES, QB), _MXU_DTYPE),
                        pltpu.VMEM((N_HEADS, 2 * D_LAT, QB), _MXU_DTYPE),
                        pltpu.VMEM((S, QB), F32),
                        pltpu.VMEM((N_HEADS, QB, QB), F32),
                        pltpu.VMEM((N_HEADS, 1, QB), F32),
                        pltpu.VMEM((N_HEADS, 1, QB), F32),
                        pltpu.VMEM((N_HEADS, D_LAT + DENOM_ROWS, QB), F32),
                        pltpu.VMEM((W_ATTN, QB), F32)],
        compiler_params=_params("arbitrary", "arbitrary"),
        name="dsa_attention",
    )(slopes, cqn, ckvn, misc, wqi, wuq, wuv)


def _ssm_kernel(u_ref, bd_ref, lr_ref, li_ref, cd_ref, d_ref, wg_ref, bg_ref, o_ref, x_sc, bu_sc, *, nb, cw):
    tc = u_ref.shape[0] // nb

    @pl.when(pl.program_id(0) == 0)
    def _():
        x_sc[...] = jnp.zeros_like(x_sc)

    u = u_ref[...]
    bu_sc[...] = _mm(u, bd_ref[...])
    for c in range(N_STATE // cw):
        re = slice(c * cw, (c + 1) * cw)
        im = slice(N_STATE + c * cw, N_STATE + (c + 1) * cw)
        lr = lr_ref[:, re]
        li = li_ref[:, re]

        def step(t, carry, re=re, im=im, lr=lr, li=li):
            xr, xi = carry
            rows = pl.ds(pl.multiple_of(t * nb, nb), nb)
            nr = lr * xr - li * xi + bu_sc[rows, re]
            ni = lr * xi + li * xr + bu_sc[rows, im]
            bu_sc[rows, re] = nr
            bu_sc[rows, im] = ni
            return nr, ni

        xr, xi = lax.fori_loop(0, tc, step, (x_sc[:, re], x_sc[:, im]))
        x_sc[:, re] = xr
        x_sc[:, im] = xi
    y = _mm(bu_sc[...], cd_ref[...]) + d_ref[...] * u
    z = _gelu(y)
    o_ref[...] = z * _sigmoid(_mm(z, wg_ref[...]) + bg_ref[...])


def _ssm_call(u_tm, bd, lr, li, cd, dskip, w_glu, b_glu, nb, tc):
    rows = u_tm.shape[0]
    const2 = lambda i: (0, 0)
    return pl.pallas_call(
        functools.partial(_ssm_kernel, nb=nb, cw=256),
        grid=(rows // (tc * nb),),
        in_specs=[pl.BlockSpec((tc * nb, W_SSM), lambda i: (i, 0)),
                  pl.BlockSpec((W_SSM, 2 * N_STATE), const2),
                  pl.BlockSpec((1, N_STATE), const2),
                  pl.BlockSpec((1, N_STATE), const2),
                  pl.BlockSpec((2 * N_STATE, W_SSM), const2),
                  pl.BlockSpec((1, W_SSM), const2),
                  pl.BlockSpec((W_SSM, W_SSM), const2),
                  pl.BlockSpec((1, W_SSM), const2)],
        out_specs=pl.BlockSpec((tc * nb, W_SSM), lambda i: (i, 0)),
        out_shape=jax.ShapeDtypeStruct((rows, W_SSM), F32),
        scratch_shapes=[pltpu.VMEM((nb, 2 * N_STATE), F32),
                        pltpu.VMEM((tc * nb, 2 * N_STATE), F32)],
        compiler_params=_params("arbitrary"),
        name="s5_ssm",
    )(u_tm, bd, lr, li, cd, dskip, w_glu, b_glu)


def _ssm_weights(a_re, a_im, b_re, b_im, c_re, c_im, log_step):
    step = jnp.exp(log_step)[:, None]
    er = jnp.exp(a_re * step)
    ang = a_im * step
    lr, li = er * jnp.cos(ang), er * jnp.sin(ang)
    den = a_re * a_re + a_im * a_im
    fr = ((lr - 1.0) * a_re + li * a_im) / den
    fi = (li * a_re - (lr - 1.0) * a_im) / den
    br = fr[:, :, None] * b_re - fi[:, :, None] * b_im
    bi = fr[:, :, None] * b_im + fi[:, :, None] * b_re
    eye = jnp.eye(SSM_GROUPS, dtype=F32)
    pack_b = lambda m: jnp.einsum('gnp,gh->gphn', m, eye).reshape(W_SSM, N_STATE)
    pack_c = lambda m: jnp.einsum('gpn,gh->gnhp', m, eye).reshape(N_STATE, W_SSM)
    bd = jnp.concatenate([pack_b(br), pack_b(bi)], axis=1)
    cd = jnp.concatenate([pack_c(c_re), pack_c(-c_im)], axis=0)
    return (bd.astype(_MXU_DTYPE), lr.reshape(1, N_STATE), li.reshape(1, N_STATE), cd.astype(_MXU_DTYPE))


def _merge_kernel(x_ref, gpre_ref, sc_ref, sh_ref, gt_ref, gpost_ref, wg_ref, ya_ref, ys_ref,
                  up_ref, uh_ref, wp_ref, ps_ref, pa_ref, pb_ref, pc_ref, wo_ref, o_ref, halo_sc):
    i = pl.program_id(1)
    ts = x_ref.shape[1]
    x = x_ref[0]
    h = _rms(x, gpre_ref[...]) * (1.0 + sc_ref[0]) + sh_ref[0]

    u = up_ref[0]
    halo_sc[:POOL_HALO, :] = jnp.where(i > 0, uh_ref[0], 0.0)
    halo_sc[POOL_HALO:, :] = u
    lane = lax.broadcasted_iota(jnp.int32, (ts, W_POOL), 1)
    tpos = (i * ts + lax.broadcasted_iota(jnp.int32, (ts, W_POOL), 0) + 1).astype(F32)
    run = u
    pooled = jnp.zeros_like(u)
    prev = 1
    for g, win in enumerate(POOL_WINDOWS):
        for k in range(prev, win):
            run = run + halo_sc[POOL_HALO - k:POOL_HALO - k + ts, :]
        prev = win
        in_group = (lane >= g * POOL_GROUP_DIM) & (lane < (g + 1) * POOL_GROUP_DIM)
        pooled = jnp.where(in_group, run / jnp.minimum(tpos, float(win)), pooled)
    y_pool = _mm(pooled - u, wp_ref[...]) * ps_ref[...]

    d = x.shape[-1]
    merged = (_sigmoid(_mm(h, wg_ref[:, :d])) * _mm(ya_ref[0], pa_ref[...])
              + _sigmoid(_mm(h, wg_ref[:, d:2 * d])) * _mm(ys_ref[...], pb_ref[...])
              + _sigmoid(_mm(h, wg_ref[:, 2 * d:])) * _mm(y_pool, pc_ref[...]))
    y = _mm(merged, wo_ref[...])
    o_ref[0] = x + gt_ref[0] * _rms(y, gpost_ref[...])


def _merge_call(x, g_pre, sc, sh, gt, g_post, w_gate, y_attn, y_ssm_tm, u_pool, w_pool_bd, pool_scale,
                p_a, p_b, p_c, w_out, ts):
    B, S, D = x.shape
    row = lambda b, i: (b, i, 0)
    per_b = lambda b, i: (b, 0, 0)
    const2 = lambda b, i: (0, 0)
    hb = ts // POOL_HALO
    return pl.pallas_call(
        _merge_kernel,
        grid=(B, S // ts),
        in_specs=[pl.BlockSpec((1, ts, D), row),
                  pl.BlockSpec((1, D), const2),
                  pl.BlockSpec((1, 1, D), per_b),
                  pl.BlockSpec((1, 1, D), per_b),
                  pl.BlockSpec((1, 1, D), per_b),
                  pl.BlockSpec((1, D), const2),
                  pl.BlockSpec((D, N_BRANCH * D), const2),
                  pl.BlockSpec((1, ts, W_ATTN), row),
                  pl.BlockSpec((ts, W_SSM), lambda b, i: (i, b)),
                  pl.BlockSpec((1, ts, W_POOL), row),
                  pl.BlockSpec((1, POOL_HALO, W_POOL), lambda b, i: (b, jnp.maximum(i * hb - 1, 0), 0)),
                  pl.BlockSpec((W_POOL, W_POOL), const2),
                  pl.BlockSpec((1, W_POOL), const2),
                  pl.BlockSpec((W_ATTN, D), const2),
                  pl.BlockSpec((W_SSM, D), const2),
                  pl.BlockSpec((W_POOL, D), const2),
                  pl.BlockSpec((D, D), const2)],
        out_specs=pl.BlockSpec((1, ts, D), row),
        out_shape=jax.ShapeDtypeStruct((B, S, D), F32),
        scratch_shapes=[pltpu.VMEM((POOL_HALO + ts, W_POOL), F32)],
        compiler_params=_params("arbitrary", "arbitrary"),
        name="mixer_merge",
    )(x, g_pre, sc, sh, gt, g_post, w_gate, y_attn, y_ssm_tm, u_pool, u_pool, w_pool_bd, pool_scale,
      p_a, p_b, p_c, w_out)


def _ffn_kernel(x_ref, xh_ref, gpre_ref, sc_ref, sh_ref, gt_ref, gpost_ref, wup_ref, cw_ref, cb_ref,
                wdn_ref, o_ref, h_sc, u_sc, a_sc, *, cols, down_cols, ahead):
    i = pl.program_id(1)
    ts = x_ref.shape[1]
    dff = wdn_ref.shape[0]
    x = x_ref[0]
    adaln = lambda v: _rms(v, gpre_ref[...]) * (1.0 + sc_ref[0]) + sh_ref[0]
    h_sc[:CONV_HALO, :] = adaln(xh_ref[0]).astype(h_sc.dtype)
    h_sc[CONV_HALO:, :] = adaln(x).astype(h_sc.dtype)
    keep = jnp.where(i > 0, 1.0, 0.0)
    nchunk = dff // cols
    per_down = down_cols // cols
    col = lambda c, half: slice(half * dff + c * cols, half * dff + (c + 1) * cols)

    def up_project(c):
        for half in range(2):
            up = jnp.dot(h_sc[...], wup_ref[:, col(c, half)], preferred_element_type=F32)
            u_sc[half, :CONV_HALO, col(c, 0)] = up[:CONV_HALO] * keep
            u_sc[half, CONV_HALO:, col(c, 0)] = up[CONV_HALO:]

    def conv(c, half):
        acc = cb_ref[:, col(c, half)]
        for k in range(CONV_WIDTH):
            r0 = CONV_HALO - (CONV_WIDTH - 1) + k
            acc = acc + cw_ref[k:k + 1, col(c, half)] * u_sc[half, r0:r0 + ts, col(c, 0)]
        return acc

    y = jnp.zeros((ts, x.shape[-1]), F32)
    for c in range(min(ahead, nchunk)):
        up_project(c)
    for c in range(nchunk):
        if c + ahead < nchunk:
            up_project(c + ahead)
        a_sc[:, c * cols:(c + 1) * cols] = (_gelu(conv(c, 0)) * conv(c, 1)).astype(a_sc.dtype)
        if (c + 1) % per_down == 0:
            rows = slice((c + 1) * cols - down_cols, (c + 1) * cols)
            y = y + jnp.dot(a_sc[:, rows], wdn_ref[rows, :], preferred_element_type=F32)
    o_ref[0] = x + gt_ref[0] * _rms(y, gpost_ref[...])


def _ffn_call(x, g_pre, sc, sh, gt, g_post, w_up, conv_w, conv_b, w_down, ts, cols=256, down_cols=512, ahead=3):
    B, S, D = x.shape
    dff = w_down.shape[0]
    row = lambda b, i: (b, i, 0)
    per_b = lambda b, i: (b, 0, 0)
    const2 = lambda b, i: (0, 0)
    hb = ts // CONV_HALO
    return pl.pallas_call(
        functools.partial(_ffn_kernel, cols=cols, down_cols=down_cols, ahead=ahead),
        grid=(B, S // ts),
        in_specs=[pl.BlockSpec((1, ts, D), row),
                  pl.BlockSpec((1, CONV_HALO, D), lambda b, i: (b, jnp.maximum(i * hb - 1, 0), 0)),
                  pl.BlockSpec((1, D), const2),
                  pl.BlockSpec((1, 1, D), per_b),
                  pl.BlockSpec((1, 1, D), per_b),
                  pl.BlockSpec((1, 1, D), per_b),
                  pl.BlockSpec((1, D), const2),
                  pl.BlockSpec((D, 2 * dff), const2),
                  pl.BlockSpec((CONV_WIDTH, 2 * dff), const2),
                  pl.BlockSpec((1, 2 * dff), const2),
                  pl.BlockSpec((dff, D), const2)],
        out_specs=pl.BlockSpec((1, ts, D), row),
        out_shape=jax.ShapeDtypeStruct((B, S, D), F32),
        scratch_shapes=[pltpu.VMEM((CONV_HALO + ts, D), _MXU_DTYPE),
                        pltpu.VMEM((2, CONV_HALO + ts, dff), F32),
                        pltpu.VMEM((ts, dff), _MXU_DTYPE)],
        compiler_params=_params("arbitrary", "arbitrary"),
        name="conv_gated_ffn",
    )(x, x, g_pre, sc, sh, gt, g_post, w_up, conv_w, conv_b, w_down)


def _pick(n, pref):
    t = min(n, pref)
    assert n % t == 0, (n, t)
    return t


def kernel(x, c, mod_w, mod_b, mix_pre_g, mix_post_g, ffn_pre_g, ffn_post_g, w_in, g_cq, w_uq, w_qi, g_ckv, w_uv, a_re, a_im, b_re, b_im, c_re, c_im, d_skip, log_step, w_glu, b_glu, w_pool, pool_scale, p_a, p_b, p_c, w_out, w_up, conv_w, conv_b, w_down):
    B, S, D = x.shape
    depth = mod_w.shape[0]
    assert S % QB == 0 and QB % CHUNK == 0
    topk = min(DSA_TOPK_MAX, S // 4)
    ts = _pick(S, 512)
    ts_tail = _pick(S, 256)
    tc = _pick(S, 32)
    cast = lambda w: w.astype(_MXU_DTYPE)
    row = lambda v: v.reshape(1, -1)

    mod = _mod_call(c, mod_w, mod_b)
    cuts = [0]
    for wdt in IN_SPLITS:
        cuts.append(cuts[-1] + wdt)
    eye_p = jnp.eye(POOL_GROUPS, dtype=F32)
    for l in range(depth):
        sh_m, sc_m, gt_m, sh_f, sc_f, gt_f = [mod[l][:, None, k * D:(k + 1) * D] for k in range(6)]
        wl = w_in[l]
        pad = jnp.zeros((D, MISC_W - IDX_DIM - IDX_HEADS), F32)
        w_small = cast(jnp.concatenate(
            [wl[:, cuts[0]:cuts[2]], wl[:, cuts[2]:cuts[4]], pad, wl[:, cuts[4]:cuts[6]]], axis=1))
        w_gate = cast(wl[:, cuts[6]:])
        cqn, ckvn, misc, u_ssm, u_pool = _inproj_call(
            x, row(mix_pre_g[l]), sc_m, sh_m, w_small, row(g_cq[l]), row(g_ckv[l]), ts)

        wqi = jnp.transpose(w_qi[l], (1, 2, 0))
        wqi = jnp.pad(wqi, ((0, 0), (0, LANES - IDX_DIM), (0, 0))).reshape(IDX_HEADS * LANES, D_QLAT)
        y_attn = _attn_call(cqn, ckvn, misc, cast(wqi), cast(jnp.transpose(w_uq[l], (1, 2, 0))),
                            cast(jnp.transpose(w_uv[l], (0, 2, 1))), topk)

        bd, lr, li, cd = _ssm_weights(a_re[l], a_im[l], b_re[l], b_im[l], c_re[l], c_im[l], log_step[l])
        y_ssm = _ssm_call(u_ssm.reshape(S * B, W_SSM), bd, lr, li, cd, row(d_skip[l]), cast(w_glu[l]),
                          row(b_glu[l]), B, tc).reshape(S, B * W_SSM)

        w_pool_bd = cast(jnp.einsum('gcd,gh->gchd', w_pool[l], eye_p).reshape(W_POOL, W_POOL))
        x = _merge_call(x, row(mix_pre_g[l]), sc_m, sh_m, gt_m, row(mix_post_g[l]), w_gate, y_attn, y_ssm,
                        u_pool, w_pool_bd, row(pool_scale[l]), cast(p_a[l]), cast(p_b[l]), cast(p_c[l]),
                        cast(w_out[l]), ts)
        x = _ffn_call(x, row(ffn_pre_g[l]), sc_f, sh_f, gt_f, row(ffn_post_g[l]), cast(w_up[l]), conv_w[l],
                      row(conv_b[l]), cast(w_down[l]), ts)
    return x
```

```python
import functools
import math

import jax
import jax.numpy as jnp
from jax import lax
from jax.experimental import pallas as pl
from jax.experimental.pallas import tpu as pltpu

F32 = jnp.float32
_MXU_DTYPE = jnp.bfloat16

CHUNK = 64
N_HEADS = 8
D_QLAT = 256
D_LAT = 128
D_VHEAD = 64
IDX_HEADS = 8
IDX_DIM = 32
DSA_TOPK_MAX = 256
SSM_GROUPS = 16
SSM_GROUP_DIM = 16
SSM_STATE = 64
W_SSM = SSM_GROUPS * SSM_GROUP_DIM
N_STATE = SSM_GROUPS * SSM_STATE
POOL_WINDOWS = (2, 4, 8, 16)
POOL_GROUPS = 4
POOL_GROUP_DIM = 64
W_POOL = POOL_GROUPS * POOL_GROUP_DIM
W_ATTN = N_HEADS * D_VHEAD
N_BRANCH = 3
CONV_WIDTH = 3
RMS_EPS = 1e-6
NEG_INF = -1e30
ATTN_SCALE = D_LAT ** -0.5
IDX_SCALE = IDX_DIM ** -0.5
IDX_HEAD_SCALE = IDX_HEADS ** -0.5
IN_SPLITS = (D_QLAT, D_LAT, IDX_DIM, IDX_HEADS, W_SSM, W_POOL)

LANES = 128
SUBLANES = 8
MISC_W = LANES
W_SMALL = D_QLAT + D_LAT + MISC_W + W_SSM + W_POOL
VMEM_LIMIT = 56 * 1024 * 1024

QB = 256
BISECT_PASSES = 13
SLOPE_PARTS = 3
DENOM_ROWS = 16
LOG2E = math.log2(math.e)
COUNT_ROWS = 4 * SUBLANES
WALK_ROWS = 2 * SUBLANES
POOL_HALO = 16
MERGE_ROWS = 256
SSM_ROW_GROUPS = 2
CONV_HALO = 16


def _mm(a, b):
    return jnp.dot(a.astype(_MXU_DTYPE), b.astype(_MXU_DTYPE), preferred_element_type=F32)


def _rms(x, g):
    return x * lax.rsqrt(jnp.mean(x * x, axis=-1, keepdims=True) + RMS_EPS) * g


def _gelu(x):
    return 0.5 * x * (1.0 + jnp.tanh(math.sqrt(2.0 / math.pi) * (x + 0.044715 * (x * x * x))))


def _sigmoid(x):
    return 1.0 / (1.0 + jnp.exp(-x))


def _params(*sem):
    return pltpu.CompilerParams(dimension_semantics=sem, vmem_limit_bytes=VMEM_LIMIT)


def _mod_kernel(c_ref, w_ref, b_ref, o_ref):
    c = c_ref[...]
    cond = c * _sigmoid(c)
    o_ref[0] = _mm(cond, w_ref[0]) + b_ref[0]


def _mod_call(c, mod_w, mod_b):
    L, D, D6 = mod_w.shape
    B = c.shape[0]
    nt = D6 // D
    return pl.pallas_call(
        _mod_kernel,
        grid=(L, nt),
        in_specs=[pl.BlockSpec((B, D), lambda l, n: (0, 0)),
                  pl.BlockSpec((1, D, D), lambda l, n: (l, 0, n)),
                  pl.BlockSpec((1, 1, D), lambda l, n: (l, 0, n))],
        out_specs=pl.BlockSpec((1, B, D), lambda l, n: (l, 0, n)),
        out_shape=jax.ShapeDtypeStruct((L, B, D6), F32),
        compiler_params=_params("arbitrary", "arbitrary"),
        name="adaln_mod",
    )(c, mod_w.astype(_MXU_DTYPE), mod_b.reshape(L, 1, D6))


def _inproj_kernel(x_ref, g_ref, sc_ref, sh_ref, w_ref, gcq_ref, gckv_ref,
                   cq_o, ckv_o, misc_o, ussm_o, upool_o):
    h = _rms(x_ref[0], g_ref[...]) * (1.0 + sc_ref[0]) + sh_ref[0]
    z = _mm(h, w_ref[...])
    o0, o1, o2, o3 = D_QLAT, D_QLAT + D_LAT, D_QLAT + D_LAT + MISC_W, D_QLAT + D_LAT + MISC_W + W_SSM
    cq_o[0] = _rms(z[:, :o0], gcq_ref[...]).astype(cq_o.dtype)
    ckv_o[0] = _rms(z[:, o0:o1], gckv_ref[...]).astype(ckv_o.dtype)
    misc_o[0] = z[:, o1:o2]
    ussm_o[...] = z[:, o2:o3]
    upool_o[0] = z[:, o3:]


def _inproj_call(x, g_pre, sc, sh, w_small, g_cq, g_ckv, ts):
    B, S, D = x.shape
    row = lambda b, i: (b, i, 0)
    per_b = lambda b, i: (b, 0, 0)
    const2 = lambda b, i: (0, 0)
    return pl.pallas_call(
        _inproj_kernel,
        grid=(B, S // ts),
        in_specs=[pl.BlockSpec((1, ts, D), row),
                  pl.BlockSpec((1, D), const2),
                  pl.BlockSpec((1, 1, D), per_b),
                  pl.BlockSpec((1, 1, D), per_b),
                  pl.BlockSpec((D, W_SMALL), const2),
                  pl.BlockSpec((1, D_QLAT), const2),
                  pl.BlockSpec((1, D_LAT), const2)],
        out_specs=[pl.BlockSpec((1, ts, D_QLAT), row),
                   pl.BlockSpec((1, ts, D_LAT), row),
                   pl.BlockSpec((1, ts, MISC_W), row),
                   pl.BlockSpec((ts, W_SSM), lambda b, i: (i, b)),
                   pl.BlockSpec((1, ts, W_POOL), row)],
        out_shape=[jax.ShapeDtypeStruct((B, S, D_QLAT), _MXU_DTYPE),
                   jax.ShapeDtypeStruct((B, S, D_LAT), _MXU_DTYPE),
                   jax.ShapeDtypeStruct((B, S, MISC_W), F32),
                   jax.ShapeDtypeStruct((S, B * W_SSM), F32),
                   jax.ShapeDtypeStruct((B, S, W_POOL), F32)],
        compiler_params=_params("arbitrary", "arbitrary"),
        name="in_projection",
    )(x, g_pre, sc, sh, w_small, g_cq, g_ckv)


def _sortable(i):
    return i ^ ((i >> 31) & jnp.int32(0x7FFFFFFF))


def _attn_kernel(slope_ref, cq_ref, ckv_ref, misc_ref, wqi_ref, wuq_ref, wuv_ref, o_ref,
                 kaug_sc, ckvt_sc, kidx_sc, qi_sc, qt_sc, score_sc, lg_sc, tmax_sc, m_sc, acc_sc, yt_sc, *, topk):
    j = pl.program_id(1)
    nkt = ckvt_sc.shape[0]
    q0 = pl.multiple_of(j * QB, QB)

    @pl.when((pl.program_id(0) == 0) & (j == 0))
    def _():
        lane = lax.broadcasted_iota(jnp.int32, (QB, D_LAT), 1)
        s_in = lax.broadcasted_iota(jnp.int32, (QB, D_LAT), 0).astype(F32)
        for kt in range(nkt):
            pos = jnp.where(lane < SLOPE_PARTS, s_in, jnp.where(lane < 2 * SLOPE_PARTS, float(kt), 0.0))
            kaug_sc[kt * QB:(kt + 1) * QB, D_LAT:] = pos.astype(kaug_sc.dtype)
        sub = lax.broadcasted_iota(jnp.int32, (D_LAT, QB), 0)
        for h in range(N_HEADS):
            rows = jnp.zeros((D_LAT, QB), F32)
            for part in range(SLOPE_PARTS):
                rows = jnp.where(sub == part, slope_ref[h, part], rows)
                rows = jnp.where(sub == SLOPE_PARTS + part, slope_ref[h, part] * QB, rows)
            qt_sc[h, D_LAT:, :] = rows.astype(qt_sc.dtype)
        ones_row = jnp.where(lax.broadcasted_iota(jnp.int32, (DENOM_ROWS, QB), 0) == 0, 1.0, 0.0)
        for kt in range(nkt):
            ckvt_sc[kt, D_LAT:, :] = ones_row.astype(ckvt_sc.dtype)

    @pl.when(j == 0)
    def _():
        for kt in range(nkt):
            kv = ckv_ref[0, kt * QB:(kt + 1) * QB, :]
            ckvt_sc[kt, :D_LAT, :] = kv.astype(F32).T.astype(ckvt_sc.dtype)
            kaug_sc[kt * QB:(kt + 1) * QB, :D_LAT] = kv
            kidx_sc[kt * QB:(kt + 1) * QB, :] = misc_ref[0, kt * QB:(kt + 1) * QB, :].astype(kidx_sc.dtype)

    cqt = cq_ref[0].astype(F32).T.astype(_MXU_DTYPE)
    misct = misc_ref[0, pl.ds(q0, QB), :].T
    qi_sc[...] = _mm(wqi_ref[...], cqt).astype(qi_sc.dtype)
    for h in range(N_HEADS):
        qt_sc[h, :D_LAT, :] = (_mm(wuq_ref[h], cqt) * (ATTN_SCALE * LOG2E)).astype(qt_sc.dtype)

    kl = lax.broadcasted_iota(jnp.int32, (QB, QB), 0)
    ql = lax.broadcasted_iota(jnp.int32, (QB, QB), 1)
    diag_ok = kl < (ql // CHUNK + 1) * CHUNK
    ahead = 2.0 * jnp.maximum(kl - ql, 0).astype(F32)

    wq = [misct[IDX_DIM + h:IDX_DIM + h + 1, :] * (IDX_SCALE * IDX_HEAD_SCALE) for h in range(IDX_HEADS)]

    def index_logits(kt, h):
        keys = kidx_sc[pl.ds(pl.multiple_of(kt * QB, QB), QB), :]
        lg_sc[h] = jnp.dot(keys, qi_sc[h * LANES:(h + 1) * LANES, :], preferred_element_type=F32)

    def weighted_relu(kt, h0, carry, diagonal):
        rows = pl.ds(pl.multiple_of(kt * QB, QB), QB)
        s = jnp.maximum(lg_sc[h0], 0.0) * wq[h0] + jnp.maximum(lg_sc[h0 + 1], 0.0) * wq[h0 + 1]
        if h0 > 0:
            s = s + score_sc[rows, :]
        if h0 + 2 == IDX_HEADS:
            cmin, cmax = carry
            lo, hi = (jnp.where(diag_ok, s, 3.0e38), jnp.where(diag_ok, s, -3.0e38)) if diagonal else (s, s)
            carry = (jnp.minimum(cmin, jnp.min(lo, axis=0, keepdims=True)),
                     jnp.maximum(cmax, jnp.max(hi, axis=0, keepdims=True)))
            if diagonal:
                s = jnp.where(diag_ok, s, -jnp.inf)
        score_sc[rows, :] = s
        return carry

    for h in range(IDX_HEADS):
        index_logits(0, h)

    def score_body(kt, carry):
        for h0 in range(0, IDX_HEADS, 2):
            carry = weighted_relu(kt, h0, carry, False)
            index_logits(kt + 1, h0)
            index_logits(kt + 1, h0 + 1)
        return carry

    big = jnp.full((1, QB), 3.0e38, F32)
    carry = lax.fori_loop(0, j, score_body, (big, -big))
    for h0 in range(0, IDX_HEADS, 2):
        carry = weighted_relu(j, h0, carry, True)
    cmin, cmax = carry

    def rows_reduce(x, op, rows):
        return op(x.reshape(QB // rows, rows, QB), axis=0)

    def count(pred, thr):
        def body(kt, acc):
            t = score_sc[pl.ds(pl.multiple_of(kt * QB, QB), QB), :]
            return acc + rows_reduce(jnp.where(pred(t, thr), 1.0, 0.0), jnp.sum, COUNT_ROWS)
        acc = lax.fori_loop(0, j + 1, body, jnp.zeros((COUNT_ROWS, QB), F32))
        return jnp.sum(acc, axis=0, keepdims=True)

    def count_and_next(v):
        def body(kt, carry):
            acc, below = carry
            t = score_sc[pl.ds(pl.multiple_of(kt * QB, QB), QB), :]
            hit = t >= v
            return (acc + rows_reduce(jnp.where(hit, 1.0, 0.0), jnp.sum, WALK_ROWS),
                    jnp.maximum(below, rows_reduce(jnp.where(hit, -jnp.inf, t), jnp.max, WALK_ROWS)))
        init = (jnp.zeros((WALK_ROWS, QB), F32), jnp.full((WALK_ROWS, QB), -jnp.inf, F32))
        acc, below = lax.fori_loop(0, j + 1, body, init)
        return jnp.sum(acc, axis=0, keepdims=True), jnp.max(below, axis=0, keepdims=True)

    ge = lambda t, thr: t >= thr
    gt = lambda t, thr: t > thr

    @pl.when((j + 1) * QB > topk)
    def _():
        kf = jnp.float32(topk)

        def any_lane(flag):
            f = jnp.where(flag, 1.0, 0.0)
            parts = [f[:, k * LANES:(k + 1) * LANES] for k in range(QB // LANES)]
            return jnp.max(functools.reduce(jnp.maximum, parts)) > 0.0

        above_max = pltpu.bitcast(_sortable(_sortable(pltpu.bitcast(cmax, jnp.int32)) + 1), F32)

        def halve(_, c):
            lo, hi = c
            mid = 0.5 * lo + 0.5 * hi
            keep_low = count(ge, mid) >= kf
            return jnp.where(keep_low, mid, lo), jnp.where(keep_low, hi, mid)

        _, hi = lax.fori_loop(0, BISECT_PASSES, halve, (cmin, above_max))

        def unresolved(c):
            return any_lane(c[1] == 0)

        def step(c):
            v, done, thr, cnt_ge = c
            cnt, below = count_and_next(v)
            hit = (done == 0) & ((cnt >= kf) | (v <= cmin))
            thr = jnp.where(hit, v, thr)
            cnt_ge = jnp.where(hit, cnt, cnt_ge)
            done = jnp.where(hit, 1, done)
            return jnp.where(done > 0, v, jnp.maximum(below, cmin)), done, thr, cnt_ge

        zero = jnp.zeros((1, QB), F32)
        _, _, thr, cnt_ge = lax.while_loop(unresolved, step, (hi, jnp.zeros((1, QB), jnp.int32), zero, zero))
        ties = any_lane(cnt_ge > kf)

        @pl.when(jnp.logical_not(ties))
        def _():
            def body(kt, _):
                r = pl.ds(pl.multiple_of(kt * QB, QB), QB)
                score_sc[r, :] = jnp.where(score_sc[r, :] >= thr, 0.0, NEG_INF)
                return 0
            lax.fori_loop(0, j + 1, body, 0)

        @pl.when(ties)
        def _():
            need = kf - count(gt, thr)
            tri = (lax.broadcasted_iota(jnp.int32, (QB, QB), 0)
                   >= lax.broadcasted_iota(jnp.int32, (QB, QB), 1)).astype(_MXU_DTYPE)

            def body(kt, seen):
                r = pl.ds(pl.multiple_of(kt * QB, QB), QB)
                t = score_sc[r, :]
                eq = jnp.where(t == thr, 1.0, 0.0)
                rank = seen + jnp.dot(tri, eq.astype(_MXU_DTYPE), preferred_element_type=F32)
                sel = (t > thr) | ((t == thr) & (rank <= need))
                score_sc[r, :] = jnp.where(sel, 0.0, NEG_INF)
                return seen + jnp.sum(eq, axis=0, keepdims=True)
            lax.fori_loop(0, j + 1, body, jnp.zeros((1, QB), F32))

    @pl.when((j + 1) * QB <= topk)
    def _():
        def body(kt, _):
            r = pl.ds(pl.multiple_of(kt * QB, QB), QB)
            score_sc[r, :] = jnp.where(score_sc[r, :] > -jnp.inf, 0.0, NEG_INF)
            return 0
        lax.fori_loop(0, j + 1, body, 0)

    m_sc[...] = jnp.full(m_sc.shape, -3.0e38, F32)
    acc_sc[...] = jnp.zeros(acc_sc.shape, F32)

    def qk_stage(kt, h, ahead_of_query):
        k0 = pl.multiple_of(kt * QB, QB)
        lg = score_sc[pl.ds(k0, QB), :] + jnp.dot(kaug_sc[pl.ds(k0, QB), :], qt_sc[h], preferred_element_type=F32)
        if ahead_of_query is not None:
            lg = lg - slope_ref[h, SLOPE_PARTS] * ahead_of_query
        lg_sc[h] = lg
        tmax_sc[h] = jnp.max(lg, axis=0, keepdims=True)

    def pv_stage(kt, h):
        m = m_sc[h]
        m_new = jnp.maximum(m, tmax_sc[h])
        alpha = jnp.exp2(m - m_new)
        m_sc[h] = m_new
        p = jnp.exp2(lg_sc[h] - m_new)
        acc_sc[h] = alpha * acc_sc[h] + jnp.dot(ckvt_sc[kt], p.astype(_MXU_DTYPE), preferred_element_type=F32)

    @pl.when(j == 0)
    def _():
        for h in range(N_HEADS):
            qk_stage(j, h, ahead)

    @pl.when(j > 0)
    def _():
        for h in range(N_HEADS):
            qk_stage(0, h, None)

        def steady(kt, _):
            for h in range(N_HEADS):
                pv_stage(kt, h)
                qk_stage(kt + 1, h, None)
            return 0

        lax.fori_loop(0, j - 1, steady, 0)
        for h in range(N_HEADS):
            pv_stage(j - 1, h)
            qk_stage(j, h, ahead)

    for h in range(N_HEADS):
        pv_stage(j, h)
    for h in range(N_HEADS):
        out = acc_sc[h, :D_LAT, :] / acc_sc[h, D_LAT:D_LAT + 1, :]
        yt_sc[h * D_VHEAD:(h + 1) * D_VHEAD, :] = _mm(wuv_ref[h], out)
    o_ref[0] = yt_sc[...].T.astype(o_ref.dtype)


def _attn_call(cqn, ckvn, misc, wqi, wuq, wuv, topk):
    B, S, _ = cqn.shape
    nkt = S // QB
    slope = jnp.exp2(-8.0 * jnp.arange(1, N_HEADS + 1, dtype=F32) / N_HEADS) * LOG2E
    parts, rest = [], slope
    for _ in range(SLOPE_PARTS):
        piece = rest.astype(_MXU_DTYPE).astype(F32)
        parts.append(piece)
        rest = rest - piece
    slopes = jnp.stack(parts + [slope], axis=1)
    per_b = lambda b, j: (b, 0, 0)
    const2 = lambda b, j: (0, 0)
    const3 = lambda b, j: (0, 0, 0)
    return pl.pallas_call(
        functools.partial(_attn_kernel, topk=topk),
        grid=(B, nkt),
        in_specs=[pl.BlockSpec(memory_space=pltpu.SMEM),
                  pl.BlockSpec((1, QB, D_QLAT), lambda b, j: (b, j, 0)),
                  pl.BlockSpec((1, S, D_LAT), per_b),
                  pl.BlockSpec((1, S, MISC_W), per_b),
                  pl.BlockSpec((IDX_HEADS * LANES, D_QLAT), const2),
                  pl.BlockSpec((N_HEADS, D_LAT, D_QLAT), const3),
                  pl.BlockSpec((N_HEADS, D_VHEAD, D_LAT), const3)],
        out_specs=pl.BlockSpec((1, QB, W_ATTN), lambda b, j: (b, j, 0)),
        out_shape=jax.ShapeDtypeStruct((B, S, W_ATTN), _MXU_DTYPE),
        scratch_shapes=[pltpu.VMEM((S, 2 * D_LAT), _MXU_DTYPE),
                        pltpu.VMEM((nkt, D_LAT + DENOM_ROWS, QB), _MXU_DTYPE),
                        pltpu.VMEM((S, MISC_W), _MXU_DTYPE),
                        pltpu.VMEM((IDX_HEADS * LANES, QB), _MXU_DTYPE),
                        pltpu.VMEM((N_HEADS, 2 * D_LAT, QB), _MXU_DTYPE),
                        pltpu.VMEM((S, QB), F32),
                        pltpu.VMEM((N_HEADS, QB, QB), F32),
                        pltpu.VMEM((N_HEADS, 1, QB), F32),
                        pltpu.VMEM((N_HEADS, 1, QB), F32),
                        pltpu.VMEM((N_HEADS, D_LAT + DENOM_ROWS, QB), F32),
                        pltpu.VMEM((W_ATTN, QB), F32)],
        compiler_params=_params("arbitrary", "arbitrary"),
        name="dsa_attention",
    )(slopes, cqn, ckvn, misc, wqi, wuq, wuv)


def _ssm_kernel(u_ref, bd_ref, lr_ref, li_ref, cd_ref, d_ref, wg_ref, bg_ref, o_ref, x_sc, bu_sc, *, nb, cw):
    tc = u_ref.shape[0] // nb

    @pl.when(pl.program_id(0) == 0)
    def _():
        x_sc[...] = jnp.zeros_like(x_sc)

    nrow = u_ref.shape[0]
    groups = [slice(g * nrow // SSM_ROW_GROUPS, (g + 1) * nrow // SSM_ROW_GROUPS) for g in range(SSM_ROW_GROUPS)]
    for rs in groups:
        bu_sc[rs, :] = _mm(u_ref[rs, :], bd_ref[...])
    for c in range(N_STATE // cw):
        re = slice(c * cw, (c + 1) * cw)
        im = slice(N_STATE + c * cw, N_STATE + (c + 1) * cw)
        lr = lr_ref[:, re]
        li = li_ref[:, re]

        def step(t, carry, re=re, im=im, lr=lr, li=li):
            xr, xi = carry
            rows = pl.ds(pl.multiple_of(t * nb, nb), nb)
            nr = lr * xr - li * xi + bu_sc[rows, re]
            ni = lr * xi + li * xr + bu_sc[rows, im]
            bu_sc[rows, re] = nr
            bu_sc[rows, im] = ni
            return nr, ni

        xr, xi = lax.fori_loop(0, tc, step, (x_sc[:, re], x_sc[:, im]), unroll=4)
        x_sc[:, re] = xr
        x_sc[:, im] = xi
    for rs in groups:
        y = _mm(bu_sc[rs, :], cd_ref[...]) + d_ref[...] * u_ref[rs, :]
        z = _gelu(y)
        o_ref[rs, :] = z * _sigmoid(_mm(z, wg_ref[...]) + bg_ref[...])


def _ssm_call(u_tm, bd, lr, li, cd, dskip, w_glu, b_glu, nb, tc):
    rows = u_tm.shape[0]
    const2 = lambda i: (0, 0)
    return pl.pallas_call(
        functools.partial(_ssm_kernel, nb=nb, cw=256),
        grid=(rows // (tc * nb),),
        in_specs=[pl.BlockSpec((tc * nb, W_SSM), lambda i: (i, 0)),
                  pl.BlockSpec((W_SSM, 2 * N_STATE), const2),
                  pl.BlockSpec((1, N_STATE), const2),
                  pl.BlockSpec((1, N_STATE), const2),
                  pl.BlockSpec((2 * N_STATE, W_SSM), const2),
                  pl.BlockSpec((1, W_SSM), const2),
                  pl.BlockSpec((W_SSM, W_SSM), const2),
                  pl.BlockSpec((1, W_SSM), const2)],
        out_specs=pl.BlockSpec((tc * nb, W_SSM), lambda i: (i, 0)),
        out_shape=jax.ShapeDtypeStruct((rows, W_SSM), F32),
        scratch_shapes=[pltpu.VMEM((nb, 2 * N_STATE), F32),
                        pltpu.VMEM((tc * nb, 2 * N_STATE), F32)],
        compiler_params=_params("arbitrary"),
        name="s5_ssm",
    )(u_tm, bd, lr, li, cd, dskip, w_glu, b_glu)


def _ssm_weights(a_re, a_im, b_re, b_im, c_re, c_im, log_step):
    step = jnp.exp(log_step)[:, None]
    er = jnp.exp(a_re * step)
    ang = a_im * step
    lr, li = er * jnp.cos(ang), er * jnp.sin(ang)
    den = a_re * a_re + a_im * a_im
    fr = ((lr - 1.0) * a_re + li * a_im) / den
    fi = (li * a_re - (lr - 1.0) * a_im) / den
    br = fr[:, :, None] * b_re - fi[:, :, None] * b_im
    bi = fr[:, :, None] * b_im + fi[:, :, None] * b_re
    eye = jnp.eye(SSM_GROUPS, dtype=F32)
    pack_b = lambda m: jnp.einsum('gnp,gh->gphn', m, eye).reshape(W_SSM, N_STATE)
    pack_c = lambda m: jnp.einsum('gpn,gh->gnhp', m, eye).reshape(N_STATE, W_SSM)
    bd = jnp.concatenate([pack_b(br), pack_b(bi)], axis=1)
    cd = jnp.concatenate([pack_c(c_re), pack_c(-c_im)], axis=0)
    return (bd.astype(_MXU_DTYPE), lr.reshape(1, N_STATE), li.reshape(1, N_STATE), cd.astype(_MXU_DTYPE))


def _merge_kernel(x_ref, gpre_ref, sc_ref, sh_ref, gt_ref, gpost_ref, wg_ref, ya_ref, ys_ref,
                  up_ref, uh_ref, wp_ref, ps_ref, pa_ref, pb_ref, pc_ref, wo_ref, o_ref, halo_sc):
    i = pl.program_id(1)
    ts = x_ref.shape[1]

    u = up_ref[0]
    halo_sc[:POOL_HALO, :] = jnp.where(i > 0, uh_ref[0], 0.0)
    halo_sc[POOL_HALO:, :] = u
    lane = lax.broadcasted_iota(jnp.int32, (ts, W_POOL), 1)
    tpos = (i * ts + lax.broadcasted_iota(jnp.int32, (ts, W_POOL), 0) + 1).astype(F32)
    run = u
    pooled = jnp.zeros_like(u)
    prev = 1
    for g, win in enumerate(POOL_WINDOWS):
        for k in range(prev, win):
            run = run + halo_sc[POOL_HALO - k:POOL_HALO - k + ts, :]
        prev = win
        in_group = (lane >= g * POOL_GROUP_DIM) & (lane < (g + 1) * POOL_GROUP_DIM)
        pooled = jnp.where(in_group, run / jnp.minimum(tpos, float(win)), pooled)
    centred = pooled - u

    d = x_ref.shape[-1]
    for r0 in range(0, ts, MERGE_ROWS):
        rs = slice(r0, r0 + MERGE_ROWS)
        x = x_ref[0, rs, :]
        h = _rms(x, gpre_ref[...]) * (1.0 + sc_ref[0]) + sh_ref[0]
        y_pool = _mm(centred[rs], wp_ref[...]) * ps_ref[...]
        merged = (_sigmoid(_mm(h, wg_ref[:, :d])) * _mm(ya_ref[0, rs, :], pa_ref[...])
                  + _sigmoid(_mm(h, wg_ref[:, d:2 * d])) * _mm(ys_ref[rs, :], pb_ref[...])
                  + _sigmoid(_mm(h, wg_ref[:, 2 * d:])) * _mm(y_pool, pc_ref[...]))
        y = _mm(merged, wo_ref[...])
        o_ref[0, rs, :] = x + gt_ref[0] * _rms(y, gpost_ref[...])


def _merge_call(x, g_pre, sc, sh, gt, g_post, w_gate, y_attn, y_ssm_tm, u_pool, w_pool_bd, pool_scale,
                p_a, p_b, p_c, w_out, ts):
    B, S, D = x.shape
    row = lambda b, i: (b, i, 0)
    per_b = lambda b, i: (b, 0, 0)
    const2 = lambda b, i: (0, 0)
    hb = ts // POOL_HALO
    return pl.pallas_call(
        _merge_kernel,
        grid=(B, S // ts),
        in_specs=[pl.BlockSpec((1, ts, D), row),
                  pl.BlockSpec((1, D), const2),
                  pl.BlockSpec((1, 1, D), per_b),
                  pl.BlockSpec((1, 1, D), per_b),
                  pl.BlockSpec((1, 1, D), per_b),
                  pl.BlockSpec((1, D), const2),
                  pl.BlockSpec((D, N_BRANCH * D), const2),
                  pl.BlockSpec((1, ts, W_ATTN), row),
                  pl.BlockSpec((ts, W_SSM), lambda b, i: (i, b)),
                  pl.BlockSpec((1, ts, W_POOL), row),
                  pl.BlockSpec((1, POOL_HALO, W_POOL), lambda b, i: (b, jnp.maximum(i * hb - 1, 0), 0)),
                  pl.BlockSpec((W_POOL, W_POOL), const2),
                  pl.BlockSpec((1, W_POOL), const2),
                  pl.BlockSpec((W_ATTN, D), const2),
                  pl.BlockSpec((W_SSM, D), const2),
                  pl.BlockSpec((W_POOL, D), const2),
                  pl.BlockSpec((D, D), const2)],
        out_specs=pl.BlockSpec((1, ts, D), row),
        out_shape=jax.ShapeDtypeStruct((B, S, D), F32),
        scratch_shapes=[pltpu.VMEM((POOL_HALO + ts, W_POOL), F32)],
        compiler_params=_params("arbitrary", "arbitrary"),
        name="mixer_merge",
    )(x, g_pre, sc, sh, gt, g_post, w_gate, y_attn, y_ssm_tm, u_pool, u_pool, w_pool_bd, pool_scale,
      p_a, p_b, p_c, w_out)


def _ffn_kernel(x_ref, xh_ref, gpre_ref, sc_ref, sh_ref, gt_ref, gpost_ref, wup_ref, cw_ref, cb_ref,
                wdn_ref, o_ref, h_sc, u_sc, a_sc, *, cols, down_cols, ahead):
    i = pl.program_id(1)
    ts = x_ref.shape[1]
    dff = wdn_ref.shape[0]
    x = x_ref[0]
    adaln = lambda v: _rms(v, gpre_ref[...]) * (1.0 + sc_ref[0]) + sh_ref[0]
    h_sc[:CONV_HALO, :] = adaln(xh_ref[0]).astype(h_sc.dtype)
    h_sc[CONV_HALO:, :] = adaln(x).astype(h_sc.dtype)
    keep = jnp.where(i > 0, 1.0, 0.0)
    nchunk = dff // cols
    per_down = down_cols // cols
    col = lambda c, half: slice(half * dff + c * cols, half * dff + (c + 1) * cols)

    def up_project(c):
        for half in range(2):
            up = jnp.dot(h_sc[...], wup_ref[:, col(c, half)], preferred_element_type=F32)
            u_sc[half, :CONV_HALO, col(c, 0)] = up[:CONV_HALO] * keep
            u_sc[half, CONV_HALO:, col(c, 0)] = up[CONV_HALO:]

    def conv(c, half):
        acc = cb_ref[:, col(c, half)]
        for k in range(CONV_WIDTH):
            r0 = CONV_HALO - (CONV_WIDTH - 1) + k
            acc = acc + cw_ref[k:k + 1, col(c, half)] * u_sc[half, r0:r0 + ts, col(c, 0)]
        return acc

    y = jnp.zeros((ts, x.shape[-1]), F32)
    for c in range(min(ahead, nchunk)):
        up_project(c)
    for c in range(nchunk):
        if c + ahead < nchunk:
            up_project(c + ahead)
        a_sc[:, c * cols:(c + 1) * cols] = (_gelu(conv(c, 0)) * conv(c, 1)).astype(a_sc.dtype)
        if (c + 1) % per_down == 0:
            rows = slice((c + 1) * cols - down_cols, (c + 1) * cols)
            y = y + jnp.dot(a_sc[:, rows], wdn_ref[rows, :], preferred_element_type=F32)
    o_ref[0] = x + gt_ref[0] * _rms(y, gpost_ref[...])


def _ffn_call(x, g_pre, sc, sh, gt, g_post, w_up, conv_w, conv_b, w_down, ts, cols=256, down_cols=512, ahead=3):
    B, S, D = x.shape
    dff = w_down.shape[0]
    row = lambda b, i: (b, i, 0)
    per_b = lambda b, i: (b, 0, 0)
    const2 = lambda b, i: (0, 0)
    hb = ts // CONV_HALO
    return pl.pallas_call(
        functools.partial(_ffn_kernel, cols=cols, down_cols=down_cols, ahead=ahead),
        grid=(B, S // ts),
        in_specs=[pl.BlockSpec((1, ts, D), row),
                  pl.BlockSpec((1, CONV_HALO, D), lambda b, i: (b, jnp.maximum(i * hb - 1, 0), 0)),
                  pl.BlockSpec((1, D), const2),
                  pl.BlockSpec((1, 1, D), per_b),
                  pl.BlockSpec((1, 1, D), per_b),
                  pl.BlockSpec((1, 1, D), per_b),
                  pl.BlockSpec((1, D), const2),
                  pl.BlockSpec((D, 2 * dff), const2),
                  pl.BlockSpec((CONV_WIDTH, 2 * dff), const2),
                  pl.BlockSpec((1, 2 * dff), const2),
                  pl.BlockSpec((dff, D), const2)],
        out_specs=pl.BlockSpec((1, ts, D), row),
        out_shape=jax.ShapeDtypeStruct((B, S, D), F32),
        scratch_shapes=[pltpu.VMEM((CONV_HALO + ts, D), _MXU_DTYPE),
                        pltpu.VMEM((2, CONV_HALO + ts, dff), F32),
                        pltpu.VMEM((ts, dff), _MXU_DTYPE)],
        compiler_params=_params("arbitrary", "arbitrary"),
        name="conv_gated_ffn",
    )(x, x, g_pre, sc, sh, gt, g_post, w_up, conv_w, conv_b, w_down)


def _pick(n, pref):
    t = min(n, pref)
    assert n % t == 0, (n, t)
    return t


def kernel(x, c, mod_w, mod_b, mix_pre_g, mix_post_g, ffn_pre_g, ffn_post_g, w_in, g_cq, w_uq, w_qi, g_ckv, w_uv, a_re, a_im, b_re, b_im, c_re, c_im, d_skip, log_step, w_glu, b_glu, w_pool, pool_scale, p_a, p_b, p_c, w_out, w_up, conv_w, conv_b, w_down):
    B, S, D = x.shape
    depth = mod_w.shape[0]
    assert S % QB == 0 and QB % CHUNK == 0
    topk = min(DSA_TOPK_MAX, S // 4)
    ts = _pick(S, 512)
    ts_tail = _pick(S, 256)
    tc = _pick(S, 32)
    cast = lambda w: w.astype(_MXU_DTYPE)
    row = lambda v: v.reshape(1, -1)

    mod = _mod_call(c, mod_w, mod_b)
    cuts = [0]
    for wdt in IN_SPLITS:
        cuts.append(cuts[-1] + wdt)
    eye_p = jnp.eye(POOL_GROUPS, dtype=F32)
    for l in range(depth):
        sh_m, sc_m, gt_m, sh_f, sc_f, gt_f = [mod[l][:, None, k * D:(k + 1) * D] for k in range(6)]
        wl = w_in[l]
        pad = jnp.zeros((D, MISC_W - IDX_DIM - IDX_HEADS), F32)
        w_small = cast(jnp.concatenate(
            [wl[:, cuts[0]:cuts[2]], wl[:, cuts[2]:cuts[4]], pad, wl[:, cuts[4]:cuts[6]]], axis=1))
        w_gate = cast(wl[:, cuts[6]:])
        cqn, ckvn, misc, u_ssm, u_pool = _inproj_call(
            x, row(mix_pre_g[l]), sc_m, sh_m, w_small, row(g_cq[l]), row(g_ckv[l]), ts)

        wqi = jnp.transpose(w_qi[l], (1, 2, 0))
        wqi = jnp.pad(wqi, ((0, 0), (0, LANES - IDX_DIM), (0, 0))).reshape(IDX_HEADS * LANES, D_QLAT)
        y_attn = _attn_call(cqn, ckvn, misc, cast(wqi), cast(jnp.transpose(w_uq[l], (1, 2, 0))),
                            cast(jnp.transpose(w_uv[l], (0, 2, 1))), topk)

        bd, lr, li, cd = _ssm_weights(a_re[l], a_im[l], b_re[l], b_im[l], c_re[l], c_im[l], log_step[l])
        y_ssm = _ssm_call(u_ssm.reshape(S * B, W_SSM), bd, lr, li, cd, row(d_skip[l]), cast(w_glu[l]),
                          row(b_glu[l]), B, tc).reshape(S, B * W_SSM)

        w_pool_bd = cast(jnp.einsum('gcd,gh->gchd', w_pool[l], eye_p).reshape(W_POOL, W_POOL))
        x = _merge_call(x, row(mix_pre_g[l]), sc_m, sh_m, gt_m, row(mix_post_g[l]), w_gate, y_attn, y_ssm,
                        u_pool, w_pool_bd, row(pool_scale[l]), cast(p_a[l]), cast(p_b[l]), cast(p_c[l]),
                        cast(w_out[l]), ts)
        x = _ffn_call(x, row(ffn_pre_g[l]), sc_f, sh_f, gt_f, row(ffn_post_g[l]), cast(w_up[l]), conv_w[l],
                      row(conv_b[l]), cast(w_down[l]), ts)
    return x
```

```python
import functools
import math

import jax
import jax.numpy as jnp
from jax import lax
from jax.experimental import pallas as pl
from jax.experimental.pallas import tpu as pltpu

F32 = jnp.float32
_MXU_DTYPE = jnp.bfloat16

CHUNK = 64
N_HEADS = 8
D_QLAT = 256
D_LAT = 128
D_VHEAD = 64
IDX_HEADS = 8
IDX_DIM = 32
DSA_TOPK_MAX = 256
SSM_GROUPS = 16
SSM_GROUP_DIM = 16
SSM_STATE = 64
W_SSM = SSM_GROUPS * SSM_GROUP_DIM
N_STATE = SSM_GROUPS * SSM_STATE
POOL_WINDOWS = (2, 4, 8, 16)
POOL_GROUPS = 4
POOL_GROUP_DIM = 64
W_POOL = POOL_GROUPS * POOL_GROUP_DIM
W_ATTN = N_HEADS * D_VHEAD
N_BRANCH = 3
CONV_WIDTH = 3
RMS_EPS = 1e-6
NEG_INF = -1e30
ATTN_SCALE = D_LAT ** -0.5
IDX_SCALE = IDX_DIM ** -0.5
IDX_HEAD_SCALE = IDX_HEADS ** -0.5
IN_SPLITS = (D_QLAT, D_LAT, IDX_DIM, IDX_HEADS, W_SSM, W_POOL)

LANES = 128
SUBLANES = 8
MISC_W = LANES
W_SMALL = D_QLAT + D_LAT + MISC_W + W_SSM + W_POOL
VMEM_LIMIT = 56 * 1024 * 1024

QB = 256
BISECT_PASSES = 13
SLOPE_PARTS = 3
DENOM_ROWS = 16
LOG2E = math.log2(math.e)
COUNT_ROWS = 4 * SUBLANES
WALK_ROWS = 2 * SUBLANES
POOL_HALO = 16
MERGE_ROWS = 256
SSM_ROW_GROUPS = 2
CONV_HALO = 16


def _mm(a, b):
    return jnp.dot(a.astype(_MXU_DTYPE), b.astype(_MXU_DTYPE), preferred_element_type=F32)


def _rms(x, g):
    return x * lax.rsqrt(jnp.mean(x * x, axis=-1, keepdims=True) + RMS_EPS) * g


def _gelu(x):
    return 0.5 * x * (1.0 + jnp.tanh(math.sqrt(2.0 / math.pi) * (x + 0.044715 * (x * x * x))))


def _sigmoid(x):
    return 1.0 / (1.0 + jnp.exp(-x))


def _params(*sem):
    return pltpu.CompilerParams(dimension_semantics=sem, vmem_limit_bytes=VMEM_LIMIT)


def _mod_kernel(c_ref, w_ref, b_ref, o_ref):
    c = c_ref[...]
    cond = c * _sigmoid(c)
    o_ref[0] = _mm(cond, w_ref[0]) + b_ref[0]


def _mod_call(c, mod_w, mod_b):
    L, D, D6 = mod_w.shape
    B = c.shape[0]
    nt = D6 // D
    return pl.pallas_call(
        _mod_kernel,
        grid=(L, nt),
        in_specs=[pl.BlockSpec((B, D), lambda l, n: (0, 0)),
                  pl.BlockSpec((1, D, D), lambda l, n: (l, 0, n)),
                  pl.BlockSpec((1, 1, D), lambda l, n: (l, 0, n))],
        out_specs=pl.BlockSpec((1, B, D), lambda l, n: (l, 0, n)),
        out_shape=jax.ShapeDtypeStruct((L, B, D6), F32),
        compiler_params=_params("arbitrary", "arbitrary"),
        name="adaln_mod",
    )(c, mod_w.astype(_MXU_DTYPE), mod_b.reshape(L, 1, D6))


def _inproj_kernel(x_ref, g_ref, sc_ref, sh_ref, w_ref, gcq_ref, gckv_ref,
                   cq_o, ckv_o, misc_o, ussm_o, upool_o):
    o0, o1, o2, o3 = D_QLAT, D_QLAT + D_LAT, D_QLAT + D_LAT + MISC_W, D_QLAT + D_LAT + MISC_W + W_SSM
    for r0 in range(0, x_ref.shape[1], MERGE_ROWS):
        rs = slice(r0, r0 + MERGE_ROWS)
        h = _rms(x_ref[0, rs, :], g_ref[...]) * (1.0 + sc_ref[0]) + sh_ref[0]
        z = _mm(h, w_ref[...])
        cq_o[0, rs, :] = _rms(z[:, :o0], gcq_ref[...]).astype(cq_o.dtype)
        ckv_o[0, rs, :] = _rms(z[:, o0:o1], gckv_ref[...]).astype(ckv_o.dtype)
        misc_o[0, rs, :] = z[:, o1:o2]
        ussm_o[rs, :] = z[:, o2:o3]
        upool_o[0, rs, :] = z[:, o3:]


def _inproj_call(x, g_pre, sc, sh, w_small, g_cq, g_ckv, ts):
    B, S, D = x.shape
    row = lambda b, i: (b, i, 0)
    per_b = lambda b, i: (b, 0, 0)
    const2 = lambda b, i: (0, 0)
    return pl.pallas_call(
        _inproj_kernel,
        grid=(B, S // ts),
        in_specs=[pl.BlockSpec((1, ts, D), row),
                  pl.BlockSpec((1, D), const2),
                  pl.BlockSpec((1, 1, D), per_b),
                  pl.BlockSpec((1, 1, D), per_b),
                  pl.BlockSpec((D, W_SMALL), const2),
                  pl.BlockSpec((1, D_QLAT), const2),
                  pl.BlockSpec((1, D_LAT), const2)],
        out_specs=[pl.BlockSpec((1, ts, D_QLAT), row),
                   pl.BlockSpec((1, ts, D_LAT), row),
                   pl.BlockSpec((1, ts, MISC_W), row),
                   pl.BlockSpec((ts, W_SSM), lambda b, i: (i, b)),
                   pl.BlockSpec((1, ts, W_POOL), row)],
        out_shape=[jax.ShapeDtypeStruct((B, S, D_QLAT), _MXU_DTYPE),
                   jax.ShapeDtypeStruct((B, S, D_LAT), _MXU_DTYPE),
                   jax.ShapeDtypeStruct((B, S, MISC_W), F32),
                   jax.ShapeDtypeStruct((S, B * W_SSM), F32),
                   jax.ShapeDtypeStruct((B, S, W_POOL), F32)],
        compiler_params=_params("arbitrary", "arbitrary"),
        name="in_projection",
    )(x, g_pre, sc, sh, w_small, g_cq, g_ckv)


def _sortable(i):
    return i ^ ((i >> 31) & jnp.int32(0x7FFFFFFF))


def _attn_kernel(slope_ref, cq_ref, ckv_ref, misc_ref, wqi_ref, wuq_ref, wuv_ref, o_ref,
                 kaug_sc, ckvt_sc, qi_sc, qt_sc, score_sc, lg_sc, tmax_sc, m_sc, acc_sc, yt_sc, *, topk):
    j = pl.program_id(1)
    nkt = ckvt_sc.shape[0]
    q0 = pl.multiple_of(j * QB, QB)

    @pl.when((pl.program_id(0) == 0) & (j == 0))
    def _():
        lane = lax.broadcasted_iota(jnp.int32, (QB, D_LAT), 1)
        s_in = lax.broadcasted_iota(jnp.int32, (QB, D_LAT), 0).astype(F32)
        for kt in range(nkt):
            pos = jnp.where(lane < SLOPE_PARTS, s_in, jnp.where(lane < 2 * SLOPE_PARTS, float(kt), 0.0))
            kaug_sc[kt * QB:(kt + 1) * QB, D_LAT:] = pos.astype(kaug_sc.dtype)
        sub = lax.broadcasted_iota(jnp.int32, (D_LAT, QB), 0)
        for h in range(N_HEADS):
            rows = jnp.zeros((D_LAT, QB), F32)
            for part in range(SLOPE_PARTS):
                rows = jnp.where(sub == part, slope_ref[h, part], rows)
                rows = jnp.where(sub == SLOPE_PARTS + part, slope_ref[h, part] * QB, rows)
            qt_sc[h, D_LAT:, :] = rows.astype(qt_sc.dtype)
        ones_row = jnp.where(lax.broadcasted_iota(jnp.int32, (DENOM_ROWS, QB), 0) == 0, 1.0, 0.0)
        for kt in range(nkt):
            ckvt_sc[kt, D_LAT:, :] = ones_row.astype(ckvt_sc.dtype)

    @pl.when(j == 0)
    def _():
        for kt in range(nkt):
            kv = ckv_ref[0, kt * QB:(kt + 1) * QB, :]
            ckvt_sc[kt, :D_LAT, :] = kv.astype(F32).T.astype(ckvt_sc.dtype)
            kaug_sc[kt * QB:(kt + 1) * QB, :D_LAT] = kv

    cqt = cq_ref[0].astype(F32).T.astype(_MXU_DTYPE)
    misct = misc_ref[0, pl.ds(q0, QB), :].T
    qi_sc[...] = _mm(wqi_ref[...], cqt).astype(qi_sc.dtype)
    for h in range(N_HEADS):
        qt_sc[h, :D_LAT, :] = (_mm(wuq_ref[h], cqt) * (ATTN_SCALE * LOG2E)).astype(qt_sc.dtype)

    kl = lax.broadcasted_iota(jnp.int32, (QB, QB), 0)
    ql = lax.broadcasted_iota(jnp.int32, (QB, QB), 1)
    diag_ok = kl < (ql // CHUNK + 1) * CHUNK
    ahead = 2.0 * jnp.maximum(kl - ql, 0).astype(F32)

    def score_tile(kt):
        k0 = pl.multiple_of(kt * QB, QB)
        kmat = misc_ref[0, pl.ds(k0, QB), :].astype(_MXU_DTYPE)
        acc = jnp.zeros((QB, QB), F32)
        for h in range(IDX_HEADS):
            lg = jnp.dot(kmat, qi_sc[h * LANES:(h + 1) * LANES, :], preferred_element_type=F32)
            wq = misct[IDX_DIM + h:IDX_DIM + h + 1, :] * (IDX_SCALE * IDX_HEAD_SCALE)
            acc = acc + jnp.maximum(lg, 0.0) * wq
        return acc

    def score_body(kt, carry):
        cmin, cmax = carry
        s = score_tile(kt)
        score_sc[pl.ds(pl.multiple_of(kt * QB, QB), QB), :] = s
        return (jnp.minimum(cmin, jnp.min(s, axis=0, keepdims=True)),
                jnp.maximum(cmax, jnp.max(s, axis=0, keepdims=True)))

    big = jnp.full((1, QB), 3.0e38, F32)
    cmin, cmax = lax.fori_loop(0, j, score_body, (big, -big))
    s = score_tile(j)
    cmin = jnp.minimum(cmin, jnp.min(jnp.where(diag_ok, s, 3.0e38), axis=0, keepdims=True))
    cmax = jnp.maximum(cmax, jnp.max(jnp.where(diag_ok, s, -3.0e38), axis=0, keepdims=True))
    score_sc[pl.ds(q0, QB), :] = jnp.where(diag_ok, s, -jnp.inf)

    def rows_reduce(x, op, rows):
        return op(x.reshape(QB // rows, rows, QB), axis=0)

    def count(pred, thr):
        def body(kt, acc):
            t = score_sc[pl.ds(pl.multiple_of(kt * QB, QB), QB), :]
            return acc + rows_reduce(jnp.where(pred(t, thr), 1.0, 0.0), jnp.sum, COUNT_ROWS)
        acc = lax.fori_loop(0, j + 1, body, jnp.zeros((COUNT_ROWS, QB), F32))
        return jnp.sum(acc, axis=0, keepdims=True)

    def count_and_next(v):
        def body(kt, carry):
            acc, below = carry
            t = score_sc[pl.ds(pl.multiple_of(kt * QB, QB), QB), :]
            hit = t >= v
            return (acc + rows_reduce(jnp.where(hit, 1.0, 0.0), jnp.sum, WALK_ROWS),
                    jnp.maximum(below, rows_reduce(jnp.where(hit, -jnp.inf, t), jnp.max, WALK_ROWS)))
        init = (jnp.zeros((WALK_ROWS, QB), F32), jnp.full((WALK_ROWS, QB), -jnp.inf, F32))
        acc, below = lax.fori_loop(0, j + 1, body, init)
        return jnp.sum(acc, axis=0, keepdims=True), jnp.max(below, axis=0, keepdims=True)

    ge = lambda t, thr: t >= thr
    gt = lambda t, thr: t > thr

    @pl.when((j + 1) * QB > topk)
    def _():
        kf = jnp.float32(topk)

        def any_lane(flag):
            f = jnp.where(flag, 1.0, 0.0)
            parts = [f[:, k * LANES:(k + 1) * LANES] for k in range(QB // LANES)]
            return jnp.max(functools.reduce(jnp.maximum, parts)) > 0.0

        above_max = pltpu.bitcast(_sortable(_sortable(pltpu.bitcast(cmax, jnp.int32)) + 1), F32)

        def halve(_, c):
            lo, hi = c
            mid = 0.5 * lo + 0.5 * hi
            keep_low = count(ge, mid) >= kf
            return jnp.where(keep_low, mid, lo), jnp.where(keep_low, hi, mid)

        _, hi = lax.fori_loop(0, BISECT_PASSES, halve, (cmin, above_max))

        def unresolved(c):
            return any_lane(c[1] == 0)

        def step(c):
            v, done, thr, cnt_ge = c
            cnt, below = count_and_next(v)
            hit = (done == 0) & ((cnt >= kf) | (v <= cmin))
            thr = jnp.where(hit, v, thr)
            cnt_ge = jnp.where(hit, cnt, cnt_ge)
            done = jnp.where(hit, 1, done)
            return jnp.where(done > 0, v, jnp.maximum(below, cmin)), done, thr, cnt_ge

        zero = jnp.zeros((1, QB), F32)
        _, _, thr, cnt_ge = lax.while_loop(unresolved, step, (hi, jnp.zeros((1, QB), jnp.int32), zero, zero))
        ties = any_lane(cnt_ge > kf)

        @pl.when(jnp.logical_not(ties))
        def _():
            def body(kt, _):
                r = pl.ds(pl.multiple_of(kt * QB, QB), QB)
                score_sc[r, :] = jnp.where(score_sc[r, :] >= thr, 0.0, NEG_INF)
                return 0
            lax.fori_loop(0, j + 1, body, 0)

        @pl.when(ties)
        def _():
            need = kf - count(gt, thr)
            tri = (lax.broadcasted_iota(jnp.int32, (QB, QB), 0)
                   >= lax.broadcasted_iota(jnp.int32, (QB, QB), 1)).astype(_MXU_DTYPE)

            def body(kt, seen):
                r = pl.ds(pl.multiple_of(kt * QB, QB), QB)
                t = score_sc[r, :]
                eq = jnp.where(t == thr, 1.0, 0.0)
                rank = seen + jnp.dot(tri, eq.astype(_MXU_DTYPE), preferred_element_type=F32)
                sel = (t > thr) | ((t == thr) & (rank <= need))
                score_sc[r, :] = jnp.where(sel, 0.0, NEG_INF)
                return seen + jnp.sum(eq, axis=0, keepdims=True)
            lax.fori_loop(0, j + 1, body, jnp.zeros((1, QB), F32))

    @pl.when((j + 1) * QB <= topk)
    def _():
        def body(kt, _):
            r = pl.ds(pl.multiple_of(kt * QB, QB), QB)
            score_sc[r, :] = jnp.where(score_sc[r, :] > -jnp.inf, 0.0, NEG_INF)
            return 0
        lax.fori_loop(0, j + 1, body, 0)

    m_sc[...] = jnp.full(m_sc.shape, -3.0e38, F32)
    acc_sc[...] = jnp.zeros(acc_sc.shape, F32)

    def qk_stage(kt, h, ahead_of_query):
        k0 = pl.multiple_of(kt * QB, QB)
        lg = score_sc[pl.ds(k0, QB), :] + jnp.dot(kaug_sc[pl.ds(k0, QB), :], qt_sc[h], preferred_element_type=F32)
        if ahead_of_query is not None:
            lg = lg - slope_ref[h, SLOPE_PARTS] * ahead_of_query
        lg_sc[h] = lg
        tmax_sc[h] = jnp.max(lg, axis=0, keepdims=True)

    def pv_stage(kt, h):
        m = m_sc[h]
        m_new = jnp.maximum(m, tmax_sc[h])
        alpha = jnp.exp2(m - m_new)
        m_sc[h] = m_new
        p = jnp.exp2(lg_sc[h] - m_new)
        acc_sc[h] = alpha * acc_sc[h] + jnp.dot(ckvt_sc[kt], p.astype(_MXU_DTYPE), preferred_element_type=F32)

    @pl.when(j == 0)
    def _():
        for h in range(N_HEADS):
            qk_stage(j, h, ahead)

    @pl.when(j > 0)
    def _():
        for h in range(N_HEADS):
            qk_stage(0, h, None)

        def steady(kt, _):
            for h in range(N_HEADS):
                pv_stage(kt, h)
                qk_stage(kt + 1, h, None)
            return 0

        lax.fori_loop(0, j - 1, steady, 0)
        for h in range(N_HEADS):
            pv_stage(j - 1, h)
            qk_stage(j, h, ahead)

    for h in range(N_HEADS):
        pv_stage(j, h)
    for h in range(N_HEADS):
        out = acc_sc[h, :D_LAT, :] / acc_sc[h, D_LAT:D_LAT + 1, :]
        yt_sc[h * D_VHEAD:(h + 1) * D_VHEAD, :] = _mm(wuv_ref[h], out)
    o_ref[0] = yt_sc[...].T.astype(o_ref.dtype)


def _attn_call(cqn, ckvn, misc, wqi, wuq, wuv, topk):
    B, S, _ = cqn.shape
    nkt = S // QB
    slope = jnp.exp2(-8.0 * jnp.arange(1, N_HEADS + 1, dtype=F32) / N_HEADS) * LOG2E
    parts, rest = [], slope
    for _ in range(SLOPE_PARTS):
        piece = rest.astype(_MXU_DTYPE).astype(F32)
        parts.append(piece)
        rest = rest - piece
    slopes = jnp.stack(parts + [slope], axis=1)
    per_b = lambda b, j: (b, 0, 0)
    const2 = lambda b, j: (0, 0)
    const3 = lambda b, j: (0, 0, 0)
    return pl.pallas_call(
        functools.partial(_attn_kernel, topk=topk),
        grid=(B, nkt),
        in_specs=[pl.BlockSpec(memory_space=pltpu.SMEM),
                  pl.BlockSpec((1, QB, D_QLAT), lambda b, j: (b, j, 0)),
                  pl.BlockSpec((1, S, D_LAT), per_b),
                  pl.BlockSpec((1, S, MISC_W), per_b),
                  pl.BlockSpec((IDX_HEADS * LANES, D_QLAT), const2),
                  pl.BlockSpec((N_HEADS, D_LAT, D_QLAT), const3),
                  pl.BlockSpec((N_HEADS, D_VHEAD, D_LAT), const3)],
        out_specs=pl.BlockSpec((1, QB, W_ATTN), lambda b, j: (b, j, 0)),
        out_shape=jax.ShapeDtypeStruct((B, S, W_ATTN), _MXU_DTYPE),
        scratch_shapes=[pltpu.VMEM((S, 2 * D_LAT), _MXU_DTYPE),
                        pltpu.VMEM((nkt, D_LAT + DENOM_ROWS, QB), _MXU_DTYPE),
                        pltpu.VMEM((IDX_HEADS * LANES, QB), _MXU_DTYPE),
                        pltpu.VMEM((N_HEADS, 2 * D_LAT, QB), _MXU_DTYPE),
                        pltpu.VMEM((S, QB), F32),
                        pltpu.VMEM((N_HEADS, QB, QB), F32),
                        pltpu.VMEM((N_HEADS, 1, QB), F32),
                        pltpu.VMEM((N_HEADS, 1, QB), F32),
                        pltpu.VMEM((N_HEADS, D_LAT + DENOM_ROWS, QB), F32),
                        pltpu.VMEM((W_ATTN, QB), F32)],
        compiler_params=_params("arbitrary", "arbitrary"),
        name="dsa_attention",
    )(slopes, cqn, ckvn, misc, wqi, wuq, wuv)


def _ssm_kernel(u_ref, bd_ref, lr_ref, li_ref, cd_ref, d_ref, wg_ref, bg_ref, o_ref, x_sc, bu_sc, *, nb, cw):
    tc = u_ref.shape[0] // nb

    @pl.when(pl.program_id(0) == 0)
    def _():
        x_sc[...] = jnp.zeros_like(x_sc)

    nrow = u_ref.shape[0]
    groups = [slice(g * nrow // SSM_ROW_GROUPS, (g + 1) * nrow // SSM_ROW_GROUPS) for g in range(SSM_ROW_GROUPS)]
    for rs in groups:
        bu_sc[rs, :] = _mm(u_ref[rs, :], bd_ref[...])
    for c in range(N_STATE // cw):
        re = slice(c * cw, (c + 1) * cw)
        im = slice(N_STATE + c * cw, N_STATE + (c + 1) * cw)
        lr = lr_ref[:, re]
        li = li_ref[:, re]

        def step(t, carry, re=re, im=im, lr=lr, li=li):
            xr, xi = carry
            rows = pl.ds(pl.multiple_of(t * nb, nb), nb)
            nr = lr * xr - li * xi + bu_sc[rows, re]
            ni = lr * xi + li * xr + bu_sc[rows, im]
            bu_sc[rows, re] = nr
            bu_sc[rows, im] = ni
            return nr, ni

        xr, xi = lax.fori_loop(0, tc, step, (x_sc[:, re], x_sc[:, im]), unroll=4)
        x_sc[:, re] = xr
        x_sc[:, im] = xi
    for rs in groups:
        y = _mm(bu_sc[rs, :], cd_ref[...]) + d_ref[...] * u_ref[rs, :]
        z = _gelu(y)
        o_ref[rs, :] = (z * _sigmoid(_mm(z, wg_ref[...]) + bg_ref[...])).astype(o_ref.dtype)


def _ssm_call(u_tm, bd, lr, li, cd, dskip, w_glu, b_glu, nb, tc):
    rows = u_tm.shape[0]
    const2 = lambda i: (0, 0)
    return pl.pallas_call(
        functools.partial(_ssm_kernel, nb=nb, cw=256),
        grid=(rows // (tc * nb),),
        in_specs=[pl.BlockSpec((tc * nb, W_SSM), lambda i: (i, 0)),
                  pl.BlockSpec((W_SSM, 2 * N_STATE), const2),
                  pl.BlockSpec((1, N_STATE), const2),
                  pl.BlockSpec((1, N_STATE), const2),
                  pl.BlockSpec((2 * N_STATE, W_SSM), const2),
                  pl.BlockSpec((1, W_SSM), const2),
                  pl.BlockSpec((W_SSM, W_SSM), const2),
                  pl.BlockSpec((1, W_SSM), const2)],
        out_specs=pl.BlockSpec((tc * nb, W_SSM), lambda i: (i, 0)),
        out_shape=jax.ShapeDtypeStruct((rows, W_SSM), _MXU_DTYPE),
        scratch_shapes=[pltpu.VMEM((nb, 2 * N_STATE), F32),
                        pltpu.VMEM((tc * nb, 2 * N_STATE), F32)],
        compiler_params=_params("arbitrary"),
        name="s5_ssm",
    )(u_tm, bd, lr, li, cd, dskip, w_glu, b_glu)


def _ssm_weights(a_re, a_im, b_re, b_im, c_re, c_im, log_step):
    step = jnp.exp(log_step)[:, None]
    er = jnp.exp(a_re * step)
    ang = a_im * step
    lr, li = er * jnp.cos(ang), er * jnp.sin(ang)
    den = a_re * a_re + a_im * a_im
    fr = ((lr - 1.0) * a_re + li * a_im) / den
    fi = (li * a_re - (lr - 1.0) * a_im) / den
    br = fr[:, :, None] * b_re - fi[:, :, None] * b_im
    bi = fr[:, :, None] * b_im + fi[:, :, None] * b_re
    eye = jnp.eye(SSM_GROUPS, dtype=F32)
    pack_b = lambda m: jnp.einsum('gnp,gh->gphn', m, eye).reshape(W_SSM, N_STATE)
    pack_c = lambda m: jnp.einsum('gpn,gh->gnhp', m, eye).reshape(N_STATE, W_SSM)
    bd = jnp.concatenate([pack_b(br), pack_b(bi)], axis=1)
    cd = jnp.concatenate([pack_c(c_re), pack_c(-c_im)], axis=0)
    return (bd.astype(_MXU_DTYPE), lr.reshape(1, N_STATE), li.reshape(1, N_STATE), cd.astype(_MXU_DTYPE))


def _merge_kernel(x_ref, gpre_ref, sc_ref, sh_ref, gt_ref, gpost_ref, wg_ref, ya_ref, ys_ref,
                  up_ref, uh_ref, wp_ref, ps_ref, pa_ref, pb_ref, pc_ref, wo_ref, o_ref, halo_sc):
    i = pl.program_id(1)
    ts = x_ref.shape[1]

    u = up_ref[0]
    halo_sc[:POOL_HALO, :] = jnp.where(i > 0, uh_ref[0], 0.0)
    halo_sc[POOL_HALO:, :] = u
    lane = lax.broadcasted_iota(jnp.int32, (ts, W_POOL), 1)
    tpos = (i * ts + lax.broadcasted_iota(jnp.int32, (ts, W_POOL), 0) + 1).astype(F32)
    run = u
    pooled = jnp.zeros_like(u)
    prev = 1
    for g, win in enumerate(POOL_WINDOWS):
        for k in range(prev, win):
            run = run + halo_sc[POOL_HALO - k:POOL_HALO - k + ts, :]
        prev = win
        in_group = (lane >= g * POOL_GROUP_DIM) & (lane < (g + 1) * POOL_GROUP_DIM)
        pooled = jnp.where(in_group, run / jnp.minimum(tpos, float(win)), pooled)
    centred = pooled - u

    d = x_ref.shape[-1]
    for r0 in range(0, ts, MERGE_ROWS):
        rs = slice(r0, r0 + MERGE_ROWS)
        x = x_ref[0, rs, :]
        h = _rms(x, gpre_ref[...]) * (1.0 + sc_ref[0]) + sh_ref[0]
        y_pool = _mm(centred[rs], wp_ref[...]) * ps_ref[...]
        merged = (_sigmoid(_mm(h, wg_ref[:, :d])) * _mm(ya_ref[0, rs, :], pa_ref[...])
                  + _sigmoid(_mm(h, wg_ref[:, d:2 * d])) * _mm(ys_ref[rs, :], pb_ref[...])
                  + _sigmoid(_mm(h, wg_ref[:, 2 * d:])) * _mm(y_pool, pc_ref[...]))
        y = _mm(merged, wo_ref[...])
        o_ref[0, rs, :] = x + gt_ref[0] * _rms(y, gpost_ref[...])


def _merge_call(x, g_pre, sc, sh, gt, g_post, w_gate, y_attn, y_ssm_tm, u_pool, w_pool_bd, pool_scale,
                p_a, p_b, p_c, w_out, ts):
    B, S, D = x.shape
    row = lambda b, i: (b, i, 0)
    per_b = lambda b, i: (b, 0, 0)
    const2 = lambda b, i: (0, 0)
    hb = ts // POOL_HALO
    return pl.pallas_call(
        _merge_kernel,
        grid=(B, S // ts),
        in_specs=[pl.BlockSpec((1, ts, D), row),
                  pl.BlockSpec((1, D), const2),
                  pl.BlockSpec((1, 1, D), per_b),
                  pl.BlockSpec((1, 1, D), per_b),
                  pl.BlockSpec((1, 1, D), per_b),
                  pl.BlockSpec((1, D), const2),
                  pl.BlockSpec((D, N_BRANCH * D), const2),
                  pl.BlockSpec((1, ts, W_ATTN), row),
                  pl.BlockSpec((ts, W_SSM), lambda b, i: (i, b)),
                  pl.BlockSpec((1, ts, W_POOL), row),
                  pl.BlockSpec((1, POOL_HALO, W_POOL), lambda b, i: (b, jnp.maximum(i * hb - 1, 0), 0)),
                  pl.BlockSpec((W_POOL, W_POOL), const2),
                  pl.BlockSpec((1, W_POOL), const2),
                  pl.BlockSpec((W_ATTN, D), const2),
                  pl.BlockSpec((W_SSM, D), const2),
                  pl.BlockSpec((W_POOL, D), const2),
                  pl.BlockSpec((D, D), const2)],
        out_specs=pl.BlockSpec((1, ts, D), row),
        out_shape=jax.ShapeDtypeStruct((B, S, D), F32),
        scratch_shapes=[pltpu.VMEM((POOL_HALO + ts, W_POOL), F32)],
        compiler_params=_params("arbitrary", "arbitrary"),
        name="mixer_merge",
    )(x, g_pre, sc, sh, gt, g_post, w_gate, y_attn, y_ssm_tm, u_pool, u_pool, w_pool_bd, pool_scale,
      p_a, p_b, p_c, w_out)


def _ffn_kernel(x_ref, xh_ref, gpre_ref, sc_ref, sh_ref, gt_ref, gpost_ref, wup_ref, cw_ref, cb_ref,
                wdn_ref, o_ref, h_sc, u_sc, a_sc, *, cols, down_cols, ahead):
    i = pl.program_id(1)
    ts = x_ref.shape[1]
    dff = wdn_ref.shape[0]
    x = x_ref[0]
    adaln = lambda v: _rms(v, gpre_ref[...]) * (1.0 + sc_ref[0]) + sh_ref[0]
    h_sc[:CONV_HALO, :] = adaln(xh_ref[0]).astype(h_sc.dtype)
    h_sc[CONV_HALO:, :] = adaln(x).astype(h_sc.dtype)
    keep = jnp.where(i > 0, 1.0, 0.0)
    nchunk = dff // cols
    per_down = down_cols // cols
    col = lambda c, half: slice(half * dff + c * cols, half * dff + (c + 1) * cols)

    def up_project(c):
        for half in range(2):
            up = jnp.dot(h_sc[...], wup_ref[:, col(c, half)], preferred_element_type=F32)
            u_sc[half, :CONV_HALO, col(c, 0)] = up[:CONV_HALO] * keep
            u_sc[half, CONV_HALO:, col(c, 0)] = up[CONV_HALO:]

    def conv(c, half):
        acc = cb_ref[:, col(c, half)]
        for k in range(CONV_WIDTH):
            r0 = CONV_HALO - (CONV_WIDTH - 1) + k
            acc = acc + cw_ref[k:k + 1, col(c, half)] * u_sc[half, r0:r0 + ts, col(c, 0)]
        return acc

    y = jnp.zeros((ts, x.shape[-1]), F32)
    for c in range(min(ahead, nchunk)):
        up_project(c)
    for c in range(nchunk):
        if c + ahead < nchunk:
            up_project(c + ahead)
        a_sc[:, c * cols:(c + 1) * cols] = (_gelu(conv(c, 0)) * conv(c, 1)).astype(a_sc.dtype)
        if (c + 1) % per_down == 0:
            rows = slice((c + 1) * cols - down_cols, (c + 1) * cols)
            y = y + jnp.dot(a_sc[:, rows], wdn_ref[rows, :], preferred_element_type=F32)
    o_ref[0] = x + gt_ref[0] * _rms(y, gpost_ref[...])


def _ffn_call(x, g_pre, sc, sh, gt, g_post, w_up, conv_w, conv_b, w_down, ts, cols=256, down_cols=512, ahead=3):
    B, S, D = x.shape
    dff = w_down.shape[0]
    row = lambda b, i: (b, i, 0)
    per_b = lambda b, i: (b, 0, 0)
    const2 = lambda b, i: (0, 0)
    hb = ts // CONV_HALO
    return pl.pallas_call(
        functools.partial(_ffn_kernel, cols=cols, down_cols=down_cols, ahead=ahead),
        grid=(B, S // ts),
        in_specs=[pl.BlockSpec((1, ts, D), row),
                  pl.BlockSpec((1, CONV_HALO, D), lambda b, i: (b, jnp.maximum(i * hb - 1, 0), 0)),
                  pl.BlockSpec((1, D), const2),
                  pl.BlockSpec((1, 1, D), per_b),
                  pl.BlockSpec((1, 1, D), per_b),
                  pl.BlockSpec((1, 1, D), per_b),
                  pl.BlockSpec((1, D), const2),
                  pl.BlockSpec((D, 2 * dff), const2),
                  pl.BlockSpec((CONV_WIDTH, 2 * dff), const2),
                  pl.BlockSpec((1, 2 * dff), const2),
                  pl.BlockSpec((dff, D), const2)],
        out_specs=pl.BlockSpec((1, ts, D), row),
        out_shape=jax.ShapeDtypeStruct((B, S, D), F32),
        scratch_shapes=[pltpu.VMEM((CONV_HALO + ts, D), _MXU_DTYPE),
                        pltpu.VMEM((2, CONV_HALO + ts, dff), F32),
                        pltpu.VMEM((ts, dff), _MXU_DTYPE)],
        compiler_params=_params("arbitrary", "arbitrary"),
        name="conv_gated_ffn",
    )(x, x, g_pre, sc, sh, gt, g_post, w_up, conv_w, conv_b, w_down)


def _pick(n, pref):
    t = min(n, pref)
    assert n % t == 0, (n, t)
    return t


def kernel(x, c, mod_w, mod_b, mix_pre_g, mix_post_g, ffn_pre_g, ffn_post_g, w_in, g_cq, w_uq, w_qi, g_ckv, w_uv, a_re, a_im, b_re, b_im, c_re, c_im, d_skip, log_step, w_glu, b_glu, w_pool, pool_scale, p_a, p_b, p_c, w_out, w_up, conv_w, conv_b, w_down):
    B, S, D = x.shape
    depth = mod_w.shape[0]
    assert S % QB == 0 and QB % CHUNK == 0
    topk = min(DSA_TOPK_MAX, S // 4)
    ts = _pick(S, 512)
    ts_tail = _pick(S, 256)
    tc = _pick(S, 32)
    cast = lambda w: w.astype(_MXU_DTYPE)
    row = lambda v: v.reshape(1, -1)

    mod = _mod_call(c, mod_w, mod_b)
    cuts = [0]
    for wdt in IN_SPLITS:
        cuts.append(cuts[-1] + wdt)
    eye_p = jnp.eye(POOL_GROUPS, dtype=F32)
    for l in range(depth):
        sh_m, sc_m, gt_m, sh_f, sc_f, gt_f = [mod[l][:, None, k * D:(k + 1) * D] for k in range(6)]
        wl = w_in[l]
        pad = jnp.zeros((D, MISC_W - IDX_DIM - IDX_HEADS), F32)
        w_small = cast(jnp.concatenate(
            [wl[:, cuts[0]:cuts[2]], wl[:, cuts[2]:cuts[4]], pad, wl[:, cuts[4]:cuts[6]]], axis=1))
        w_gate = cast(wl[:, cuts[6]:])
        cqn, ckvn, misc, u_ssm, u_pool = _inproj_call(
            x, row(mix_pre_g[l]), sc_m, sh_m, w_small, row(g_cq[l]), row(g_ckv[l]), ts)

        wqi = jnp.transpose(w_qi[l], (1, 2, 0))
        wqi = jnp.pad(wqi, ((0, 0), (0, LANES - IDX_DIM), (0, 0))).reshape(IDX_HEADS * LANES, D_QLAT)
        y_attn = _attn_call(cqn, ckvn, misc, cast(wqi), cast(jnp.transpose(w_uq[l], (1, 2, 0))),
                            cast(jnp.transpose(w_uv[l], (0, 2, 1))), topk)

        bd, lr, li, cd = _ssm_weights(a_re[l], a_im[l], b_re[l], b_im[l], c_re[l], c_im[l], log_step[l])
        y_ssm = _ssm_call(u_ssm.reshape(S * B, W_SSM), bd, lr, li, cd, row(d_skip[l]), cast(w_glu[l]),
                          row(b_glu[l]), B, tc).reshape(S, B * W_SSM)

        w_pool_bd = cast(jnp.einsum('gcd,gh->gchd', w_pool[l], eye_p).reshape(W_POOL, W_POOL))
        x = _merge_call(x, row(mix_pre_g[l]), sc_m, sh_m, gt_m, row(mix_post_g[l]), w_gate, y_attn, y_ssm,
                        u_pool, w_pool_bd, row(pool_scale[l]), cast(p_a[l]), cast(p_b[l]), cast(p_c[l]),
                        cast(w_out[l]), ts)
        x = _ffn_call(x, row(ffn_pre_g[l]), sc_f, sh_f, gt_f, row(ffn_post_g[l]), cast(w_up[l]), conv_w[l],
                      row(conv_b[l]), cast(w_down[l]), ts)
    return x
```

```python
import functools
import math

import jax
import jax.numpy as jnp
from jax import lax
from jax.experimental import pallas as pl
from jax.experimental.pallas import tpu as pltpu

F32 = jnp.float32
_MXU_DTYPE = jnp.bfloat16

CHUNK = 64
N_HEADS = 8
D_QLAT = 256
D_LAT = 128
D_VHEAD = 64
IDX_HEADS = 8
IDX_DIM = 32
DSA_TOPK_MAX = 256
SSM_GROUPS = 16
SSM_GROUP_DIM = 16
SSM_STATE = 64
W_SSM = SSM_GROUPS * SSM_GROUP_DIM
N_STATE = SSM_GROUPS * SSM_STATE
POOL_WINDOWS = (2, 4, 8, 16)
POOL_GROUPS = 4
POOL_GROUP_DIM = 64
W_POOL = POOL_GROUPS * POOL_GROUP_DIM
W_ATTN = N_HEADS * D_VHEAD
N_BRANCH = 3
CONV_WIDTH = 3
RMS_EPS = 1e-6
NEG_INF = -1e30
ATTN_SCALE = D_LAT ** -0.5
IDX_SCALE = IDX_DIM ** -0.5
IDX_HEAD_SCALE = IDX_HEADS ** -0.5
IN_SPLITS = (D_QLAT, D_LAT, IDX_DIM, IDX_HEADS, W_SSM, W_POOL)

LANES = 128
SUBLANES = 8
MISC_W = LANES
W_SMALL = D_QLAT + D_LAT + MISC_W + W_SSM + W_POOL
VMEM_LIMIT = 56 * 1024 * 1024

QB = 256
BISECT_PASSES = 13
SLOPE_PARTS = 3
DENOM_ROWS = 16
LOG2E = math.log2(math.e)
COUNT_ROWS = 4 * SUBLANES
WALK_ROWS = 2 * SUBLANES
POOL_HALO = 16
ROW_TILE = 512
SSM_TIME_CHUNK = 64
MERGE_ROWS = 256
SSM_ROW_GROUPS = 2
CONV_HALO = 16


def _mm(a, b):
    return jnp.dot(a.astype(_MXU_DTYPE), b.astype(_MXU_DTYPE), preferred_element_type=F32)


def _rms(x, g):
    return x * lax.rsqrt(jnp.mean(x * x, axis=-1, keepdims=True) + RMS_EPS) * g


def _gelu(x):
    return 0.5 * x * (1.0 + jnp.tanh(math.sqrt(2.0 / math.pi) * (x + 0.044715 * (x * x * x))))


def _sigmoid(x):
    return 1.0 / (1.0 + jnp.exp(-x))


def _params(*sem):
    return pltpu.CompilerParams(dimension_semantics=sem, vmem_limit_bytes=VMEM_LIMIT)


def _mod_kernel(c_ref, w_ref, b_ref, o_ref):
    c = c_ref[...]
    cond = c * _sigmoid(c)
    o_ref[0] = _mm(cond, w_ref[0]) + b_ref[0]


def _mod_call(c, mod_w, mod_b):
    L, D, D6 = mod_w.shape
    B = c.shape[0]
    nt = D6 // D
    return pl.pallas_call(
        _mod_kernel,
        grid=(L, nt),
        in_specs=[pl.BlockSpec((B, D), lambda l, n: (0, 0)),
                  pl.BlockSpec((1, D, D), lambda l, n: (l, 0, n)),
                  pl.BlockSpec((1, 1, D), lambda l, n: (l, 0, n))],
        out_specs=pl.BlockSpec((1, B, D), lambda l, n: (l, 0, n)),
        out_shape=jax.ShapeDtypeStruct((L, B, D6), F32),
        compiler_params=_params("arbitrary", "arbitrary"),
        name="adaln_mod",
    )(c, mod_w.astype(_MXU_DTYPE), mod_b.reshape(L, 1, D6))


def _inproj_kernel(x_ref, g_ref, sc_ref, sh_ref, w_ref, gcq_ref, gckv_ref,
                   cq_o, ckv_o, misc_o, ussm_o, upool_o):
    o0, o1, o2, o3 = D_QLAT, D_QLAT + D_LAT, D_QLAT + D_LAT + MISC_W, D_QLAT + D_LAT + MISC_W + W_SSM
    for r0 in range(0, x_ref.shape[1], MERGE_ROWS):
        rs = slice(r0, r0 + MERGE_ROWS)
        h = _rms(x_ref[0, rs, :], g_ref[...]) * (1.0 + sc_ref[0]) + sh_ref[0]
        z = _mm(h, w_ref[...])
        cq_o[0, rs, :] = _rms(z[:, :o0], gcq_ref[...]).astype(cq_o.dtype)
        ckv_o[0, rs, :] = _rms(z[:, o0:o1], gckv_ref[...]).astype(ckv_o.dtype)
        misc_o[0, rs, :] = z[:, o1:o2]
        ussm_o[rs, :] = z[:, o2:o3]
        upool_o[0, rs, :] = z[:, o3:]


def _inproj_call(x, g_pre, sc, sh, w_small, g_cq, g_ckv, ts):
    B, S, D = x.shape
    row = lambda b, i: (b, i, 0)
    per_b = lambda b, i: (b, 0, 0)
    const2 = lambda b, i: (0, 0)
    return pl.pallas_call(
        _inproj_kernel,
        grid=(B, S // ts),
        in_specs=[pl.BlockSpec((1, ts, D), row),
                  pl.BlockSpec((1, D), const2),
                  pl.BlockSpec((1, 1, D), per_b),
                  pl.BlockSpec((1, 1, D), per_b),
                  pl.BlockSpec((D, W_SMALL), const2),
                  pl.BlockSpec((1, D_QLAT), const2),
                  pl.BlockSpec((1, D_LAT), const2)],
        out_specs=[pl.BlockSpec((1, ts, D_QLAT), row),
                   pl.BlockSpec((1, ts, D_LAT), row),
                   pl.BlockSpec((1, ts, MISC_W), row),
                   pl.BlockSpec((ts, W_SSM), lambda b, i: (i, b)),
                   pl.BlockSpec((1, ts, W_POOL), row)],
        out_shape=[jax.ShapeDtypeStruct((B, S, D_QLAT), _MXU_DTYPE),
                   jax.ShapeDtypeStruct((B, S, D_LAT), _MXU_DTYPE),
                   jax.ShapeDtypeStruct((B, S, MISC_W), F32),
                   jax.ShapeDtypeStruct((S, B * W_SSM), F32),
                   jax.ShapeDtypeStruct((B, S, W_POOL), F32)],
        compiler_params=_params("arbitrary", "arbitrary"),
        name="in_projection",
    )(x, g_pre, sc, sh, w_small, g_cq, g_ckv)


def _sortable(i):
    return i ^ ((i >> 31) & jnp.int32(0x7FFFFFFF))


def _attn_kernel(slope_ref, cq_ref, ckv_ref, misc_ref, wqi_ref, wuq_ref, wuv_ref, o_ref,
                 kaug_sc, ckvt_sc, qi_sc, qt_sc, score_sc, lg_sc, tmax_sc, m_sc, acc_sc, yt_sc, *, topk):
    j = pl.program_id(1)
    nkt = ckvt_sc.shape[0]
    q0 = pl.multiple_of(j * QB, QB)

    @pl.when((pl.program_id(0) == 0) & (j == 0))
    def _():
        lane = lax.broadcasted_iota(jnp.int32, (QB, D_LAT), 1)
        s_in = lax.broadcasted_iota(jnp.int32, (QB, D_LAT), 0).astype(F32)
        for kt in range(nkt):
            pos = jnp.where(lane < SLOPE_PARTS, s_in, jnp.where(lane < 2 * SLOPE_PARTS, float(kt), 0.0))
            kaug_sc[kt * QB:(kt + 1) * QB, D_LAT:] = pos.astype(kaug_sc.dtype)
        sub = lax.broadcasted_iota(jnp.int32, (D_LAT, QB), 0)
        for h in range(N_HEADS):
            rows = jnp.zeros((D_LAT, QB), F32)
            for part in range(SLOPE_PARTS):
                rows = jnp.where(sub == part, slope_ref[h, part], rows)
                rows = jnp.where(sub == SLOPE_PARTS + part, slope_ref[h, part] * QB, rows)
            qt_sc[h, D_LAT:, :] = rows.astype(qt_sc.dtype)
        ones_row = jnp.where(lax.broadcasted_iota(jnp.int32, (DENOM_ROWS, QB), 0) == 0, 1.0, 0.0)
        for kt in range(nkt):
            ckvt_sc[kt, D_LAT:, :] = ones_row.astype(ckvt_sc.dtype)

    @pl.when(j == 0)
    def _():
        for kt in range(nkt):
            kv = ckv_ref[0, kt * QB:(kt + 1) * QB, :]
            ckvt_sc[kt, :D_LAT, :] = kv.astype(F32).T.astype(ckvt_sc.dtype)
            kaug_sc[kt * QB:(kt + 1) * QB, :D_LAT] = kv

    cqt = cq_ref[0].astype(F32).T.astype(_MXU_DTYPE)
    misct = misc_ref[0, pl.ds(q0, QB), :].T
    qi_sc[...] = _mm(wqi_ref[...], cqt).astype(qi_sc.dtype)
    for h in range(N_HEADS):
        qt_sc[h, :D_LAT, :] = (_mm(wuq_ref[h], cqt) * (ATTN_SCALE * LOG2E)).astype(qt_sc.dtype)

    kl = lax.broadcasted_iota(jnp.int32, (QB, QB), 0)
    ql = lax.broadcasted_iota(jnp.int32, (QB, QB), 1)
    diag_ok = kl < (ql // CHUNK + 1) * CHUNK
    ahead = 2.0 * jnp.maximum(kl - ql, 0).astype(F32)

    def score_tile(kt):
        k0 = pl.multiple_of(kt * QB, QB)
        kmat = misc_ref[0, pl.ds(k0, QB), :].astype(_MXU_DTYPE)
        acc = jnp.zeros((QB, QB), F32)
        for h in range(IDX_HEADS):
            lg = jnp.dot(kmat, qi_sc[h * LANES:(h + 1) * LANES, :], preferred_element_type=F32)
            wq = misct[IDX_DIM + h:IDX_DIM + h + 1, :] * (IDX_SCALE * IDX_HEAD_SCALE)
            acc = acc + jnp.maximum(lg, 0.0) * wq
        return acc

    def score_body(kt, carry):
        cmin, cmax = carry
        s = score_tile(kt)
        score_sc[pl.ds(pl.multiple_of(kt * QB, QB), QB), :] = s
        return (jnp.minimum(cmin, jnp.min(s, axis=0, keepdims=True)),
                jnp.maximum(cmax, jnp.max(s, axis=0, keepdims=True)))

    big = jnp.full((1, QB), 3.0e38, F32)
    cmin, cmax = lax.fori_loop(0, j, score_body, (big, -big))
    s = score_tile(j)
    cmin = jnp.minimum(cmin, jnp.min(jnp.where(diag_ok, s, 3.0e38), axis=0, keepdims=True))
    cmax = jnp.maximum(cmax, jnp.max(jnp.where(diag_ok, s, -3.0e38), axis=0, keepdims=True))
    score_sc[pl.ds(q0, QB), :] = jnp.where(diag_ok, s, -jnp.inf)

    def rows_reduce(x, op, rows):
        return op(x.reshape(QB // rows, rows, QB), axis=0)

    def count(pred, thr):
        def body(kt, acc):
            t = score_sc[pl.ds(pl.multiple_of(kt * QB, QB), QB), :]
            return acc + rows_reduce(jnp.where(pred(t, thr), 1.0, 0.0), jnp.sum, COUNT_ROWS)
        acc = lax.fori_loop(0, j + 1, body, jnp.zeros((COUNT_ROWS, QB), F32))
        return jnp.sum(acc, axis=0, keepdims=True)

    def count_and_next(v):
        def body(kt, carry):
            acc, below = carry
            t = score_sc[pl.ds(pl.multiple_of(kt * QB, QB), QB), :]
            hit = t >= v
            return (acc + rows_reduce(jnp.where(hit, 1.0, 0.0), jnp.sum, WALK_ROWS),
                    jnp.maximum(below, rows_reduce(jnp.where(hit, -jnp.inf, t), jnp.max, WALK_ROWS)))
        init = (jnp.zeros((WALK_ROWS, QB), F32), jnp.full((WALK_ROWS, QB), -jnp.inf, F32))
        acc, below = lax.fori_loop(0, j + 1, body, init)
        return jnp.sum(acc, axis=0, keepdims=True), jnp.max(below, axis=0, keepdims=True)

    ge = lambda t, thr: t >= thr
    gt = lambda t, thr: t > thr

    @pl.when((j + 1) * QB > topk)
    def _():
        kf = jnp.float32(topk)

        def any_lane(flag):
            f = jnp.where(flag, 1.0, 0.0)
            parts = [f[:, k * LANES:(k + 1) * LANES] for k in range(QB // LANES)]
            return jnp.max(functools.reduce(jnp.maximum, parts)) > 0.0

        above_max = pltpu.bitcast(_sortable(_sortable(pltpu.bitcast(cmax, jnp.int32)) + 1), F32)

        def halve(_, c):
            lo, hi = c
            mid = 0.5 * lo + 0.5 * hi
            keep_low = count(ge, mid) >= kf
            return jnp.where(keep_low, mid, lo), jnp.where(keep_low, hi, mid)

        _, hi = lax.fori_loop(0, BISECT_PASSES, halve, (cmin, above_max))

        def unresolved(c):
            return any_lane(c[1] == 0)

        def step(c):
            v, done, thr, cnt_ge = c
            cnt, below = count_and_next(v)
            hit = (done == 0) & ((cnt >= kf) | (v <= cmin))
            thr = jnp.where(hit, v, thr)
            cnt_ge = jnp.where(hit, cnt, cnt_ge)
            done = jnp.where(hit, 1, done)
            return jnp.where(done > 0, v, jnp.maximum(below, cmin)), done, thr, cnt_ge

        zero = jnp.zeros((1, QB), F32)
        _, _, thr, cnt_ge = lax.while_loop(unresolved, step, (hi, jnp.zeros((1, QB), jnp.int32), zero, zero))
        ties = any_lane(cnt_ge > kf)

        @pl.when(jnp.logical_not(ties))
        def _():
            def body(kt, _):
                r = pl.ds(pl.multiple_of(kt * QB, QB), QB)
                score_sc[r, :] = jnp.where(score_sc[r, :] >= thr, 0.0, NEG_INF)
                return 0
            lax.fori_loop(0, j + 1, body, 0)

        @pl.when(ties)
        def _():
            need = kf - count(gt, thr)
            tri = (lax.broadcasted_iota(jnp.int32, (QB, QB), 0)
                   >= lax.broadcasted_iota(jnp.int32, (QB, QB), 1)).astype(_MXU_DTYPE)

            def body(kt, seen):
                r = pl.ds(pl.multiple_of(kt * QB, QB), QB)
                t = score_sc[r, :]
                eq = jnp.where(t == thr, 1.0, 0.0)
                rank = seen + jnp.dot(tri, eq.astype(_MXU_DTYPE), preferred_element_type=F32)
                sel = (t > thr) | ((t == thr) & (rank <= need))
                score_sc[r, :] = jnp.where(sel, 0.0, NEG_INF)
                return seen + jnp.sum(eq, axis=0, keepdims=True)
            lax.fori_loop(0, j + 1, body, jnp.zeros((1, QB), F32))

    @pl.when((j + 1) * QB <= topk)
    def _():
        def body(kt, _):
            r = pl.ds(pl.multiple_of(kt * QB, QB), QB)
            score_sc[r, :] = jnp.where(score_sc[r, :] > -jnp.inf, 0.0, NEG_INF)
            return 0
        lax.fori_loop(0, j + 1, body, 0)

    m_sc[...] = jnp.full(m_sc.shape, -3.0e38, F32)
    acc_sc[...] = jnp.zeros(acc_sc.shape, F32)

    def qk_stage(kt, h, ahead_of_query):
        k0 = pl.multiple_of(kt * QB, QB)
        lg = score_sc[pl.ds(k0, QB), :] + jnp.dot(kaug_sc[pl.ds(k0, QB), :], qt_sc[h], preferred_element_type=F32)
        if ahead_of_query is not None:
            lg = lg - slope_ref[h, SLOPE_PARTS] * ahead_of_query
        lg_sc[h] = lg
        tmax_sc[h] = jnp.max(lg, axis=0, keepdims=True)

    def pv_stage(kt, h):
        m = m_sc[h]
        m_new = jnp.maximum(m, tmax_sc[h])
        alpha = jnp.exp2(m - m_new)
        m_sc[h] = m_new
        p = jnp.exp2(lg_sc[h] - m_new)
        acc_sc[h] = alpha * acc_sc[h] + jnp.dot(ckvt_sc[kt], p.astype(_MXU_DTYPE), preferred_element_type=F32)

    @pl.when(j == 0)
    def _():
        for h in range(N_HEADS):
            qk_stage(j, h, ahead)

    @pl.when(j > 0)
    def _():
        for h in range(N_HEADS):
            qk_stage(0, h, None)

        def steady(kt, _):
            for h in range(N_HEADS):
                pv_stage(kt, h)
                qk_stage(kt + 1, h, None)
            return 0

        lax.fori_loop(0, j - 1, steady, 0)
        for h in range(N_HEADS):
            pv_stage(j - 1, h)
            qk_stage(j, h, ahead)

    for h in range(N_HEADS):
        pv_stage(j, h)
    for h in range(N_HEADS):
        out = acc_sc[h, :D_LAT, :] / acc_sc[h, D_LAT:D_LAT + 1, :]
        yt_sc[h * D_VHEAD:(h + 1) * D_VHEAD, :] = _mm(wuv_ref[h], out)
    o_ref[0] = yt_sc[...].T.astype(o_ref.dtype)


def _attn_call(cqn, ckvn, misc, wqi, wuq, wuv, topk):
    B, S, _ = cqn.shape
    nkt = S // QB
    slope = jnp.exp2(-8.0 * jnp.arange(1, N_HEADS + 1, dtype=F32) / N_HEADS) * LOG2E
    parts, rest = [], slope
    for _ in range(SLOPE_PARTS):
        piece = rest.astype(_MXU_DTYPE).astype(F32)
        parts.append(piece)
        rest = rest - piece
    slopes = jnp.stack(parts + [slope], axis=1)
    per_b = lambda b, j: (b, 0, 0)
    const2 = lambda b, j: (0, 0)
    const3 = lambda b, j: (0, 0, 0)
    return pl.pallas_call(
        functools.partial(_attn_kernel, topk=topk),
        grid=(B, nkt),
        in_specs=[pl.BlockSpec(memory_space=pltpu.SMEM),
                  pl.BlockSpec((1, QB, D_QLAT), lambda b, j: (b, j, 0)),
                  pl.BlockSpec((1, S, D_LAT), per_b),
                  pl.BlockSpec((1, S, MISC_W), per_b),
                  pl.BlockSpec((IDX_HEADS * LANES, D_QLAT), const2),
                  pl.BlockSpec((N_HEADS, D_LAT, D_QLAT), const3),
                  pl.BlockSpec((N_HEADS, D_VHEAD, D_LAT), const3)],
        out_specs=pl.BlockSpec((1, QB, W_ATTN), lambda b, j: (b, j, 0)),
        out_shape=jax.ShapeDtypeStruct((B, S, W_ATTN), _MXU_DTYPE),
        scratch_shapes=[pltpu.VMEM((S, 2 * D_LAT), _MXU_DTYPE),
                        pltpu.VMEM((nkt, D_LAT + DENOM_ROWS, QB), _MXU_DTYPE),
                        pltpu.VMEM((IDX_HEADS * LANES, QB), _MXU_DTYPE),
                        pltpu.VMEM((N_HEADS, 2 * D_LAT, QB), _MXU_DTYPE),
                        pltpu.VMEM((S, QB), F32),
                        pltpu.VMEM((N_HEADS, QB, QB), F32),
                        pltpu.VMEM((N_HEADS, 1, QB), F32),
                        pltpu.VMEM((N_HEADS, 1, QB), F32),
                        pltpu.VMEM((N_HEADS, D_LAT + DENOM_ROWS, QB), F32),
                        pltpu.VMEM((W_ATTN, QB), F32)],
        compiler_params=_params("arbitrary", "arbitrary"),
        name="dsa_attention",
    )(slopes, cqn, ckvn, misc, wqi, wuq, wuv)


def _ssm_kernel(u_ref, bd_ref, lr_ref, li_ref, cd_ref, d_ref, wg_ref, bg_ref, o_ref, x_sc, bu_sc, *, nb, cw):
    tc = u_ref.shape[0] // nb

    @pl.when(pl.program_id(0) == 0)
    def _():
        x_sc[...] = jnp.zeros_like(x_sc)

    nrow = u_ref.shape[0]
    groups = [slice(g * nrow // SSM_ROW_GROUPS, (g + 1) * nrow // SSM_ROW_GROUPS) for g in range(SSM_ROW_GROUPS)]
    for rs in groups:
        bu_sc[rs, :] = _mm(u_ref[rs, :], bd_ref[...])
    for c in range(N_STATE // cw):
        re = slice(c * cw, (c + 1) * cw)
        im = slice(N_STATE + c * cw, N_STATE + (c + 1) * cw)
        lr = lr_ref[:, re]
        li = li_ref[:, re]

        def step(t, carry, re=re, im=im, lr=lr, li=li):
            xr, xi = carry
            rows = pl.ds(pl.multiple_of(t * nb, nb), nb)
            nr = lr * xr - li * xi + bu_sc[rows, re]
            ni = lr * xi + li * xr + bu_sc[rows, im]
            bu_sc[rows, re] = nr
            bu_sc[rows, im] = ni
            return nr, ni

        xr, xi = lax.fori_loop(0, tc, step, (x_sc[:, re], x_sc[:, im]), unroll=4)
        x_sc[:, re] = xr
        x_sc[:, im] = xi
    n_tiles = W_SSM // LANES
    span = N_STATE // n_tiles
    for rs in groups:
        cols = []
        for k in range(n_tiles):
            out = slice(k * LANES, (k + 1) * LANES)
            re = slice(k * span, (k + 1) * span)
            im = slice(N_STATE + k * span, N_STATE + (k + 1) * span)
            cols.append(_mm(bu_sc[rs, re], cd_ref[re, out]) + _mm(bu_sc[rs, im], cd_ref[im, out]))
        y = jnp.concatenate(cols, axis=1) + d_ref[...] * u_ref[rs, :]
        z = _gelu(y)
        o_ref[rs, :] = (z * _sigmoid(_mm(z, wg_ref[...]) + bg_ref[...])).astype(o_ref.dtype)


def _ssm_call(u_tm, bd, lr, li, cd, dskip, w_glu, b_glu, nb, tc):
    rows = u_tm.shape[0]
    const2 = lambda i: (0, 0)
    return pl.pallas_call(
        functools.partial(_ssm_kernel, nb=nb, cw=256),
        grid=(rows // (tc * nb),),
        in_specs=[pl.BlockSpec((tc * nb, W_SSM), lambda i: (i, 0)),
                  pl.BlockSpec((W_SSM, 2 * N_STATE), const2),
                  pl.BlockSpec((1, N_STATE), const2),
                  pl.BlockSpec((1, N_STATE), const2),
                  pl.BlockSpec((2 * N_STATE, W_SSM), const2),
                  pl.BlockSpec((1, W_SSM), const2),
                  pl.BlockSpec((W_SSM, W_SSM), const2),
                  pl.BlockSpec((1, W_SSM), const2)],
        out_specs=pl.BlockSpec((tc * nb, W_SSM), lambda i: (i, 0)),
        out_shape=jax.ShapeDtypeStruct((rows, W_SSM), _MXU_DTYPE),
        scratch_shapes=[pltpu.VMEM((nb, 2 * N_STATE), F32),
                        pltpu.VMEM((tc * nb, 2 * N_STATE), F32)],
        compiler_params=_params("arbitrary"),
        name="s5_ssm",
    )(u_tm, bd, lr, li, cd, dskip, w_glu, b_glu)


def _ssm_weights(a_re, a_im, b_re, b_im, c_re, c_im, log_step):
    step = jnp.exp(log_step)[:, None]
    er = jnp.exp(a_re * step)
    ang = a_im * step
    lr, li = er * jnp.cos(ang), er * jnp.sin(ang)
    den = a_re * a_re + a_im * a_im
    fr = ((lr - 1.0) * a_re + li * a_im) / den
    fi = (li * a_re - (lr - 1.0) * a_im) / den
    br = fr[:, :, None] * b_re - fi[:, :, None] * b_im
    bi = fr[:, :, None] * b_im + fi[:, :, None] * b_re
    eye = jnp.eye(SSM_GROUPS, dtype=F32)
    pack_b = lambda m: jnp.einsum('gnp,gh->gphn', m, eye).reshape(W_SSM, N_STATE)
    pack_c = lambda m: jnp.einsum('gpn,gh->gnhp', m, eye).reshape(N_STATE, W_SSM)
    bd = jnp.concatenate([pack_b(br), pack_b(bi)], axis=1)
    cd = jnp.concatenate([pack_c(c_re), pack_c(-c_im)], axis=0)
    return (bd.astype(_MXU_DTYPE), lr.reshape(1, N_STATE), li.reshape(1, N_STATE), cd.astype(_MXU_DTYPE))


def _merge_kernel(x_ref, gpre_ref, sc_ref, sh_ref, gt_ref, gpost_ref, wg_ref, ya_ref, ys_ref,
                  up_ref, uh_ref, wp_ref, ps_ref, pa_ref, pb_ref, pc_ref, wo_ref, o_ref, halo_sc):
    i = pl.program_id(1)
    ts = x_ref.shape[1]

    u = up_ref[0]
    halo_sc[:POOL_HALO, :] = jnp.where(i > 0, uh_ref[0], 0.0)
    halo_sc[POOL_HALO:, :] = u
    lane = lax.broadcasted_iota(jnp.int32, (ts, W_POOL), 1)
    tpos = (i * ts + lax.broadcasted_iota(jnp.int32, (ts, W_POOL), 0) + 1).astype(F32)
    run = u
    pooled = jnp.zeros_like(u)
    prev = 1
    for g, win in enumerate(POOL_WINDOWS):
        for k in range(prev, win):
            run = run + halo_sc[POOL_HALO - k:POOL_HALO - k + ts, :]
        prev = win
        in_group = (lane >= g * POOL_GROUP_DIM) & (lane < (g + 1) * POOL_GROUP_DIM)
        pooled = jnp.where(in_group, run / jnp.minimum(tpos, float(win)), pooled)
    centred = pooled - u

    d = x_ref.shape[-1]
    for r0 in range(0, ts, MERGE_ROWS):
        rs = slice(r0, r0 + MERGE_ROWS)
        x = x_ref[0, rs, :]
        h = _rms(x, gpre_ref[...]) * (1.0 + sc_ref[0]) + sh_ref[0]
        y_pool = _mm(centred[rs], wp_ref[...]) * ps_ref[...]
        merged = (_sigmoid(_mm(h, wg_ref[:, :d])) * _mm(ya_ref[0, rs, :], pa_ref[...])
                  + _sigmoid(_mm(h, wg_ref[:, d:2 * d])) * _mm(ys_ref[rs, :], pb_ref[...])
                  + _sigmoid(_mm(h, wg_ref[:, 2 * d:])) * _mm(y_pool, pc_ref[...]))
        y = _mm(merged, wo_ref[...])
        o_ref[0, rs, :] = x + gt_ref[0] * _rms(y, gpost_ref[...])


def _merge_call(x, g_pre, sc, sh, gt, g_post, w_gate, y_attn, y_ssm_tm, u_pool, w_pool_bd, pool_scale,
                p_a, p_b, p_c, w_out, ts):
    B, S, D = x.shape
    row = lambda b, i: (b, i, 0)
    per_b = lambda b, i: (b, 0, 0)
    const2 = lambda b, i: (0, 0)
    hb = ts // POOL_HALO
    return pl.pallas_call(
        _merge_kernel,
        grid=(B, S // ts),
        in_specs=[pl.BlockSpec((1, ts, D), row),
                  pl.BlockSpec((1, D), const2),
                  pl.BlockSpec((1, 1, D), per_b),
                  pl.BlockSpec((1, 1, D), per_b),
                  pl.BlockSpec((1, 1, D), per_b),
                  pl.BlockSpec((1, D), const2),
                  pl.BlockSpec((D, N_BRANCH * D), const2),
                  pl.BlockSpec((1, ts, W_ATTN), row),
                  pl.BlockSpec((ts, W_SSM), lambda b, i: (i, b)),
                  pl.BlockSpec((1, ts, W_POOL), row),
                  pl.BlockSpec((1, POOL_HALO, W_POOL), lambda b, i: (b, jnp.maximum(i * hb - 1, 0), 0)),
                  pl.BlockSpec((W_POOL, W_POOL), const2),
                  pl.BlockSpec((1, W_POOL), const2),
                  pl.BlockSpec((W_ATTN, D), const2),
                  pl.BlockSpec((W_SSM, D), const2),
                  pl.BlockSpec((W_POOL, D), const2),
                  pl.BlockSpec((D, D), const2)],
        out_specs=pl.BlockSpec((1, ts, D), row),
        out_shape=jax.ShapeDtypeStruct((B, S, D), F32),
        scratch_shapes=[pltpu.VMEM((POOL_HALO + ts, W_POOL), F32)],
        compiler_params=_params("arbitrary", "arbitrary"),
        name="mixer_merge",
    )(x, g_pre, sc, sh, gt, g_post, w_gate, y_attn, y_ssm_tm, u_pool, u_pool, w_pool_bd, pool_scale,
      p_a, p_b, p_c, w_out)


def _ffn_kernel(x_ref, xh_ref, gpre_ref, sc_ref, sh_ref, gt_ref, gpost_ref, wup_ref, cw_ref, cb_ref,
                wdn_ref, o_ref, h_sc, u_sc, a_sc, *, cols, down_cols, ahead):
    i = pl.program_id(1)
    ts = x_ref.shape[1]
    dff = wdn_ref.shape[0]
    x = x_ref[0]
    adaln = lambda v: _rms(v, gpre_ref[...]) * (1.0 + sc_ref[0]) + sh_ref[0]
    h_sc[:CONV_HALO, :] = adaln(xh_ref[0]).astype(h_sc.dtype)
    h_sc[CONV_HALO:, :] = adaln(x).astype(h_sc.dtype)
    keep = jnp.where(i > 0, 1.0, 0.0)
    nchunk = dff // cols
    per_down = down_cols // cols
    col = lambda c, half: slice(half * dff + c * cols, half * dff + (c + 1) * cols)

    def up_project(c):
        for half in range(2):
            up = jnp.dot(h_sc[...], wup_ref[:, col(c, half)], preferred_element_type=F32)
            u_sc[half, :CONV_HALO, col(c, 0)] = up[:CONV_HALO] * keep
            u_sc[half, CONV_HALO:, col(c, 0)] = up[CONV_HALO:]

    def conv(c, half):
        acc = cb_ref[:, col(c, half)]
        for k in range(CONV_WIDTH):
            r0 = CONV_HALO - (CONV_WIDTH - 1) + k
            acc = acc + cw_ref[k:k + 1, col(c, half)] * u_sc[half, r0:r0 + ts, col(c, 0)]
        return acc

    y = jnp.zeros((ts, x.shape[-1]), F32)
    for c in range(min(ahead, nchunk)):
        up_project(c)
    for c in range(nchunk):
        if c + ahead < nchunk:
            up_project(c + ahead)
        a_sc[:, c * cols:(c + 1) * cols] = (_gelu(conv(c, 0)) * conv(c, 1)).astype(a_sc.dtype)
        if (c + 1) % per_down == 0:
            rows = slice((c + 1) * cols - down_cols, (c + 1) * cols)
            y = y + jnp.dot(a_sc[:, rows], wdn_ref[rows, :], preferred_element_type=F32)
    o_ref[0] = x + gt_ref[0] * _rms(y, gpost_ref[...])


def _ffn_call(x, g_pre, sc, sh, gt, g_post, w_up, conv_w, conv_b, w_down, ts, cols=256, down_cols=512, ahead=3):
    B, S, D = x.shape
    dff = w_down.shape[0]
    row = lambda b, i: (b, i, 0)
    per_b = lambda b, i: (b, 0, 0)
    const2 = lambda b, i: (0, 0)
    hb = ts // CONV_HALO
    return pl.pallas_call(
        functools.partial(_ffn_kernel, cols=cols, down_cols=down_cols, ahead=ahead),
        grid=(B, S // ts),
        in_specs=[pl.BlockSpec((1, ts, D), row),
                  pl.BlockSpec((1, CONV_HALO, D), lambda b, i: (b, jnp.maximum(i * hb - 1, 0), 0)),
                  pl.BlockSpec((1, D), const2),
                  pl.BlockSpec((1, 1, D), per_b),
                  pl.BlockSpec((1, 1, D), per_b),
                  pl.BlockSpec((1, 1, D), per_b),
                  pl.BlockSpec((1, D), const2),
                  pl.BlockSpec((D, 2 * dff), const2),
                  pl.BlockSpec((CONV_WIDTH, 2 * dff), const2),
                  pl.BlockSpec((1, 2 * dff), const2),
                  pl.BlockSpec((dff, D), const2)],
        out_specs=pl.BlockSpec((1, ts, D), row),
        out_shape=jax.ShapeDtypeStruct((B, S, D), F32),
        scratch_shapes=[pltpu.VMEM((CONV_HALO + ts, D), _MXU_DTYPE),
                        pltpu.VMEM((2, CONV_HALO + ts, dff), F32),
                        pltpu.VMEM((ts, dff), _MXU_DTYPE)],
        compiler_params=_params("arbitrary", "arbitrary"),
        name="conv_gated_ffn",
    )(x, x, g_pre, sc, sh, gt, g_post, w_up, conv_w, conv_b, w_down)


def _pick(n, pref):
    t = min(n, pref)
    assert n % t == 0, (n, t)
    return t


def kernel(x, c, mod_w, mod_b, mix_pre_g, mix_post_g, ffn_pre_g, ffn_post_g, w_in, g_cq, w_uq, w_qi, g_ckv, w_uv, a_re, a_im, b_re, b_im, c_re, c_im, d_skip, log_step, w_glu, b_glu, w_pool, pool_scale, p_a, p_b, p_c, w_out, w_up, conv_w, conv_b, w_down):
    B, S, D = x.shape
    depth = mod_w.shape[0]
    assert S % QB == 0 and QB % CHUNK == 0
    topk = min(DSA_TOPK_MAX, S // 4)
    ts = _pick(S, ROW_TILE)
    tc = _pick(S, SSM_TIME_CHUNK)
    cast = lambda w: w.astype(_MXU_DTYPE)
    row = lambda v: v.reshape(1, -1)

    mod = _mod_call(c, mod_w, mod_b)
    cuts = [0]
    for wdt in IN_SPLITS:
        cuts.append(cuts[-1] + wdt)
    eye_p = jnp.eye(POOL_GROUPS, dtype=F32)
    for l in range(depth):
        sh_m, sc_m, gt_m, sh_f, sc_f, gt_f = [mod[l][:, None, k * D:(k + 1) * D] for k in range(6)]
        wl = w_in[l]
        pad = jnp.zeros((D, MISC_W - IDX_DIM - IDX_HEADS), F32)
        w_small = cast(jnp.concatenate(
            [wl[:, cuts[0]:cuts[2]], wl[:, cuts[2]:cuts[4]], pad, wl[:, cuts[4]:cuts[6]]], axis=1))
        w_gate = cast(wl[:, cuts[6]:])
        cqn, ckvn, misc, u_ssm, u_pool = _inproj_call(
            x, row(mix_pre_g[l]), sc_m, sh_m, w_small, row(g_cq[l]), row(g_ckv[l]), ts)

        wqi = jnp.transpose(w_qi[l], (1, 2, 0))
        wqi = jnp.pad(wqi, ((0, 0), (0, LANES - IDX_DIM), (0, 0))).reshape(IDX_HEADS * LANES, D_QLAT)
        y_attn = _attn_call(cqn, ckvn, misc, cast(wqi), cast(jnp.transpose(w_uq[l], (1, 2, 0))),
                            cast(jnp.transpose(w_uv[l], (0, 2, 1))), topk)

        bd, lr, li, cd = _ssm_weights(a_re[l], a_im[l], b_re[l], b_im[l], c_re[l], c_im[l], log_step[l])
        y_ssm = _ssm_call(u_ssm.reshape(S * B, W_SSM), bd, lr, li, cd, row(d_skip[l]), cast(w_glu[l]),
                          row(b_glu[l]), B, tc).reshape(S, B * W_SSM)

        w_pool_bd = cast(jnp.einsum('gcd,gh->gchd', w_pool[l], eye_p).reshape(W_POOL, W_POOL))
        x = _merge_call(x, row(mix_pre_g[l]), sc_m, sh_m, gt_m, row(mix_post_g[l]), w_gate, y_attn, y_ssm,
                        u_pool, w_pool_bd, row(pool_scale[l]), cast(p_a[l]), cast(p_b[l]), cast(p_c[l]),
                        cast(w_out[l]), ts)
        x = _ffn_call(x, row(ffn_pre_g[l]), sc_f, sh_f, gt_f, row(ffn_post_g[l]), cast(w_up[l]), conv_w[l],
                      row(conv_b[l]), cast(w_down[l]), ts)
    return x
```

```python
import functools
import math

import jax
import jax.numpy as jnp
from jax import lax
from jax.experimental import pallas as pl
from jax.experimental.pallas import tpu as pltpu

F32 = jnp.float32
_MXU_DTYPE = jnp.bfloat16

CHUNK = 64
N_HEADS = 8
D_QLAT = 256
D_LAT = 128
D_VHEAD = 64
IDX_HEADS = 8
IDX_DIM = 32
DSA_TOPK_MAX = 256
SSM_GROUPS = 16
SSM_GROUP_DIM = 16
SSM_STATE = 64
W_SSM = SSM_GROUPS * SSM_GROUP_DIM
N_STATE = SSM_GROUPS * SSM_STATE
POOL_WINDOWS = (2, 4, 8, 16)
POOL_GROUPS = 4
POOL_GROUP_DIM = 64
W_POOL = POOL_GROUPS * POOL_GROUP_DIM
W_ATTN = N_HEADS * D_VHEAD
N_BRANCH = 3
CONV_WIDTH = 3
RMS_EPS = 1e-6
NEG_INF = -1e30
ATTN_SCALE = D_LAT ** -0.5
IDX_SCALE = IDX_DIM ** -0.5
IDX_HEAD_SCALE = IDX_HEADS ** -0.5
IN_SPLITS = (D_QLAT, D_LAT, IDX_DIM, IDX_HEADS, W_SSM, W_POOL)

LANES = 128
SUBLANES = 8
MISC_W = LANES
W_SMALL = D_QLAT + D_LAT + MISC_W + W_SSM + W_POOL
VMEM_LIMIT = 56 * 1024 * 1024

QB = 256
BISECT_PASSES = 13
SLOPE_PARTS = 3
DENOM_ROWS = 16
LOG2E = math.log2(math.e)
COUNT_ROWS = 4 * SUBLANES
WALK_ROWS = 2 * SUBLANES
POOL_HALO = 16
ROW_TILE = 512
SSM_TIME_CHUNK = 64
MERGE_ROWS = 256
SSM_ROW_GROUPS = 2
CONV_HALO = 16


def _mm(a, b):
    return jnp.dot(a.astype(_MXU_DTYPE), b.astype(_MXU_DTYPE), preferred_element_type=F32)


def _rms(x, g):
    return x * lax.rsqrt(jnp.mean(x * x, axis=-1, keepdims=True) + RMS_EPS) * g


def _gelu(x):
    return 0.5 * x * (1.0 + jnp.tanh(math.sqrt(2.0 / math.pi) * (x + 0.044715 * (x * x * x))))


def _sigmoid(x):
    return 1.0 / (1.0 + jnp.exp(-x))


def _params(*sem):
    return pltpu.CompilerParams(dimension_semantics=sem, vmem_limit_bytes=VMEM_LIMIT)


def _mod_kernel(c_ref, w_ref, b_ref, o_ref):
    c = c_ref[...]
    cond = c * _sigmoid(c)
    o_ref[0] = _mm(cond, w_ref[0]) + b_ref[0]


def _mod_call(c, mod_w, mod_b):
    L, D, D6 = mod_w.shape
    B = c.shape[0]
    nt = D6 // D
    return pl.pallas_call(
        _mod_kernel,
        grid=(L, nt),
        in_specs=[pl.BlockSpec((B, D), lambda l, n: (0, 0)),
                  pl.BlockSpec((1, D, D), lambda l, n: (l, 0, n)),
                  pl.BlockSpec((1, 1, D), lambda l, n: (l, 0, n))],
        out_specs=pl.BlockSpec((1, B, D), lambda l, n: (l, 0, n)),
        out_shape=jax.ShapeDtypeStruct((L, B, D6), F32),
        compiler_params=_params("arbitrary", "arbitrary"),
        name="adaln_mod",
    )(c, mod_w.astype(_MXU_DTYPE), mod_b.reshape(L, 1, D6))


def _inproj_kernel(x_ref, g_ref, sc_ref, sh_ref, w_ref, gcq_ref, gckv_ref,
                   cq_o, ckv_o, misc_o, ussm_o, upool_o):
    o0, o1, o2, o3 = D_QLAT, D_QLAT + D_LAT, D_QLAT + D_LAT + MISC_W, D_QLAT + D_LAT + MISC_W + W_SSM
    for r0 in range(0, x_ref.shape[1], MERGE_ROWS):
        rs = slice(r0, r0 + MERGE_ROWS)
        h = _rms(x_ref[0, rs, :], g_ref[...]) * (1.0 + sc_ref[0]) + sh_ref[0]
        z = _mm(h, w_ref[...])
        cq_o[0, rs, :] = _rms(z[:, :o0], gcq_ref[...]).astype(cq_o.dtype)
        ckv_o[0, rs, :] = _rms(z[:, o0:o1], gckv_ref[...]).astype(ckv_o.dtype)
        misc_o[0, rs, :] = z[:, o1:o2]
        ussm_o[rs, :] = z[:, o2:o3]
        upool_o[0, rs, :] = z[:, o3:]


def _inproj_call(x, g_pre, sc, sh, w_small, g_cq, g_ckv, ts):
    B, S, D = x.shape
    row = lambda b, i: (b, i, 0)
    per_b = lambda b, i: (b, 0, 0)
    const2 = lambda b, i: (0, 0)
    return pl.pallas_call(
        _inproj_kernel,
        grid=(B, S // ts),
        in_specs=[pl.BlockSpec((1, ts, D), row),
                  pl.BlockSpec((1, D), const2),
                  pl.BlockSpec((1, 1, D), per_b),
                  pl.BlockSpec((1, 1, D), per_b),
                  pl.BlockSpec((D, W_SMALL), const2),
                  pl.BlockSpec((1, D_QLAT), const2),
                  pl.BlockSpec((1, D_LAT), const2)],
        out_specs=[pl.BlockSpec((1, ts, D_QLAT), row),
                   pl.BlockSpec((1, ts, D_LAT), row),
                   pl.BlockSpec((1, ts, MISC_W), row),
                   pl.BlockSpec((ts, W_SSM), lambda b, i: (i, b)),
                   pl.BlockSpec((1, ts, W_POOL), row)],
        out_shape=[jax.ShapeDtypeStruct((B, S, D_QLAT), _MXU_DTYPE),
                   jax.ShapeDtypeStruct((B, S, D_LAT), _MXU_DTYPE),
                   jax.ShapeDtypeStruct((B, S, MISC_W), F32),
                   jax.ShapeDtypeStruct((S, B * W_SSM), F32),
                   jax.ShapeDtypeStruct((B, S, W_POOL), F32)],
        compiler_params=_params("arbitrary", "arbitrary"),
        name="in_projection",
    )(x, g_pre, sc, sh, w_small, g_cq, g_ckv)


def _sortable(i):
    return i ^ ((i >> 31) & jnp.int32(0x7FFFFFFF))


def _attn_kernel(slope_ref, cq_ref, ckv_ref, misc_ref, wqi_ref, wuq_ref, wuv_ref, o_ref,
                 kaug_sc, ckvt_sc, qi_sc, qt_sc, score_sc, lg_sc, tmax_sc, m_sc, acc_sc, yt_sc, *, topk):
    j = pl.program_id(1)
    nkt = ckvt_sc.shape[0]
    q0 = pl.multiple_of(j * QB, QB)

    @pl.when((pl.program_id(0) == 0) & (j == 0))
    def _():
        lane = lax.broadcasted_iota(jnp.int32, (QB, D_LAT), 1)
        s_in = lax.broadcasted_iota(jnp.int32, (QB, D_LAT), 0).astype(F32)
        for kt in range(nkt):
            pos = jnp.where(lane < SLOPE_PARTS, s_in, jnp.where(lane < 2 * SLOPE_PARTS, float(kt), 0.0))
            kaug_sc[kt * QB:(kt + 1) * QB, D_LAT:] = pos.astype(kaug_sc.dtype)
        sub = lax.broadcasted_iota(jnp.int32, (D_LAT, QB), 0)
        for h in range(N_HEADS):
            rows = jnp.zeros((D_LAT, QB), F32)
            for part in range(SLOPE_PARTS):
                rows = jnp.where(sub == part, slope_ref[h, part], rows)
                rows = jnp.where(sub == SLOPE_PARTS + part, slope_ref[h, part] * QB, rows)
            qt_sc[h, D_LAT:, :] = rows.astype(qt_sc.dtype)
        ones_row = jnp.where(lax.broadcasted_iota(jnp.int32, (DENOM_ROWS, QB), 0) == 0, 1.0, 0.0)
        for kt in range(nkt):
            ckvt_sc[kt, D_LAT:, :] = ones_row.astype(ckvt_sc.dtype)

    @pl.when(j == 0)
    def _():
        for kt in range(nkt):
            kv = ckv_ref[0, kt * QB:(kt + 1) * QB, :]
            ckvt_sc[kt, :D_LAT, :] = kv.astype(F32).T.astype(ckvt_sc.dtype)
            kaug_sc[kt * QB:(kt + 1) * QB, :D_LAT] = kv

    cqt = cq_ref[0].astype(F32).T.astype(_MXU_DTYPE)
    misct = misc_ref[0, pl.ds(q0, QB), :].T
    qi_sc[...] = _mm(wqi_ref[...], cqt).astype(qi_sc.dtype)
    for h in range(N_HEADS):
        qt_sc[h, :D_LAT, :] = (_mm(wuq_ref[h], cqt) * (ATTN_SCALE * LOG2E)).astype(qt_sc.dtype)

    kl = lax.broadcasted_iota(jnp.int32, (QB, QB), 0)
    ql = lax.broadcasted_iota(jnp.int32, (QB, QB), 1)
    diag_ok = kl < (ql // CHUNK + 1) * CHUNK
    ahead = 2.0 * jnp.maximum(kl - ql, 0).astype(F32)

    def score_tile(kt):
        k0 = pl.multiple_of(kt * QB, QB)
        kmat = misc_ref[0, pl.ds(k0, QB), :].astype(_MXU_DTYPE)
        acc = jnp.zeros((QB, QB), F32)
        for h in range(IDX_HEADS):
            lg = jnp.dot(kmat, qi_sc[h * LANES:(h + 1) * LANES, :], preferred_element_type=F32)
            wq = misct[IDX_DIM + h:IDX_DIM + h + 1, :] * (IDX_SCALE * IDX_HEAD_SCALE)
            acc = acc + jnp.maximum(lg, 0.0) * wq
        return acc

    def score_body(kt, carry):
        cmin, cmax = carry
        s = score_tile(kt)
        score_sc[pl.ds(pl.multiple_of(kt * QB, QB), QB), :] = s
        return (jnp.minimum(cmin, jnp.min(s, axis=0, keepdims=True)),
                jnp.maximum(cmax, jnp.max(s, axis=0, keepdims=True)))

    big = jnp.full((1, QB), 3.0e38, F32)
    cmin, cmax = lax.fori_loop(0, j, score_body, (big, -big))
    s = score_tile(j)
    cmin = jnp.minimum(cmin, jnp.min(jnp.where(diag_ok, s, 3.0e38), axis=0, keepdims=True))
    cmax = jnp.maximum(cmax, jnp.max(jnp.where(diag_ok, s, -3.0e38), axis=0, keepdims=True))
    score_sc[pl.ds(q0, QB), :] = jnp.where(diag_ok, s, -jnp.inf)

    def rows_reduce(x, op, rows):
        return op(x.reshape(QB // rows, rows, QB), axis=0)

    def count(pred, thr):
        def body(kt, acc):
            t = score_sc[pl.ds(pl.multiple_of(kt * QB, QB), QB), :]
            return acc + rows_reduce(jnp.where(pred(t, thr), 1.0, 0.0), jnp.sum, COUNT_ROWS)
        acc = lax.fori_loop(0, j + 1, body, jnp.zeros((COUNT_ROWS, QB), F32))
        return jnp.sum(acc, axis=0, keepdims=True)

    def count_and_next(v):
        def body(kt, carry):
            acc, below = carry
            t = score_sc[pl.ds(pl.multiple_of(kt * QB, QB), QB), :]
            hit = t >= v
            return (acc + rows_reduce(jnp.where(hit, 1.0, 0.0), jnp.sum, WALK_ROWS),
                    jnp.maximum(below, rows_reduce(jnp.where(hit, -jnp.inf, t), jnp.max, WALK_ROWS)))
        init = (jnp.zeros((WALK_ROWS, QB), F32), jnp.full((WALK_ROWS, QB), -jnp.inf, F32))
        acc, below = lax.fori_loop(0, j + 1, body, init)
        return jnp.sum(acc, axis=0, keepdims=True), jnp.max(below, axis=0, keepdims=True)

    def max_below(v):
        def body(kt, below):
            t = score_sc[pl.ds(pl.multiple_of(kt * QB, QB), QB), :]
            return jnp.maximum(below, rows_reduce(jnp.where(t < v, t, -jnp.inf), jnp.max, COUNT_ROWS))
        below = lax.fori_loop(0, j + 1, body, jnp.full((COUNT_ROWS, QB), -jnp.inf, F32))
        return jnp.max(below, axis=0, keepdims=True)

    ge = lambda t, thr: t >= thr
    gt = lambda t, thr: t > thr

    @pl.when((j + 1) * QB > topk)
    def _():
        kf = jnp.float32(topk)

        def any_lane(flag):
            f = jnp.where(flag, 1.0, 0.0)
            parts = [f[:, k * LANES:(k + 1) * LANES] for k in range(QB // LANES)]
            return jnp.max(functools.reduce(jnp.maximum, parts)) > 0.0

        above_max = pltpu.bitcast(_sortable(_sortable(pltpu.bitcast(cmax, jnp.int32)) + 1), F32)

        def halve(_, c):
            lo, hi = c
            mid = 0.5 * lo + 0.5 * hi
            keep_low = count(ge, mid) >= kf
            return jnp.where(keep_low, mid, lo), jnp.where(keep_low, hi, mid)

        _, hi = lax.fori_loop(0, BISECT_PASSES, halve, (cmin, above_max))

        def unresolved(c):
            return any_lane(c[1] == 0)

        def step(c):
            v, done, thr, cnt_ge = c
            cnt, below = count_and_next(v)
            hit = (done == 0) & ((cnt >= kf) | (v <= cmin))
            thr = jnp.where(hit, v, thr)
            cnt_ge = jnp.where(hit, cnt, cnt_ge)
            done = jnp.where(hit, 1, done)
            return jnp.where(done > 0, v, jnp.maximum(below, cmin)), done, thr, cnt_ge

        zero = jnp.zeros((1, QB), F32)
        first = jnp.maximum(max_below(hi), cmin)
        _, _, thr, cnt_ge = lax.while_loop(unresolved, step, (first, jnp.zeros((1, QB), jnp.int32), zero, zero))
        ties = any_lane(cnt_ge > kf)

        @pl.when(jnp.logical_not(ties))
        def _():
            def body(kt, _):
                r = pl.ds(pl.multiple_of(kt * QB, QB), QB)
                score_sc[r, :] = jnp.where(score_sc[r, :] >= thr, 0.0, NEG_INF)
                return 0
            lax.fori_loop(0, j + 1, body, 0)

        @pl.when(ties)
        def _():
            need = kf - count(gt, thr)
            tri = (lax.broadcasted_iota(jnp.int32, (QB, QB), 0)
                   >= lax.broadcasted_iota(jnp.int32, (QB, QB), 1)).astype(_MXU_DTYPE)

            def body(kt, seen):
                r = pl.ds(pl.multiple_of(kt * QB, QB), QB)
                t = score_sc[r, :]
                eq = jnp.where(t == thr, 1.0, 0.0)
                rank = seen + jnp.dot(tri, eq.astype(_MXU_DTYPE), preferred_element_type=F32)
                sel = (t > thr) | ((t == thr) & (rank <= need))
                score_sc[r, :] = jnp.where(sel, 0.0, NEG_INF)
                return seen + jnp.sum(eq, axis=0, keepdims=True)
            lax.fori_loop(0, j + 1, body, jnp.zeros((1, QB), F32))

    @pl.when((j + 1) * QB <= topk)
    def _():
        def body(kt, _):
            r = pl.ds(pl.multiple_of(kt * QB, QB), QB)
            score_sc[r, :] = jnp.where(score_sc[r, :] > -jnp.inf, 0.0, NEG_INF)
            return 0
        lax.fori_loop(0, j + 1, body, 0)

    m_sc[...] = jnp.full(m_sc.shape, -3.0e38, F32)
    acc_sc[...] = jnp.zeros(acc_sc.shape, F32)

    def qk_stage(kt, h, ahead_of_query):
        k0 = pl.multiple_of(kt * QB, QB)
        lg = score_sc[pl.ds(k0, QB), :] + jnp.dot(kaug_sc[pl.ds(k0, QB), :], qt_sc[h], preferred_element_type=F32)
        if ahead_of_query is not None:
            lg = lg - slope_ref[h, SLOPE_PARTS] * ahead_of_query
        lg_sc[h] = lg
        tmax_sc[h] = jnp.max(lg, axis=0, keepdims=True)

    def pv_stage(kt, h):
        m = m_sc[h]
        m_new = jnp.maximum(m, tmax_sc[h])
        alpha = jnp.exp2(m - m_new)
        m_sc[h] = m_new
        p = jnp.exp2(lg_sc[h] - m_new)
        acc_sc[h] = alpha * acc_sc[h] + jnp.dot(ckvt_sc[kt], p.astype(_MXU_DTYPE), preferred_element_type=F32)

    @pl.when(j == 0)
    def _():
        for h in range(N_HEADS):
            qk_stage(j, h, ahead)

    @pl.when(j > 0)
    def _():
        for h in range(N_HEADS):
            qk_stage(0, h, None)

        def steady(kt, _):
            for h in range(N_HEADS):
                pv_stage(kt, h)
                qk_stage(kt + 1, h, None)
            return 0

        lax.fori_loop(0, j - 1, steady, 0)
        for h in range(N_HEADS):
            pv_stage(j - 1, h)
            qk_stage(j, h, ahead)

    for h in range(N_HEADS):
        pv_stage(j, h)
    for h in range(N_HEADS):
        out = acc_sc[h, :D_LAT, :] / acc_sc[h, D_LAT:D_LAT + 1, :]
        yt_sc[h * D_VHEAD:(h + 1) * D_VHEAD, :] = _mm(wuv_ref[h], out)
    o_ref[0] = yt_sc[...].T.astype(o_ref.dtype)


def _attn_call(cqn, ckvn, misc, wqi, wuq, wuv, topk):
    B, S, _ = cqn.shape
    nkt = S // QB
    slope = jnp.exp2(-8.0 * jnp.arange(1, N_HEADS + 1, dtype=F32) / N_HEADS) * LOG2E
    parts, rest = [], slope
    for _ in range(SLOPE_PARTS):
        piece = rest.astype(_MXU_DTYPE).astype(F32)
        parts.append(piece)
        rest = rest - piece
    slopes = jnp.stack(parts + [slope], axis=1)
    per_b = lambda b, j: (b, 0, 0)
    const2 = lambda b, j: (0, 0)
    const3 = lambda b, j: (0, 0, 0)
    return pl.pallas_call(
        functools.partial(_attn_kernel, topk=topk),
        grid=(B, nkt),
        in_specs=[pl.BlockSpec(memory_space=pltpu.SMEM),
                  pl.BlockSpec((1, QB, D_QLAT), lambda b, j: (b, j, 0)),
                  pl.BlockSpec((1, S, D_LAT), per_b),
                  pl.BlockSpec((1, S, MISC_W), per_b),
                  pl.BlockSpec((IDX_HEADS * LANES, D_QLAT), const2),
                  pl.BlockSpec((N_HEADS, D_LAT, D_QLAT), const3),
                  pl.BlockSpec((N_HEADS, D_VHEAD, D_LAT), const3)],
        out_specs=pl.BlockSpec((1, QB, W_ATTN), lambda b, j: (b, j, 0)),
        out_shape=jax.ShapeDtypeStruct((B, S, W_ATTN), _MXU_DTYPE),
        scratch_shapes=[pltpu.VMEM((S, 2 * D_LAT), _MXU_DTYPE),
                        pltpu.VMEM((nkt, D_LAT + DENOM_ROWS, QB), _MXU_DTYPE),
                        pltpu.VMEM((IDX_HEADS * LANES, QB), _MXU_DTYPE),
                        pltpu.VMEM((N_HEADS, 2 * D_LAT, QB), _MXU_DTYPE),
                        pltpu.VMEM((S, QB), F32),
                        pltpu.VMEM((N_HEADS, QB, QB), F32),
                        pltpu.VMEM((N_HEADS, 1, QB), F32),
                        pltpu.VMEM((N_HEADS, 1, QB), F32),
                        pltpu.VMEM((N_HEADS, D_LAT + DENOM_ROWS, QB), F32),
                        pltpu.VMEM((W_ATTN, QB), F32)],
        compiler_params=_params("arbitrary", "arbitrary"),
        name="dsa_attention",
    )(slopes, cqn, ckvn, misc, wqi, wuq, wuv)


def _ssm_kernel(u_ref, bd_ref, lr_ref, li_ref, cd_ref, d_ref, wg_ref, bg_ref, o_ref, x_sc, bu_sc, *, nb, cw):
    tc = u_ref.shape[0] // nb

    @pl.when(pl.program_id(0) == 0)
    def _():
        x_sc[...] = jnp.zeros_like(x_sc)

    nrow = u_ref.shape[0]
    groups = [slice(g * nrow // SSM_ROW_GROUPS, (g + 1) * nrow // SSM_ROW_GROUPS) for g in range(SSM_ROW_GROUPS)]
    for rs in groups:
        bu_sc[rs, :] = _mm(u_ref[rs, :], bd_ref[...])
    for c in range(N_STATE // cw):
        re = slice(c * cw, (c + 1) * cw)
        im = slice(N_STATE + c * cw, N_STATE + (c + 1) * cw)
        lr = lr_ref[:, re]
        li = li_ref[:, re]

        def step(t, carry, re=re, im=im, lr=lr, li=li):
            xr, xi = carry
            rows = pl.ds(pl.multiple_of(t * nb, nb), nb)
            nr = lr * xr - li * xi + bu_sc[rows, re]
            ni = lr * xi + li * xr + bu_sc[rows, im]
            bu_sc[rows, re] = nr
            bu_sc[rows, im] = ni
            return nr, ni

        xr, xi = lax.fori_loop(0, tc, step, (x_sc[:, re], x_sc[:, im]), unroll=4)
        x_sc[:, re] = xr
        x_sc[:, im] = xi
    n_tiles = W_SSM // LANES
    span = N_STATE // n_tiles
    for rs in groups:
        cols = []
        for k in range(n_tiles):
            out = slice(k * LANES, (k + 1) * LANES)
            re = slice(k * span, (k + 1) * span)
            im = slice(N_STATE + k * span, N_STATE + (k + 1) * span)
            cols.append(_mm(bu_sc[rs, re], cd_ref[re, out]) + _mm(bu_sc[rs, im], cd_ref[im, out]))
        y = jnp.concatenate(cols, axis=1) + d_ref[...] * u_ref[rs, :]
        z = _gelu(y)
        o_ref[rs, :] = (z * _sigmoid(_mm(z, wg_ref[...]) + bg_ref[...])).astype(o_ref.dtype)


def _ssm_call(u_tm, bd, lr, li, cd, dskip, w_glu, b_glu, nb, tc):
    rows = u_tm.shape[0]
    const2 = lambda i: (0, 0)
    return pl.pallas_call(
        functools.partial(_ssm_kernel, nb=nb, cw=256),
        grid=(rows // (tc * nb),),
        in_specs=[pl.BlockSpec((tc * nb, W_SSM), lambda i: (i, 0)),
                  pl.BlockSpec((W_SSM, 2 * N_STATE), const2),
                  pl.BlockSpec((1, N_STATE), const2),
                  pl.BlockSpec((1, N_STATE), const2),
                  pl.BlockSpec((2 * N_STATE, W_SSM), const2),
                  pl.BlockSpec((1, W_SSM), const2),
                  pl.BlockSpec((W_SSM, W_SSM), const2),
                  pl.BlockSpec((1, W_SSM), const2)],
        out_specs=pl.BlockSpec((tc * nb, W_SSM), lambda i: (i, 0)),
        out_shape=jax.ShapeDtypeStruct((rows, W_SSM), _MXU_DTYPE),
        scratch_shapes=[pltpu.VMEM((nb, 2 * N_STATE), F32),
                        pltpu.VMEM((tc * nb, 2 * N_STATE), F32)],
        compiler_params=_params("arbitrary"),
        name="s5_ssm",
    )(u_tm, bd, lr, li, cd, dskip, w_glu, b_glu)


def _ssm_weights(a_re, a_im, b_re, b_im, c_re, c_im, log_step):
    step = jnp.exp(log_step)[:, None]
    er = jnp.exp(a_re * step)
    ang = a_im * step
    lr, li = er * jnp.cos(ang), er * jnp.sin(ang)
    den = a_re * a_re + a_im * a_im
    fr = ((lr - 1.0) * a_re + li * a_im) / den
    fi = (li * a_re - (lr - 1.0) * a_im) / den
    br = fr[:, :, None] * b_re - fi[:, :, None] * b_im
    bi = fr[:, :, None] * b_im + fi[:, :, None] * b_re
    eye = jnp.eye(SSM_GROUPS, dtype=F32)
    pack_b = lambda m: jnp.einsum('gnp,gh->gphn', m, eye).reshape(W_SSM, N_STATE)
    pack_c = lambda m: jnp.einsum('gpn,gh->gnhp', m, eye).reshape(N_STATE, W_SSM)
    bd = jnp.concatenate([pack_b(br), pack_b(bi)], axis=1)
    cd = jnp.concatenate([pack_c(c_re), pack_c(-c_im)], axis=0)
    return (bd.astype(_MXU_DTYPE), lr.reshape(1, N_STATE), li.reshape(1, N_STATE), cd.astype(_MXU_DTYPE))


def _merge_kernel(x_ref, gpre_ref, sc_ref, sh_ref, gt_ref, gpost_ref, wg_ref, ya_ref, ys_ref,
                  up_ref, uh_ref, wp_ref, ps_ref, pa_ref, pb_ref, pc_ref, wo_ref, o_ref, halo_sc):
    i = pl.program_id(1)
    ts = x_ref.shape[1]

    u = up_ref[0]
    halo_sc[:POOL_HALO, :] = jnp.where(i > 0, uh_ref[0], 0.0)
    halo_sc[POOL_HALO:, :] = u
    lane = lax.broadcasted_iota(jnp.int32, (ts, W_POOL), 1)
    tpos = (i * ts + lax.broadcasted_iota(jnp.int32, (ts, W_POOL), 0) + 1).astype(F32)
    run = u
    pooled = jnp.zeros_like(u)
    prev = 1
    for g, win in enumerate(POOL_WINDOWS):
        for k in range(prev, win):
            run = run + halo_sc[POOL_HALO - k:POOL_HALO - k + ts, :]
        prev = win
        in_group = (lane >= g * POOL_GROUP_DIM) & (lane < (g + 1) * POOL_GROUP_DIM)
        pooled = jnp.where(in_group, run / jnp.minimum(tpos, float(win)), pooled)
    centred = pooled - u

    d = x_ref.shape[-1]
    for r0 in range(0, ts, MERGE_ROWS):
        rs = slice(r0, r0 + MERGE_ROWS)
        x = x_ref[0, rs, :]
        h = _rms(x, gpre_ref[...]) * (1.0 + sc_ref[0]) + sh_ref[0]
        y_pool = _mm(centred[rs], wp_ref[...]) * ps_ref[...]
        merged = (_sigmoid(_mm(h, wg_ref[:, :d])) * _mm(ya_ref[0, rs, :], pa_ref[...])
                  + _sigmoid(_mm(h, wg_ref[:, d:2 * d])) * _mm(ys_ref[rs, :], pb_ref[...])
                  + _sigmoid(_mm(h, wg_ref[:, 2 * d:])) * _mm(y_pool, pc_ref[...]))
        y = _mm(merged, wo_ref[...])
        o_ref[0, rs, :] = x + gt_ref[0] * _rms(y, gpost_ref[...])


def _merge_call(x, g_pre, sc, sh, gt, g_post, w_gate, y_attn, y_ssm_tm, u_pool, w_pool_bd, pool_scale,
                p_a, p_b, p_c, w_out, ts):
    B, S, D = x.shape
    row = lambda b, i: (b, i, 0)
    per_b = lambda b, i: (b, 0, 0)
    const2 = lambda b, i: (0, 0)
    hb = ts // POOL_HALO
    return pl.pallas_call(
        _merge_kernel,
        grid=(B, S // ts),
        in_specs=[pl.BlockSpec((1, ts, D), row),
                  pl.BlockSpec((1, D), const2),
                  pl.BlockSpec((1, 1, D), per_b),
                  pl.BlockSpec((1, 1, D), per_b),
                  pl.BlockSpec((1, 1, D), per_b),
                  pl.BlockSpec((1, D), const2),
                  pl.BlockSpec((D, N_BRANCH * D), const2),
                  pl.BlockSpec((1, ts, W_ATTN), row),
                  pl.BlockSpec((ts, W_SSM), lambda b, i: (i, b)),
                  pl.BlockSpec((1, ts, W_POOL), row),
                  pl.BlockSpec((1, POOL_HALO, W_POOL), lambda b, i: (b, jnp.maximum(i * hb - 1, 0), 0)),
                  pl.BlockSpec((W_POOL, W_POOL), const2),
                  pl.BlockSpec((1, W_POOL), const2),
                  pl.BlockSpec((W_ATTN, D), const2),
                  pl.BlockSpec((W_SSM, D), const2),
                  pl.BlockSpec((W_POOL, D), const2),
                  pl.BlockSpec((D, D), const2)],
        out_specs=pl.BlockSpec((1, ts, D), row),
        out_shape=jax.ShapeDtypeStruct((B, S, D), F32),
        scratch_shapes=[pltpu.VMEM((POOL_HALO + ts, W_POOL), F32)],
        compiler_params=_params("arbitrary", "arbitrary"),
        name="mixer_merge",
    )(x, g_pre, sc, sh, gt, g_post, w_gate, y_attn, y_ssm_tm, u_pool, u_pool, w_pool_bd, pool_scale,
      p_a, p_b, p_c, w_out)


def _ffn_kernel(x_ref, xh_ref, gpre_ref, sc_ref, sh_ref, gt_ref, gpost_ref, wup_ref, cw_ref, cb_ref,
                wdn_ref, o_ref, h_sc, u_sc, a_sc, *, cols, down_cols, ahead):
    i = pl.program_id(1)
    ts = x_ref.shape[1]
    dff = wdn_ref.shape[0]
    x = x_ref[0]
    adaln = lambda v: _rms(v, gpre_ref[...]) * (1.0 + sc_ref[0]) + sh_ref[0]
    h_sc[:CONV_HALO, :] = adaln(xh_ref[0]).astype(h_sc.dtype)
    h_sc[CONV_HALO:, :] = adaln(x).astype(h_sc.dtype)
    keep = jnp.where(i > 0, 1.0, 0.0)
    nchunk = dff // cols
    per_down = down_cols // cols
    col = lambda c, half: slice(half * dff + c * cols, half * dff + (c + 1) * cols)

    def up_project(c):
        for half in range(2):
            up = jnp.dot(h_sc[...], wup_ref[:, col(c, half)], preferred_element_type=F32)
            u_sc[half, :CONV_HALO, col(c, 0)] = up[:CONV_HALO] * keep
            u_sc[half, CONV_HALO:, col(c, 0)] = up[CONV_HALO:]

    def conv(c, half):
        acc = cb_ref[:, col(c, half)]
        u = u_sc[half, :, col(c, 0)]
        for k in range(CONV_WIDTH):
            lag = CONV_WIDTH - 1 - k
            past = pltpu.roll(u, lag, axis=0) if lag else u
            acc = acc + cw_ref[k:k + 1, col(c, half)] * past[CONV_HALO:, :]
        return acc

    y = jnp.zeros((ts, x.shape[-1]), F32)
    for c in range(min(ahead, nchunk)):
        up_project(c)
    for c in range(nchunk):
        if c + ahead < nchunk:
            up_project(c + ahead)
        a_sc[:, c * cols:(c + 1) * cols] = (_gelu(conv(c, 0)) * conv(c, 1)).astype(a_sc.dtype)
        if (c + 1) % per_down == 0:
            rows = slice((c + 1) * cols - down_cols, (c + 1) * cols)
            y = y + jnp.dot(a_sc[:, rows], wdn_ref[rows, :], preferred_element_type=F32)
    o_ref[0] = x + gt_ref[0] * _rms(y, gpost_ref[...])


def _ffn_call(x, g_pre, sc, sh, gt, g_post, w_up, conv_w, conv_b, w_down, ts, cols=256, down_cols=512, ahead=3):
    B, S, D = x.shape
    dff = w_down.shape[0]
    row = lambda b, i: (b, i, 0)
    per_b = lambda b, i: (b, 0, 0)
    const2 = lambda b, i: (0, 0)
    hb = ts // CONV_HALO
    return pl.pallas_call(
        functools.partial(_ffn_kernel, cols=cols, down_cols=down_cols, ahead=ahead),
        grid=(B, S // ts),
        in_specs=[pl.BlockSpec((1, ts, D), row),
                  pl.BlockSpec((1, CONV_HALO, D), lambda b, i: (b, jnp.maximum(i * hb - 1, 0), 0)),
                  pl.BlockSpec((1, D), const2),
                  pl.BlockSpec((1, 1, D), per_b),
                  pl.BlockSpec((1, 1, D), per_b),
                  pl.BlockSpec((1, 1, D), per_b),
                  pl.BlockSpec((1, D), const2),
                  pl.BlockSpec((D, 2 * dff), const2),
                  pl.BlockSpec((CONV_WIDTH, 2 * dff), const2),
                  pl.BlockSpec((1, 2 * dff), const2),
                  pl.BlockSpec((dff, D), const2)],
        out_specs=pl.BlockSpec((1, ts, D), row),
        out_shape=jax.ShapeDtypeStruct((B, S, D), F32),
        scratch_shapes=[pltpu.VMEM((CONV_HALO + ts, D), _MXU_DTYPE),
                        pltpu.VMEM((2, CONV_HALO + ts, dff), F32),
                        pltpu.VMEM((ts, dff), _MXU_DTYPE)],
        compiler_params=_params("arbitrary", "arbitrary"),
        name="conv_gated_ffn",
    )(x, x, g_pre, sc, sh, gt, g_post, w_up, conv_w, conv_b, w_down)


def _pick(n, pref):
    t = min(n, pref)
    assert n % t == 0, (n, t)
    return t


def kernel(x, c, mod_w, mod_b, mix_pre_g, mix_post_g, ffn_pre_g, ffn_post_g, w_in, g_cq, w_uq, w_qi, g_ckv, w_uv, a_re, a_im, b_re, b_im, c_re, c_im, d_skip, log_step, w_glu, b_glu, w_pool, pool_scale, p_a, p_b, p_c, w_out, w_up, conv_w, conv_b, w_down):
    B, S, D = x.shape
    depth = mod_w.shape[0]
    assert S % QB == 0 and QB % CHUNK == 0
    topk = min(DSA_TOPK_MAX, S // 4)
    ts = _pick(S, ROW_TILE)
    tc = _pick(S, SSM_TIME_CHUNK)
    cast = lambda w: w.astype(_MXU_DTYPE)
    row = lambda v: v.reshape(1, -1)

    mod = _mod_call(c, mod_w, mod_b)
    cuts = [0]
    for wdt in IN_SPLITS:
        cuts.append(cuts[-1] + wdt)
    eye_p = jnp.eye(POOL_GROUPS, dtype=F32)
    for l in range(depth):
        sh_m, sc_m, gt_m, sh_f, sc_f, gt_f = [mod[l][:, None, k * D:(k + 1) * D] for k in range(6)]
        wl = w_in[l]
        pad = jnp.zeros((D, MISC_W - IDX_DIM - IDX_HEADS), F32)
        w_small = cast(jnp.concatenate(
            [wl[:, cuts[0]:cuts[2]], wl[:, cuts[2]:cuts[4]], pad, wl[:, cuts[4]:cuts[6]]], axis=1))
        w_gate = cast(wl[:, cuts[6]:])
        cqn, ckvn, misc, u_ssm, u_pool = _inproj_call(
            x, row(mix_pre_g[l]), sc_m, sh_m, w_small, row(g_cq[l]), row(g_ckv[l]), ts)

        wqi = jnp.transpose(w_qi[l], (1, 2, 0))
        wqi = jnp.pad(wqi, ((0, 0), (0, LANES - IDX_DIM), (0, 0))).reshape(IDX_HEADS * LANES, D_QLAT)
        y_attn = _attn_call(cqn, ckvn, misc, cast(wqi), cast(jnp.transpose(w_uq[l], (1, 2, 0))),
                            cast(jnp.transpose(w_uv[l], (0, 2, 1))), topk)

        bd, lr, li, cd = _ssm_weights(a_re[l], a_im[l], b_re[l], b_im[l], c_re[l], c_im[l], log_step[l])
        y_ssm = _ssm_call(u_ssm.reshape(S * B, W_SSM), bd, lr, li, cd, row(d_skip[l]), cast(w_glu[l]),
                          row(b_glu[l]), B, tc).reshape(S, B * W_SSM)

        w_pool_bd = cast(jnp.einsum('gcd,gh->gchd', w_pool[l], eye_p).reshape(W_POOL, W_POOL))
        x = _merge_call(x, row(mix_pre_g[l]), sc_m, sh_m, gt_m, row(mix_post_g[l]), w_gate, y_attn, y_ssm,
                        u_pool, w_pool_bd, row(pool_scale[l]), cast(p_a[l]), cast(p_b[l]), cast(p_c[l]),
                        cast(w_out[l]), ts)
        x = _ffn_call(x, row(ffn_pre_g[l]), sc_f, sh_f, gt_f, row(ffn_post_g[l]), cast(w_up[l]), conv_w[l],
                      row(conv_b[l]), cast(w_down[l]), ts)
    return x
```

```python
import functools
import math

import jax
import jax.numpy as jnp
from jax import lax
from jax.experimental import pallas as pl
from jax.experimental.pallas import tpu as pltpu

F32 = jnp.float32
_MXU_DTYPE = jnp.bfloat16

CHUNK = 64
N_HEADS = 8
D_QLAT = 256
D_LAT = 128
D_VHEAD = 64
IDX_HEADS = 8
IDX_DIM = 32
DSA_TOPK_MAX = 256
SSM_GROUPS = 16
SSM_GROUP_DIM = 16
SSM_STATE = 64
W_SSM = SSM_GROUPS * SSM_GROUP_DIM
N_STATE = SSM_GROUPS * SSM_STATE
POOL_WINDOWS = (2, 4, 8, 16)
POOL_GROUPS = 4
POOL_GROUP_DIM = 64
W_POOL = POOL_GROUPS * POOL_GROUP_DIM
W_ATTN = N_HEADS * D_VHEAD
N_BRANCH = 3
CONV_WIDTH = 3
RMS_EPS = 1e-6
NEG_INF = -1e30
ATTN_SCALE = D_LAT ** -0.5
IDX_SCALE = IDX_DIM ** -0.5
IDX_HEAD_SCALE = IDX_HEADS ** -0.5
IN_SPLITS = (D_QLAT, D_LAT, IDX_DIM, IDX_HEADS, W_SSM, W_POOL)

LANES = 128
SUBLANES = 8
MISC_W = LANES
W_SMALL = D_QLAT + D_LAT + MISC_W + W_SSM + W_POOL
VMEM_LIMIT = 56 * 1024 * 1024

QB = 256
BISECT_PASSES = 13
SLOPE_PARTS = 3
DENOM_ROWS = 16
LOG2E = math.log2(math.e)
COUNT_ROWS = 4 * SUBLANES
WALK_ROWS = 2 * SUBLANES
POOL_HALO = 16
ROW_TILE = 512
SSM_TIME_CHUNK = 64
MERGE_ROWS = 256
SSM_ROW_GROUPS = 2
CONV_HALO = 16


def _mm(a, b):
    return jnp.dot(a.astype(_MXU_DTYPE), b.astype(_MXU_DTYPE), preferred_element_type=F32)


def _rms(x, g):
    return x * lax.rsqrt(jnp.mean(x * x, axis=-1, keepdims=True) + RMS_EPS) * g


def _gelu(x):
    return 0.5 * x * (1.0 + jnp.tanh(math.sqrt(2.0 / math.pi) * (x + 0.044715 * (x * x * x))))


def _sigmoid(x):
    return 1.0 / (1.0 + jnp.exp(-x))


def _params(*sem):
    return pltpu.CompilerParams(dimension_semantics=sem, vmem_limit_bytes=VMEM_LIMIT)


def _mod_kernel(c_ref, w_ref, b_ref, o_ref):
    c = c_ref[...]
    cond = c * _sigmoid(c)
    o_ref[0] = _mm(cond, w_ref[0]) + b_ref[0]


def _mod_call(c, mod_w, mod_b):
    L, D, D6 = mod_w.shape
    B = c.shape[0]
    nt = D6 // D
    return pl.pallas_call(
        _mod_kernel,
        grid=(L, nt),
        in_specs=[pl.BlockSpec((B, D), lambda l, n: (0, 0)),
                  pl.BlockSpec((1, D, D), lambda l, n: (l, 0, n)),
                  pl.BlockSpec((1, 1, D), lambda l, n: (l, 0, n))],
        out_specs=pl.BlockSpec((1, B, D), lambda l, n: (l, 0, n)),
        out_shape=jax.ShapeDtypeStruct((L, B, D6), F32),
        compiler_params=_params("arbitrary", "arbitrary"),
        name="adaln_mod",
    )(c, mod_w.astype(_MXU_DTYPE), mod_b.reshape(L, 1, D6))


def _inproj_kernel(x_ref, g_ref, sc_ref, sh_ref, w_ref, gcq_ref, gckv_ref,
                   cq_o, ckv_o, misc_o, ussm_o, upool_o):
    o0, o1, o2, o3 = D_QLAT, D_QLAT + D_LAT, D_QLAT + D_LAT + MISC_W, D_QLAT + D_LAT + MISC_W + W_SSM
    for r0 in range(0, x_ref.shape[1], MERGE_ROWS):
        rs = slice(r0, r0 + MERGE_ROWS)
        h = _rms(x_ref[0, rs, :], g_ref[...]) * (1.0 + sc_ref[0]) + sh_ref[0]
        z = _mm(h, w_ref[...])
        cq_o[0, rs, :] = _rms(z[:, :o0], gcq_ref[...]).astype(cq_o.dtype)
        ckv_o[0, rs, :] = _rms(z[:, o0:o1], gckv_ref[...]).astype(ckv_o.dtype)
        misc_o[0, rs, :] = z[:, o1:o2]
        ussm_o[rs, :] = z[:, o2:o3]
        upool_o[0, rs, :] = z[:, o3:]


def _inproj_call(x, g_pre, sc, sh, w_small, g_cq, g_ckv, ts):
    B, S, D = x.shape
    row = lambda b, i: (b, i, 0)
    per_b = lambda b, i: (b, 0, 0)
    const2 = lambda b, i: (0, 0)
    return pl.pallas_call(
        _inproj_kernel,
        grid=(B, S // ts),
        in_specs=[pl.BlockSpec((1, ts, D), row),
                  pl.BlockSpec((1, D), const2),
                  pl.BlockSpec((1, 1, D), per_b),
                  pl.BlockSpec((1, 1, D), per_b),
                  pl.BlockSpec((D, W_SMALL), const2),
                  pl.BlockSpec((1, D_QLAT), const2),
                  pl.BlockSpec((1, D_LAT), const2)],
        out_specs=[pl.BlockSpec((1, ts, D_QLAT), row),
                   pl.BlockSpec((1, ts, D_LAT), row),
                   pl.BlockSpec((1, ts, MISC_W), row),
                   pl.BlockSpec((ts, W_SSM), lambda b, i: (i, b)),
                   pl.BlockSpec((1, ts, W_POOL), row)],
        out_shape=[jax.ShapeDtypeStruct((B, S, D_QLAT), _MXU_DTYPE),
                   jax.ShapeDtypeStruct((B, S, D_LAT), _MXU_DTYPE),
                   jax.ShapeDtypeStruct((B, S, MISC_W), F32),
                   jax.ShapeDtypeStruct((S, B * W_SSM), F32),
                   jax.ShapeDtypeStruct((B, S, W_POOL), F32)],
        compiler_params=_params("arbitrary", "arbitrary"),
        name="in_projection",
    )(x, g_pre, sc, sh, w_small, g_cq, g_ckv)


def _sortable(i):
    return i ^ ((i >> 31) & jnp.int32(0x7FFFFFFF))


def _attn_kernel(slope_ref, cq_ref, ckv_ref, misc_ref, wqi_ref, wuq_ref, wuv_ref, o_ref,
                 kaug_sc, ckvt_sc, qi_sc, qt_sc, score_sc, lg_sc, tmax_sc, m_sc, acc_sc, yt_sc, *, topk):
    j = pl.program_id(1)
    nkt = ckvt_sc.shape[0]
    q0 = pl.multiple_of(j * QB, QB)

    @pl.when((pl.program_id(0) == 0) & (j == 0))
    def _():
        lane = lax.broadcasted_iota(jnp.int32, (QB, D_LAT), 1)
        s_in = lax.broadcasted_iota(jnp.int32, (QB, D_LAT), 0).astype(F32)
        for kt in range(nkt):
            pos = jnp.where(lane < SLOPE_PARTS, s_in, jnp.where(lane < 2 * SLOPE_PARTS, float(kt), 0.0))
            kaug_sc[kt * QB:(kt + 1) * QB, D_LAT:] = pos.astype(kaug_sc.dtype)
        sub = lax.broadcasted_iota(jnp.int32, (D_LAT, QB), 0)
        for h in range(N_HEADS):
            rows = jnp.zeros((D_LAT, QB), F32)
            for part in range(SLOPE_PARTS):
                rows = jnp.where(sub == part, slope_ref[h, part], rows)
                rows = jnp.where(sub == SLOPE_PARTS + part, slope_ref[h, part] * QB, rows)
            qt_sc[h, D_LAT:, :] = rows.astype(qt_sc.dtype)
        ones_row = jnp.where(lax.broadcasted_iota(jnp.int32, (DENOM_ROWS, QB), 0) == 0, 1.0, 0.0)
        for kt in range(nkt):
            ckvt_sc[kt, D_LAT:, :] = ones_row.astype(ckvt_sc.dtype)

    @pl.when(j == 0)
    def _():
        for kt in range(nkt):
            kv = ckv_ref[0, kt * QB:(kt + 1) * QB, :]
            ckvt_sc[kt, :D_LAT, :] = kv.astype(F32).T.astype(ckvt_sc.dtype)
            kaug_sc[kt * QB:(kt + 1) * QB, :D_LAT] = kv

    cqt = cq_ref[0].astype(F32).T.astype(_MXU_DTYPE)
    misct = misc_ref[0, pl.ds(q0, QB), :].T
    qi_sc[...] = _mm(wqi_ref[...], cqt).astype(qi_sc.dtype)
    for h in range(N_HEADS):
        qt_sc[h, :D_LAT, :] = (_mm(wuq_ref[h], cqt) * (ATTN_SCALE * LOG2E)).astype(qt_sc.dtype)

    kl = lax.broadcasted_iota(jnp.int32, (QB, QB), 0)
    ql = lax.broadcasted_iota(jnp.int32, (QB, QB), 1)
    diag_ok = kl < (ql // CHUNK + 1) * CHUNK
    ahead = 2.0 * jnp.maximum(kl - ql, 0).astype(F32)

    def score_tile(kt):
        k0 = pl.multiple_of(kt * QB, QB)
        kmat = misc_ref[0, pl.ds(k0, QB), :].astype(_MXU_DTYPE)
        acc = jnp.zeros((QB, QB), F32)
        for h in range(IDX_HEADS):
            lg = jnp.dot(kmat, qi_sc[h * LANES:(h + 1) * LANES, :], preferred_element_type=F32)
            wq = misct[IDX_DIM + h:IDX_DIM + h + 1, :] * (IDX_SCALE * IDX_HEAD_SCALE)
            acc = acc + jnp.maximum(lg, 0.0) * wq
        return acc

    def score_body(kt, carry):
        cmin, cmax = carry
        s = score_tile(kt)
        score_sc[pl.ds(pl.multiple_of(kt * QB, QB), QB), :] = s
        return (jnp.minimum(cmin, jnp.min(s, axis=0, keepdims=True)),
                jnp.maximum(cmax, jnp.max(s, axis=0, keepdims=True)))

    big = jnp.full((1, QB), 3.0e38, F32)
    cmin, cmax = lax.fori_loop(0, j, score_body, (big, -big))
    s = score_tile(j)
    cmin = jnp.minimum(cmin, jnp.min(jnp.where(diag_ok, s, 3.0e38), axis=0, keepdims=True))
    cmax = jnp.maximum(cmax, jnp.max(jnp.where(diag_ok, s, -3.0e38), axis=0, keepdims=True))
    score_sc[pl.ds(q0, QB), :] = jnp.where(diag_ok, s, -jnp.inf)

    def rows_reduce(x, op, rows):
        return op(x.reshape(QB // rows, rows, QB), axis=0)

    def count(pred, thr):
        def body(kt, acc):
            t = score_sc[pl.ds(pl.multiple_of(kt * QB, QB), QB), :]
            return acc + rows_reduce(jnp.where(pred(t, thr), 1.0, 0.0), jnp.sum, COUNT_ROWS)
        acc = lax.fori_loop(0, j + 1, body, jnp.zeros((COUNT_ROWS, QB), F32))
        return jnp.sum(acc, axis=0, keepdims=True)

    def count_and_next(v):
        def body(kt, carry):
            acc, below = carry
            t = score_sc[pl.ds(pl.multiple_of(kt * QB, QB), QB), :]
            hit = t >= v
            return (acc + rows_reduce(jnp.where(hit, 1.0, 0.0), jnp.sum, WALK_ROWS),
                    jnp.maximum(below, rows_reduce(jnp.where(hit, -jnp.inf, t), jnp.max, WALK_ROWS)))
        init = (jnp.zeros((WALK_ROWS, QB), F32), jnp.full((WALK_ROWS, QB), -jnp.inf, F32))
        acc, below = lax.fori_loop(0, j + 1, body, init)
        return jnp.sum(acc, axis=0, keepdims=True), jnp.max(below, axis=0, keepdims=True)

    def max_below(v):
        def body(kt, below):
            t = score_sc[pl.ds(pl.multiple_of(kt * QB, QB), QB), :]
            return jnp.maximum(below, rows_reduce(jnp.where(t < v, t, -jnp.inf), jnp.max, COUNT_ROWS))
        below = lax.fori_loop(0, j + 1, body, jnp.full((COUNT_ROWS, QB), -jnp.inf, F32))
        return jnp.max(below, axis=0, keepdims=True)

    ge = lambda t, thr: t >= thr
    gt = lambda t, thr: t > thr

    @pl.when((j + 1) * QB > topk)
    def _():
        kf = jnp.float32(topk)

        def any_lane(flag):
            f = jnp.where(flag, 1.0, 0.0)
            parts = [f[:, k * LANES:(k + 1) * LANES] for k in range(QB // LANES)]
            return jnp.max(functools.reduce(jnp.maximum, parts)) > 0.0

        above_max = pltpu.bitcast(_sortable(_sortable(pltpu.bitcast(cmax, jnp.int32)) + 1), F32)

        def halve(_, c):
            lo, hi = c
            mid = 0.5 * lo + 0.5 * hi
            keep_low = count(ge, mid) >= kf
            return jnp.where(keep_low, mid, lo), jnp.where(keep_low, hi, mid)

        _, hi = lax.fori_loop(0, BISECT_PASSES, halve, (cmin, above_max))

        def unresolved(c):
            return any_lane(c[1] == 0)

        def step(c):
            v, done, thr, cnt_ge = c
            cnt, below = count_and_next(v)
            hit = (done == 0) & ((cnt >= kf) | (v <= cmin))
            thr = jnp.where(hit, v, thr)
            cnt_ge = jnp.where(hit, cnt, cnt_ge)
            done = jnp.where(hit, 1, done)
            return jnp.where(done > 0, v, jnp.maximum(below, cmin)), done, thr, cnt_ge

        zero = jnp.zeros((1, QB), F32)
        first = jnp.maximum(max_below(hi), cmin)
        state = step((first, jnp.zeros((1, QB), jnp.int32), zero, zero))
        _, _, thr, cnt_ge = lax.while_loop(unresolved, step, state)
        ties = any_lane(cnt_ge > kf)

        @pl.when(jnp.logical_not(ties))
        def _():
            def body(kt, _):
                r = pl.ds(pl.multiple_of(kt * QB, QB), QB)
                score_sc[r, :] = jnp.where(score_sc[r, :] >= thr, 0.0, NEG_INF)
                return 0
            lax.fori_loop(0, j + 1, body, 0)

        @pl.when(ties)
        def _():
            need = kf - count(gt, thr)
            tri = (lax.broadcasted_iota(jnp.int32, (QB, QB), 0)
                   >= lax.broadcasted_iota(jnp.int32, (QB, QB), 1)).astype(_MXU_DTYPE)

            def body(kt, seen):
                r = pl.ds(pl.multiple_of(kt * QB, QB), QB)
                t = score_sc[r, :]
                eq = jnp.where(t == thr, 1.0, 0.0)
                rank = seen + jnp.dot(tri, eq.astype(_MXU_DTYPE), preferred_element_type=F32)
                sel = (t > thr) | ((t == thr) & (rank <= need))
                score_sc[r, :] = jnp.where(sel, 0.0, NEG_INF)
                return seen + jnp.sum(eq, axis=0, keepdims=True)
            lax.fori_loop(0, j + 1, body, jnp.zeros((1, QB), F32))

    @pl.when((j + 1) * QB <= topk)
    def _():
        def body(kt, _):
            r = pl.ds(pl.multiple_of(kt * QB, QB), QB)
            score_sc[r, :] = jnp.where(score_sc[r, :] > -jnp.inf, 0.0, NEG_INF)
            return 0
        lax.fori_loop(0, j + 1, body, 0)

    m_sc[...] = jnp.full(m_sc.shape, -3.0e38, F32)
    acc_sc[...] = jnp.zeros(acc_sc.shape, F32)

    def qk_stage(kt, h, ahead_of_query):
        k0 = pl.multiple_of(kt * QB, QB)
        lg = score_sc[pl.ds(k0, QB), :] + jnp.dot(kaug_sc[pl.ds(k0, QB), :], qt_sc[h], preferred_element_type=F32)
        if ahead_of_query is not None:
            lg = lg - slope_ref[h, SLOPE_PARTS] * ahead_of_query
        lg_sc[h] = lg
        tmax_sc[h] = jnp.max(lg, axis=0, keepdims=True)

    def pv_stage(kt, h):
        m = m_sc[h]
        m_new = jnp.maximum(m, tmax_sc[h])
        alpha = jnp.exp2(m - m_new)
        m_sc[h] = m_new
        p = jnp.exp2(lg_sc[h] - m_new)
        acc_sc[h] = alpha * acc_sc[h] + jnp.dot(ckvt_sc[kt], p.astype(_MXU_DTYPE), preferred_element_type=F32)

    @pl.when(j == 0)
    def _():
        for h in range(N_HEADS):
            qk_stage(j, h, ahead)

    @pl.when(j > 0)
    def _():
        for h in range(N_HEADS):
            qk_stage(0, h, None)

        def steady(kt, _):
            for h in range(N_HEADS):
                pv_stage(kt, h)
                qk_stage(kt + 1, h, None)
            return 0

        lax.fori_loop(0, j - 1, steady, 0)
        for h in range(N_HEADS):
            pv_stage(j - 1, h)
            qk_stage(j, h, ahead)

    for h in range(N_HEADS):
        pv_stage(j, h)
    for h in range(N_HEADS):
        out = acc_sc[h, :D_LAT, :] / acc_sc[h, D_LAT:D_LAT + 1, :]
        yt_sc[h * D_VHEAD:(h + 1) * D_VHEAD, :] = _mm(wuv_ref[h], out)
    o_ref[0] = yt_sc[...].T.astype(o_ref.dtype)


def _attn_call(cqn, ckvn, misc, wqi, wuq, wuv, topk):
    B, S, _ = cqn.shape
    nkt = S // QB
    slope = jnp.exp2(-8.0 * jnp.arange(1, N_HEADS + 1, dtype=F32) / N_HEADS) * LOG2E
    parts, rest = [], slope
    for _ in range(SLOPE_PARTS):
        piece = rest.astype(_MXU_DTYPE).astype(F32)
        parts.append(piece)
        rest = rest - piece
    slopes = jnp.stack(parts + [slope], axis=1)
    per_b = lambda b, j: (b, 0, 0)
    const2 = lambda b, j: (0, 0)
    const3 = lambda b, j: (0, 0, 0)
    return pl.pallas_call(
        functools.partial(_attn_kernel, topk=topk),
        grid=(B, nkt),
        in_specs=[pl.BlockSpec(memory_space=pltpu.SMEM),
                  pl.BlockSpec((1, QB, D_QLAT), lambda b, j: (b, j, 0)),
                  pl.BlockSpec((1, S, D_LAT), per_b),
                  pl.BlockSpec((1, S, MISC_W), per_b),
                  pl.BlockSpec((IDX_HEADS * LANES, D_QLAT), const2),
                  pl.BlockSpec((N_HEADS, D_LAT, D_QLAT), const3),
                  pl.BlockSpec((N_HEADS, D_VHEAD, D_LAT), const3)],
        out_specs=pl.BlockSpec((1, QB, W_ATTN), lambda b, j: (b, j, 0)),
        out_shape=jax.ShapeDtypeStruct((B, S, W_ATTN), _MXU_DTYPE),
        scratch_shapes=[pltpu.VMEM((S, 2 * D_LAT), _MXU_DTYPE),
                        pltpu.VMEM((nkt, D_LAT + DENOM_ROWS, QB), _MXU_DTYPE),
                        pltpu.VMEM((IDX_HEADS * LANES, QB), _MXU_DTYPE),
                        pltpu.VMEM((N_HEADS, 2 * D_LAT, QB), _MXU_DTYPE),
                        pltpu.VMEM((S, QB), F32),
                        pltpu.VMEM((N_HEADS, QB, QB), F32),
                        pltpu.VMEM((N_HEADS, 1, QB), F32),
                        pltpu.VMEM((N_HEADS, 1, QB), F32),
                        pltpu.VMEM((N_HEADS, D_LAT + DENOM_ROWS, QB), F32),
                        pltpu.VMEM((W_ATTN, QB), F32)],
        compiler_params=_params("arbitrary", "arbitrary"),
        name="dsa_attention",
    )(slopes, cqn, ckvn, misc, wqi, wuq, wuv)


def _ssm_kernel(u_ref, bd_ref, lr_ref, li_ref, cd_ref, d_ref, wg_ref, bg_ref, o_ref, x_sc, bu_sc, *, nb, cw):
    tc = u_ref.shape[0] // nb

    @pl.when(pl.program_id(0) == 0)
    def _():
        x_sc[...] = jnp.zeros_like(x_sc)

    nrow = u_ref.shape[0]
    groups = [slice(g * nrow // SSM_ROW_GROUPS, (g + 1) * nrow // SSM_ROW_GROUPS) for g in range(SSM_ROW_GROUPS)]
    for rs in groups:
        bu_sc[rs, :] = _mm(u_ref[rs, :], bd_ref[...])
    for c in range(N_STATE // cw):
        re = slice(c * cw, (c + 1) * cw)
        im = slice(N_STATE + c * cw, N_STATE + (c + 1) * cw)
        lr = lr_ref[:, re]
        li = li_ref[:, re]

        def step(t, carry, re=re, im=im, lr=lr, li=li):
            xr, xi = carry
            rows = pl.ds(pl.multiple_of(t * nb, nb), nb)
            nr = lr * xr - li * xi + bu_sc[rows, re]
            ni = lr * xi + li * xr + bu_sc[rows, im]
            bu_sc[rows, re] = nr
            bu_sc[rows, im] = ni
            return nr, ni

        xr, xi = lax.fori_loop(0, tc, step, (x_sc[:, re], x_sc[:, im]), unroll=4)
        x_sc[:, re] = xr
        x_sc[:, im] = xi
    n_tiles = W_SSM // LANES
    span = N_STATE // n_tiles
    for rs in groups:
        cols = []
        for k in range(n_tiles):
            out = slice(k * LANES, (k + 1) * LANES)
            re = slice(k * span, (k + 1) * span)
            im = slice(N_STATE + k * span, N_STATE + (k + 1) * span)
            cols.append(_mm(bu_sc[rs, re], cd_ref[re, out]) + _mm(bu_sc[rs, im], cd_ref[im, out]))
        y = jnp.concatenate(cols, axis=1) + d_ref[...] * u_ref[rs, :]
        z = _gelu(y)
        o_ref[rs, :] = (z * _sigmoid(_mm(z, wg_ref[...]) + bg_ref[...])).astype(o_ref.dtype)


def _ssm_call(u_tm, bd, lr, li, cd, dskip, w_glu, b_glu, nb, tc):
    rows = u_tm.shape[0]
    const2 = lambda i: (0, 0)
    return pl.pallas_call(
        functools.partial(_ssm_kernel, nb=nb, cw=256),
        grid=(rows // (tc * nb),),
        in_specs=[pl.BlockSpec((tc * nb, W_SSM), lambda i: (i, 0)),
                  pl.BlockSpec((W_SSM, 2 * N_STATE), const2),
                  pl.BlockSpec((1, N_STATE), const2),
                  pl.BlockSpec((1, N_STATE), const2),
                  pl.BlockSpec((2 * N_STATE, W_SSM), const2),
                  pl.BlockSpec((1, W_SSM), const2),
                  pl.BlockSpec((W_SSM, W_SSM), const2),
                  pl.BlockSpec((1, W_SSM), const2)],
        out_specs=pl.BlockSpec((tc * nb, W_SSM), lambda i: (i, 0)),
        out_shape=jax.ShapeDtypeStruct((rows, W_SSM), _MXU_DTYPE),
        scratch_shapes=[pltpu.VMEM((nb, 2 * N_STATE), F32),
                        pltpu.VMEM((tc * nb, 2 * N_STATE), F32)],
        compiler_params=_params("arbitrary"),
        name="s5_ssm",
    )(u_tm, bd, lr, li, cd, dskip, w_glu, b_glu)


def _ssm_weights(a_re, a_im, b_re, b_im, c_re, c_im, log_step):
    step = jnp.exp(log_step)[:, None]
    er = jnp.exp(a_re * step)
    ang = a_im * step
    lr, li = er * jnp.cos(ang), er * jnp.sin(ang)
    den = a_re * a_re + a_im * a_im
    fr = ((lr - 1.0) * a_re + li * a_im) / den
    fi = (li * a_re - (lr - 1.0) * a_im) / den
    br = fr[:, :, None] * b_re - fi[:, :, None] * b_im
    bi = fr[:, :, None] * b_im + fi[:, :, None] * b_re
    eye = jnp.eye(SSM_GROUPS, dtype=F32)
    pack_b = lambda m: jnp.einsum('gnp,gh->gphn', m, eye).reshape(W_SSM, N_STATE)
    pack_c = lambda m: jnp.einsum('gpn,gh->gnhp', m, eye).reshape(N_STATE, W_SSM)
    bd = jnp.concatenate([pack_b(br), pack_b(bi)], axis=1)
    cd = jnp.concatenate([pack_c(c_re), pack_c(-c_im)], axis=0)
    return (bd.astype(_MXU_DTYPE), lr.reshape(1, N_STATE), li.reshape(1, N_STATE), cd.astype(_MXU_DTYPE))


def _merge_kernel(x_ref, gpre_ref, sc_ref, sh_ref, gt_ref, gpost_ref, wg_ref, ya_ref, ys_ref,
                  up_ref, uh_ref, wp_ref, ps_ref, pa_ref, pb_ref, pc_ref, wo_ref, o_ref, halo_sc):
    i = pl.program_id(1)
    ts = x_ref.shape[1]

    u = up_ref[0]
    halo_sc[:POOL_HALO, :] = jnp.where(i > 0, uh_ref[0], 0.0)
    halo_sc[POOL_HALO:, :] = u
    lane = lax.broadcasted_iota(jnp.int32, (ts, W_POOL), 1)
    tpos = (i * ts + lax.broadcasted_iota(jnp.int32, (ts, W_POOL), 0) + 1).astype(F32)
    run = u
    pooled = jnp.zeros_like(u)
    prev = 1
    for g, win in enumerate(POOL_WINDOWS):
        for k in range(prev, win):
            run = run + halo_sc[POOL_HALO - k:POOL_HALO - k + ts, :]
        prev = win
        in_group = (lane >= g * POOL_GROUP_DIM) & (lane < (g + 1) * POOL_GROUP_DIM)
        pooled = jnp.where(in_group, run / jnp.minimum(tpos, float(win)), pooled)
    centred = pooled - u

    d = x_ref.shape[-1]
    for r0 in range(0, ts, MERGE_ROWS):
        rs = slice(r0, r0 + MERGE_ROWS)
        x = x_ref[0, rs, :]
        h = _rms(x, gpre_ref[...]) * (1.0 + sc_ref[0]) + sh_ref[0]
        y_pool = _mm(centred[rs], wp_ref[...]) * ps_ref[...]
        merged = (_sigmoid(_mm(h, wg_ref[:, :d])) * _mm(ya_ref[0, rs, :], pa_ref[...])
                  + _sigmoid(_mm(h, wg_ref[:, d:2 * d])) * _mm(ys_ref[rs, :], pb_ref[...])
                  + _sigmoid(_mm(h, wg_ref[:, 2 * d:])) * _mm(y_pool, pc_ref[...]))
        y = _mm(merged, wo_ref[...])
        o_ref[0, rs, :] = x + gt_ref[0] * _rms(y, gpost_ref[...])


def _merge_call(x, g_pre, sc, sh, gt, g_post, w_gate, y_attn, y_ssm_tm, u_pool, w_pool_bd, pool_scale,
                p_a, p_b, p_c, w_out, ts):
    B, S, D = x.shape
    row = lambda b, i: (b, i, 0)
    per_b = lambda b, i: (b, 0, 0)
    const2 = lambda b, i: (0, 0)
    hb = ts // POOL_HALO
    return pl.pallas_call(
        _merge_kernel,
        grid=(B, S // ts),
        in_specs=[pl.BlockSpec((1, ts, D), row),
                  pl.BlockSpec((1, D), const2),
                  pl.BlockSpec((1, 1, D), per_b),
                  pl.BlockSpec((1, 1, D), per_b),
                  pl.BlockSpec((1, 1, D), per_b),
                  pl.BlockSpec((1, D), const2),
                  pl.BlockSpec((D, N_BRANCH * D), const2),
                  pl.BlockSpec((1, ts, W_ATTN), row),
                  pl.BlockSpec((ts, W_SSM), lambda b, i: (i, b)),
                  pl.BlockSpec((1, ts, W_POOL), row),
                  pl.BlockSpec((1, POOL_HALO, W_POOL), lambda b, i: (b, jnp.maximum(i * hb - 1, 0), 0)),
                  pl.BlockSpec((W_POOL, W_POOL), const2),
                  pl.BlockSpec((1, W_POOL), const2),
                  pl.BlockSpec((W_ATTN, D), const2),
                  pl.BlockSpec((W_SSM, D), const2),
                  pl.BlockSpec((W_POOL, D), const2),
                  pl.BlockSpec((D, D), const2)],
        out_specs=pl.BlockSpec((1, ts, D), row),
        out_shape=jax.ShapeDtypeStruct((B, S, D), F32),
        scratch_shapes=[pltpu.VMEM((POOL_HALO + ts, W_POOL), F32)],
        compiler_params=_params("arbitrary", "arbitrary"),
        name="mixer_merge",
    )(x, g_pre, sc, sh, gt, g_post, w_gate, y_attn, y_ssm_tm, u_pool, u_pool, w_pool_bd, pool_scale,
      p_a, p_b, p_c, w_out)


def _ffn_kernel(x_ref, xh_ref, gpre_ref, sc_ref, sh_ref, gt_ref, gpost_ref, wup_ref, cw_ref, cb_ref,
                wdn_ref, o_ref, h_sc, u_sc, a_sc, *, cols, down_cols, ahead):
    i = pl.program_id(1)
    ts = x_ref.shape[1]
    dff = wdn_ref.shape[0]
    x = x_ref[0]
    adaln = lambda v: _rms(v, gpre_ref[...]) * (1.0 + sc_ref[0]) + sh_ref[0]
    h_sc[:CONV_HALO, :] = adaln(xh_ref[0]).astype(h_sc.dtype)
    h_sc[CONV_HALO:, :] = adaln(x).astype(h_sc.dtype)
    keep = jnp.where(i > 0, 1.0, 0.0)
    nchunk = dff // cols
    per_down = down_cols // cols
    col = lambda c, half: slice(half * dff + c * cols, half * dff + (c + 1) * cols)

    def up_project(c):
        for half in range(2):
            up = jnp.dot(h_sc[...], wup_ref[:, col(c, half)], preferred_element_type=F32)
            u_sc[half, :CONV_HALO, col(c, 0)] = up[:CONV_HALO] * keep
            u_sc[half, CONV_HALO:, col(c, 0)] = up[CONV_HALO:]

    def conv(c, half):
        acc = cb_ref[:, col(c, half)]
        for k in range(CONV_WIDTH):
            r0 = CONV_HALO - (CONV_WIDTH - 1) + k
            acc = acc + cw_ref[k:k + 1, col(c, half)] * u_sc[half, r0:r0 + ts, col(c, 0)]
        return acc

    y = jnp.zeros((ts, x.shape[-1]), F32)
    for c in range(min(ahead, nchunk)):
        up_project(c)
    for c in range(nchunk):
        if c + ahead < nchunk:
            up_project(c + ahead)
        a_sc[:, c * cols:(c + 1) * cols] = (_gelu(conv(c, 0)) * conv(c, 1)).astype(a_sc.dtype)
        if (c + 1) % per_down == 0:
            rows = slice((c + 1) * cols - down_cols, (c + 1) * cols)
            y = y + jnp.dot(a_sc[:, rows], wdn_ref[rows, :], preferred_element_type=F32)
    o_ref[0] = x + gt_ref[0] * _rms(y, gpost_ref[...])


def _ffn_call(x, g_pre, sc, sh, gt, g_post, w_up, conv_w, conv_b, w_down, ts, cols=256, down_cols=512, ahead=3):
    B, S, D = x.shape
    dff = w_down.shape[0]
    row = lambda b, i: (b, i, 0)
    per_b = lambda b, i: (b, 0, 0)
    const2 = lambda b, i: (0, 0)
    hb = ts // CONV_HALO
    return pl.pallas_call(
        functools.partial(_ffn_kernel, cols=cols, down_cols=down_cols, ahead=ahead),
        grid=(B, S // ts),
        in_specs=[pl.BlockSpec((1, ts, D), row),
                  pl.BlockSpec((1, CONV_HALO, D), lambda b, i: (b, jnp.maximum(i * hb - 1, 0), 0)),
                  pl.BlockSpec((1, D), const2),
                  pl.BlockSpec((1, 1, D), per_b),
                  pl.BlockSpec((1, 1, D), per_b),
                  pl.BlockSpec((1, 1, D), per_b),
                  pl.BlockSpec((1, D), const2),
                  pl.BlockSpec((D, 2 * dff), const2),
                  pl.BlockSpec((CONV_WIDTH, 2 * dff), const2),
                  pl.BlockSpec((1, 2 * dff), const2),
                  pl.BlockSpec((dff, D), const2)],
        out_specs=pl.BlockSpec((1, ts, D), row),
        out_shape=jax.ShapeDtypeStruct((B, S, D), F32),
        scratch_shapes=[pltpu.VMEM((CONV_HALO + ts, D), _MXU_DTYPE),
                        pltpu.VMEM((2, CONV_HALO + ts, dff), F32),
                        pltpu.VMEM((ts, dff), _MXU_DTYPE)],
        compiler_params=_params("arbitrary", "arbitrary"),
        name="conv_gated_ffn",
    )(x, x, g_pre, sc, sh, gt, g_post, w_up, conv_w, conv_b, w_down)


def _pick(n, pref):
    t = min(n, pref)
    assert n % t == 0, (n, t)
    return t


def kernel(x, c, mod_w, mod_b, mix_pre_g, mix_post_g, ffn_pre_g, ffn_post_g, w_in, g_cq, w_uq, w_qi, g_ckv, w_uv, a_re, a_im, b_re, b_im, c_re, c_im, d_skip, log_step, w_glu, b_glu, w_pool, pool_scale, p_a, p_b, p_c, w_out, w_up, conv_w, conv_b, w_down):
    B, S, D = x.shape
    depth = mod_w.shape[0]
    assert S % QB == 0 and QB % CHUNK == 0
    topk = min(DSA_TOPK_MAX, S // 4)
    ts = _pick(S, ROW_TILE)
    tc = _pick(S, SSM_TIME_CHUNK)
    cast = lambda w: w.astype(_MXU_DTYPE)
    row = lambda v: v.reshape(1, -1)

    mod = _mod_call(c, mod_w, mod_b)
    cuts = [0]
    for wdt in IN_SPLITS:
        cuts.append(cuts[-1] + wdt)
    eye_p = jnp.eye(POOL_GROUPS, dtype=F32)
    for l in range(depth):
        sh_m, sc_m, gt_m, sh_f, sc_f, gt_f = [mod[l][:, None, k * D:(k + 1) * D] for k in range(6)]
        wl = w_in[l]
        pad = jnp.zeros((D, MISC_W - IDX_DIM - IDX_HEADS), F32)
        w_small = cast(jnp.concatenate(
            [wl[:, cuts[0]:cuts[2]], wl[:, cuts[2]:cuts[4]], pad, wl[:, cuts[4]:cuts[6]]], axis=1))
        w_gate = cast(wl[:, cuts[6]:])
        cqn, ckvn, misc, u_ssm, u_pool = _inproj_call(
            x, row(mix_pre_g[l]), sc_m, sh_m, w_small, row(g_cq[l]), row(g_ckv[l]), ts)

        wqi = jnp.transpose(w_qi[l], (1, 2, 0))
        wqi = jnp.pad(wqi, ((0, 0), (0, LANES - IDX_DIM), (0, 0))).reshape(IDX_HEADS * LANES, D_QLAT)
        y_attn = _attn_call(cqn, ckvn, misc, cast(wqi), cast(jnp.transpose(w_uq[l], (1, 2, 0))),
                            cast(jnp.transpose(w_uv[l], (0, 2, 1))), topk)

        bd, lr, li, cd = _ssm_weights(a_re[l], a_im[l], b_re[l], b_im[l], c_re[l], c_im[l], log_step[l])
        y_ssm = _ssm_call(u_ssm.reshape(S * B, W_SSM), bd, lr, li, cd, row(d_skip[l]), cast(w_glu[l]),
                          row(b_glu[l]), B, tc).reshape(S, B * W_SSM)

        w_pool_bd = cast(jnp.einsum('gcd,gh->gchd', w_pool[l], eye_p).reshape(W_POOL, W_POOL))
        x = _merge_call(x, row(mix_pre_g[l]), sc_m, sh_m, gt_m, row(mix_post_g[l]), w_gate, y_attn, y_ssm,
                        u_pool, w_pool_bd, row(pool_scale[l]), cast(p_a[l]), cast(p_b[l]), cast(p_c[l]),
                        cast(w_out[l]), ts)
        x = _ffn_call(x, row(ffn_pre_g[l]), sc_f, sh_f, gt_f, row(ffn_post_g[l]), cast(w_up[l]), conv_w[l],
                      row(conv_b[l]), cast(w_down[l]), ts)
    return x
```

```python
import functools
import math

import jax
import jax.numpy as jnp
from jax import lax
from jax.experimental import pallas as pl
from jax.experimental.pallas import tpu as pltpu

F32 = jnp.float32
_MXU_DTYPE = jnp.bfloat16

CHUNK = 64
N_HEADS = 8
D_QLAT = 256
D_LAT = 128
D_VHEAD = 64
IDX_HEADS = 8
IDX_DIM = 32
DSA_TOPK_MAX = 256
SSM_GROUPS = 16
SSM_GROUP_DIM = 16
SSM_STATE = 64
W_SSM = SSM_GROUPS * SSM_GROUP_DIM
N_STATE = SSM_GROUPS * SSM_STATE
POOL_WINDOWS = (2, 4, 8, 16)
POOL_GROUPS = 4
POOL_GROUP_DIM = 64
W_POOL = POOL_GROUPS * POOL_GROUP_DIM
W_ATTN = N_HEADS * D_VHEAD
N_BRANCH = 3
CONV_WIDTH = 3
RMS_EPS = 1e-6
NEG_INF = -1e30
ATTN_SCALE = D_LAT ** -0.5
IDX_SCALE = IDX_DIM ** -0.5
IDX_HEAD_SCALE = IDX_HEADS ** -0.5
IN_SPLITS = (D_QLAT, D_LAT, IDX_DIM, IDX_HEADS, W_SSM, W_POOL)

LANES = 128
SUBLANES = 8
MISC_W = LANES
W_SMALL = D_QLAT + D_LAT + MISC_W + W_SSM + W_POOL
VMEM_LIMIT = 56 * 1024 * 1024

QB = 256
BISECT_PASSES = 14
SLOPE_PARTS = 3
DENOM_ROWS = 16
LOG2E = math.log2(math.e)
COUNT_ROWS = 4 * SUBLANES
WALK_ROWS = 2 * SUBLANES
POOL_HALO = 16
ROW_TILE = 512
INPROJ_TILE = 1024
SSM_TIME_CHUNK = 128
MERGE_ROWS = 256
SSM_ROW_GROUPS = 2
CONV_HALO = 16


def _mm(a, b):
    return jnp.dot(a.astype(_MXU_DTYPE), b.astype(_MXU_DTYPE), preferred_element_type=F32)


def _rms(x, g):
    return x * lax.rsqrt(jnp.mean(x * x, axis=-1, keepdims=True) + RMS_EPS) * g


def _gelu(x):
    return 0.5 * x * (1.0 + jnp.tanh(math.sqrt(2.0 / math.pi) * (x + 0.044715 * (x * x * x))))


def _sigmoid(x):
    return 1.0 / (1.0 + jnp.exp(-x))


def _params(*sem):
    return pltpu.CompilerParams(dimension_semantics=sem, vmem_limit_bytes=VMEM_LIMIT)


def _mod_kernel(c_ref, w_ref, b_ref, o_ref):
    c = c_ref[...]
    cond = c * _sigmoid(c)
    o_ref[0] = _mm(cond, w_ref[0]) + b_ref[0]


def _mod_call(c, mod_w, mod_b):
    L, D, D6 = mod_w.shape
    B = c.shape[0]
    nt = D6 // D
    return pl.pallas_call(
        _mod_kernel,
        grid=(L, nt),
        in_specs=[pl.BlockSpec((B, D), lambda l, n: (0, 0)),
                  pl.BlockSpec((1, D, D), lambda l, n: (l, 0, n)),
                  pl.BlockSpec((1, 1, D), lambda l, n: (l, 0, n))],
        out_specs=pl.BlockSpec((1, B, D), lambda l, n: (l, 0, n)),
        out_shape=jax.ShapeDtypeStruct((L, B, D6), F32),
        compiler_params=_params("arbitrary", "arbitrary"),
        name="adaln_mod",
    )(c, mod_w.astype(_MXU_DTYPE), mod_b.reshape(L, 1, D6))


def _inproj_kernel(x_ref, g_ref, sc_ref, sh_ref, w_ref, gcq_ref, gckv_ref,
                   cq_o, ckv_o, misc_o, ussm_o, upool_o):
    o0, o1, o2, o3 = D_QLAT, D_QLAT + D_LAT, D_QLAT + D_LAT + MISC_W, D_QLAT + D_LAT + MISC_W + W_SSM
    for r0 in range(0, x_ref.shape[1], MERGE_ROWS):
        rs = slice(r0, r0 + MERGE_ROWS)
        h = _rms(x_ref[0, rs, :], g_ref[...]) * (1.0 + sc_ref[0]) + sh_ref[0]
        z = _mm(h, w_ref[...])
        cq_o[0, rs, :] = _rms(z[:, :o0], gcq_ref[...]).astype(cq_o.dtype)
        ckv_o[0, rs, :] = _rms(z[:, o0:o1], gckv_ref[...]).astype(ckv_o.dtype)
        misc_o[0, rs, :] = z[:, o1:o2]
        ussm_o[rs, :] = z[:, o2:o3]
        upool_o[0, rs, :] = z[:, o3:]


def _inproj_call(x, g_pre, sc, sh, w_small, g_cq, g_ckv, ts):
    B, S, D = x.shape
    row = lambda b, i: (b, i, 0)
    per_b = lambda b, i: (b, 0, 0)
    const2 = lambda b, i: (0, 0)
    return pl.pallas_call(
        _inproj_kernel,
        grid=(B, S // ts),
        in_specs=[pl.BlockSpec((1, ts, D), row),
                  pl.BlockSpec((1, D), const2),
                  pl.BlockSpec((1, 1, D), per_b),
                  pl.BlockSpec((1, 1, D), per_b),
                  pl.BlockSpec((D, W_SMALL), const2),
                  pl.BlockSpec((1, D_QLAT), const2),
                  pl.BlockSpec((1, D_LAT), const2)],
        out_specs=[pl.BlockSpec((1, ts, D_QLAT), row),
                   pl.BlockSpec((1, ts, D_LAT), row),
                   pl.BlockSpec((1, ts, MISC_W), row),
                   pl.BlockSpec((ts, W_SSM), lambda b, i: (i, b)),
                   pl.BlockSpec((1, ts, W_POOL), row)],
        out_shape=[jax.ShapeDtypeStruct((B, S, D_QLAT), _MXU_DTYPE),
                   jax.ShapeDtypeStruct((B, S, D_LAT), _MXU_DTYPE),
                   jax.ShapeDtypeStruct((B, S, MISC_W), F32),
                   jax.ShapeDtypeStruct((S, B * W_SSM), F32),
                   jax.ShapeDtypeStruct((B, S, W_POOL), F32)],
        compiler_params=_params("arbitrary", "arbitrary"),
        name="in_projection",
    )(x, g_pre, sc, sh, w_small, g_cq, g_ckv)


def _sortable(i):
    return i ^ ((i >> 31) & jnp.int32(0x7FFFFFFF))


def _attn_kernel(slope_ref, cq_ref, ckv_ref, misc_ref, wqi_ref, wuq_ref, wuv_ref, o_ref,
                 kaug_sc, ckvt_sc, qi_sc, qt_sc, score_sc, lg_sc, tmax_sc, m_sc, acc_sc, yt_sc, *, topk):
    j = pl.program_id(1)
    nkt = ckvt_sc.shape[0]
    q0 = pl.multiple_of(j * QB, QB)

    @pl.when((pl.program_id(0) == 0) & (j == 0))
    def _():
        lane = lax.broadcasted_iota(jnp.int32, (QB, D_LAT), 1)
        s_in = lax.broadcasted_iota(jnp.int32, (QB, D_LAT), 0).astype(F32)
        for kt in range(nkt):
            pos = jnp.where(lane < SLOPE_PARTS, s_in, jnp.where(lane < 2 * SLOPE_PARTS, float(kt), 0.0))
            kaug_sc[kt * QB:(kt + 1) * QB, D_LAT:] = pos.astype(kaug_sc.dtype)
        sub = lax.broadcasted_iota(jnp.int32, (D_LAT, QB), 0)
        for h in range(N_HEADS):
            rows = jnp.zeros((D_LAT, QB), F32)
            for part in range(SLOPE_PARTS):
                rows = jnp.where(sub == part, slope_ref[h, part], rows)
                rows = jnp.where(sub == SLOPE_PARTS + part, slope_ref[h, part] * QB, rows)
            qt_sc[h, D_LAT:, :] = rows.astype(qt_sc.dtype)
        ones_row = jnp.where(lax.broadcasted_iota(jnp.int32, (DENOM_ROWS, QB), 0) == 0, 1.0, 0.0)
        for kt in range(nkt):
            ckvt_sc[kt, D_LAT:, :] = ones_row.astype(ckvt_sc.dtype)

    @pl.when(j == 0)
    def _():
        for kt in range(nkt):
            kv = ckv_ref[0, kt * QB:(kt + 1) * QB, :]
            ckvt_sc[kt, :D_LAT, :] = kv.astype(F32).T.astype(ckvt_sc.dtype)
            kaug_sc[kt * QB:(kt + 1) * QB, :D_LAT] = kv

    cqt = cq_ref[0].astype(F32).T.astype(_MXU_DTYPE)
    misct = misc_ref[0, pl.ds(q0, QB), :].T
    qi_sc[...] = _mm(wqi_ref[...], cqt).astype(qi_sc.dtype)
    for h in range(N_HEADS):
        qt_sc[h, :D_LAT, :] = (_mm(wuq_ref[h], cqt) * (ATTN_SCALE * LOG2E)).astype(qt_sc.dtype)

    kl = lax.broadcasted_iota(jnp.int32, (QB, QB), 0)
    ql = lax.broadcasted_iota(jnp.int32, (QB, QB), 1)
    diag_ok = kl < (ql // CHUNK + 1) * CHUNK
    ahead = 2.0 * jnp.maximum(kl - ql, 0).astype(F32)

    def score_tile(kt):
        k0 = pl.multiple_of(kt * QB, QB)
        kmat = misc_ref[0, pl.ds(k0, QB), :].astype(_MXU_DTYPE)
        acc = jnp.zeros((QB, QB), F32)
        for h in range(IDX_HEADS):
            lg = jnp.dot(kmat, qi_sc[h * LANES:(h + 1) * LANES, :], preferred_element_type=F32)
            wq = misct[IDX_DIM + h:IDX_DIM + h + 1, :] * (IDX_SCALE * IDX_HEAD_SCALE)
            acc = acc + jnp.maximum(lg, 0.0) * wq
        return acc

    def score_body(kt, carry):
        cmin, cmax = carry
        s = score_tile(kt)
        score_sc[pl.ds(pl.multiple_of(kt * QB, QB), QB), :] = s
        return (jnp.minimum(cmin, jnp.min(s, axis=0, keepdims=True)),
                jnp.maximum(cmax, jnp.max(s, axis=0, keepdims=True)))

    big = jnp.full((1, QB), 3.0e38, F32)
    cmin, cmax = lax.fori_loop(0, j, score_body, (big, -big))
    s = score_tile(j)
    cmin = jnp.minimum(cmin, jnp.min(jnp.where(diag_ok, s, 3.0e38), axis=0, keepdims=True))
    cmax = jnp.maximum(cmax, jnp.max(jnp.where(diag_ok, s, -3.0e38), axis=0, keepdims=True))
    score_sc[pl.ds(q0, QB), :] = jnp.where(diag_ok, s, -jnp.inf)

    def rows_reduce(x, op, rows):
        return op(x.reshape(QB // rows, rows, QB), axis=0)

    def count(pred, thr):
        def body(kt, acc):
            t = score_sc[pl.ds(pl.multiple_of(kt * QB, QB), QB), :]
            return acc + rows_reduce(jnp.where(pred(t, thr), 1.0, 0.0), jnp.sum, COUNT_ROWS)
        acc = lax.fori_loop(0, j + 1, body, jnp.zeros((COUNT_ROWS, QB), F32))
        return jnp.sum(acc, axis=0, keepdims=True)

    def count_and_next(v):
        def body(kt, carry):
            acc, below = carry
            t = score_sc[pl.ds(pl.multiple_of(kt * QB, QB), QB), :]
            hit = t >= v
            return (acc + rows_reduce(jnp.where(hit, 1.0, 0.0), jnp.sum, WALK_ROWS),
                    jnp.maximum(below, rows_reduce(jnp.where(hit, -jnp.inf, t), jnp.max, WALK_ROWS)))
        init = (jnp.zeros((WALK_ROWS, QB), F32), jnp.full((WALK_ROWS, QB), -jnp.inf, F32))
        acc, below = lax.fori_loop(0, j + 1, body, init)
        return jnp.sum(acc, axis=0, keepdims=True), jnp.max(below, axis=0, keepdims=True)

    def max_below(v):
        def body(kt, below):
            t = score_sc[pl.ds(pl.multiple_of(kt * QB, QB), QB), :]
            return jnp.maximum(below, rows_reduce(jnp.where(t < v, t, -jnp.inf), jnp.max, COUNT_ROWS))
        below = lax.fori_loop(0, j + 1, body, jnp.full((COUNT_ROWS, QB), -jnp.inf, F32))
        return jnp.max(below, axis=0, keepdims=True)

    ge = lambda t, thr: t >= thr
    gt = lambda t, thr: t > thr

    @pl.when((j + 1) * QB > topk)
    def _():
        kf = jnp.float32(topk)

        def any_lane(flag):
            f = jnp.where(flag, 1.0, 0.0)
            parts = [f[:, k * LANES:(k + 1) * LANES] for k in range(QB // LANES)]
            return jnp.max(functools.reduce(jnp.maximum, parts)) > 0.0

        above_max = pltpu.bitcast(_sortable(_sortable(pltpu.bitcast(cmax, jnp.int32)) + 1), F32)

        def halve(_, c):
            lo, hi = c
            mid = 0.5 * lo + 0.5 * hi
            keep_low = count(ge, mid) >= kf
            return jnp.where(keep_low, mid, lo), jnp.where(keep_low, hi, mid)

        _, hi = lax.fori_loop(0, BISECT_PASSES, halve, (cmin, above_max))

        def unresolved(c):
            return any_lane(c[1] == 0)

        def step(c):
            v, done, thr, cnt_ge = c
            cnt, below = count_and_next(v)
            hit = (done == 0) & ((cnt >= kf) | (v <= cmin))
            thr = jnp.where(hit, v, thr)
            cnt_ge = jnp.where(hit, cnt, cnt_ge)
            done = jnp.where(hit, 1, done)
            return jnp.where(done > 0, v, jnp.maximum(below, cmin)), done, thr, cnt_ge

        zero = jnp.zeros((1, QB), F32)
        first = jnp.maximum(max_below(hi), cmin)
        state = step((first, jnp.zeros((1, QB), jnp.int32), zero, zero))
        _, _, thr, cnt_ge = lax.while_loop(unresolved, step, state)
        ties = any_lane(cnt_ge > kf)

        @pl.when(jnp.logical_not(ties))
        def _():
            def body(kt, _):
                r = pl.ds(pl.multiple_of(kt * QB, QB), QB)
                score_sc[r, :] = jnp.where(score_sc[r, :] >= thr, 0.0, NEG_INF)
                return 0
            lax.fori_loop(0, j + 1, body, 0)

        @pl.when(ties)
        def _():
            need = kf - count(gt, thr)
            tri = (lax.broadcasted_iota(jnp.int32, (QB, QB), 0)
                   >= lax.broadcasted_iota(jnp.int32, (QB, QB), 1)).astype(_MXU_DTYPE)

            def body(kt, seen):
                r = pl.ds(pl.multiple_of(kt * QB, QB), QB)
                t = score_sc[r, :]
                eq = jnp.where(t == thr, 1.0, 0.0)
                rank = seen + jnp.dot(tri, eq.astype(_MXU_DTYPE), preferred_element_type=F32)
                sel = (t > thr) | ((t == thr) & (rank <= need))
                score_sc[r, :] = jnp.where(sel, 0.0, NEG_INF)
                return seen + jnp.sum(eq, axis=0, keepdims=True)
            lax.fori_loop(0, j + 1, body, jnp.zeros((1, QB), F32))

    @pl.when((j + 1) * QB <= topk)
    def _():
        def body(kt, _):
            r = pl.ds(pl.multiple_of(kt * QB, QB), QB)
            score_sc[r, :] = jnp.where(score_sc[r, :] > -jnp.inf, 0.0, NEG_INF)
            return 0
        lax.fori_loop(0, j + 1, body, 0)

    m_sc[...] = jnp.full(m_sc.shape, -3.0e38, F32)
    acc_sc[...] = jnp.zeros(acc_sc.shape, F32)

    def qk_stage(kt, h, ahead_of_query):
        k0 = pl.multiple_of(kt * QB, QB)
        lg = score_sc[pl.ds(k0, QB), :] + jnp.dot(kaug_sc[pl.ds(k0, QB), :], qt_sc[h], preferred_element_type=F32)
        if ahead_of_query is not None:
            lg = lg - slope_ref[h, SLOPE_PARTS] * ahead_of_query
        lg_sc[h] = lg
        tmax_sc[h] = jnp.max(lg, axis=0, keepdims=True)

    def pv_stage(kt, h):
        m = m_sc[h]
        m_new = jnp.maximum(m, tmax_sc[h])
        alpha = jnp.exp2(m - m_new)
        m_sc[h] = m_new
        p = jnp.exp2(lg_sc[h] - m_new)
        acc_sc[h] = alpha * acc_sc[h] + jnp.dot(ckvt_sc[kt], p.astype(_MXU_DTYPE), preferred_element_type=F32)

    @pl.when(j == 0)
    def _():
        for h in range(N_HEADS):
            qk_stage(j, h, ahead)

    @pl.when(j > 0)
    def _():
        for h in range(N_HEADS):
            qk_stage(0, h, None)

        def steady(kt, _):
            for h in range(N_HEADS):
                pv_stage(kt, h)
                qk_stage(kt + 1, h, None)
            return 0

        lax.fori_loop(0, j - 1, steady, 0)
        for h in range(N_HEADS):
            pv_stage(j - 1, h)
            qk_stage(j, h, ahead)

    for h in range(N_HEADS):
        pv_stage(j, h)
    for h in range(N_HEADS):
        out = acc_sc[h, :D_LAT, :] / acc_sc[h, D_LAT:D_LAT + 1, :]
        yt_sc[h * D_VHEAD:(h + 1) * D_VHEAD, :] = _mm(wuv_ref[h], out)
    o_ref[0] = yt_sc[...].T.astype(o_ref.dtype)


def _attn_call(cqn, ckvn, misc, wqi, wuq, wuv, topk):
    B, S, _ = cqn.shape
    nkt = S // QB
    slope = jnp.exp2(-8.0 * jnp.arange(1, N_HEADS + 1, dtype=F32) / N_HEADS) * LOG2E
    parts, rest = [], slope
    for _ in range(SLOPE_PARTS):
        piece = rest.astype(_MXU_DTYPE).astype(F32)
        parts.append(piece)
        rest = rest - piece
    slopes = jnp.stack(parts + [slope], axis=1)
    per_b = lambda b, j: (b, 0, 0)
    const2 = lambda b, j: (0, 0)
    const3 = lambda b, j: (0, 0, 0)
    return pl.pallas_call(
        functools.partial(_attn_kernel, topk=topk),
        grid=(B, nkt),
        in_specs=[pl.BlockSpec(memory_space=pltpu.SMEM),
                  pl.BlockSpec((1, QB, D_QLAT), lambda b, j: (b, j, 0)),
                  pl.BlockSpec((1, S, D_LAT), per_b),
                  pl.BlockSpec((1, S, MISC_W), per_b),
                  pl.BlockSpec((IDX_HEADS * LANES, D_QLAT), const2),
                  pl.BlockSpec((N_HEADS, D_LAT, D_QLAT), const3),
                  pl.BlockSpec((N_HEADS, D_VHEAD, D_LAT), const3)],
        out_specs=pl.BlockSpec((1, QB, W_ATTN), lambda b, j: (b, j, 0)),
        out_shape=jax.ShapeDtypeStruct((B, S, W_ATTN), _MXU_DTYPE),
        scratch_shapes=[pltpu.VMEM((S, 2 * D_LAT), _MXU_DTYPE),
                        pltpu.VMEM((nkt, D_LAT + DENOM_ROWS, QB), _MXU_DTYPE),
                        pltpu.VMEM((IDX_HEADS * LANES, QB), _MXU_DTYPE),
                        pltpu.VMEM((N_HEADS, 2 * D_LAT, QB), _MXU_DTYPE),
                        pltpu.VMEM((S, QB), F32),
                        pltpu.VMEM((N_HEADS, QB, QB), F32),
                        pltpu.VMEM((N_HEADS, 1, QB), F32),
                        pltpu.VMEM((N_HEADS, 1, QB), F32),
                        pltpu.VMEM((N_HEADS, D_LAT + DENOM_ROWS, QB), F32),
                        pltpu.VMEM((W_ATTN, QB), F32)],
        compiler_params=_params("arbitrary", "arbitrary"),
        name="dsa_attention",
    )(slopes, cqn, ckvn, misc, wqi, wuq, wuv)


def _ssm_kernel(u_ref, bd_ref, lr_ref, li_ref, cd_ref, d_ref, wg_ref, bg_ref, o_ref, x_sc, bu_sc, *, nb, cw):
    tc = u_ref.shape[0] // nb

    @pl.when(pl.program_id(0) == 0)
    def _():
        x_sc[...] = jnp.zeros_like(x_sc)

    nrow = u_ref.shape[0]
    groups = [slice(g * nrow // SSM_ROW_GROUPS, (g + 1) * nrow // SSM_ROW_GROUPS) for g in range(SSM_ROW_GROUPS)]
    for rs in groups:
        bu_sc[rs, :] = _mm(u_ref[rs, :], bd_ref[...])
    for c in range(N_STATE // cw):
        re = slice(c * cw, (c + 1) * cw)
        im = slice(N_STATE + c * cw, N_STATE + (c + 1) * cw)
        lr = lr_ref[:, re]
        li = li_ref[:, re]

        def step(t, carry, re=re, im=im, lr=lr, li=li):
            xr, xi = carry
            rows = pl.ds(pl.multiple_of(t * nb, nb), nb)
            nr = lr * xr - li * xi + bu_sc[rows, re]
            ni = lr * xi + li * xr + bu_sc[rows, im]
            bu_sc[rows, re] = nr
            bu_sc[rows, im] = ni
            return nr, ni

        xr, xi = lax.fori_loop(0, tc, step, (x_sc[:, re], x_sc[:, im]), unroll=4)
        x_sc[:, re] = xr
        x_sc[:, im] = xi
    n_tiles = W_SSM // LANES
    span = N_STATE // n_tiles
    for rs in groups:
        cols = []
        for k in range(n_tiles):
            out = slice(k * LANES, (k + 1) * LANES)
            re = slice(k * span, (k + 1) * span)
            im = slice(N_STATE + k * span, N_STATE + (k + 1) * span)
            cols.append(_mm(bu_sc[rs, re], cd_ref[re, out]) + _mm(bu_sc[rs, im], cd_ref[im, out]))
        y = jnp.concatenate(cols, axis=1) + d_ref[...] * u_ref[rs, :]
        z = _gelu(y)
        o_ref[rs, :] = (z * _sigmoid(_mm(z, wg_ref[...]) + bg_ref[...])).astype(o_ref.dtype)


def _ssm_call(u_tm, bd, lr, li, cd, dskip, w_glu, b_glu, nb, tc):
    rows = u_tm.shape[0]
    const2 = lambda i: (0, 0)
    return pl.pallas_call(
        functools.partial(_ssm_kernel, nb=nb, cw=256),
        grid=(rows // (tc * nb),),
        in_specs=[pl.BlockSpec((tc * nb, W_SSM), lambda i: (i, 0)),
                  pl.BlockSpec((W_SSM, 2 * N_STATE), const2),
                  pl.BlockSpec((1, N_STATE), const2),
                  pl.BlockSpec((1, N_STATE), const2),
                  pl.BlockSpec((2 * N_STATE, W_SSM), const2),
                  pl.BlockSpec((1, W_SSM), const2),
                  pl.BlockSpec((W_SSM, W_SSM), const2),
                  pl.BlockSpec((1, W_SSM), const2)],
        out_specs=pl.BlockSpec((tc * nb, W_SSM), lambda i: (i, 0)),
        out_shape=jax.ShapeDtypeStruct((rows, W_SSM), _MXU_DTYPE),
        scratch_shapes=[pltpu.VMEM((nb, 2 * N_STATE), F32),
                        pltpu.VMEM((tc * nb, 2 * N_STATE), F32)],
        compiler_params=_params("arbitrary"),
        name="s5_ssm",
    )(u_tm, bd, lr, li, cd, dskip, w_glu, b_glu)


def _ssm_weights(a_re, a_im, b_re, b_im, c_re, c_im, log_step):
    step = jnp.exp(log_step)[:, None]
    er = jnp.exp(a_re * step)
    ang = a_im * step
    lr, li = er * jnp.cos(ang), er * jnp.sin(ang)
    den = a_re * a_re + a_im * a_im
    fr = ((lr - 1.0) * a_re + li * a_im) / den
    fi = (li * a_re - (lr - 1.0) * a_im) / den
    br = fr[:, :, None] * b_re - fi[:, :, None] * b_im
    bi = fr[:, :, None] * b_im + fi[:, :, None] * b_re
    eye = jnp.eye(SSM_GROUPS, dtype=F32)
    pack_b = lambda m: jnp.einsum('gnp,gh->gphn', m, eye).reshape(W_SSM, N_STATE)
    pack_c = lambda m: jnp.einsum('gpn,gh->gnhp', m, eye).reshape(N_STATE, W_SSM)
    bd = jnp.concatenate([pack_b(br), pack_b(bi)], axis=1)
    cd = jnp.concatenate([pack_c(c_re), pack_c(-c_im)], axis=0)
    return (bd.astype(_MXU_DTYPE), lr.reshape(1, N_STATE), li.reshape(1, N_STATE), cd.astype(_MXU_DTYPE))


def _merge_kernel(x_ref, gpre_ref, sc_ref, sh_ref, gt_ref, gpost_ref, wg_ref, ya_ref, ys_ref,
                  up_ref, uh_ref, wp_ref, ps_ref, pa_ref, pb_ref, pc_ref, wo_ref, o_ref, halo_sc):
    i = pl.program_id(1)
    ts = x_ref.shape[1]

    u = up_ref[0]
    halo_sc[:POOL_HALO, :] = jnp.where(i > 0, uh_ref[0], 0.0)
    halo_sc[POOL_HALO:, :] = u
    lane = lax.broadcasted_iota(jnp.int32, (ts, W_POOL), 1)
    tpos = (i * ts + lax.broadcasted_iota(jnp.int32, (ts, W_POOL), 0) + 1).astype(F32)
    run = u
    pooled = jnp.zeros_like(u)
    prev = 1
    for g, win in enumerate(POOL_WINDOWS):
        for k in range(prev, win):
            run = run + halo_sc[POOL_HALO - k:POOL_HALO - k + ts, :]
        prev = win
        in_group = (lane >= g * POOL_GROUP_DIM) & (lane < (g + 1) * POOL_GROUP_DIM)
        pooled = jnp.where(in_group, run / jnp.minimum(tpos, float(win)), pooled)
    centred = pooled - u

    d = x_ref.shape[-1]
    for r0 in range(0, ts, MERGE_ROWS):
        rs = slice(r0, r0 + MERGE_ROWS)
        x = x_ref[0, rs, :]
        h = _rms(x, gpre_ref[...]) * (1.0 + sc_ref[0]) + sh_ref[0]
        y_pool = _mm(centred[rs], wp_ref[...]) * ps_ref[...]
        merged = (_sigmoid(_mm(h, wg_ref[:, :d])) * _mm(ya_ref[0, rs, :], pa_ref[...])
                  + _sigmoid(_mm(h, wg_ref[:, d:2 * d])) * _mm(ys_ref[rs, :], pb_ref[...])
                  + _sigmoid(_mm(h, wg_ref[:, 2 * d:])) * _mm(y_pool, pc_ref[...]))
        y = _mm(merged, wo_ref[...])
        o_ref[0, rs, :] = x + gt_ref[0] * _rms(y, gpost_ref[...])


def _merge_call(x, g_pre, sc, sh, gt, g_post, w_gate, y_attn, y_ssm_tm, u_pool, w_pool_bd, pool_scale,
                p_a, p_b, p_c, w_out, ts):
    B, S, D = x.shape
    row = lambda b, i: (b, i, 0)
    per_b = lambda b, i: (b, 0, 0)
    const2 = lambda b, i: (0, 0)
    hb = ts // POOL_HALO
    return pl.pallas_call(
        _merge_kernel,
        grid=(B, S // ts),
        in_specs=[pl.BlockSpec((1, ts, D), row),
                  pl.BlockSpec((1, D), const2),
                  pl.BlockSpec((1, 1, D), per_b),
                  pl.BlockSpec((1, 1, D), per_b),
                  pl.BlockSpec((1, 1, D), per_b),
                  pl.BlockSpec((1, D), const2),
                  pl.BlockSpec((D, N_BRANCH * D), const2),
                  pl.BlockSpec((1, ts, W_ATTN), row),
                  pl.BlockSpec((ts, W_SSM), lambda b, i: (i, b)),
                  pl.BlockSpec((1, ts, W_POOL), row),
                  pl.BlockSpec((1, POOL_HALO, W_POOL), lambda b, i: (b, jnp.maximum(i * hb - 1, 0), 0)),
                  pl.BlockSpec((W_POOL, W_POOL), const2),
                  pl.BlockSpec((1, W_POOL), const2),
                  pl.BlockSpec((W_ATTN, D), const2),
                  pl.BlockSpec((W_SSM, D), const2),
                  pl.BlockSpec((W_POOL, D), const2),
                  pl.BlockSpec((D, D), const2)],
        out_specs=pl.BlockSpec((1, ts, D), row),
        out_shape=jax.ShapeDtypeStruct((B, S, D), F32),
        scratch_shapes=[pltpu.VMEM((POOL_HALO + ts, W_POOL), F32)],
        compiler_params=_params("arbitrary", "arbitrary"),
        name="mixer_merge",
    )(x, g_pre, sc, sh, gt, g_post, w_gate, y_attn, y_ssm_tm, u_pool, u_pool, w_pool_bd, pool_scale,
      p_a, p_b, p_c, w_out)


def _ffn_kernel(x_ref, xh_ref, gpre_ref, sc_ref, sh_ref, gt_ref, gpost_ref, wup_ref, cw_ref, cb_ref,
                wdn_ref, o_ref, h_sc, u_sc, a_sc, *, cols, down_cols, ahead):
    i = pl.program_id(1)
    ts = x_ref.shape[1]
    dff = wdn_ref.shape[0]
    x = x_ref[0]
    adaln = lambda v: _rms(v, gpre_ref[...]) * (1.0 + sc_ref[0]) + sh_ref[0]
    h_sc[:CONV_HALO, :] = adaln(xh_ref[0]).astype(h_sc.dtype)
    h_sc[CONV_HALO:, :] = adaln(x).astype(h_sc.dtype)
    keep = jnp.where(i > 0, 1.0, 0.0)
    nchunk = dff // cols
    per_down = down_cols // cols
    col = lambda c, half: slice(half * dff + c * cols, half * dff + (c + 1) * cols)

    def up_project(c):
        for half in range(2):
            up = jnp.dot(h_sc[...], wup_ref[:, col(c, half)], preferred_element_type=F32)
            u_sc[half, :CONV_HALO, col(c, 0)] = up[:CONV_HALO] * keep
            u_sc[half, CONV_HALO:, col(c, 0)] = up[CONV_HALO:]

    def conv(c, half):
        acc = cb_ref[:, col(c, half)]
        for k in range(CONV_WIDTH):
            r0 = CONV_HALO - (CONV_WIDTH - 1) + k
            acc = acc + cw_ref[k:k + 1, col(c, half)] * u_sc[half, r0:r0 + ts, col(c, 0)]
        return acc

    y = jnp.zeros((ts, x.shape[-1]), F32)
    for c in range(min(ahead, nchunk)):
        up_project(c)
    for c in range(nchunk):
        if c + ahead < nchunk:
            up_project(c + ahead)
        a_sc[:, c * cols:(c + 1) * cols] = (_gelu(conv(c, 0)) * conv(c, 1)).astype(a_sc.dtype)
        if (c + 1) % per_down == 0:
            rows = slice((c + 1) * cols - down_cols, (c + 1) * cols)
            y = y + jnp.dot(a_sc[:, rows], wdn_ref[rows, :], preferred_element_type=F32)
    o_ref[0] = x + gt_ref[0] * _rms(y, gpost_ref[...])


def _ffn_call(x, g_pre, sc, sh, gt, g_post, w_up, conv_w, conv_b, w_down, ts, cols=256, down_cols=512, ahead=3):
    B, S, D = x.shape
    dff = w_down.shape[0]
    row = lambda b, i: (b, i, 0)
    per_b = lambda b, i: (b, 0, 0)
    const2 = lambda b, i: (0, 0)
    hb = ts // CONV_HALO
    return pl.pallas_call(
        functools.partial(_ffn_kernel, cols=cols, down_cols=down_cols, ahead=ahead),
        grid=(B, S // ts),
        in_specs=[pl.BlockSpec((1, ts, D), row),
                  pl.BlockSpec((1, CONV_HALO, D), lambda b, i: (b, jnp.maximum(i * hb - 1, 0), 0)),
                  pl.BlockSpec((1, D), const2),
                  pl.BlockSpec((1, 1, D), per_b),
                  pl.BlockSpec((1, 1, D), per_b),
                  pl.BlockSpec((1, 1, D), per_b),
                  pl.BlockSpec((1, D), const2),
                  pl.BlockSpec((D, 2 * dff), const2),
                  pl.BlockSpec((CONV_WIDTH, 2 * dff), const2),
                  pl.BlockSpec((1, 2 * dff), const2),
                  pl.BlockSpec((dff, D), const2)],
        out_specs=pl.BlockSpec((1, ts, D), row),
        out_shape=jax.ShapeDtypeStruct((B, S, D), F32),
        scratch_shapes=[pltpu.VMEM((CONV_HALO + ts, D), _MXU_DTYPE),
                        pltpu.VMEM((2, CONV_HALO + ts, dff), F32),
                        pltpu.VMEM((ts, dff), _MXU_DTYPE)],
        compiler_params=_params("arbitrary", "arbitrary"),
        name="conv_gated_ffn",
    )(x, x, g_pre, sc, sh, gt, g_post, w_up, conv_w, conv_b, w_down)


def _pick(n, pref):
    t = min(n, pref)
    assert n % t == 0, (n, t)
    return t


def kernel(x, c, mod_w, mod_b, mix_pre_g, mix_post_g, ffn_pre_g, ffn_post_g, w_in, g_cq, w_uq, w_qi, g_ckv, w_uv, a_re, a_im, b_re, b_im, c_re, c_im, d_skip, log_step, w_glu, b_glu, w_pool, pool_scale, p_a, p_b, p_c, w_out, w_up, conv_w, conv_b, w_down):
    B, S, D = x.shape
    depth = mod_w.shape[0]
    assert S % QB == 0 and QB % CHUNK == 0
    topk = min(DSA_TOPK_MAX, S // 4)
    ts = _pick(S, ROW_TILE)
    tc = _pick(S, SSM_TIME_CHUNK)
    cast = lambda w: w.astype(_MXU_DTYPE)
    row = lambda v: v.reshape(1, -1)

    mod = _mod_call(c, mod_w, mod_b)
    cuts = [0]
    for wdt in IN_SPLITS:
        cuts.append(cuts[-1] + wdt)
    eye_p = jnp.eye(POOL_GROUPS, dtype=F32)
    for l in range(depth):
        sh_m, sc_m, gt_m, sh_f, sc_f, gt_f = [mod[l][:, None, k * D:(k + 1) * D] for k in range(6)]
        wl = w_in[l]
        pad = jnp.zeros((D, MISC_W - IDX_DIM - IDX_HEADS), F32)
        w_small = cast(jnp.concatenate(
            [wl[:, cuts[0]:cuts[2]], wl[:, cuts[2]:cuts[4]], pad, wl[:, cuts[4]:cuts[6]]], axis=1))
        w_gate = cast(wl[:, cuts[6]:])
        cqn, ckvn, misc, u_ssm, u_pool = _inproj_call(
            x, row(mix_pre_g[l]), sc_m, sh_m, w_small, row(g_cq[l]), row(g_ckv[l]), _pick(S, INPROJ_TILE))

        wqi = jnp.transpose(w_qi[l], (1, 2, 0))
        wqi = jnp.pad(wqi, ((0, 0), (0, LANES - IDX_DIM), (0, 0))).reshape(IDX_HEADS * LANES, D_QLAT)
        y_attn = _attn_call(cqn, ckvn, misc, cast(wqi), cast(jnp.transpose(w_uq[l], (1, 2, 0))),
                            cast(jnp.transpose(w_uv[l], (0, 2, 1))), topk)

        bd, lr, li, cd = _ssm_weights(a_re[l], a_im[l], b_re[l], b_im[l], c_re[l], c_im[l], log_step[l])
        y_ssm = _ssm_call(u_ssm.reshape(S * B, W_SSM), bd, lr, li, cd, row(d_skip[l]), cast(w_glu[l]),
                          row(b_glu[l]), B, tc).reshape(S, B * W_SSM)

        w_pool_bd = cast(jnp.einsum('gcd,gh->gchd', w_pool[l], eye_p).reshape(W_POOL, W_POOL))
        x = _merge_call(x, row(mix_pre_g[l]), sc_m, sh_m, gt_m, row(mix_post_g[l]), w_gate, y_attn, y_ssm,
                        u_pool, w_pool_bd, row(pool_scale[l]), cast(p_a[l]), cast(p_b[l]), cast(p_c[l]),
                        cast(w_out[l]), ts)
        x = _ffn_call(x, row(ffn_pre_g[l]), sc_f, sh_f, gt_f, row(ffn_post_g[l]), cast(w_up[l]), conv_w[l],
                      row(conv_b[l]), cast(w_down[l]), ts)
    return x
```

```python
import functools
import math

import jax
import jax.numpy as jnp
from jax import lax
from jax.experimental import pallas as pl
from jax.experimental.pallas import tpu as pltpu

F32 = jnp.float32
_MXU_DTYPE = jnp.bfloat16

CHUNK = 64
N_HEADS = 8
D_QLAT = 256
D_LAT = 128
D_VHEAD = 64
IDX_HEADS = 8
IDX_DIM = 32
DSA_TOPK_MAX = 256
SSM_GROUPS = 16
SSM_GROUP_DIM = 16
SSM_STATE = 64
W_SSM = SSM_GROUPS * SSM_GROUP_DIM
N_STATE = SSM_GROUPS * SSM_STATE
POOL_WINDOWS = (2, 4, 8, 16)
POOL_GROUPS = 4
POOL_GROUP_DIM = 64
W_POOL = POOL_GROUPS * POOL_GROUP_DIM
W_ATTN = N_HEADS * D_VHEAD
N_BRANCH = 3
CONV_WIDTH = 3
RMS_EPS = 1e-6
NEG_INF = -1e30
ATTN_SCALE = D_LAT ** -0.5
IDX_SCALE = IDX_DIM ** -0.5
IDX_HEAD_SCALE = IDX_HEADS ** -0.5
IN_SPLITS = (D_QLAT, D_LAT, IDX_DIM, IDX_HEADS, W_SSM, W_POOL)

LANES = 128
SUBLANES = 8
MISC_W = LANES
W_SMALL = D_QLAT + D_LAT + MISC_W + W_SSM + W_POOL
VMEM_LIMIT = 56 * 1024 * 1024

QB = 256
BISECT_PASSES = 15
SLOPE_PARTS = 3
DENOM_ROWS = 16
LOG2E = math.log2(math.e)
COUNT_ROWS = 4 * SUBLANES
WALK_ROWS = 2 * SUBLANES
POOL_HALO = 16
ROW_TILE = 512
INPROJ_TILE = 2048
MERGE_TILE = 1024
SSM_TIME_CHUNK = 128
MERGE_ROWS = 256
SSM_ROW_GROUPS = 2
CONV_HALO = 16


def _mm(a, b):
    return jnp.dot(a.astype(_MXU_DTYPE), b.astype(_MXU_DTYPE), preferred_element_type=F32)


def _rms(x, g):
    return x * lax.rsqrt(jnp.mean(x * x, axis=-1, keepdims=True) + RMS_EPS) * g


def _gelu(x):
    return 0.5 * x * (1.0 + jnp.tanh(math.sqrt(2.0 / math.pi) * (x + 0.044715 * (x * x * x))))


def _sigmoid(x):
    return 1.0 / (1.0 + jnp.exp(-x))


def _params(*sem):
    return pltpu.CompilerParams(dimension_semantics=sem, vmem_limit_bytes=VMEM_LIMIT)


def _mod_kernel(c_ref, w_ref, b_ref, o_ref):
    c = c_ref[...]
    cond = c * _sigmoid(c)
    o_ref[0] = _mm(cond, w_ref[0]) + b_ref[0]


def _mod_call(c, mod_w, mod_b):
    L, D, D6 = mod_w.shape
    B = c.shape[0]
    nt = D6 // D
    return pl.pallas_call(
        _mod_kernel,
        grid=(L, nt),
        in_specs=[pl.BlockSpec((B, D), lambda l, n: (0, 0)),
                  pl.BlockSpec((1, D, D), lambda l, n: (l, 0, n)),
                  pl.BlockSpec((1, 1, D), lambda l, n: (l, 0, n))],
        out_specs=pl.BlockSpec((1, B, D), lambda l, n: (l, 0, n)),
        out_shape=jax.ShapeDtypeStruct((L, B, D6), F32),
        compiler_params=_params("arbitrary", "arbitrary"),
        name="adaln_mod",
    )(c, mod_w.astype(_MXU_DTYPE), mod_b.reshape(L, 1, D6))


def _inproj_kernel(x_ref, g_ref, sc_ref, sh_ref, w_ref, gcq_ref, gckv_ref,
                   cq_o, ckv_o, misc_o, ussm_o, upool_o):
    o0, o1, o2, o3 = D_QLAT, D_QLAT + D_LAT, D_QLAT + D_LAT + MISC_W, D_QLAT + D_LAT + MISC_W + W_SSM
    for r0 in range(0, x_ref.shape[1], MERGE_ROWS):
        rs = slice(r0, r0 + MERGE_ROWS)
        h = _rms(x_ref[0, rs, :], g_ref[...]) * (1.0 + sc_ref[0]) + sh_ref[0]
        z = _mm(h, w_ref[...])
        cq_o[0, rs, :] = _rms(z[:, :o0], gcq_ref[...]).astype(cq_o.dtype)
        ckv_o[0, rs, :] = _rms(z[:, o0:o1], gckv_ref[...]).astype(ckv_o.dtype)
        misc_o[0, rs, :] = z[:, o1:o2]
        ussm_o[rs, :] = z[:, o2:o3]
        upool_o[0, rs, :] = z[:, o3:]


def _inproj_call(x, g_pre, sc, sh, w_small, g_cq, g_ckv, ts):
    B, S, D = x.shape
    row = lambda b, i: (b, i, 0)
    per_b = lambda b, i: (b, 0, 0)
    const2 = lambda b, i: (0, 0)
    return pl.pallas_call(
        _inproj_kernel,
        grid=(B, S // ts),
        in_specs=[pl.BlockSpec((1, ts, D), row),
                  pl.BlockSpec((1, D), const2),
                  pl.BlockSpec((1, 1, D), per_b),
                  pl.BlockSpec((1, 1, D), per_b),
                  pl.BlockSpec((D, W_SMALL), const2),
                  pl.BlockSpec((1, D_QLAT), const2),
                  pl.BlockSpec((1, D_LAT), const2)],
        out_specs=[pl.BlockSpec((1, ts, D_QLAT), row),
                   pl.BlockSpec((1, ts, D_LAT), row),
                   pl.BlockSpec((1, ts, MISC_W), row),
                   pl.BlockSpec((ts, W_SSM), lambda b, i: (i, b)),
                   pl.BlockSpec((1, ts, W_POOL), row)],
        out_shape=[jax.ShapeDtypeStruct((B, S, D_QLAT), _MXU_DTYPE),
                   jax.ShapeDtypeStruct((B, S, D_LAT), _MXU_DTYPE),
                   jax.ShapeDtypeStruct((B, S, MISC_W), F32),
                   jax.ShapeDtypeStruct((S, B * W_SSM), F32),
                   jax.ShapeDtypeStruct((B, S, W_POOL), F32)],
        compiler_params=_params("arbitrary", "arbitrary"),
        name="in_projection",
    )(x, g_pre, sc, sh, w_small, g_cq, g_ckv)


def _sortable(i):
    return i ^ ((i >> 31) & jnp.int32(0x7FFFFFFF))


def _attn_kernel(slope_ref, cq_ref, ckv_ref, misc_ref, wqi_ref, wuq_ref, wuv_ref, o_ref,
                 kaug_sc, ckvt_sc, qi_sc, qt_sc, score_sc, lg_sc, tmax_sc, m_sc, acc_sc, yt_sc, *, topk):
    j = pl.program_id(1)
    nkt = ckvt_sc.shape[0]
    q0 = pl.multiple_of(j * QB, QB)

    @pl.when((pl.program_id(0) == 0) & (j == 0))
    def _():
        lane = lax.broadcasted_iota(jnp.int32, (QB, D_LAT), 1)
        s_in = lax.broadcasted_iota(jnp.int32, (QB, D_LAT), 0).astype(F32)
        for kt in range(nkt):
            pos = jnp.where(lane < SLOPE_PARTS, s_in, jnp.where(lane < 2 * SLOPE_PARTS, float(kt), 0.0))
            kaug_sc[kt * QB:(kt + 1) * QB, D_LAT:] = pos.astype(kaug_sc.dtype)
        sub = lax.broadcasted_iota(jnp.int32, (D_LAT, QB), 0)
        for h in range(N_HEADS):
            rows = jnp.zeros((D_LAT, QB), F32)
            for part in range(SLOPE_PARTS):
                rows = jnp.where(sub == part, slope_ref[h, part], rows)
                rows = jnp.where(sub == SLOPE_PARTS + part, slope_ref[h, part] * QB, rows)
            qt_sc[h, D_LAT:, :] = rows.astype(qt_sc.dtype)
        ones_row = jnp.where(lax.broadcasted_iota(jnp.int32, (DENOM_ROWS, QB), 0) == 0, 1.0, 0.0)
        for kt in range(nkt):
            ckvt_sc[kt, D_LAT:, :] = ones_row.astype(ckvt_sc.dtype)

    @pl.when(j == 0)
    def _():
        for kt in range(nkt):
            kv = ckv_ref[0, kt * QB:(kt + 1) * QB, :]
            ckvt_sc[kt, :D_LAT, :] = kv.astype(F32).T.astype(ckvt_sc.dtype)
            kaug_sc[kt * QB:(kt + 1) * QB, :D_LAT] = kv

    cqt = cq_ref[0].astype(F32).T.astype(_MXU_DTYPE)
    misct = misc_ref[0, pl.ds(q0, QB), :].T
    qi_sc[...] = _mm(wqi_ref[...], cqt).astype(qi_sc.dtype)
    for h in range(N_HEADS):
        qt_sc[h, :D_LAT, :] = (_mm(wuq_ref[h], cqt) * (ATTN_SCALE * LOG2E)).astype(qt_sc.dtype)

    kl = lax.broadcasted_iota(jnp.int32, (QB, QB), 0)
    ql = lax.broadcasted_iota(jnp.int32, (QB, QB), 1)
    diag_ok = kl < (ql // CHUNK + 1) * CHUNK
    ahead = 2.0 * jnp.maximum(kl - ql, 0).astype(F32)

    def score_tile(kt):
        k0 = pl.multiple_of(kt * QB, QB)
        kmat = misc_ref[0, pl.ds(k0, QB), :].astype(_MXU_DTYPE)
        acc = jnp.zeros((QB, QB), F32)
        for h in range(IDX_HEADS):
            lg = jnp.dot(kmat, qi_sc[h * LANES:(h + 1) * LANES, :], preferred_element_type=F32)
            wq = misct[IDX_DIM + h:IDX_DIM + h + 1, :] * (IDX_SCALE * IDX_HEAD_SCALE)
            acc = acc + jnp.maximum(lg, 0.0) * wq
        return acc

    def score_body(kt, carry):
        cmin, cmax = carry
        s = score_tile(kt)
        score_sc[pl.ds(pl.multiple_of(kt * QB, QB), QB), :] = s
        return (jnp.minimum(cmin, jnp.min(s, axis=0, keepdims=True)),
                jnp.maximum(cmax, jnp.max(s, axis=0, keepdims=True)))

    big = jnp.full((1, QB), 3.0e38, F32)
    cmin, cmax = lax.fori_loop(0, j, score_body, (big, -big))
    s = score_tile(j)
    cmin = jnp.minimum(cmin, jnp.min(jnp.where(diag_ok, s, 3.0e38), axis=0, keepdims=True))
    cmax = jnp.maximum(cmax, jnp.max(jnp.where(diag_ok, s, -3.0e38), axis=0, keepdims=True))
    score_sc[pl.ds(q0, QB), :] = jnp.where(diag_ok, s, -jnp.inf)

    def rows_reduce(x, op, rows):
        return op(x.reshape(QB // rows, rows, QB), axis=0)

    def count(pred, thr):
        def body(kt, acc):
            t = score_sc[pl.ds(pl.multiple_of(kt * QB, QB), QB), :]
            return acc + rows_reduce(jnp.where(pred(t, thr), 1.0, 0.0), jnp.sum, COUNT_ROWS)
        acc = lax.fori_loop(0, j + 1, body, jnp.zeros((COUNT_ROWS, QB), F32))
        return jnp.sum(acc, axis=0, keepdims=True)

    def count_and_next(v):
        def body(kt, carry):
            acc, below = carry
            t = score_sc[pl.ds(pl.multiple_of(kt * QB, QB), QB), :]
            hit = t >= v
            return (acc + rows_reduce(jnp.where(hit, 1.0, 0.0), jnp.sum, WALK_ROWS),
                    jnp.maximum(below, rows_reduce(jnp.where(hit, -jnp.inf, t), jnp.max, WALK_ROWS)))
        init = (jnp.zeros((WALK_ROWS, QB), F32), jnp.full((WALK_ROWS, QB), -jnp.inf, F32))
        acc, below = lax.fori_loop(0, j + 1, body, init)
        return jnp.sum(acc, axis=0, keepdims=True), jnp.max(below, axis=0, keepdims=True)

    def max_below(v):
        def body(kt, below):
            t = score_sc[pl.ds(pl.multiple_of(kt * QB, QB), QB), :]
            return jnp.maximum(below, rows_reduce(jnp.where(t < v, t, -jnp.inf), jnp.max, COUNT_ROWS))
        below = lax.fori_loop(0, j + 1, body, jnp.full((COUNT_ROWS, QB), -jnp.inf, F32))
        return jnp.max(below, axis=0, keepdims=True)

    ge = lambda t, thr: t >= thr
    gt = lambda t, thr: t > thr

    @pl.when((j + 1) * QB > topk)
    def _():
        kf = jnp.float32(topk)

        def any_lane(flag):
            f = jnp.where(flag, 1.0, 0.0)
            parts = [f[:, k * LANES:(k + 1) * LANES] for k in range(QB // LANES)]
            return jnp.max(functools.reduce(jnp.maximum, parts)) > 0.0

        above_max = pltpu.bitcast(_sortable(_sortable(pltpu.bitcast(cmax, jnp.int32)) + 1), F32)

        def halve(_, c):
            lo, hi = c
            mid = 0.5 * lo + 0.5 * hi
            keep_low = count(ge, mid) >= kf
            return jnp.where(keep_low, mid, lo), jnp.where(keep_low, hi, mid)

        _, hi = lax.fori_loop(0, BISECT_PASSES, halve, (cmin, above_max))

        def unresolved(c):
            return any_lane(c[1] == 0)

        def step(c):
            v, done, thr, cnt_ge = c
            cnt, below = count_and_next(v)
            hit = (done == 0) & ((cnt >= kf) | (v <= cmin))
            thr = jnp.where(hit, v, thr)
            cnt_ge = jnp.where(hit, cnt, cnt_ge)
            done = jnp.where(hit, 1, done)
            return jnp.where(done > 0, v, jnp.maximum(below, cmin)), done, thr, cnt_ge

        zero = jnp.zeros((1, QB), F32)
        first = jnp.maximum(max_below(hi), cmin)
        state = step((first, jnp.zeros((1, QB), jnp.int32), zero, zero))
        _, _, thr, cnt_ge = lax.while_loop(unresolved, step, state)
        ties = any_lane(cnt_ge > kf)

        @pl.when(jnp.logical_not(ties))
        def _():
            def body(kt, _):
                r = pl.ds(pl.multiple_of(kt * QB, QB), QB)
                score_sc[r, :] = jnp.where(score_sc[r, :] >= thr, 0.0, NEG_INF)
                return 0
            lax.fori_loop(0, j + 1, body, 0)

        @pl.when(ties)
        def _():
            need = kf - count(gt, thr)
            tri = (lax.broadcasted_iota(jnp.int32, (QB, QB), 0)
                   >= lax.broadcasted_iota(jnp.int32, (QB, QB), 1)).astype(_MXU_DTYPE)

            def body(kt, seen):
                r = pl.ds(pl.multiple_of(kt * QB, QB), QB)
                t = score_sc[r, :]
                eq = jnp.where(t == thr, 1.0, 0.0)
                rank = seen + jnp.dot(tri, eq.astype(_MXU_DTYPE), preferred_element_type=F32)
                sel = (t > thr) | ((t == thr) & (rank <= need))
                score_sc[r, :] = jnp.where(sel, 0.0, NEG_INF)
                return seen + jnp.sum(eq, axis=0, keepdims=True)
            lax.fori_loop(0, j + 1, body, jnp.zeros((1, QB), F32))

    @pl.when((j + 1) * QB <= topk)
    def _():
        def body(kt, _):
            r = pl.ds(pl.multiple_of(kt * QB, QB), QB)
            score_sc[r, :] = jnp.where(score_sc[r, :] > -jnp.inf, 0.0, NEG_INF)
            return 0
        lax.fori_loop(0, j + 1, body, 0)

    m_sc[...] = jnp.full(m_sc.shape, -3.0e38, F32)
    acc_sc[...] = jnp.zeros(acc_sc.shape, F32)

    def qk_stage(kt, h, ahead_of_query):
        k0 = pl.multiple_of(kt * QB, QB)
        lg = score_sc[pl.ds(k0, QB), :] + jnp.dot(kaug_sc[pl.ds(k0, QB), :], qt_sc[h], preferred_element_type=F32)
        if ahead_of_query is not None:
            lg = lg - slope_ref[h, SLOPE_PARTS] * ahead_of_query
        lg_sc[h] = lg
        tmax_sc[h] = jnp.max(lg, axis=0, keepdims=True)

    def pv_stage(kt, h):
        m = m_sc[h]
        m_new = jnp.maximum(m, tmax_sc[h])
        alpha = jnp.exp2(m - m_new)
        m_sc[h] = m_new
        p = jnp.exp2(lg_sc[h] - m_new)
        acc_sc[h] = alpha * acc_sc[h] + jnp.dot(ckvt_sc[kt], p.astype(_MXU_DTYPE), preferred_element_type=F32)

    @pl.when(j == 0)
    def _():
        for h in range(N_HEADS):
            qk_stage(j, h, ahead)

    @pl.when(j > 0)
    def _():
        for h in range(N_HEADS):
            qk_stage(0, h, None)

        def steady(kt, _):
            for h in range(N_HEADS):
                pv_stage(kt, h)
                qk_stage(kt + 1, h, None)
            return 0

        lax.fori_loop(0, j - 1, steady, 0)
        for h in range(N_HEADS):
            pv_stage(j - 1, h)
            qk_stage(j, h, ahead)

    for h in range(N_HEADS):
        pv_stage(j, h)
    for h in range(N_HEADS):
        out = acc_sc[h, :D_LAT, :] / acc_sc[h, D_LAT:D_LAT + 1, :]
        yt_sc[h * D_VHEAD:(h + 1) * D_VHEAD, :] = _mm(wuv_ref[h], out)
    o_ref[0] = yt_sc[...].T.astype(o_ref.dtype)


def _attn_call(cqn, ckvn, misc, wqi, wuq, wuv, topk):
    B, S, _ = cqn.shape
    nkt = S // QB
    slope = jnp.exp2(-8.0 * jnp.arange(1, N_HEADS + 1, dtype=F32) / N_HEADS) * LOG2E
    parts, rest = [], slope
    for _ in range(SLOPE_PARTS):
        piece = rest.astype(_MXU_DTYPE).astype(F32)
        parts.append(piece)
        rest = rest - piece
    slopes = jnp.stack(parts + [slope], axis=1)
    per_b = lambda b, j: (b, 0, 0)
    const2 = lambda b, j: (0, 0)
    const3 = lambda b, j: (0, 0, 0)
    return pl.pallas_call(
        functools.partial(_attn_kernel, topk=topk),
        grid=(B, nkt),
        in_specs=[pl.BlockSpec(memory_space=pltpu.SMEM),
                  pl.BlockSpec((1, QB, D_QLAT), lambda b, j: (b, j, 0)),
                  pl.BlockSpec((1, S, D_LAT), per_b),
                  pl.BlockSpec((1, S, MISC_W), per_b),
                  pl.BlockSpec((IDX_HEADS * LANES, D_QLAT), const2),
                  pl.BlockSpec((N_HEADS, D_LAT, D_QLAT), const3),
                  pl.BlockSpec((N_HEADS, D_VHEAD, D_LAT), const3)],
        out_specs=pl.BlockSpec((1, QB, W_ATTN), lambda b, j: (b, j, 0)),
        out_shape=jax.ShapeDtypeStruct((B, S, W_ATTN), _MXU_DTYPE),
        scratch_shapes=[pltpu.VMEM((S, 2 * D_LAT), _MXU_DTYPE),
                        pltpu.VMEM((nkt, D_LAT + DENOM_ROWS, QB), _MXU_DTYPE),
                        pltpu.VMEM((IDX_HEADS * LANES, QB), _MXU_DTYPE),
                        pltpu.VMEM((N_HEADS, 2 * D_LAT, QB), _MXU_DTYPE),
                        pltpu.VMEM((S, QB), F32),
                        pltpu.VMEM((N_HEADS, QB, QB), F32),
                        pltpu.VMEM((N_HEADS, 1, QB), F32),
                        pltpu.VMEM((N_HEADS, 1, QB), F32),
                        pltpu.VMEM((N_HEADS, D_LAT + DENOM_ROWS, QB), F32),
                        pltpu.VMEM((W_ATTN, QB), F32)],
        compiler_params=_params("arbitrary", "arbitrary"),
        name="dsa_attention",
    )(slopes, cqn, ckvn, misc, wqi, wuq, wuv)


def _ssm_kernel(u_ref, bd_ref, lr_ref, li_ref, cd_ref, d_ref, wg_ref, bg_ref, o_ref, x_sc, bu_sc, *, nb, cw):
    tc = u_ref.shape[0] // nb

    @pl.when(pl.program_id(0) == 0)
    def _():
        x_sc[...] = jnp.zeros_like(x_sc)

    nrow = u_ref.shape[0]
    groups = [slice(g * nrow // SSM_ROW_GROUPS, (g + 1) * nrow // SSM_ROW_GROUPS) for g in range(SSM_ROW_GROUPS)]
    for rs in groups:
        bu_sc[rs, :] = _mm(u_ref[rs, :], bd_ref[...])
    for c in range(N_STATE // cw):
        re = slice(c * cw, (c + 1) * cw)
        im = slice(N_STATE + c * cw, N_STATE + (c + 1) * cw)
        lr = lr_ref[:, re]
        li = li_ref[:, re]

        def step(t, carry, re=re, im=im, lr=lr, li=li):
            xr, xi = carry
            rows = pl.ds(pl.multiple_of(t * nb, nb), nb)
            nr = lr * xr - li * xi + bu_sc[rows, re]
            ni = lr * xi + li * xr + bu_sc[rows, im]
            bu_sc[rows, re] = nr
            bu_sc[rows, im] = ni
            return nr, ni

        xr, xi = lax.fori_loop(0, tc, step, (x_sc[:, re], x_sc[:, im]), unroll=4)
        x_sc[:, re] = xr
        x_sc[:, im] = xi
    n_tiles = W_SSM // LANES
    span = N_STATE // n_tiles
    for rs in groups:
        cols = []
        for k in range(n_tiles):
            out = slice(k * LANES, (k + 1) * LANES)
            re = slice(k * span, (k + 1) * span)
            im = slice(N_STATE + k * span, N_STATE + (k + 1) * span)
            cols.append(_mm(bu_sc[rs, re], cd_ref[re, out]) + _mm(bu_sc[rs, im], cd_ref[im, out]))
        y = jnp.concatenate(cols, axis=1) + d_ref[...] * u_ref[rs, :]
        z = _gelu(y)
        o_ref[rs, :] = (z * _sigmoid(_mm(z, wg_ref[...]) + bg_ref[...])).astype(o_ref.dtype)


def _ssm_call(u_tm, bd, lr, li, cd, dskip, w_glu, b_glu, nb, tc):
    rows = u_tm.shape[0]
    const2 = lambda i: (0, 0)
    return pl.pallas_call(
        functools.partial(_ssm_kernel, nb=nb, cw=256),
        grid=(rows // (tc * nb),),
        in_specs=[pl.BlockSpec((tc * nb, W_SSM), lambda i: (i, 0)),
                  pl.BlockSpec((W_SSM, 2 * N_STATE), const2),
                  pl.BlockSpec((1, N_STATE), const2),
                  pl.BlockSpec((1, N_STATE), const2),
                  pl.BlockSpec((2 * N_STATE, W_SSM), const2),
                  pl.BlockSpec((1, W_SSM), const2),
                  pl.BlockSpec((W_SSM, W_SSM), const2),
                  pl.BlockSpec((1, W_SSM), const2)],
        out_specs=pl.BlockSpec((tc * nb, W_SSM), lambda i: (i, 0)),
        out_shape=jax.ShapeDtypeStruct((rows, W_SSM), _MXU_DTYPE),
        scratch_shapes=[pltpu.VMEM((nb, 2 * N_STATE), F32),
                        pltpu.VMEM((tc * nb, 2 * N_STATE), F32)],
        compiler_params=_params("arbitrary"),
        name="s5_ssm",
    )(u_tm, bd, lr, li, cd, dskip, w_glu, b_glu)


def _ssm_weights(a_re, a_im, b_re, b_im, c_re, c_im, log_step):
    step = jnp.exp(log_step)[:, None]
    er = jnp.exp(a_re * step)
    ang = a_im * step
    lr, li = er * jnp.cos(ang), er * jnp.sin(ang)
    den = a_re * a_re + a_im * a_im
    fr = ((lr - 1.0) * a_re + li * a_im) / den
    fi = (li * a_re - (lr - 1.0) * a_im) / den
    br = fr[:, :, None] * b_re - fi[:, :, None] * b_im
    bi = fr[:, :, None] * b_im + fi[:, :, None] * b_re
    eye = jnp.eye(SSM_GROUPS, dtype=F32)
    pack_b = lambda m: jnp.einsum('gnp,gh->gphn', m, eye).reshape(W_SSM, N_STATE)
    pack_c = lambda m: jnp.einsum('gpn,gh->gnhp', m, eye).reshape(N_STATE, W_SSM)
    bd = jnp.concatenate([pack_b(br), pack_b(bi)], axis=1)
    cd = jnp.concatenate([pack_c(c_re), pack_c(-c_im)], axis=0)
    return (bd.astype(_MXU_DTYPE), lr.reshape(1, N_STATE), li.reshape(1, N_STATE), cd.astype(_MXU_DTYPE))


def _merge_kernel(x_ref, gpre_ref, sc_ref, sh_ref, gt_ref, gpost_ref, wg_ref, ya_ref, ys_ref,
                  up_ref, uh_ref, wp_ref, ps_ref, pa_ref, pb_ref, pc_ref, wo_ref, o_ref, halo_sc):
    i = pl.program_id(1)
    ts = x_ref.shape[1]

    u = up_ref[0]
    halo_sc[:POOL_HALO, :] = jnp.where(i > 0, uh_ref[0], 0.0)
    halo_sc[POOL_HALO:, :] = u
    lane = lax.broadcasted_iota(jnp.int32, (ts, W_POOL), 1)
    tpos = (i * ts + lax.broadcasted_iota(jnp.int32, (ts, W_POOL), 0) + 1).astype(F32)
    run = u
    pooled = jnp.zeros_like(u)
    prev = 1
    for g, win in enumerate(POOL_WINDOWS):
        for k in range(prev, win):
            run = run + halo_sc[POOL_HALO - k:POOL_HALO - k + ts, :]
        prev = win
        in_group = (lane >= g * POOL_GROUP_DIM) & (lane < (g + 1) * POOL_GROUP_DIM)
        pooled = jnp.where(in_group, run / jnp.minimum(tpos, float(win)), pooled)
    centred = pooled - u

    d = x_ref.shape[-1]
    for r0 in range(0, ts, MERGE_ROWS):
        rs = slice(r0, r0 + MERGE_ROWS)
        x = x_ref[0, rs, :]
        h = _rms(x, gpre_ref[...]) * (1.0 + sc_ref[0]) + sh_ref[0]
        y_pool = _mm(centred[rs], wp_ref[...]) * ps_ref[...]
        merged = (_sigmoid(_mm(h, wg_ref[:, :d])) * _mm(ya_ref[0, rs, :], pa_ref[...])
                  + _sigmoid(_mm(h, wg_ref[:, d:2 * d])) * _mm(ys_ref[rs, :], pb_ref[...])
                  + _sigmoid(_mm(h, wg_ref[:, 2 * d:])) * _mm(y_pool, pc_ref[...]))
        y = _mm(merged, wo_ref[...])
        o_ref[0, rs, :] = x + gt_ref[0] * _rms(y, gpost_ref[...])


def _merge_call(x, g_pre, sc, sh, gt, g_post, w_gate, y_attn, y_ssm_tm, u_pool, w_pool_bd, pool_scale,
                p_a, p_b, p_c, w_out, ts):
    B, S, D = x.shape
    row = lambda b, i: (b, i, 0)
    per_b = lambda b, i: (b, 0, 0)
    const2 = lambda b, i: (0, 0)
    hb = ts // POOL_HALO
    return pl.pallas_call(
        _merge_kernel,
        grid=(B, S // ts),
        in_specs=[pl.BlockSpec((1, ts, D), row),
                  pl.BlockSpec((1, D), const2),
                  pl.BlockSpec((1, 1, D), per_b),
                  pl.BlockSpec((1, 1, D), per_b),
                  pl.BlockSpec((1, 1, D), per_b),
                  pl.BlockSpec((1, D), const2),
                  pl.BlockSpec((D, N_BRANCH * D), const2),
                  pl.BlockSpec((1, ts, W_ATTN), row),
                  pl.BlockSpec((ts, W_SSM), lambda b, i: (i, b)),
                  pl.BlockSpec((1, ts, W_POOL), row),
                  pl.BlockSpec((1, POOL_HALO, W_POOL), lambda b, i: (b, jnp.maximum(i * hb - 1, 0), 0)),
                  pl.BlockSpec((W_POOL, W_POOL), const2),
                  pl.BlockSpec((1, W_POOL), const2),
                  pl.BlockSpec((W_ATTN, D), const2),
                  pl.BlockSpec((W_SSM, D), const2),
                  pl.BlockSpec((W_POOL, D), const2),
                  pl.BlockSpec((D, D), const2)],
        out_specs=pl.BlockSpec((1, ts, D), row),
        out_shape=jax.ShapeDtypeStruct((B, S, D), F32),
        scratch_shapes=[pltpu.VMEM((POOL_HALO + ts, W_POOL), F32)],
        compiler_params=_params("arbitrary", "arbitrary"),
        name="mixer_merge",
    )(x, g_pre, sc, sh, gt, g_post, w_gate, y_attn, y_ssm_tm, u_pool, u_pool, w_pool_bd, pool_scale,
      p_a, p_b, p_c, w_out)


def _ffn_kernel(x_ref, xh_ref, gpre_ref, sc_ref, sh_ref, gt_ref, gpost_ref, wup_ref, cw_ref, cb_ref,
                wdn_ref, o_ref, h_sc, u_sc, a_sc, *, cols, down_cols, ahead):
    i = pl.program_id(1)
    ts = x_ref.shape[1]
    dff = wdn_ref.shape[0]
    x = x_ref[0]
    adaln = lambda v: _rms(v, gpre_ref[...]) * (1.0 + sc_ref[0]) + sh_ref[0]
    h_sc[:CONV_HALO, :] = adaln(xh_ref[0]).astype(h_sc.dtype)
    h_sc[CONV_HALO:, :] = adaln(x).astype(h_sc.dtype)
    keep = jnp.where(i > 0, 1.0, 0.0)
    nchunk = dff // cols
    per_down = down_cols // cols
    col = lambda c, half: slice(half * dff + c * cols, half * dff + (c + 1) * cols)

    def up_project(c):
        for half in range(2):
            up = jnp.dot(h_sc[...], wup_ref[:, col(c, half)], preferred_element_type=F32)
            u_sc[half, :CONV_HALO, col(c, 0)] = up[:CONV_HALO] * keep
            u_sc[half, CONV_HALO:, col(c, 0)] = up[CONV_HALO:]

    def conv(c, half):
        acc = cb_ref[:, col(c, half)]
        for k in range(CONV_WIDTH):
            r0 = CONV_HALO - (CONV_WIDTH - 1) + k
            acc = acc + cw_ref[k:k + 1, col(c, half)] * u_sc[half, r0:r0 + ts, col(c, 0)]
        return acc

    y = jnp.zeros((ts, x.shape[-1]), F32)
    for c in range(min(ahead, nchunk)):
        up_project(c)
    for c in range(nchunk):
        if c + ahead < nchunk:
            up_project(c + ahead)
        a_sc[:, c * cols:(c + 1) * cols] = (_gelu(conv(c, 0)) * conv(c, 1)).astype(a_sc.dtype)
        if (c + 1) % per_down == 0:
            rows = slice((c + 1) * cols - down_cols, (c + 1) * cols)
            y = y + jnp.dot(a_sc[:, rows], wdn_ref[rows, :], preferred_element_type=F32)
    o_ref[0] = x + gt_ref[0] * _rms(y, gpost_ref[...])


def _ffn_call(x, g_pre, sc, sh, gt, g_post, w_up, conv_w, conv_b, w_down, ts, cols=256, down_cols=512, ahead=3):
    B, S, D = x.shape
    dff = w_down.shape[0]
    row = lambda b, i: (b, i, 0)
    per_b = lambda b, i: (b, 0, 0)
    const2 = lambda b, i: (0, 0)
    hb = ts // CONV_HALO
    return pl.pallas_call(
        functools.partial(_ffn_kernel, cols=cols, down_cols=down_cols, ahead=ahead),
        grid=(B, S // ts),
        in_specs=[pl.BlockSpec((1, ts, D), row),
                  pl.BlockSpec((1, CONV_HALO, D), lambda b, i: (b, jnp.maximum(i * hb - 1, 0), 0)),
                  pl.BlockSpec((1, D), const2),
                  pl.BlockSpec((1, 1, D), per_b),
                  pl.BlockSpec((1, 1, D), per_b),
                  pl.BlockSpec((1, 1, D), per_b),
                  pl.BlockSpec((1, D), const2),
                  pl.BlockSpec((D, 2 * dff), const2),
                  pl.BlockSpec((CONV_WIDTH, 2 * dff), const2),
                  pl.BlockSpec((1, 2 * dff), const2),
                  pl.BlockSpec((dff, D), const2)],
        out_specs=pl.BlockSpec((1, ts, D), row),
        out_shape=jax.ShapeDtypeStruct((B, S, D), F32),
        scratch_shapes=[pltpu.VMEM((CONV_HALO + ts, D), _MXU_DTYPE),
                        pltpu.VMEM((2, CONV_HALO + ts, dff), F32),
                        pltpu.VMEM((ts, dff), _MXU_DTYPE)],
        compiler_params=_params("arbitrary", "arbitrary"),
        name="conv_gated_ffn",
    )(x, x, g_pre, sc, sh, gt, g_post, w_up, conv_w, conv_b, w_down)


def _pick(n, pref):
    t = min(n, pref)
    assert n % t == 0, (n, t)
    return t


def kernel(x, c, mod_w, mod_b, mix_pre_g, mix_post_g, ffn_pre_g, ffn_post_g, w_in, g_cq, w_uq, w_qi, g_ckv, w_uv, a_re, a_im, b_re, b_im, c_re, c_im, d_skip, log_step, w_glu, b_glu, w_pool, pool_scale, p_a, p_b, p_c, w_out, w_up, conv_w, conv_b, w_down):
    B, S, D = x.shape
    depth = mod_w.shape[0]
    assert S % QB == 0 and QB % CHUNK == 0
    topk = min(DSA_TOPK_MAX, S // 4)
    ts = _pick(S, ROW_TILE)
    tc = _pick(S, SSM_TIME_CHUNK)
    cast = lambda w: w.astype(_MXU_DTYPE)
    row = lambda v: v.reshape(1, -1)

    mod = _mod_call(c, mod_w, mod_b)
    cuts = [0]
    for wdt in IN_SPLITS:
        cuts.append(cuts[-1] + wdt)
    eye_p = jnp.eye(POOL_GROUPS, dtype=F32)
    for l in range(depth):
        sh_m, sc_m, gt_m, sh_f, sc_f, gt_f = [mod[l][:, None, k * D:(k + 1) * D] for k in range(6)]
        wl = w_in[l]
        pad = jnp.zeros((D, MISC_W - IDX_DIM - IDX_HEADS), F32)
        w_small = cast(jnp.concatenate(
            [wl[:, cuts[0]:cuts[2]], wl[:, cuts[2]:cuts[4]], pad, wl[:, cuts[4]:cuts[6]]], axis=1))
        w_gate = cast(wl[:, cuts[6]:])
        cqn, ckvn, misc, u_ssm, u_pool = _inproj_call(
            x, row(mix_pre_g[l]), sc_m, sh_m, w_small, row(g_cq[l]), row(g_ckv[l]), _pick(S, INPROJ_TILE))

        wqi = jnp.transpose(w_qi[l], (1, 2, 0))
        wqi = jnp.pad(wqi, ((0, 0), (0, LANES - IDX_DIM), (0, 0))).reshape(IDX_HEADS * LANES, D_QLAT)
        y_attn = _attn_call(cqn, ckvn, misc, cast(wqi), cast(jnp.transpose(w_uq[l], (1, 2, 0))),
                            cast(jnp.transpose(w_uv[l], (0, 2, 1))), topk)

        bd, lr, li, cd = _ssm_weights(a_re[l], a_im[l], b_re[l], b_im[l], c_re[l], c_im[l], log_step[l])
        y_ssm = _ssm_call(u_ssm.reshape(S * B, W_SSM), bd, lr, li, cd, row(d_skip[l]), cast(w_glu[l]),
                          row(b_glu[l]), B, tc).reshape(S, B * W_SSM)

        w_pool_bd = cast(jnp.einsum('gcd,gh->gchd', w_pool[l], eye_p).reshape(W_POOL, W_POOL))
        x = _merge_call(x, row(mix_pre_g[l]), sc_m, sh_m, gt_m, row(mix_post_g[l]), w_gate, y_attn, y_ssm,
                        u_pool, w_pool_bd, row(pool_scale[l]), cast(p_a[l]), cast(p_b[l]), cast(p_c[l]),
                        cast(w_out[l]), _pick(S, MERGE_TILE))
        x = _ffn_call(x, row(ffn_pre_g[l]), sc_f, sh_f, gt_f, row(ffn_post_g[l]), cast(w_up[l]), conv_w[l],
                      row(conv_b[l]), cast(w_down[l]), ts)
    return x
```

```python
import functools
import math

import jax
import jax.numpy as jnp
from jax import lax
from jax.experimental import pallas as pl
from jax.experimental.pallas import tpu as pltpu

F32 = jnp.float32
_MXU_DTYPE = jnp.bfloat16

CHUNK = 64
N_HEADS = 8
D_QLAT = 256
D_LAT = 128
D_VHEAD = 64
IDX_HEADS = 8
IDX_DIM = 32
DSA_TOPK_MAX = 256
SSM_GROUPS = 16
SSM_GROUP_DIM = 16
SSM_STATE = 64
W_SSM = SSM_GROUPS * SSM_GROUP_DIM
N_STATE = SSM_GROUPS * SSM_STATE
POOL_WINDOWS = (2, 4, 8, 16)
POOL_GROUPS = 4
POOL_GROUP_DIM = 64
W_POOL = POOL_GROUPS * POOL_GROUP_DIM
W_ATTN = N_HEADS * D_VHEAD
N_BRANCH = 3
CONV_WIDTH = 3
RMS_EPS = 1e-6
NEG_INF = -1e30
ATTN_SCALE = D_LAT ** -0.5
IDX_SCALE = IDX_DIM ** -0.5
IDX_HEAD_SCALE = IDX_HEADS ** -0.5
IN_SPLITS = (D_QLAT, D_LAT, IDX_DIM, IDX_HEADS, W_SSM, W_POOL)

LANES = 128
SUBLANES = 8
MISC_W = LANES
W_SMALL = D_QLAT + D_LAT + MISC_W + W_SSM + W_POOL
VMEM_LIMIT = 56 * 1024 * 1024

QB = 256
BISECT_PASSES = 15
SLOPE_PARTS = 3
DENOM_ROWS = 16
LOG2E = math.log2(math.e)
COUNT_ROWS = 4 * SUBLANES
WALK_ROWS = 2 * SUBLANES
POOL_HALO = 16
INPROJ_TILE = 2048
MERGE_TILE = 1024
FFN_TILE = 512
SSM_TIME_CHUNK = 128
MERGE_ROWS = 256
INPROJ_ROWS = 512
SSM_ROW_GROUPS = 2
CONV_HALO = 16


def _mm(a, b):
    return jnp.dot(a.astype(_MXU_DTYPE), b.astype(_MXU_DTYPE), preferred_element_type=F32)


def _rms(x, g):
    return x * lax.rsqrt(jnp.mean(x * x, axis=-1, keepdims=True) + RMS_EPS) * g


def _gelu(x):
    return 0.5 * x * (1.0 + jnp.tanh(math.sqrt(2.0 / math.pi) * (x + 0.044715 * (x * x * x))))


def _sigmoid(x):
    return 1.0 / (1.0 + jnp.exp(-x))


def _params(*sem):
    return pltpu.CompilerParams(dimension_semantics=sem, vmem_limit_bytes=VMEM_LIMIT)


def _mod_kernel(c_ref, w_ref, b_ref, o_ref):
    c = c_ref[...]
    cond = c * _sigmoid(c)
    o_ref[0] = _mm(cond, w_ref[0]) + b_ref[0]


def _mod_call(c, mod_w, mod_b):
    L, D, D6 = mod_w.shape
    B = c.shape[0]
    nt = D6 // D
    return pl.pallas_call(
        _mod_kernel,
        grid=(L, nt),
        in_specs=[pl.BlockSpec((B, D), lambda l, n: (0, 0)),
                  pl.BlockSpec((1, D, D), lambda l, n: (l, 0, n)),
                  pl.BlockSpec((1, 1, D), lambda l, n: (l, 0, n))],
        out_specs=pl.BlockSpec((1, B, D), lambda l, n: (l, 0, n)),
        out_shape=jax.ShapeDtypeStruct((L, B, D6), F32),
        compiler_params=_params("arbitrary", "arbitrary"),
        name="adaln_mod",
    )(c, mod_w.astype(_MXU_DTYPE), mod_b.reshape(L, 1, D6))


def _inproj_kernel(x_ref, g_ref, sc_ref, sh_ref, w_ref, gcq_ref, gckv_ref,
                   cq_o, ckv_o, misc_o, ussm_o, upool_o):
    o0, o1, o2, o3 = D_QLAT, D_QLAT + D_LAT, D_QLAT + D_LAT + MISC_W, D_QLAT + D_LAT + MISC_W + W_SSM
    for r0 in range(0, x_ref.shape[1], INPROJ_ROWS):
        rs = slice(r0, r0 + INPROJ_ROWS)
        h = _rms(x_ref[0, rs, :], g_ref[...]) * (1.0 + sc_ref[0]) + sh_ref[0]
        z = _mm(h, w_ref[...])
        cq_o[0, rs, :] = _rms(z[:, :o0], gcq_ref[...]).astype(cq_o.dtype)
        ckv_o[0, rs, :] = _rms(z[:, o0:o1], gckv_ref[...]).astype(ckv_o.dtype)
        misc_o[0, rs, :] = z[:, o1:o2]
        ussm_o[rs, :] = z[:, o2:o3]
        upool_o[0, rs, :] = z[:, o3:]


def _inproj_call(x, g_pre, sc, sh, w_small, g_cq, g_ckv, ts):
    B, S, D = x.shape
    row = lambda b, i: (b, i, 0)
    per_b = lambda b, i: (b, 0, 0)
    const2 = lambda b, i: (0, 0)
    return pl.pallas_call(
        _inproj_kernel,
        grid=(B, S // ts),
        in_specs=[pl.BlockSpec((1, ts, D), row),
                  pl.BlockSpec((1, D), const2),
                  pl.BlockSpec((1, 1, D), per_b),
                  pl.BlockSpec((1, 1, D), per_b),
                  pl.BlockSpec((D, W_SMALL), const2),
                  pl.BlockSpec((1, D_QLAT), const2),
                  pl.BlockSpec((1, D_LAT), const2)],
        out_specs=[pl.BlockSpec((1, ts, D_QLAT), row),
                   pl.BlockSpec((1, ts, D_LAT), row),
                   pl.BlockSpec((1, ts, MISC_W), row),
                   pl.BlockSpec((ts, W_SSM), lambda b, i: (i, b)),
                   pl.BlockSpec((1, ts, W_POOL), row)],
        out_shape=[jax.ShapeDtypeStruct((B, S, D_QLAT), _MXU_DTYPE),
                   jax.ShapeDtypeStruct((B, S, D_LAT), _MXU_DTYPE),
                   jax.ShapeDtypeStruct((B, S, MISC_W), F32),
                   jax.ShapeDtypeStruct((S, B * W_SSM), F32),
                   jax.ShapeDtypeStruct((B, S, W_POOL), F32)],
        compiler_params=_params("arbitrary", "arbitrary"),
        name="in_projection",
    )(x, g_pre, sc, sh, w_small, g_cq, g_ckv)


def _sortable(i):
    return i ^ ((i >> 31) & jnp.int32(0x7FFFFFFF))


def _attn_kernel(slope_ref, cq_ref, ckv_ref, misc_ref, wqi_ref, wuq_ref, wuv_ref, o_ref,
                 kaug_sc, ckvt_sc, qi_sc, qt_sc, score_sc, lg_sc, tmax_sc, m_sc, acc_sc, yt_sc, *, topk):
    j = pl.program_id(1)
    nkt = ckvt_sc.shape[0]
    q0 = pl.multiple_of(j * QB, QB)

    @pl.when((pl.program_id(0) == 0) & (j == 0))
    def _():
        lane = lax.broadcasted_iota(jnp.int32, (QB, D_LAT), 1)
        s_in = lax.broadcasted_iota(jnp.int32, (QB, D_LAT), 0).astype(F32)
        for kt in range(nkt):
            pos = jnp.where(lane < SLOPE_PARTS, s_in, jnp.where(lane < 2 * SLOPE_PARTS, float(kt), 0.0))
            kaug_sc[kt * QB:(kt + 1) * QB, D_LAT:] = pos.astype(kaug_sc.dtype)
        sub = lax.broadcasted_iota(jnp.int32, (D_LAT, QB), 0)
        for h in range(N_HEADS):
            rows = jnp.zeros((D_LAT, QB), F32)
            for part in range(SLOPE_PARTS):
                rows = jnp.where(sub == part, slope_ref[h, part], rows)
                rows = jnp.where(sub == SLOPE_PARTS + part, slope_ref[h, part] * QB, rows)
            qt_sc[h, D_LAT:, :] = rows.astype(qt_sc.dtype)
        ones_row = jnp.where(lax.broadcasted_iota(jnp.int32, (DENOM_ROWS, QB), 0) == 0, 1.0, 0.0)
        for kt in range(nkt):
            ckvt_sc[kt, D_LAT:, :] = ones_row.astype(ckvt_sc.dtype)

    @pl.when(j == 0)
    def _():
        for kt in range(nkt):
            kv = ckv_ref[0, kt * QB:(kt + 1) * QB, :]
            ckvt_sc[kt, :D_LAT, :] = kv.astype(F32).T.astype(ckvt_sc.dtype)
            kaug_sc[kt * QB:(kt + 1) * QB, :D_LAT] = kv

    cqt = cq_ref[0].astype(F32).T.astype(_MXU_DTYPE)
    misct = misc_ref[0, pl.ds(q0, QB), :].T
    qi_sc[...] = _mm(wqi_ref[...], cqt).astype(qi_sc.dtype)
    for h in range(N_HEADS):
        qt_sc[h, :D_LAT, :] = (_mm(wuq_ref[h], cqt) * (ATTN_SCALE * LOG2E)).astype(qt_sc.dtype)

    kl = lax.broadcasted_iota(jnp.int32, (QB, QB), 0)
    ql = lax.broadcasted_iota(jnp.int32, (QB, QB), 1)
    diag_ok = kl < (ql // CHUNK + 1) * CHUNK
    ahead = 2.0 * jnp.maximum(kl - ql, 0).astype(F32)

    def score_tile(kt):
        k0 = pl.multiple_of(kt * QB, QB)
        kmat = misc_ref[0, pl.ds(k0, QB), :].astype(_MXU_DTYPE)
        acc = jnp.zeros((QB, QB), F32)
        for h in range(IDX_HEADS):
            lg = jnp.dot(kmat, qi_sc[h * LANES:(h + 1) * LANES, :], preferred_element_type=F32)
            wq = misct[IDX_DIM + h:IDX_DIM + h + 1, :] * (IDX_SCALE * IDX_HEAD_SCALE)
            acc = acc + jnp.maximum(lg, 0.0) * wq
        return acc

    def score_body(kt, carry):
        cmin, cmax = carry
        s = score_tile(kt)
        score_sc[pl.ds(pl.multiple_of(kt * QB, QB), QB), :] = s
        return (jnp.minimum(cmin, jnp.min(s, axis=0, keepdims=True)),
                jnp.maximum(cmax, jnp.max(s, axis=0, keepdims=True)))

    big = jnp.full((1, QB), 3.0e38, F32)
    cmin, cmax = lax.fori_loop(0, j, score_body, (big, -big))
    s = score_tile(j)
    cmin = jnp.minimum(cmin, jnp.min(jnp.where(diag_ok, s, 3.0e38), axis=0, keepdims=True))
    cmax = jnp.maximum(cmax, jnp.max(jnp.where(diag_ok, s, -3.0e38), axis=0, keepdims=True))
    score_sc[pl.ds(q0, QB), :] = jnp.where(diag_ok, s, -jnp.inf)

    def rows_reduce(x, op, rows):
        return op(x.reshape(QB // rows, rows, QB), axis=0)

    def count(pred, thr):
        def body(kt, acc):
            t = score_sc[pl.ds(pl.multiple_of(kt * QB, QB), QB), :]
            return acc + rows_reduce(jnp.where(pred(t, thr), 1.0, 0.0), jnp.sum, COUNT_ROWS)
        acc = lax.fori_loop(0, j + 1, body, jnp.zeros((COUNT_ROWS, QB), F32))
        return jnp.sum(acc, axis=0, keepdims=True)

    def count_and_next(v):
        def body(kt, carry):
            acc, below = carry
            t = score_sc[pl.ds(pl.multiple_of(kt * QB, QB), QB), :]
            hit = t >= v
            return (acc + rows_reduce(jnp.where(hit, 1.0, 0.0), jnp.sum, WALK_ROWS),
                    jnp.maximum(below, rows_reduce(jnp.where(hit, -jnp.inf, t), jnp.max, WALK_ROWS)))
        init = (jnp.zeros((WALK_ROWS, QB), F32), jnp.full((WALK_ROWS, QB), -jnp.inf, F32))
        acc, below = lax.fori_loop(0, j + 1, body, init)
        return jnp.sum(acc, axis=0, keepdims=True), jnp.max(below, axis=0, keepdims=True)

    def max_below(v):
        def body(kt, below):
            t = score_sc[pl.ds(pl.multiple_of(kt * QB, QB), QB), :]
            return jnp.maximum(below, rows_reduce(jnp.where(t < v, t, -jnp.inf), jnp.max, COUNT_ROWS))
        below = lax.fori_loop(0, j + 1, body, jnp.full((COUNT_ROWS, QB), -jnp.inf, F32))
        return jnp.max(below, axis=0, keepdims=True)

    ge = lambda t, thr: t >= thr
    gt = lambda t, thr: t > thr

    @pl.when((j + 1) * QB > topk)
    def _():
        kf = jnp.float32(topk)

        def any_lane(flag):
            f = jnp.where(flag, 1.0, 0.0)
            parts = [f[:, k * LANES:(k + 1) * LANES] for k in range(QB // LANES)]
            return jnp.max(functools.reduce(jnp.maximum, parts)) > 0.0

        above_max = pltpu.bitcast(_sortable(_sortable(pltpu.bitcast(cmax, jnp.int32)) + 1), F32)

        def halve(_, c):
            lo, hi = c
            mid = 0.5 * lo + 0.5 * hi
            keep_low = count(ge, mid) >= kf
            return jnp.where(keep_low, mid, lo), jnp.where(keep_low, hi, mid)

        _, hi = lax.fori_loop(0, BISECT_PASSES, halve, (cmin, above_max))

        def unresolved(c):
            return any_lane(c[1] == 0)

        def step(c):
            v, done, thr, cnt_ge = c
            cnt, below = count_and_next(v)
            hit = (done == 0) & ((cnt >= kf) | (v <= cmin))
            thr = jnp.where(hit, v, thr)
            cnt_ge = jnp.where(hit, cnt, cnt_ge)
            done = jnp.where(hit, 1, done)
            return jnp.where(done > 0, v, jnp.maximum(below, cmin)), done, thr, cnt_ge

        zero = jnp.zeros((1, QB), F32)
        first = jnp.maximum(max_below(hi), cmin)
        state = step((first, jnp.zeros((1, QB), jnp.int32), zero, zero))
        _, _, thr, cnt_ge = lax.while_loop(unresolved, step, state)
        ties = any_lane(cnt_ge > kf)

        @pl.when(jnp.logical_not(ties))
        def _():
            def body(kt, _):
                r = pl.ds(pl.multiple_of(kt * QB, QB), QB)
                score_sc[r, :] = jnp.where(score_sc[r, :] >= thr, 0.0, NEG_INF)
                return 0
            lax.fori_loop(0, j + 1, body, 0)

        @pl.when(ties)
        def _():
            need = kf - count(gt, thr)
            tri = (lax.broadcasted_iota(jnp.int32, (QB, QB), 0)
                   >= lax.broadcasted_iota(jnp.int32, (QB, QB), 1)).astype(_MXU_DTYPE)

            def body(kt, seen):
                r = pl.ds(pl.multiple_of(kt * QB, QB), QB)
                t = score_sc[r, :]
                eq = jnp.where(t == thr, 1.0, 0.0)
                rank = seen + jnp.dot(tri, eq.astype(_MXU_DTYPE), preferred_element_type=F32)
                sel = (t > thr) | ((t == thr) & (rank <= need))
                score_sc[r, :] = jnp.where(sel, 0.0, NEG_INF)
                return seen + jnp.sum(eq, axis=0, keepdims=True)
            lax.fori_loop(0, j + 1, body, jnp.zeros((1, QB), F32))

    @pl.when((j + 1) * QB <= topk)
    def _():
        def body(kt, _):
            r = pl.ds(pl.multiple_of(kt * QB, QB), QB)
            score_sc[r, :] = jnp.where(score_sc[r, :] > -jnp.inf, 0.0, NEG_INF)
            return 0
        lax.fori_loop(0, j + 1, body, 0)

    m_sc[...] = jnp.full(m_sc.shape, -3.0e38, F32)
    acc_sc[...] = jnp.zeros(acc_sc.shape, F32)

    def qk_stage(kt, h, ahead_of_query):
        k0 = pl.multiple_of(kt * QB, QB)
        lg = score_sc[pl.ds(k0, QB), :] + jnp.dot(kaug_sc[pl.ds(k0, QB), :], qt_sc[h], preferred_element_type=F32)
        if ahead_of_query is not None:
            lg = lg - slope_ref[h, SLOPE_PARTS] * ahead_of_query
        lg_sc[h] = lg
        tmax_sc[h] = jnp.max(lg, axis=0, keepdims=True)

    def pv_stage(kt, h):
        m = m_sc[h]
        m_new = jnp.maximum(m, tmax_sc[h])
        alpha = jnp.exp2(m - m_new)
        m_sc[h] = m_new
        p = jnp.exp2(lg_sc[h] - m_new)
        acc_sc[h] = alpha * acc_sc[h] + jnp.dot(ckvt_sc[kt], p.astype(_MXU_DTYPE), preferred_element_type=F32)

    @pl.when(j == 0)
    def _():
        for h in range(N_HEADS):
            qk_stage(j, h, ahead)

    @pl.when(j > 0)
    def _():
        for h in range(N_HEADS):
            qk_stage(0, h, None)

        def steady(kt, _):
            for h in range(N_HEADS):
                pv_stage(kt, h)
                qk_stage(kt + 1, h, None)
            return 0

        lax.fori_loop(0, j - 1, steady, 0)
        for h in range(N_HEADS):
            pv_stage(j - 1, h)
            qk_stage(j, h, ahead)

    for h in range(N_HEADS):
        pv_stage(j, h)
    for h in range(N_HEADS):
        out = acc_sc[h, :D_LAT, :] / acc_sc[h, D_LAT:D_LAT + 1, :]
        yt_sc[h * D_VHEAD:(h + 1) * D_VHEAD, :] = _mm(wuv_ref[h], out)
    o_ref[0] = yt_sc[...].T.astype(o_ref.dtype)


def _attn_call(cqn, ckvn, misc, wqi, wuq, wuv, topk):
    B, S, _ = cqn.shape
    nkt = S // QB
    slope = jnp.exp2(-8.0 * jnp.arange(1, N_HEADS + 1, dtype=F32) / N_HEADS) * LOG2E
    parts, rest = [], slope
    for _ in range(SLOPE_PARTS):
        piece = rest.astype(_MXU_DTYPE).astype(F32)
        parts.append(piece)
        rest = rest - piece
    slopes = jnp.stack(parts + [slope], axis=1)
    per_b = lambda b, j: (b, 0, 0)
    const2 = lambda b, j: (0, 0)
    const3 = lambda b, j: (0, 0, 0)
    return pl.pallas_call(
        functools.partial(_attn_kernel, topk=topk),
        grid=(B, nkt),
        in_specs=[pl.BlockSpec(memory_space=pltpu.SMEM),
                  pl.BlockSpec((1, QB, D_QLAT), lambda b, j: (b, j, 0)),
                  pl.BlockSpec((1, S, D_LAT), per_b),
                  pl.BlockSpec((1, S, MISC_W), per_b),
                  pl.BlockSpec((IDX_HEADS * LANES, D_QLAT), const2),
                  pl.BlockSpec((N_HEADS, D_LAT, D_QLAT), const3),
                  pl.BlockSpec((N_HEADS, D_VHEAD, D_LAT), const3)],
        out_specs=pl.BlockSpec((1, QB, W_ATTN), lambda b, j: (b, j, 0)),
        out_shape=jax.ShapeDtypeStruct((B, S, W_ATTN), _MXU_DTYPE),
        scratch_shapes=[pltpu.VMEM((S, 2 * D_LAT), _MXU_DTYPE),
                        pltpu.VMEM((nkt, D_LAT + DENOM_ROWS, QB), _MXU_DTYPE),
                        pltpu.VMEM((IDX_HEADS * LANES, QB), _MXU_DTYPE),
                        pltpu.VMEM((N_HEADS, 2 * D_LAT, QB), _MXU_DTYPE),
                        pltpu.VMEM((S, QB), F32),
                        pltpu.VMEM((N_HEADS, QB, QB), F32),
                        pltpu.VMEM((N_HEADS, 1, QB), F32),
                        pltpu.VMEM((N_HEADS, 1, QB), F32),
                        pltpu.VMEM((N_HEADS, D_LAT + DENOM_ROWS, QB), F32),
                        pltpu.VMEM((W_ATTN, QB), F32)],
        compiler_params=_params("arbitrary", "arbitrary"),
        name="dsa_attention",
    )(slopes, cqn, ckvn, misc, wqi, wuq, wuv)


def _ssm_kernel(u_ref, bd_ref, lr_ref, li_ref, cd_ref, d_ref, wg_ref, bg_ref, o_ref, x_sc, bu_sc, *, nb, cw):
    tc = u_ref.shape[0] // nb

    @pl.when(pl.program_id(0) == 0)
    def _():
        x_sc[...] = jnp.zeros_like(x_sc)

    nrow = u_ref.shape[0]
    groups = [slice(g * nrow // SSM_ROW_GROUPS, (g + 1) * nrow // SSM_ROW_GROUPS) for g in range(SSM_ROW_GROUPS)]
    for rs in groups:
        bu_sc[rs, :] = _mm(u_ref[rs, :], bd_ref[...])
    for c in range(N_STATE // cw):
        re = slice(c * cw, (c + 1) * cw)
        im = slice(N_STATE + c * cw, N_STATE + (c + 1) * cw)
        lr = lr_ref[:, re]
        li = li_ref[:, re]

        def step(t, carry, re=re, im=im, lr=lr, li=li):
            xr, xi = carry
            rows = pl.ds(pl.multiple_of(t * nb, nb), nb)
            nr = lr * xr - li * xi + bu_sc[rows, re]
            ni = lr * xi + li * xr + bu_sc[rows, im]
            bu_sc[rows, re] = nr
            bu_sc[rows, im] = ni
            return nr, ni

        xr, xi = lax.fori_loop(0, tc, step, (x_sc[:, re], x_sc[:, im]), unroll=4)
        x_sc[:, re] = xr
        x_sc[:, im] = xi
    n_tiles = W_SSM // LANES
    span = N_STATE // n_tiles
    for rs in groups:
        cols = []
        for k in range(n_tiles):
            out = slice(k * LANES, (k + 1) * LANES)
            re = slice(k * span, (k + 1) * span)
            im = slice(N_STATE + k * span, N_STATE + (k + 1) * span)
            cols.append(_mm(bu_sc[rs, re], cd_ref[re, out]) + _mm(bu_sc[rs, im], cd_ref[im, out]))
        y = jnp.concatenate(cols, axis=1) + d_ref[...] * u_ref[rs, :]
        z = _gelu(y)
        o_ref[rs, :] = (z * _sigmoid(_mm(z, wg_ref[...]) + bg_ref[...])).astype(o_ref.dtype)


def _ssm_call(u_tm, bd, lr, li, cd, dskip, w_glu, b_glu, nb, tc):
    rows = u_tm.shape[0]
    const2 = lambda i: (0, 0)
    return pl.pallas_call(
        functools.partial(_ssm_kernel, nb=nb, cw=256),
        grid=(rows // (tc * nb),),
        in_specs=[pl.BlockSpec((tc * nb, W_SSM), lambda i: (i, 0)),
                  pl.BlockSpec((W_SSM, 2 * N_STATE), const2),
                  pl.BlockSpec((1, N_STATE), const2),
                  pl.BlockSpec((1, N_STATE), const2),
                  pl.BlockSpec((2 * N_STATE, W_SSM), const2),
                  pl.BlockSpec((1, W_SSM), const2),
                  pl.BlockSpec((W_SSM, W_SSM), const2),
                  pl.BlockSpec((1, W_SSM), const2)],
        out_specs=pl.BlockSpec((tc * nb, W_SSM), lambda i: (i, 0)),
        out_shape=jax.ShapeDtypeStruct((rows, W_SSM), _MXU_DTYPE),
        scratch_shapes=[pltpu.VMEM((nb, 2 * N_STATE), F32),
                        pltpu.VMEM((tc * nb, 2 * N_STATE), F32)],
        compiler_params=_params("arbitrary"),
        name="s5_ssm",
    )(u_tm, bd, lr, li, cd, dskip, w_glu, b_glu)


def _ssm_weights(a_re, a_im, b_re, b_im, c_re, c_im, log_step):
    step = jnp.exp(log_step)[:, None]
    er = jnp.exp(a_re * step)
    ang = a_im * step
    lr, li = er * jnp.cos(ang), er * jnp.sin(ang)
    den = a_re * a_re + a_im * a_im
    fr = ((lr - 1.0) * a_re + li * a_im) / den
    fi = (li * a_re - (lr - 1.0) * a_im) / den
    br = fr[:, :, None] * b_re - fi[:, :, None] * b_im
    bi = fr[:, :, None] * b_im + fi[:, :, None] * b_re
    eye = jnp.eye(SSM_GROUPS, dtype=F32)
    pack_b = lambda m: jnp.einsum('gnp,gh->gphn', m, eye).reshape(W_SSM, N_STATE)
    pack_c = lambda m: jnp.einsum('gpn,gh->gnhp', m, eye).reshape(N_STATE, W_SSM)
    bd = jnp.concatenate([pack_b(br), pack_b(bi)], axis=1)
    cd = jnp.concatenate([pack_c(c_re), pack_c(-c_im)], axis=0)
    return (bd.astype(_MXU_DTYPE), lr.reshape(1, N_STATE), li.reshape(1, N_STATE), cd.astype(_MXU_DTYPE))


def _merge_kernel(x_ref, gpre_ref, sc_ref, sh_ref, gt_ref, gpost_ref, wg_ref, ya_ref, ys_ref,
                  up_ref, uh_ref, wp_ref, ps_ref, pa_ref, pb_ref, pc_ref, wo_ref, o_ref, halo_sc):
    i = pl.program_id(1)
    ts = x_ref.shape[1]

    u = up_ref[0]
    halo_sc[:POOL_HALO, :] = jnp.where(i > 0, uh_ref[0], 0.0)
    halo_sc[POOL_HALO:, :] = u
    lane = lax.broadcasted_iota(jnp.int32, (ts, W_POOL), 1)
    tpos = (i * ts + lax.broadcasted_iota(jnp.int32, (ts, W_POOL), 0) + 1).astype(F32)
    run = u
    pooled = jnp.zeros_like(u)
    prev = 1
    for g, win in enumerate(POOL_WINDOWS):
        for k in range(prev, win):
            run = run + halo_sc[POOL_HALO - k:POOL_HALO - k + ts, :]
        prev = win
        in_group = (lane >= g * POOL_GROUP_DIM) & (lane < (g + 1) * POOL_GROUP_DIM)
        pooled = jnp.where(in_group, run / jnp.minimum(tpos, float(win)), pooled)
    centred = pooled - u

    d = x_ref.shape[-1]
    for r0 in range(0, ts, MERGE_ROWS):
        rs = slice(r0, r0 + MERGE_ROWS)
        x = x_ref[0, rs, :]
        h = _rms(x, gpre_ref[...]) * (1.0 + sc_ref[0]) + sh_ref[0]
        y_pool = _mm(centred[rs], wp_ref[...]) * ps_ref[...]
        merged = (_sigmoid(_mm(h, wg_ref[:, :d])) * _mm(ya_ref[0, rs, :], pa_ref[...])
                  + _sigmoid(_mm(h, wg_ref[:, d:2 * d])) * _mm(ys_ref[rs, :], pb_ref[...])
                  + _sigmoid(_mm(h, wg_ref[:, 2 * d:])) * _mm(y_pool, pc_ref[...]))
        y = _mm(merged, wo_ref[...])
        o_ref[0, rs, :] = x + gt_ref[0] * _rms(y, gpost_ref[...])


def _merge_call(x, g_pre, sc, sh, gt, g_post, w_gate, y_attn, y_ssm_tm, u_pool, w_pool_bd, pool_scale,
                p_a, p_b, p_c, w_out, ts):
    B, S, D = x.shape
    row = lambda b, i: (b, i, 0)
    per_b = lambda b, i: (b, 0, 0)
    const2 = lambda b, i: (0, 0)
    hb = ts // POOL_HALO
    return pl.pallas_call(
        _merge_kernel,
        grid=(B, S // ts),
        in_specs=[pl.BlockSpec((1, ts, D), row),
                  pl.BlockSpec((1, D), const2),
                  pl.BlockSpec((1, 1, D), per_b),
                  pl.BlockSpec((1, 1, D), per_b),
                  pl.BlockSpec((1, 1, D), per_b),
                  pl.BlockSpec((1, D), const2),
                  pl.BlockSpec((D, N_BRANCH * D), const2),
                  pl.BlockSpec((1, ts, W_ATTN), row),
                  pl.BlockSpec((ts, W_SSM), lambda b, i: (i, b)),
                  pl.BlockSpec((1, ts, W_POOL), row),
                  pl.BlockSpec((1, POOL_HALO, W_POOL), lambda b, i: (b, jnp.maximum(i * hb - 1, 0), 0)),
                  pl.BlockSpec((W_POOL, W_POOL), const2),
                  pl.BlockSpec((1, W_POOL), const2),
                  pl.BlockSpec((W_ATTN, D), const2),
                  pl.BlockSpec((W_SSM, D), const2),
                  pl.BlockSpec((W_POOL, D), const2),
                  pl.BlockSpec((D, D), const2)],
        out_specs=pl.BlockSpec((1, ts, D), row),
        out_shape=jax.ShapeDtypeStruct((B, S, D), F32),
        scratch_shapes=[pltpu.VMEM((POOL_HALO + ts, W_POOL), F32)],
        compiler_params=_params("arbitrary", "arbitrary"),
        name="mixer_merge",
    )(x, g_pre, sc, sh, gt, g_post, w_gate, y_attn, y_ssm_tm, u_pool, u_pool, w_pool_bd, pool_scale,
      p_a, p_b, p_c, w_out)


def _ffn_kernel(x_ref, xh_ref, gpre_ref, sc_ref, sh_ref, gt_ref, gpost_ref, wup_ref, cw_ref, cb_ref,
                wdn_ref, o_ref, h_sc, u_sc, a_sc, *, cols, down_cols, ahead):
    i = pl.program_id(1)
    ts = x_ref.shape[1]
    dff = wdn_ref.shape[0]
    x = x_ref[0]
    adaln = lambda v: _rms(v, gpre_ref[...]) * (1.0 + sc_ref[0]) + sh_ref[0]
    h_sc[:CONV_HALO, :] = adaln(xh_ref[0]).astype(h_sc.dtype)
    h_sc[CONV_HALO:, :] = adaln(x).astype(h_sc.dtype)
    keep = jnp.where(i > 0, 1.0, 0.0)
    nchunk = dff // cols
    per_down = down_cols // cols
    col = lambda c, half: slice(half * dff + c * cols, half * dff + (c + 1) * cols)

    def up_project(c):
        for half in range(2):
            up = jnp.dot(h_sc[...], wup_ref[:, col(c, half)], preferred_element_type=F32)
            u_sc[half, :CONV_HALO, col(c, 0)] = up[:CONV_HALO] * keep
            u_sc[half, CONV_HALO:, col(c, 0)] = up[CONV_HALO:]

    def conv(c, half):
        acc = cb_ref[:, col(c, half)]
        for k in range(CONV_WIDTH):
            r0 = CONV_HALO - (CONV_WIDTH - 1) + k
            acc = acc + cw_ref[k:k + 1, col(c, half)] * u_sc[half, r0:r0 + ts, col(c, 0)]
        return acc

    y = jnp.zeros((ts, x.shape[-1]), F32)
    for c in range(min(ahead, nchunk)):
        up_project(c)
    for c in range(nchunk):
        if c + ahead < nchunk:
            up_project(c + ahead)
        a_sc[:, c * cols:(c + 1) * cols] = (_gelu(conv(c, 0)) * conv(c, 1)).astype(a_sc.dtype)
        if (c + 1) % per_down == 0:
            rows = slice((c + 1) * cols - down_cols, (c + 1) * cols)
            y = y + jnp.dot(a_sc[:, rows], wdn_ref[rows, :], preferred_element_type=F32)
    o_ref[0] = x + gt_ref[0] * _rms(y, gpost_ref[...])


def _ffn_call(x, g_pre, sc, sh, gt, g_post, w_up, conv_w, conv_b, w_down, ts, cols=256, down_cols=512, ahead=3):
    B, S, D = x.shape
    dff = w_down.shape[0]
    row = lambda b, i: (b, i, 0)
    per_b = lambda b, i: (b, 0, 0)
    const2 = lambda b, i: (0, 0)
    hb = ts // CONV_HALO
    return pl.pallas_call(
        functools.partial(_ffn_kernel, cols=cols, down_cols=down_cols, ahead=ahead),
        grid=(B, S // ts),
        in_specs=[pl.BlockSpec((1, ts, D), row),
                  pl.BlockSpec((1, CONV_HALO, D), lambda b, i: (b, jnp.maximum(i * hb - 1, 0), 0)),
                  pl.BlockSpec((1, D), const2),
                  pl.BlockSpec((1, 1, D), per_b),
                  pl.BlockSpec((1, 1, D), per_b),
                  pl.BlockSpec((1, 1, D), per_b),
                  pl.BlockSpec((1, D), const2),
                  pl.BlockSpec((D, 2 * dff), const2, pipeline_mode=pl.Buffered(1)),
                  pl.BlockSpec((CONV_WIDTH, 2 * dff), const2),
                  pl.BlockSpec((1, 2 * dff), const2),
                  pl.BlockSpec((dff, D), const2, pipeline_mode=pl.Buffered(1))],
        out_specs=pl.BlockSpec((1, ts, D), row),
        out_shape=jax.ShapeDtypeStruct((B, S, D), F32),
        scratch_shapes=[pltpu.VMEM((CONV_HALO + ts, D), _MXU_DTYPE),
                        pltpu.VMEM((2, CONV_HALO + ts, dff), F32),
                        pltpu.VMEM((ts, dff), _MXU_DTYPE)],
        compiler_params=_params("arbitrary", "arbitrary"),
        name="conv_gated_ffn",
    )(x, x, g_pre, sc, sh, gt, g_post, w_up, conv_w, conv_b, w_down)


def _pick(n, pref):
    t = min(n, pref)
    assert n % t == 0, (n, t)
    return t


def kernel(x, c, mod_w, mod_b, mix_pre_g, mix_post_g, ffn_pre_g, ffn_post_g, w_in, g_cq, w_uq, w_qi, g_ckv, w_uv, a_re, a_im, b_re, b_im, c_re, c_im, d_skip, log_step, w_glu, b_glu, w_pool, pool_scale, p_a, p_b, p_c, w_out, w_up, conv_w, conv_b, w_down):
    B, S, D = x.shape
    depth = mod_w.shape[0]
    assert S % QB == 0 and QB % CHUNK == 0
    topk = min(DSA_TOPK_MAX, S // 4)
    tc = _pick(S, SSM_TIME_CHUNK)
    cast = lambda w: w.astype(_MXU_DTYPE)
    row = lambda v: v.reshape(1, -1)

    mod = _mod_call(c, mod_w, mod_b)
    cuts = [0]
    for wdt in IN_SPLITS:
        cuts.append(cuts[-1] + wdt)
    eye_p = jnp.eye(POOL_GROUPS, dtype=F32)
    for l in range(depth):
        sh_m, sc_m, gt_m, sh_f, sc_f, gt_f = [mod[l][:, None, k * D:(k + 1) * D] for k in range(6)]
        wl = w_in[l]
        pad = jnp.zeros((D, MISC_W - IDX_DIM - IDX_HEADS), F32)
        w_small = cast(jnp.concatenate(
            [wl[:, cuts[0]:cuts[2]], wl[:, cuts[2]:cuts[4]], pad, wl[:, cuts[4]:cuts[6]]], axis=1))
        w_gate = cast(wl[:, cuts[6]:])
        cqn, ckvn, misc, u_ssm, u_pool = _inproj_call(
            x, row(mix_pre_g[l]), sc_m, sh_m, w_small, row(g_cq[l]), row(g_ckv[l]), _pick(S, INPROJ_TILE))

        wqi = jnp.transpose(w_qi[l], (1, 2, 0))
        wqi = jnp.pad(wqi, ((0, 0), (0, LANES - IDX_DIM), (0, 0))).reshape(IDX_HEADS * LANES, D_QLAT)
        y_attn = _attn_call(cqn, ckvn, misc, cast(wqi), cast(jnp.transpose(w_uq[l], (1, 2, 0))),
                            cast(jnp.transpose(w_uv[l], (0, 2, 1))), topk)

        bd, lr, li, cd = _ssm_weights(a_re[l], a_im[l], b_re[l], b_im[l], c_re[l], c_im[l], log_step[l])
        y_ssm = _ssm_call(u_ssm.reshape(S * B, W_SSM), bd, lr, li, cd, row(d_skip[l]), cast(w_glu[l]),
                          row(b_glu[l]), B, tc).reshape(S, B * W_SSM)

        w_pool_bd = cast(jnp.einsum('gcd,gh->gchd', w_pool[l], eye_p).reshape(W_POOL, W_POOL))
        x = _merge_call(x, row(mix_pre_g[l]), sc_m, sh_m, gt_m, row(mix_post_g[l]), w_gate, y_attn, y_ssm,
                        u_pool, w_pool_bd, row(pool_scale[l]), cast(p_a[l]), cast(p_b[l]), cast(p_c[l]),
                        cast(w_out[l]), _pick(S, MERGE_TILE))
        x = _ffn_call(x, row(ffn_pre_g[l]), sc_f, sh_f, gt_f, row(ffn_post_g[l]), cast(w_up[l]), conv_w[l],
                      row(conv_b[l]), cast(w_down[l]), _pick(S, FFN_TILE))
    return x
```

```python
import functools
import math

import jax
import jax.numpy as jnp
from jax import lax
from jax.experimental import pallas as pl
from jax.experimental.pallas import tpu as pltpu

F32 = jnp.float32
_MXU_DTYPE = jnp.bfloat16

CHUNK = 64
N_HEADS = 8
D_QLAT = 256
D_LAT = 128
D_VHEAD = 64
IDX_HEADS = 8
IDX_DIM = 32
DSA_TOPK_MAX = 256
SSM_GROUPS = 16
SSM_GROUP_DIM = 16
SSM_STATE = 64
W_SSM = SSM_GROUPS * SSM_GROUP_DIM
N_STATE = SSM_GROUPS * SSM_STATE
POOL_WINDOWS = (2, 4, 8, 16)
POOL_GROUPS = 4
POOL_GROUP_DIM = 64
W_POOL = POOL_GROUPS * POOL_GROUP_DIM
W_ATTN = N_HEADS * D_VHEAD
N_BRANCH = 3
CONV_WIDTH = 3
RMS_EPS = 1e-6
NEG_INF = -1e30
BIG = 3.0e38
ATTN_SCALE = D_LAT ** -0.5
IDX_SCALE = IDX_DIM ** -0.5
IDX_HEAD_SCALE = IDX_HEADS ** -0.5
IN_SPLITS = (D_QLAT, D_LAT, IDX_DIM, IDX_HEADS, W_SSM, W_POOL)

LANES = 128
SUBLANES = 8
MISC_W = LANES
W_SMALL = D_QLAT + D_LAT + MISC_W + W_SSM + W_POOL
VMEM_LIMIT = 56 * 1024 * 1024

QB = 256
BISECT_PASSES = 15
SLOPE_PARTS = 3
DENOM_ROWS = 16
LOG2E = math.log2(math.e)
COUNT_ROWS = 4 * SUBLANES
WALK_ROWS = 2 * SUBLANES
WALK_UNTESTED = 2
POOL_HALO = 16
INPROJ_TILE = 2048
MERGE_TILE = 1024
FFN_TILE = 512
SSM_TIME_CHUNK = 128
MERGE_ROWS = 256
INPROJ_ROWS = 512
SSM_ROW_GROUPS = 2
CONV_HALO = 16


def _mm(a, b):
    return jnp.dot(a.astype(_MXU_DTYPE), b.astype(_MXU_DTYPE), preferred_element_type=F32)


def _rms(x, g):
    return x * lax.rsqrt(jnp.mean(x * x, axis=-1, keepdims=True) + RMS_EPS) * g


def _gelu(x):
    return 0.5 * x * (1.0 + jnp.tanh(math.sqrt(2.0 / math.pi) * (x + 0.044715 * (x * x * x))))


def _sigmoid(x):
    return 1.0 / (1.0 + jnp.exp(-x))


def _params(*sem):
    return pltpu.CompilerParams(dimension_semantics=sem, vmem_limit_bytes=VMEM_LIMIT)


def _mod_kernel(c_ref, w_ref, b_ref, o_ref):
    c = c_ref[...]
    cond = c * _sigmoid(c)
    o_ref[0] = _mm(cond, w_ref[0]) + b_ref[0]


def _mod_call(c, mod_w, mod_b):
    L, D, D6 = mod_w.shape
    B = c.shape[0]
    nt = D6 // D
    return pl.pallas_call(
        _mod_kernel,
        grid=(L, nt),
        in_specs=[pl.BlockSpec((B, D), lambda l, n: (0, 0)),
                  pl.BlockSpec((1, D, D), lambda l, n: (l, 0, n)),
                  pl.BlockSpec((1, 1, D), lambda l, n: (l, 0, n))],
        out_specs=pl.BlockSpec((1, B, D), lambda l, n: (l, 0, n)),
        out_shape=jax.ShapeDtypeStruct((L, B, D6), F32),
        compiler_params=_params("arbitrary", "arbitrary"),
        name="adaln_mod",
    )(c, mod_w.astype(_MXU_DTYPE), mod_b.reshape(L, 1, D6))


def _inproj_kernel(x_ref, g_ref, sc_ref, sh_ref, w_ref, gcq_ref, gckv_ref,
                   cq_o, ckv_o, misc_o, ussm_o, upool_o):
    o0, o1, o2, o3 = D_QLAT, D_QLAT + D_LAT, D_QLAT + D_LAT + MISC_W, D_QLAT + D_LAT + MISC_W + W_SSM
    for r0 in range(0, x_ref.shape[1], INPROJ_ROWS):
        rs = slice(r0, r0 + INPROJ_ROWS)
        h = _rms(x_ref[0, rs, :], g_ref[...]) * (1.0 + sc_ref[0]) + sh_ref[0]
        z = _mm(h, w_ref[...])
        cq_o[0, rs, :] = _rms(z[:, :o0], gcq_ref[...]).astype(cq_o.dtype)
        ckv_o[0, rs, :] = _rms(z[:, o0:o1], gckv_ref[...]).astype(ckv_o.dtype)
        misc_o[0, rs, :] = z[:, o1:o2]
        ussm_o[rs, :] = z[:, o2:o3]
        upool_o[0, rs, :] = z[:, o3:]


def _inproj_call(x, g_pre, sc, sh, w_small, g_cq, g_ckv, ts):
    B, S, D = x.shape
    row = lambda b, i: (b, i, 0)
    per_b = lambda b, i: (b, 0, 0)
    const2 = lambda b, i: (0, 0)
    return pl.pallas_call(
        _inproj_kernel,
        grid=(B, S // ts),
        in_specs=[pl.BlockSpec((1, ts, D), row),
                  pl.BlockSpec((1, D), const2),
                  pl.BlockSpec((1, 1, D), per_b),
                  pl.BlockSpec((1, 1, D), per_b),
                  pl.BlockSpec((D, W_SMALL), const2),
                  pl.BlockSpec((1, D_QLAT), const2),
                  pl.BlockSpec((1, D_LAT), const2)],
        out_specs=[pl.BlockSpec((1, ts, D_QLAT), row),
                   pl.BlockSpec((1, ts, D_LAT), row),
                   pl.BlockSpec((1, ts, MISC_W), row),
                   pl.BlockSpec((ts, W_SSM), lambda b, i: (i, b)),
                   pl.BlockSpec((1, ts, W_POOL), row)],
        out_shape=[jax.ShapeDtypeStruct((B, S, D_QLAT), _MXU_DTYPE),
                   jax.ShapeDtypeStruct((B, S, D_LAT), _MXU_DTYPE),
                   jax.ShapeDtypeStruct((B, S, MISC_W), F32),
                   jax.ShapeDtypeStruct((S, B * W_SSM), F32),
                   jax.ShapeDtypeStruct((B, S, W_POOL), F32)],
        compiler_params=_params("arbitrary", "arbitrary"),
        name="in_projection",
    )(x, g_pre, sc, sh, w_small, g_cq, g_ckv)


def _sortable(i):
    return i ^ ((i >> 31) & jnp.int32(0x7FFFFFFF))


def _attn_kernel(slope_ref, cq_ref, ckv_ref, misc_ref, wqi_ref, wuq_ref, wuv_ref, o_ref,
                 kaug_sc, ckvt_sc, qi_sc, qt_sc, score_sc, lg_sc, tmax_sc, m_sc, acc_sc, yt_sc, *, topk):
    j = pl.program_id(1)
    nkt = ckvt_sc.shape[0]
    q0 = pl.multiple_of(j * QB, QB)

    @pl.when((pl.program_id(0) == 0) & (j == 0))
    def _():
        lane = lax.broadcasted_iota(jnp.int32, (QB, D_LAT), 1)
        s_in = lax.broadcasted_iota(jnp.int32, (QB, D_LAT), 0).astype(F32)
        for kt in range(nkt):
            pos = jnp.where(lane < SLOPE_PARTS, s_in, jnp.where(lane < 2 * SLOPE_PARTS, float(kt), 0.0))
            kaug_sc[kt * QB:(kt + 1) * QB, D_LAT:] = pos.astype(kaug_sc.dtype)
        sub = lax.broadcasted_iota(jnp.int32, (D_LAT, QB), 0)
        for h in range(N_HEADS):
            rows = jnp.zeros((D_LAT, QB), F32)
            for part in range(SLOPE_PARTS):
                rows = jnp.where(sub == part, slope_ref[h, part], rows)
                rows = jnp.where(sub == SLOPE_PARTS + part, slope_ref[h, part] * QB, rows)
            qt_sc[h, D_LAT:, :] = rows.astype(qt_sc.dtype)
        ones_row = jnp.where(lax.broadcasted_iota(jnp.int32, (DENOM_ROWS, QB), 0) == 0, 1.0, 0.0)
        for kt in range(nkt):
            ckvt_sc[kt, D_LAT:, :] = ones_row.astype(ckvt_sc.dtype)

    @pl.when(j == 0)
    def _():
        for kt in range(nkt):
            kv = ckv_ref[0, kt * QB:(kt + 1) * QB, :]
            ckvt_sc[kt, :D_LAT, :] = kv.astype(F32).T.astype(ckvt_sc.dtype)
            kaug_sc[kt * QB:(kt + 1) * QB, :D_LAT] = kv

    cqt = cq_ref[0].astype(F32).T.astype(_MXU_DTYPE)
    misct = misc_ref[0, pl.ds(q0, QB), :].T
    qi_sc[...] = _mm(wqi_ref[...], cqt).astype(qi_sc.dtype)
    for h in range(N_HEADS):
        qt_sc[h, :D_LAT, :] = (_mm(wuq_ref[h], cqt) * (ATTN_SCALE * LOG2E)).astype(qt_sc.dtype)

    kl = lax.broadcasted_iota(jnp.int32, (QB, QB), 0)
    ql = lax.broadcasted_iota(jnp.int32, (QB, QB), 1)
    diag_ok = kl < (ql // CHUNK + 1) * CHUNK
    ahead = 2.0 * jnp.maximum(kl - ql, 0).astype(F32)

    def score_tile(kt):
        k0 = pl.multiple_of(kt * QB, QB)
        kmat = misc_ref[0, pl.ds(k0, QB), :].astype(_MXU_DTYPE)
        acc = jnp.zeros((QB, QB), F32)
        for h in range(IDX_HEADS):
            lg = jnp.dot(kmat, qi_sc[h * LANES:(h + 1) * LANES, :], preferred_element_type=F32)
            wq = misct[IDX_DIM + h:IDX_DIM + h + 1, :] * (IDX_SCALE * IDX_HEAD_SCALE)
            acc = acc + jnp.maximum(lg, 0.0) * wq
        return acc

    def score_body(kt, carry):
        cmin, cmax = carry
        s = score_tile(kt)
        score_sc[pl.ds(pl.multiple_of(kt * QB, QB), QB), :] = s
        return (jnp.minimum(cmin, jnp.min(s, axis=0, keepdims=True)),
                jnp.maximum(cmax, jnp.max(s, axis=0, keepdims=True)))

    big = jnp.full((1, QB), BIG, F32)
    cmin, cmax = lax.fori_loop(0, j, score_body, (big, -big))
    s = score_tile(j)
    cmin = jnp.minimum(cmin, jnp.min(jnp.where(diag_ok, s, BIG), axis=0, keepdims=True))
    cmax = jnp.maximum(cmax, jnp.max(jnp.where(diag_ok, s, -BIG), axis=0, keepdims=True))
    score_sc[pl.ds(q0, QB), :] = jnp.where(diag_ok, s, -jnp.inf)

    def rows_reduce(x, op, rows):
        return op(x.reshape(QB // rows, rows, QB), axis=0)

    def count(pred, thr):
        def body(kt, acc):
            t = score_sc[pl.ds(pl.multiple_of(kt * QB, QB), QB), :]
            return acc + rows_reduce(jnp.where(pred(t, thr), 1.0, 0.0), jnp.sum, COUNT_ROWS)
        acc = lax.fori_loop(0, j + 1, body, jnp.zeros((COUNT_ROWS, QB), F32))
        return jnp.sum(acc, axis=0, keepdims=True)

    def count_and_next(v):
        def body(kt, carry):
            acc, below = carry
            t = score_sc[pl.ds(pl.multiple_of(kt * QB, QB), QB), :]
            hit = t >= v
            return (acc + rows_reduce(jnp.where(hit, 1.0, 0.0), jnp.sum, WALK_ROWS),
                    jnp.maximum(below, rows_reduce(jnp.where(hit, -jnp.inf, t), jnp.max, WALK_ROWS)))
        init = (jnp.zeros((WALK_ROWS, QB), F32), jnp.full((WALK_ROWS, QB), -jnp.inf, F32))
        acc, below = lax.fori_loop(0, j + 1, body, init)
        return jnp.sum(acc, axis=0, keepdims=True), jnp.max(below, axis=0, keepdims=True)

    def max_below(v):
        def body(kt, below):
            t = score_sc[pl.ds(pl.multiple_of(kt * QB, QB), QB), :]
            return jnp.maximum(below, rows_reduce(jnp.where(t < v, t, -jnp.inf), jnp.max, COUNT_ROWS))
        below = lax.fori_loop(0, j + 1, body, jnp.full((COUNT_ROWS, QB), -jnp.inf, F32))
        return jnp.max(below, axis=0, keepdims=True)

    ge = lambda t, thr: t >= thr
    gt = lambda t, thr: t > thr

    @pl.when((j + 1) * QB > topk)
    def _():
        kf = jnp.float32(topk)

        def any_lane(flag):
            f = jnp.where(flag, 1.0, 0.0)
            parts = [f[:, k * LANES:(k + 1) * LANES] for k in range(QB // LANES)]
            return jnp.max(functools.reduce(jnp.maximum, parts)) > 0.0

        above_max = pltpu.bitcast(_sortable(_sortable(pltpu.bitcast(cmax, jnp.int32)) + 1), F32)

        def halve(_, c):
            lo, hi = c
            mid = 0.5 * lo + 0.5 * hi
            keep_low = count(ge, mid) >= kf
            return jnp.where(keep_low, mid, lo), jnp.where(keep_low, hi, mid)

        _, hi = lax.fori_loop(0, BISECT_PASSES, halve, (cmin, above_max))

        def unresolved(c):
            return any_lane(c[1] == 0)

        def step(c):
            v, done, thr, cnt_ge = c
            cnt, below = count_and_next(v)
            hit = (done == 0) & ((cnt >= kf) | (v <= cmin))
            thr = jnp.where(hit, v, thr)
            cnt_ge = jnp.where(hit, cnt, cnt_ge)
            done = jnp.where(hit, 1, done)
            return jnp.where(done > 0, v, jnp.maximum(below, cmin)), done, thr, cnt_ge

        zero = jnp.zeros((1, QB), F32)
        first = jnp.maximum(max_below(hi), cmin)
        state = (first, jnp.zeros((1, QB), jnp.int32), zero, zero)
        for _ in range(WALK_UNTESTED):
            state = step(state)
        _, _, thr, cnt_ge = lax.while_loop(unresolved, step, state)
        ties = any_lane(cnt_ge > kf)

        @pl.when(jnp.logical_not(ties))
        def _():
            def body(kt, _):
                r = pl.ds(pl.multiple_of(kt * QB, QB), QB)
                score_sc[r, :] = jnp.where(score_sc[r, :] >= thr, 0.0, NEG_INF)
                return 0
            lax.fori_loop(0, j + 1, body, 0)

        @pl.when(ties)
        def _():
            need = kf - count(gt, thr)
            tri = (lax.broadcasted_iota(jnp.int32, (QB, QB), 0)
                   >= lax.broadcasted_iota(jnp.int32, (QB, QB), 1)).astype(_MXU_DTYPE)

            def body(kt, seen):
                r = pl.ds(pl.multiple_of(kt * QB, QB), QB)
                t = score_sc[r, :]
                eq = jnp.where(t == thr, 1.0, 0.0)
                rank = seen + jnp.dot(tri, eq.astype(_MXU_DTYPE), preferred_element_type=F32)
                sel = (t > thr) | ((t == thr) & (rank <= need))
                score_sc[r, :] = jnp.where(sel, 0.0, NEG_INF)
                return seen + jnp.sum(eq, axis=0, keepdims=True)
            lax.fori_loop(0, j + 1, body, jnp.zeros((1, QB), F32))

    @pl.when((j + 1) * QB <= topk)
    def _():
        def body(kt, _):
            r = pl.ds(pl.multiple_of(kt * QB, QB), QB)
            score_sc[r, :] = jnp.where(score_sc[r, :] > -jnp.inf, 0.0, NEG_INF)
            return 0
        lax.fori_loop(0, j + 1, body, 0)

    m_sc[...] = jnp.full(m_sc.shape, -BIG, F32)
    acc_sc[...] = jnp.zeros(acc_sc.shape, F32)

    def qk_stage(kt, h, ahead_of_query):
        k0 = pl.multiple_of(kt * QB, QB)
        lg = score_sc[pl.ds(k0, QB), :] + jnp.dot(kaug_sc[pl.ds(k0, QB), :], qt_sc[h], preferred_element_type=F32)
        if ahead_of_query is not None:
            lg = lg - slope_ref[h, SLOPE_PARTS] * ahead_of_query
        lg_sc[h] = lg
        tmax_sc[h] = jnp.max(lg, axis=0, keepdims=True)

    def pv_stage(kt, h):
        m = m_sc[h]
        m_new = jnp.maximum(m, tmax_sc[h])
        alpha = jnp.exp2(m - m_new)
        m_sc[h] = m_new
        p = jnp.exp2(lg_sc[h] - m_new)
        acc_sc[h] = alpha * acc_sc[h] + jnp.dot(ckvt_sc[kt], p.astype(_MXU_DTYPE), preferred_element_type=F32)

    @pl.when(j == 0)
    def _():
        for h in range(N_HEADS):
            qk_stage(j, h, ahead)

    @pl.when(j > 0)
    def _():
        for h in range(N_HEADS):
            qk_stage(0, h, None)

        def steady(kt, _):
            for h in range(N_HEADS):
                pv_stage(kt, h)
                qk_stage(kt + 1, h, None)
            return 0

        lax.fori_loop(0, j - 1, steady, 0)
        for h in range(N_HEADS):
            pv_stage(j - 1, h)
            qk_stage(j, h, ahead)

    for h in range(N_HEADS):
        pv_stage(j, h)
    for h in range(N_HEADS):
        out = acc_sc[h, :D_LAT, :] / acc_sc[h, D_LAT:D_LAT + 1, :]
        yt_sc[h * D_VHEAD:(h + 1) * D_VHEAD, :] = _mm(wuv_ref[h], out)
    o_ref[0] = yt_sc[...].T.astype(o_ref.dtype)


def _attn_call(cqn, ckvn, misc, wqi, wuq, wuv, topk):
    B, S, _ = cqn.shape
    nkt = S // QB
    slope = jnp.exp2(-8.0 * jnp.arange(1, N_HEADS + 1, dtype=F32) / N_HEADS) * LOG2E
    parts, rest = [], slope
    for _ in range(SLOPE_PARTS):
        piece = rest.astype(_MXU_DTYPE).astype(F32)
        parts.append(piece)
        rest = rest - piece
    slopes = jnp.stack(parts + [slope], axis=1)
    per_b = lambda b, j: (b, 0, 0)
    const2 = lambda b, j: (0, 0)
    const3 = lambda b, j: (0, 0, 0)
    return pl.pallas_call(
        functools.partial(_attn_kernel, topk=topk),
        grid=(B, nkt),
        in_specs=[pl.BlockSpec(memory_space=pltpu.SMEM),
                  pl.BlockSpec((1, QB, D_QLAT), lambda b, j: (b, j, 0)),
                  pl.BlockSpec((1, S, D_LAT), per_b),
                  pl.BlockSpec((1, S, MISC_W), per_b),
                  pl.BlockSpec((IDX_HEADS * LANES, D_QLAT), const2),
                  pl.BlockSpec((N_HEADS, D_LAT, D_QLAT), const3),
                  pl.BlockSpec((N_HEADS, D_VHEAD, D_LAT), const3)],
        out_specs=pl.BlockSpec((1, QB, W_ATTN), lambda b, j: (b, j, 0)),
        out_shape=jax.ShapeDtypeStruct((B, S, W_ATTN), _MXU_DTYPE),
        scratch_shapes=[pltpu.VMEM((S, 2 * D_LAT), _MXU_DTYPE),
                        pltpu.VMEM((nkt, D_LAT + DENOM_ROWS, QB), _MXU_DTYPE),
                        pltpu.VMEM((IDX_HEADS * LANES, QB), _MXU_DTYPE),
                        pltpu.VMEM((N_HEADS, 2 * D_LAT, QB), _MXU_DTYPE),
                        pltpu.VMEM((S, QB), F32),
                        pltpu.VMEM((N_HEADS, QB, QB), F32),
                        pltpu.VMEM((N_HEADS, 1, QB), F32),
                        pltpu.VMEM((N_HEADS, 1, QB), F32),
                        pltpu.VMEM((N_HEADS, D_LAT + DENOM_ROWS, QB), F32),
                        pltpu.VMEM((W_ATTN, QB), F32)],
        compiler_params=_params("arbitrary", "arbitrary"),
        name="dsa_attention",
    )(slopes, cqn, ckvn, misc, wqi, wuq, wuv)


def _ssm_kernel(u_ref, bd_ref, lr_ref, li_ref, cd_ref, d_ref, wg_ref, bg_ref, o_ref, x_sc, bu_sc, *, nb, cw):
    tc = u_ref.shape[0] // nb

    @pl.when(pl.program_id(0) == 0)
    def _():
        x_sc[...] = jnp.zeros_like(x_sc)

    nrow = u_ref.shape[0]
    groups = [slice(g * nrow // SSM_ROW_GROUPS, (g + 1) * nrow // SSM_ROW_GROUPS) for g in range(SSM_ROW_GROUPS)]
    for rs in groups:
        bu_sc[rs, :] = _mm(u_ref[rs, :], bd_ref[...])
    for c in range(N_STATE // cw):
        re = slice(c * cw, (c + 1) * cw)
        im = slice(N_STATE + c * cw, N_STATE + (c + 1) * cw)
        lr = lr_ref[:, re]
        li = li_ref[:, re]

        def step(t, carry, re=re, im=im, lr=lr, li=li):
            xr, xi = carry
            rows = pl.ds(pl.multiple_of(t * nb, nb), nb)
            nr = lr * xr - li * xi + bu_sc[rows, re]
            ni = lr * xi + li * xr + bu_sc[rows, im]
            bu_sc[rows, re] = nr
            bu_sc[rows, im] = ni
            return nr, ni

        xr, xi = lax.fori_loop(0, tc, step, (x_sc[:, re], x_sc[:, im]), unroll=4)
        x_sc[:, re] = xr
        x_sc[:, im] = xi
    n_tiles = W_SSM // LANES
    span = N_STATE // n_tiles
    for rs in groups:
        cols = []
        for k in range(n_tiles):
            out = slice(k * LANES, (k + 1) * LANES)
            re = slice(k * span, (k + 1) * span)
            im = slice(N_STATE + k * span, N_STATE + (k + 1) * span)
            cols.append(_mm(bu_sc[rs, re], cd_ref[re, out]) + _mm(bu_sc[rs, im], cd_ref[im, out]))
        y = jnp.concatenate(cols, axis=1) + d_ref[...] * u_ref[rs, :]
        z = _gelu(y)
        o_ref[rs, :] = (z * _sigmoid(_mm(z, wg_ref[...]) + bg_ref[...])).astype(o_ref.dtype)


def _ssm_call(u_tm, bd, lr, li, cd, dskip, w_glu, b_glu, nb, tc):
    rows = u_tm.shape[0]
    const2 = lambda i: (0, 0)
    return pl.pallas_call(
        functools.partial(_ssm_kernel, nb=nb, cw=256),
        grid=(rows // (tc * nb),),
        in_specs=[pl.BlockSpec((tc * nb, W_SSM), lambda i: (i, 0)),
                  pl.BlockSpec((W_SSM, 2 * N_STATE), const2),
                  pl.BlockSpec((1, N_STATE), const2),
                  pl.BlockSpec((1, N_STATE), const2),
                  pl.BlockSpec((2 * N_STATE, W_SSM), const2),
                  pl.BlockSpec((1, W_SSM), const2),
                  pl.BlockSpec((W_SSM, W_SSM), const2),
                  pl.BlockSpec((1, W_SSM), const2)],
        out_specs=pl.BlockSpec((tc * nb, W_SSM), lambda i: (i, 0)),
        out_shape=jax.ShapeDtypeStruct((rows, W_SSM), _MXU_DTYPE),
        scratch_shapes=[pltpu.VMEM((nb, 2 * N_STATE), F32),
                        pltpu.VMEM((tc * nb, 2 * N_STATE), F32)],
        compiler_params=_params("arbitrary"),
        name="s5_ssm",
    )(u_tm, bd, lr, li, cd, dskip, w_glu, b_glu)


def _ssm_weights(a_re, a_im, b_re, b_im, c_re, c_im, log_step):
    step = jnp.exp(log_step)[:, None]
    er = jnp.exp(a_re * step)
    ang = a_im * step
    lr, li = er * jnp.cos(ang), er * jnp.sin(ang)
    den = a_re * a_re + a_im * a_im
    fr = ((lr - 1.0) * a_re + li * a_im) / den
    fi = (li * a_re - (lr - 1.0) * a_im) / den
    br = fr[:, :, None] * b_re - fi[:, :, None] * b_im
    bi = fr[:, :, None] * b_im + fi[:, :, None] * b_re
    eye = jnp.eye(SSM_GROUPS, dtype=F32)
    pack_b = lambda m: jnp.einsum('gnp,gh->gphn', m, eye).reshape(W_SSM, N_STATE)
    pack_c = lambda m: jnp.einsum('gpn,gh->gnhp', m, eye).reshape(N_STATE, W_SSM)
    bd = jnp.concatenate([pack_b(br), pack_b(bi)], axis=1)
    cd = jnp.concatenate([pack_c(c_re), pack_c(-c_im)], axis=0)
    return (bd.astype(_MXU_DTYPE), lr.reshape(1, N_STATE), li.reshape(1, N_STATE), cd.astype(_MXU_DTYPE))


def _merge_kernel(x_ref, gpre_ref, sc_ref, sh_ref, gt_ref, gpost_ref, wg_ref, ya_ref, ys_ref,
                  up_ref, uh_ref, wp_ref, ps_ref, pa_ref, pb_ref, pc_ref, wo_ref, o_ref, halo_sc):
    i = pl.program_id(1)
    ts = x_ref.shape[1]

    u = up_ref[0]
    halo_sc[:POOL_HALO, :] = jnp.where(i > 0, uh_ref[0], 0.0)
    halo_sc[POOL_HALO:, :] = u
    lane = lax.broadcasted_iota(jnp.int32, (ts, W_POOL), 1)
    tpos = (i * ts + lax.broadcasted_iota(jnp.int32, (ts, W_POOL), 0) + 1).astype(F32)
    run = u
    pooled = jnp.zeros_like(u)
    prev = 1
    for g, win in enumerate(POOL_WINDOWS):
        for k in range(prev, win):
            run = run + halo_sc[POOL_HALO - k:POOL_HALO - k + ts, :]
        prev = win
        in_group = (lane >= g * POOL_GROUP_DIM) & (lane < (g + 1) * POOL_GROUP_DIM)
        pooled = jnp.where(in_group, run / jnp.minimum(tpos, float(win)), pooled)
    centred = pooled - u

    d = x_ref.shape[-1]
    for r0 in range(0, ts, MERGE_ROWS):
        rs = slice(r0, r0 + MERGE_ROWS)
        x = x_ref[0, rs, :]
        h = _rms(x, gpre_ref[...]) * (1.0 + sc_ref[0]) + sh_ref[0]
        y_pool = _mm(centred[rs], wp_ref[...]) * ps_ref[...]
        merged = (_sigmoid(_mm(h, wg_ref[:, :d])) * _mm(ya_ref[0, rs, :], pa_ref[...])
                  + _sigmoid(_mm(h, wg_ref[:, d:2 * d])) * _mm(ys_ref[rs, :], pb_ref[...])
                  + _sigmoid(_mm(h, wg_ref[:, 2 * d:])) * _mm(y_pool, pc_ref[...]))
        y = _mm(merged, wo_ref[...])
        o_ref[0, rs, :] = x + gt_ref[0] * _rms(y, gpost_ref[...])


def _merge_call(x, g_pre, sc, sh, gt, g_post, w_gate, y_attn, y_ssm_tm, u_pool, w_pool_bd, pool_scale,
                p_a, p_b, p_c, w_out, ts):
    B, S, D = x.shape
    row = lambda b, i: (b, i, 0)
    per_b = lambda b, i: (b, 0, 0)
    const2 = lambda b, i: (0, 0)
    hb = ts // POOL_HALO
    return pl.pallas_call(
        _merge_kernel,
        grid=(B, S // ts),
        in_specs=[pl.BlockSpec((1, ts, D), row),
                  pl.BlockSpec((1, D), const2),
                  pl.BlockSpec((1, 1, D), per_b),
                  pl.BlockSpec((1, 1, D), per_b),
                  pl.BlockSpec((1, 1, D), per_b),
                  pl.BlockSpec((1, D), const2),
                  pl.BlockSpec((D, N_BRANCH * D), const2),
                  pl.BlockSpec((1, ts, W_ATTN), row),
                  pl.BlockSpec((ts, W_SSM), lambda b, i: (i, b)),
                  pl.BlockSpec((1, ts, W_POOL), row),
                  pl.BlockSpec((1, POOL_HALO, W_POOL), lambda b, i: (b, jnp.maximum(i * hb - 1, 0), 0)),
                  pl.BlockSpec((W_POOL, W_POOL), const2),
                  pl.BlockSpec((1, W_POOL), const2),
                  pl.BlockSpec((W_ATTN, D), const2),
                  pl.BlockSpec((W_SSM, D), const2),
                  pl.BlockSpec((W_POOL, D), const2),
                  pl.BlockSpec((D, D), const2)],
        out_specs=pl.BlockSpec((1, ts, D), row),
        out_shape=jax.ShapeDtypeStruct((B, S, D), F32),
        scratch_shapes=[pltpu.VMEM((POOL_HALO + ts, W_POOL), F32)],
        compiler_params=_params("arbitrary", "arbitrary"),
        name="mixer_merge",
    )(x, g_pre, sc, sh, gt, g_post, w_gate, y_attn, y_ssm_tm, u_pool, u_pool, w_pool_bd, pool_scale,
      p_a, p_b, p_c, w_out)


def _ffn_kernel(x_ref, xh_ref, gpre_ref, sc_ref, sh_ref, gt_ref, gpost_ref, wup_ref, cw_ref, cb_ref,
                wdn_ref, o_ref, h_sc, u_sc, a_sc, *, cols, down_cols, ahead):
    i = pl.program_id(1)
    ts = x_ref.shape[1]
    dff = wdn_ref.shape[0]
    x = x_ref[0]
    adaln = lambda v: _rms(v, gpre_ref[...]) * (1.0 + sc_ref[0]) + sh_ref[0]
    h_sc[:CONV_HALO, :] = adaln(xh_ref[0]).astype(h_sc.dtype)
    h_sc[CONV_HALO:, :] = adaln(x).astype(h_sc.dtype)
    keep = jnp.where(i > 0, 1.0, 0.0)
    nchunk = dff // cols
    per_down = down_cols // cols
    col = lambda c, half: slice(half * dff + c * cols, half * dff + (c + 1) * cols)

    def up_project(c):
        for half in range(2):
            up = jnp.dot(h_sc[...], wup_ref[:, col(c, half)], preferred_element_type=F32)
            u_sc[half, :CONV_HALO, col(c, 0)] = up[:CONV_HALO] * keep
            u_sc[half, CONV_HALO:, col(c, 0)] = up[CONV_HALO:]

    def conv(c, half):
        acc = cb_ref[:, col(c, half)]
        for k in range(CONV_WIDTH):
            r0 = CONV_HALO - (CONV_WIDTH - 1) + k
            acc = acc + cw_ref[k:k + 1, col(c, half)] * u_sc[half, r0:r0 + ts, col(c, 0)]
        return acc

    y = jnp.zeros((ts, x.shape[-1]), F32)
    for c in range(min(ahead, nchunk)):
        up_project(c)
    for c in range(nchunk):
        if c + ahead < nchunk:
            up_project(c + ahead)
        a_sc[:, c * cols:(c + 1) * cols] = (_gelu(conv(c, 0)) * conv(c, 1)).astype(a_sc.dtype)
        if (c + 1) % per_down == 0:
            rows = slice((c + 1) * cols - down_cols, (c + 1) * cols)
            y = y + jnp.dot(a_sc[:, rows], wdn_ref[rows, :], preferred_element_type=F32)
    o_ref[0] = x + gt_ref[0] * _rms(y, gpost_ref[...])


def _ffn_call(x, g_pre, sc, sh, gt, g_post, w_up, conv_w, conv_b, w_down, ts, cols=256, down_cols=512, ahead=3):
    B, S, D = x.shape
    dff = w_down.shape[0]
    row = lambda b, i: (b, i, 0)
    per_b = lambda b, i: (b, 0, 0)
    const2 = lambda b, i: (0, 0)
    hb = ts // CONV_HALO
    return pl.pallas_call(
        functools.partial(_ffn_kernel, cols=cols, down_cols=down_cols, ahead=ahead),
        grid=(B, S // ts),
        in_specs=[pl.BlockSpec((1, ts, D), row),
                  pl.BlockSpec((1, CONV_HALO, D), lambda b, i: (b, jnp.maximum(i * hb - 1, 0), 0)),
                  pl.BlockSpec((1, D), const2),
                  pl.BlockSpec((1, 1, D), per_b),
                  pl.BlockSpec((1, 1, D), per_b),
                  pl.BlockSpec((1, 1, D), per_b),
                  pl.BlockSpec((1, D), const2),
                  pl.BlockSpec((D, 2 * dff), const2, pipeline_mode=pl.Buffered(1)),
                  pl.BlockSpec((CONV_WIDTH, 2 * dff), const2),
                  pl.BlockSpec((1, 2 * dff), const2),
                  pl.BlockSpec((dff, D), const2, pipeline_mode=pl.Buffered(1))],
        out_specs=pl.BlockSpec((1, ts, D), row),
        out_shape=jax.ShapeDtypeStruct((B, S, D), F32),
        scratch_shapes=[pltpu.VMEM((CONV_HALO + ts, D), _MXU_DTYPE),
                        pltpu.VMEM((2, CONV_HALO + ts, dff), F32),
                        pltpu.VMEM((ts, dff), _MXU_DTYPE)],
        compiler_params=_params("arbitrary", "arbitrary"),
        name="conv_gated_ffn",
    )(x, x, g_pre, sc, sh, gt, g_post, w_up, conv_w, conv_b, w_down)


def _pick(n, pref):
    t = min(n, pref)
    assert n % t == 0, (n, t)
    return t


def kernel(x, c, mod_w, mod_b, mix_pre_g, mix_post_g, ffn_pre_g, ffn_post_g, w_in, g_cq, w_uq, w_qi, g_ckv, w_uv, a_re, a_im, b_re, b_im, c_re, c_im, d_skip, log_step, w_glu, b_glu, w_pool, pool_scale, p_a, p_b, p_c, w_out, w_up, conv_w, conv_b, w_down):
    B, S, D = x.shape
    depth = mod_w.shape[0]
    assert S % QB == 0 and QB % CHUNK == 0 and B % SUBLANES == 0
    assert _pick(S, INPROJ_TILE) % INPROJ_ROWS == 0 and _pick(S, MERGE_TILE) % MERGE_ROWS == 0
    topk = min(DSA_TOPK_MAX, S // 4)
    tc = _pick(S, SSM_TIME_CHUNK)
    cast = lambda w: w.astype(_MXU_DTYPE)
    row = lambda v: v.reshape(1, -1)

    mod = _mod_call(c, mod_w, mod_b)
    cuts = [0]
    for wdt in IN_SPLITS:
        cuts.append(cuts[-1] + wdt)
    eye_p = jnp.eye(POOL_GROUPS, dtype=F32)
    for l in range(depth):
        sh_m, sc_m, gt_m, sh_f, sc_f, gt_f = [mod[l][:, None, k * D:(k + 1) * D] for k in range(6)]
        wl = w_in[l]
        pad = jnp.zeros((D, MISC_W - IDX_DIM - IDX_HEADS), F32)
        w_small = cast(jnp.concatenate(
            [wl[:, cuts[0]:cuts[2]], wl[:, cuts[2]:cuts[4]], pad, wl[:, cuts[4]:cuts[6]]], axis=1))
        w_gate = cast(wl[:, cuts[6]:])
        cqn, ckvn, misc, u_ssm, u_pool = _inproj_call(
            x, row(mix_pre_g[l]), sc_m, sh_m, w_small, row(g_cq[l]), row(g_ckv[l]), _pick(S, INPROJ_TILE))

        wqi = jnp.transpose(w_qi[l], (1, 2, 0))
        wqi = jnp.pad(wqi, ((0, 0), (0, LANES - IDX_DIM), (0, 0))).reshape(IDX_HEADS * LANES, D_QLAT)
        y_attn = _attn_call(cqn, ckvn, misc, cast(wqi), cast(jnp.transpose(w_uq[l], (1, 2, 0))),
                            cast(jnp.transpose(w_uv[l], (0, 2, 1))), topk)

        bd, lr, li, cd = _ssm_weights(a_re[l], a_im[l], b_re[l], b_im[l], c_re[l], c_im[l], log_step[l])
        y_ssm = _ssm_call(u_ssm.reshape(S * B, W_SSM), bd, lr, li, cd, row(d_skip[l]), cast(w_glu[l]),
                          row(b_glu[l]), B, tc).reshape(S, B * W_SSM)

        w_pool_bd = cast(jnp.einsum('gcd,gh->gchd', w_pool[l], eye_p).reshape(W_POOL, W_POOL))
        x = _merge_call(x, row(mix_pre_g[l]), sc_m, sh_m, gt_m, row(mix_post_g[l]), w_gate, y_attn, y_ssm,
                        u_pool, w_pool_bd, row(pool_scale[l]), cast(p_a[l]), cast(p_b[l]), cast(p_c[l]),
                        cast(w_out[l]), _pick(S, MERGE_TILE))
        x = _ffn_call(x, row(ffn_pre_g[l]), sc_f, sh_f, gt_f, row(ffn_post_g[l]), cast(w_up[l]), conv_w[l],
                      row(conv_b[l]), cast(w_down[l]), _pick(S, FFN_TILE))
    return x
```

```python
import functools
import math

import jax
import jax.numpy as jnp
from jax import lax
from jax.experimental import pallas as pl
from jax.experimental.pallas import tpu as pltpu

F32 = jnp.float32
_MXU_DTYPE = jnp.bfloat16

CHUNK = 64
N_HEADS = 8
D_QLAT = 256
D_LAT = 128
D_VHEAD = 64
IDX_HEADS = 8
IDX_DIM = 32
DSA_TOPK_MAX = 256
SSM_GROUPS = 16
SSM_GROUP_DIM = 16
SSM_STATE = 64
W_SSM = SSM_GROUPS * SSM_GROUP_DIM
N_STATE = SSM_GROUPS * SSM_STATE
POOL_WINDOWS = (2, 4, 8, 16)
POOL_GROUPS = 4
POOL_GROUP_DIM = 64
W_POOL = POOL_GROUPS * POOL_GROUP_DIM
W_ATTN = N_HEADS * D_VHEAD
N_BRANCH = 3
CONV_WIDTH = 3
RMS_EPS = 1e-6
NEG_INF = -1e30
BIG = 3.0e38
ATTN_SCALE = D_LAT ** -0.5
IDX_SCALE = IDX_DIM ** -0.5
IDX_HEAD_SCALE = IDX_HEADS ** -0.5
IN_SPLITS = (D_QLAT, D_LAT, IDX_DIM, IDX_HEADS, W_SSM, W_POOL)

LANES = 128
SUBLANES = 8
MISC_W = LANES
W_SMALL = D_QLAT + D_LAT + MISC_W + W_SSM + W_POOL
VMEM_LIMIT = 56 * 1024 * 1024

QB = 256
BISECT_PASSES = 15
SLOPE_PARTS = 3
DENOM_ROWS = 16
LOG2E = math.log2(math.e)
COUNT_ROWS = 4 * SUBLANES
WALK_ROWS = 2 * SUBLANES
WALK_UNTESTED = 2
POOL_HALO = 16
INPROJ_TILE = 2048
MERGE_TILE = 1024
FFN_TILE = 512
SSM_TIME_CHUNK = 128
MERGE_ROWS = 256
INPROJ_ROWS = 512
SSM_ROW_GROUPS = 2
CONV_HALO = 16


def _mm(a, b):
    return jnp.dot(a.astype(_MXU_DTYPE), b.astype(_MXU_DTYPE), preferred_element_type=F32)


def _rms(x, g):
    return x * lax.rsqrt(jnp.mean(x * x, axis=-1, keepdims=True) + RMS_EPS) * g


def _gelu(x):
    return 0.5 * x * (1.0 + jnp.tanh(math.sqrt(2.0 / math.pi) * (x + 0.044715 * (x * x * x))))


def _sigmoid(x):
    return 1.0 / (1.0 + jnp.exp(-x))


def _params(*sem):
    return pltpu.CompilerParams(dimension_semantics=sem, vmem_limit_bytes=VMEM_LIMIT)


def _mod_kernel(c_ref, w_ref, b_ref, o_ref):
    c = c_ref[...]
    cond = c * _sigmoid(c)
    o_ref[0] = _mm(cond, w_ref[0]) + b_ref[0]


def _mod_call(c, mod_w, mod_b):
    L, D, D6 = mod_w.shape
    B = c.shape[0]
    nt = D6 // D
    return pl.pallas_call(
        _mod_kernel,
        grid=(L, nt),
        in_specs=[pl.BlockSpec((B, D), lambda l, n: (0, 0)),
                  pl.BlockSpec((1, D, D), lambda l, n: (l, 0, n)),
                  pl.BlockSpec((1, 1, D), lambda l, n: (l, 0, n))],
        out_specs=pl.BlockSpec((1, B, D), lambda l, n: (l, 0, n)),
        out_shape=jax.ShapeDtypeStruct((L, B, D6), F32),
        compiler_params=_params("arbitrary", "arbitrary"),
        name="adaln_mod",
    )(c, mod_w.astype(_MXU_DTYPE), mod_b.reshape(L, 1, D6))


def _inproj_kernel(x_ref, g_ref, sc_ref, sh_ref, w_ref, gcq_ref, gckv_ref,
                   cq_o, ckv_o, misc_o, ussm_o, upool_o):
    o0, o1, o2, o3 = D_QLAT, D_QLAT + D_LAT, D_QLAT + D_LAT + MISC_W, D_QLAT + D_LAT + MISC_W + W_SSM
    for r0 in range(0, x_ref.shape[1], INPROJ_ROWS):
        rs = slice(r0, r0 + INPROJ_ROWS)
        h = _rms(x_ref[0, rs, :], g_ref[...]) * (1.0 + sc_ref[0]) + sh_ref[0]
        z = _mm(h, w_ref[...])
        cq_o[0, rs, :] = _rms(z[:, :o0], gcq_ref[...]).astype(cq_o.dtype)
        ckv_o[0, rs, :] = _rms(z[:, o0:o1], gckv_ref[...]).astype(ckv_o.dtype)
        misc_o[0, rs, :] = z[:, o1:o2]
        ussm_o[rs, :] = z[:, o2:o3]
        upool_o[0, rs, :] = z[:, o3:]


def _inproj_call(x, g_pre, sc, sh, w_small, g_cq, g_ckv, ts):
    B, S, D = x.shape
    row = lambda b, i: (b, i, 0)
    per_b = lambda b, i: (b, 0, 0)
    const2 = lambda b, i: (0, 0)
    return pl.pallas_call(
        _inproj_kernel,
        grid=(B, S // ts),
        in_specs=[pl.BlockSpec((1, ts, D), row),
                  pl.BlockSpec((1, D), const2),
                  pl.BlockSpec((1, 1, D), per_b),
                  pl.BlockSpec((1, 1, D), per_b),
                  pl.BlockSpec((D, W_SMALL), const2),
                  pl.BlockSpec((1, D_QLAT), const2),
                  pl.BlockSpec((1, D_LAT), const2)],
        out_specs=[pl.BlockSpec((1, ts, D_QLAT), row),
                   pl.BlockSpec((1, ts, D_LAT), row),
                   pl.BlockSpec((1, ts, MISC_W), row),
                   pl.BlockSpec((ts, W_SSM), lambda b, i: (i, b)),
                   pl.BlockSpec((1, ts, W_POOL), row)],
        out_shape=[jax.ShapeDtypeStruct((B, S, D_QLAT), _MXU_DTYPE),
                   jax.ShapeDtypeStruct((B, S, D_LAT), _MXU_DTYPE),
                   jax.ShapeDtypeStruct((B, S, MISC_W), F32),
                   jax.ShapeDtypeStruct((S, B * W_SSM), F32),
                   jax.ShapeDtypeStruct((B, S, W_POOL), F32)],
        compiler_params=_params("arbitrary", "arbitrary"),
        name="in_projection",
    )(x, g_pre, sc, sh, w_small, g_cq, g_ckv)


def _sortable(i):
    return i ^ ((i >> 31) & jnp.int32(0x7FFFFFFF))


def _attn_kernel(slope_ref, cq_ref, ckv_ref, misc_ref, wqi_ref, wuq_ref, wuv_ref, o_ref,
                 kaug_sc, ckvt_sc, qi_sc, qt_sc, score_sc, lg_sc, tmax_sc, m_sc, acc_sc, yt_sc, *, topk):
    j = pl.program_id(1)
    nkt = ckvt_sc.shape[0]
    q0 = pl.multiple_of(j * QB, QB)

    @pl.when((pl.program_id(0) == 0) & (j == 0))
    def _():
        lane = lax.broadcasted_iota(jnp.int32, (QB, D_LAT), 1)
        s_in = lax.broadcasted_iota(jnp.int32, (QB, D_LAT), 0).astype(F32)
        for kt in range(nkt):
            pos = jnp.where(lane < SLOPE_PARTS, s_in, jnp.where(lane < 2 * SLOPE_PARTS, float(kt), 0.0))
            kaug_sc[kt * QB:(kt + 1) * QB, D_LAT:] = pos.astype(kaug_sc.dtype)
        sub = lax.broadcasted_iota(jnp.int32, (D_LAT, QB), 0)
        for h in range(N_HEADS):
            rows = jnp.zeros((D_LAT, QB), F32)
            for part in range(SLOPE_PARTS):
                rows = jnp.where(sub == part, slope_ref[h, part], rows)
                rows = jnp.where(sub == SLOPE_PARTS + part, slope_ref[h, part] * QB, rows)
            qt_sc[h, D_LAT:, :] = rows.astype(qt_sc.dtype)
        ones_row = jnp.where(lax.broadcasted_iota(jnp.int32, (DENOM_ROWS, QB), 0) == 0, 1.0, 0.0)
        for kt in range(nkt):
            ckvt_sc[kt, D_LAT:, :] = ones_row.astype(ckvt_sc.dtype)

    @pl.when(j == 0)
    def _():
        for kt in range(nkt):
            kv = ckv_ref[0, kt * QB:(kt + 1) * QB, :]
            ckvt_sc[kt, :D_LAT, :] = kv.astype(F32).T.astype(ckvt_sc.dtype)
            kaug_sc[kt * QB:(kt + 1) * QB, :D_LAT] = kv

    cqt = cq_ref[0].astype(F32).T.astype(_MXU_DTYPE)
    misct = misc_ref[0, pl.ds(q0, QB), :].T
    qi_sc[...] = _mm(wqi_ref[...], cqt).astype(qi_sc.dtype)
    for h in range(N_HEADS):
        qt_sc[h, :D_LAT, :] = (_mm(wuq_ref[h], cqt) * (ATTN_SCALE * LOG2E)).astype(qt_sc.dtype)

    kl = lax.broadcasted_iota(jnp.int32, (QB, QB), 0)
    ql = lax.broadcasted_iota(jnp.int32, (QB, QB), 1)
    diag_ok = kl < (ql // CHUNK + 1) * CHUNK
    ahead = 2.0 * jnp.maximum(kl - ql, 0).astype(F32)

    def score_tile(kt):
        k0 = pl.multiple_of(kt * QB, QB)
        kmat = misc_ref[0, pl.ds(k0, QB), :].astype(_MXU_DTYPE)
        acc = jnp.zeros((QB, QB), F32)
        for h in range(IDX_HEADS):
            lg = jnp.dot(kmat, qi_sc[h * LANES:(h + 1) * LANES, :], preferred_element_type=F32)
            wq = misct[IDX_DIM + h:IDX_DIM + h + 1, :] * (IDX_SCALE * IDX_HEAD_SCALE)
            acc = acc + jnp.maximum(lg, 0.0) * wq
        return acc

    def score_body(kt, carry):
        cmin, cmax = carry
        s = score_tile(kt)
        score_sc[pl.ds(pl.multiple_of(kt * QB, QB), QB), :] = s
        return (jnp.minimum(cmin, jnp.min(s, axis=0, keepdims=True)),
                jnp.maximum(cmax, jnp.max(s, axis=0, keepdims=True)))

    big = jnp.full((1, QB), BIG, F32)
    cmin, cmax = lax.fori_loop(0, j, score_body, (big, -big))
    s = score_tile(j)
    cmin = jnp.minimum(cmin, jnp.min(jnp.where(diag_ok, s, BIG), axis=0, keepdims=True))
    cmax = jnp.maximum(cmax, jnp.max(jnp.where(diag_ok, s, -BIG), axis=0, keepdims=True))
    score_sc[pl.ds(q0, QB), :] = jnp.where(diag_ok, s, -jnp.inf)

    def rows_reduce(x, op, rows):
        return op(x.reshape(QB // rows, rows, QB), axis=0)

    def count(pred, thr):
        def body(kt, acc):
            t = score_sc[pl.ds(pl.multiple_of(kt * QB, QB), QB), :]
            return acc + rows_reduce(jnp.where(pred(t, thr), 1.0, 0.0), jnp.sum, COUNT_ROWS)
        acc = lax.fori_loop(0, j + 1, body, jnp.zeros((COUNT_ROWS, QB), F32))
        return jnp.sum(acc, axis=0, keepdims=True)

    def count_and_next(v):
        def body(kt, carry):
            acc, below = carry
            t = score_sc[pl.ds(pl.multiple_of(kt * QB, QB), QB), :]
            hit = t >= v
            return (acc + rows_reduce(jnp.where(hit, 1.0, 0.0), jnp.sum, WALK_ROWS),
                    jnp.maximum(below, rows_reduce(jnp.where(hit, -jnp.inf, t), jnp.max, WALK_ROWS)))
        init = (jnp.zeros((WALK_ROWS, QB), F32), jnp.full((WALK_ROWS, QB), -jnp.inf, F32))
        acc, below = lax.fori_loop(0, j + 1, body, init)
        return jnp.sum(acc, axis=0, keepdims=True), jnp.max(below, axis=0, keepdims=True)

    def max_below(v):
        def body(kt, below):
            t = score_sc[pl.ds(pl.multiple_of(kt * QB, QB), QB), :]
            return jnp.maximum(below, rows_reduce(jnp.where(t < v, t, -jnp.inf), jnp.max, COUNT_ROWS))
        below = lax.fori_loop(0, j + 1, body, jnp.full((COUNT_ROWS, QB), -jnp.inf, F32))
        return jnp.max(below, axis=0, keepdims=True)

    ge = lambda t, thr: t >= thr
    gt = lambda t, thr: t > thr

    @pl.when((j + 1) * QB > topk)
    def _():
        kf = jnp.float32(topk)

        def any_lane(flag):
            f = jnp.where(flag, 1.0, 0.0)
            parts = [f[:, k * LANES:(k + 1) * LANES] for k in range(QB // LANES)]
            return jnp.max(functools.reduce(jnp.maximum, parts)) > 0.0

        above_max = pltpu.bitcast(_sortable(_sortable(pltpu.bitcast(cmax, jnp.int32)) + 1), F32)

        def halve(_, c):
            lo, hi = c
            mid = 0.5 * lo + 0.5 * hi
            keep_low = count(ge, mid) >= kf
            return jnp.where(keep_low, mid, lo), jnp.where(keep_low, hi, mid)

        _, hi = lax.fori_loop(0, BISECT_PASSES, halve, (cmin, above_max))

        def unresolved(c):
            return any_lane(c[1] == 0)

        def step(c):
            v, done, thr, cnt_ge = c
            cnt, below = count_and_next(v)
            hit = (done == 0) & ((cnt >= kf) | (v <= cmin))
            thr = jnp.where(hit, v, thr)
            cnt_ge = jnp.where(hit, cnt, cnt_ge)
            done = jnp.where(hit, 1, done)
            return jnp.where(done > 0, v, jnp.maximum(below, cmin)), done, thr, cnt_ge

        zero = jnp.zeros((1, QB), F32)
        first = jnp.maximum(max_below(hi), cmin)
        state = (first, jnp.zeros((1, QB), jnp.int32), zero, zero)
        for _ in range(WALK_UNTESTED):
            state = step(state)
        _, _, thr, cnt_ge = lax.while_loop(unresolved, step, state)
        ties = any_lane(cnt_ge > kf)

        @pl.when(jnp.logical_not(ties))
        def _():
            def body(kt, _):
                r = pl.ds(pl.multiple_of(kt * QB, QB), QB)
                score_sc[r, :] = jnp.where(score_sc[r, :] >= thr, 0.0, NEG_INF)
                return 0
            lax.fori_loop(0, j + 1, body, 0)

        @pl.when(ties)
        def _():
            need = kf - count(gt, thr)
            tri = (lax.broadcasted_iota(jnp.int32, (QB, QB), 0)
                   >= lax.broadcasted_iota(jnp.int32, (QB, QB), 1)).astype(_MXU_DTYPE)

            def body(kt, seen):
                r = pl.ds(pl.multiple_of(kt * QB, QB), QB)
                t = score_sc[r, :]
                eq = jnp.where(t == thr, 1.0, 0.0)
                rank = seen + jnp.dot(tri, eq.astype(_MXU_DTYPE), preferred_element_type=F32)
                sel = (t > thr) | ((t == thr) & (rank <= need))
                score_sc[r, :] = jnp.where(sel, 0.0, NEG_INF)
                return seen + jnp.sum(eq, axis=0, keepdims=True)
            lax.fori_loop(0, j + 1, body, jnp.zeros((1, QB), F32))

    @pl.when((j + 1) * QB <= topk)
    def _():
        def body(kt, _):
            r = pl.ds(pl.multiple_of(kt * QB, QB), QB)
            score_sc[r, :] = jnp.where(score_sc[r, :] > -jnp.inf, 0.0, NEG_INF)
            return 0
        lax.fori_loop(0, j + 1, body, 0)

    m_sc[...] = jnp.full(m_sc.shape, -BIG, F32)
    acc_sc[...] = jnp.zeros(acc_sc.shape, F32)

    def qk_stage(kt, h, ahead_of_query):
        k0 = pl.multiple_of(kt * QB, QB)
        lg = score_sc[pl.ds(k0, QB), :] + jnp.dot(kaug_sc[pl.ds(k0, QB), :], qt_sc[h], preferred_element_type=F32)
        if ahead_of_query is not None:
            lg = lg - slope_ref[h, SLOPE_PARTS] * ahead_of_query
        lg_sc[h] = lg
        tmax_sc[h] = jnp.max(lg, axis=0, keepdims=True)

    def pv_stage(kt, h):
        m = m_sc[h]
        m_new = jnp.maximum(m, tmax_sc[h])
        alpha = jnp.exp2(m - m_new)
        m_sc[h] = m_new
        p = jnp.exp2(lg_sc[h] - m_new)
        acc_sc[h] = alpha * acc_sc[h] + jnp.dot(ckvt_sc[kt], p.astype(_MXU_DTYPE), preferred_element_type=F32)

    @pl.when(j == 0)
    def _():
        for h in range(N_HEADS):
            qk_stage(j, h, ahead)

    @pl.when(j > 0)
    def _():
        for h in range(N_HEADS):
            qk_stage(0, h, None)

        def steady(kt, _):
            for h in range(N_HEADS):
                pv_stage(kt, h)
                qk_stage(kt + 1, h, None)
            return 0

        lax.fori_loop(0, j - 1, steady, 0)
        for h in range(N_HEADS):
            pv_stage(j - 1, h)
            qk_stage(j, h, ahead)

    for h in range(N_HEADS):
        pv_stage(j, h)
    for h in range(N_HEADS):
        out = acc_sc[h, :D_LAT, :] / acc_sc[h, D_LAT:D_LAT + 1, :]
        yt_sc[h * D_VHEAD:(h + 1) * D_VHEAD, :] = _mm(wuv_ref[h], out)
    o_ref[0] = yt_sc[...].T.astype(o_ref.dtype)


def _attn_call(cqn, ckvn, misc, wqi, wuq, wuv, topk):
    B, S, _ = cqn.shape
    nkt = S // QB
    slope = jnp.exp2(-8.0 * jnp.arange(1, N_HEADS + 1, dtype=F32) / N_HEADS) * LOG2E
    parts, rest = [], slope
    for _ in range(SLOPE_PARTS):
        piece = rest.astype(_MXU_DTYPE).astype(F32)
        parts.append(piece)
        rest = rest - piece
    slopes = jnp.stack(parts + [slope], axis=1)
    per_b = lambda b, j: (b, 0, 0)
    const2 = lambda b, j: (0, 0)
    const3 = lambda b, j: (0, 0, 0)
    return pl.pallas_call(
        functools.partial(_attn_kernel, topk=topk),
        grid=(B, nkt),
        in_specs=[pl.BlockSpec(memory_space=pltpu.SMEM),
                  pl.BlockSpec((1, QB, D_QLAT), lambda b, j: (b, j, 0)),
                  pl.BlockSpec((1, S, D_LAT), per_b),
                  pl.BlockSpec((1, S, MISC_W), per_b),
                  pl.BlockSpec((IDX_HEADS * LANES, D_QLAT), const2),
                  pl.BlockSpec((N_HEADS, D_LAT, D_QLAT), const3),
                  pl.BlockSpec((N_HEADS, D_VHEAD, D_LAT), const3)],
        out_specs=pl.BlockSpec((1, QB, W_ATTN), lambda b, j: (b, j, 0)),
        out_shape=jax.ShapeDtypeStruct((B, S, W_ATTN), _MXU_DTYPE),
        scratch_shapes=[pltpu.VMEM((S, 2 * D_LAT), _MXU_DTYPE),
                        pltpu.VMEM((nkt, D_LAT + DENOM_ROWS, QB), _MXU_DTYPE),
                        pltpu.VMEM((IDX_HEADS * LANES, QB), _MXU_DTYPE),
                        pltpu.VMEM((N_HEADS, 2 * D_LAT, QB), _MXU_DTYPE),
                        pltpu.VMEM((S, QB), F32),
                        pltpu.VMEM((N_HEADS, QB, QB), F32),
                        pltpu.VMEM((N_HEADS, 1, QB), F32),
                        pltpu.VMEM((N_HEADS, 1, QB), F32),
                        pltpu.VMEM((N_HEADS, D_LAT + DENOM_ROWS, QB), F32),
                        pltpu.VMEM((W_ATTN, QB), F32)],
        compiler_params=_params("arbitrary", "arbitrary"),
        name="dsa_attention",
    )(slopes, cqn, ckvn, misc, wqi, wuq, wuv)


def _ssm_kernel(u_ref, bd_ref, lr_ref, li_ref, cd_ref, d_ref, wg_ref, bg_ref, o_ref, x_sc, bu_sc, *, nb, cw):
    tc = u_ref.shape[0] // nb

    @pl.when(pl.program_id(0) == 0)
    def _():
        x_sc[...] = jnp.zeros_like(x_sc)

    nrow = u_ref.shape[0]
    groups = [slice(g * nrow // SSM_ROW_GROUPS, (g + 1) * nrow // SSM_ROW_GROUPS) for g in range(SSM_ROW_GROUPS)]
    for rs in groups:
        bu_sc[rs, :] = _mm(u_ref[rs, :], bd_ref[...])
    for c in range(N_STATE // cw):
        re = slice(c * cw, (c + 1) * cw)
        im = slice(N_STATE + c * cw, N_STATE + (c + 1) * cw)
        lr = lr_ref[:, re]
        li = li_ref[:, re]

        def step(t, carry, re=re, im=im, lr=lr, li=li):
            xr, xi = carry
            rows = pl.ds(pl.multiple_of(t * nb, nb), nb)
            nr = lr * xr - li * xi + bu_sc[rows, re]
            ni = lr * xi + li * xr + bu_sc[rows, im]
            bu_sc[rows, re] = nr
            bu_sc[rows, im] = ni
            return nr, ni

        xr, xi = lax.fori_loop(0, tc, step, (x_sc[:, re], x_sc[:, im]), unroll=8)
        x_sc[:, re] = xr
        x_sc[:, im] = xi
    n_tiles = W_SSM // LANES
    span = N_STATE // n_tiles
    for rs in groups:
        cols = []
        for k in range(n_tiles):
            out = slice(k * LANES, (k + 1) * LANES)
            re = slice(k * span, (k + 1) * span)
            im = slice(N_STATE + k * span, N_STATE + (k + 1) * span)
            cols.append(_mm(bu_sc[rs, re], cd_ref[re, out]) + _mm(bu_sc[rs, im], cd_ref[im, out]))
        y = jnp.concatenate(cols, axis=1) + d_ref[...] * u_ref[rs, :]
        z = _gelu(y)
        o_ref[rs, :] = (z * _sigmoid(_mm(z, wg_ref[...]) + bg_ref[...])).astype(o_ref.dtype)


def _ssm_call(u_tm, bd, lr, li, cd, dskip, w_glu, b_glu, nb, tc):
    rows = u_tm.shape[0]
    const2 = lambda i: (0, 0)
    return pl.pallas_call(
        functools.partial(_ssm_kernel, nb=nb, cw=256),
        grid=(rows // (tc * nb),),
        in_specs=[pl.BlockSpec((tc * nb, W_SSM), lambda i: (i, 0)),
                  pl.BlockSpec((W_SSM, 2 * N_STATE), const2),
                  pl.BlockSpec((1, N_STATE), const2),
                  pl.BlockSpec((1, N_STATE), const2),
                  pl.BlockSpec((2 * N_STATE, W_SSM), const2),
                  pl.BlockSpec((1, W_SSM), const2),
                  pl.BlockSpec((W_SSM, W_SSM), const2),
                  pl.BlockSpec((1, W_SSM), const2)],
        out_specs=pl.BlockSpec((tc * nb, W_SSM), lambda i: (i, 0)),
        out_shape=jax.ShapeDtypeStruct((rows, W_SSM), _MXU_DTYPE),
        scratch_shapes=[pltpu.VMEM((nb, 2 * N_STATE), F32),
                        pltpu.VMEM((tc * nb, 2 * N_STATE), F32)],
        compiler_params=_params("arbitrary"),
        name="s5_ssm",
    )(u_tm, bd, lr, li, cd, dskip, w_glu, b_glu)


def _ssm_weights(a_re, a_im, b_re, b_im, c_re, c_im, log_step):
    step = jnp.exp(log_step)[:, None]
    er = jnp.exp(a_re * step)
    ang = a_im * step
    lr, li = er * jnp.cos(ang), er * jnp.sin(ang)
    den = a_re * a_re + a_im * a_im
    fr = ((lr - 1.0) * a_re + li * a_im) / den
    fi = (li * a_re - (lr - 1.0) * a_im) / den
    br = fr[:, :, None] * b_re - fi[:, :, None] * b_im
    bi = fr[:, :, None] * b_im + fi[:, :, None] * b_re
    eye = jnp.eye(SSM_GROUPS, dtype=F32)
    pack_b = lambda m: jnp.einsum('gnp,gh->gphn', m, eye).reshape(W_SSM, N_STATE)
    pack_c = lambda m: jnp.einsum('gpn,gh->gnhp', m, eye).reshape(N_STATE, W_SSM)
    bd = jnp.concatenate([pack_b(br), pack_b(bi)], axis=1)
    cd = jnp.concatenate([pack_c(c_re), pack_c(-c_im)], axis=0)
    return (bd.astype(_MXU_DTYPE), lr.reshape(1, N_STATE), li.reshape(1, N_STATE), cd.astype(_MXU_DTYPE))


def _merge_kernel(x_ref, gpre_ref, sc_ref, sh_ref, gt_ref, gpost_ref, wg_ref, ya_ref, ys_ref,
                  up_ref, uh_ref, wp_ref, ps_ref, pa_ref, pb_ref, pc_ref, wo_ref, o_ref, halo_sc):
    i = pl.program_id(1)
    ts = x_ref.shape[1]

    u = up_ref[0]
    halo_sc[:POOL_HALO, :] = jnp.where(i > 0, uh_ref[0], 0.0)
    halo_sc[POOL_HALO:, :] = u
    lane = lax.broadcasted_iota(jnp.int32, (ts, W_POOL), 1)
    tpos = (i * ts + lax.broadcasted_iota(jnp.int32, (ts, W_POOL), 0) + 1).astype(F32)
    run = u
    pooled = jnp.zeros_like(u)
    prev = 1
    for g, win in enumerate(POOL_WINDOWS):
        for k in range(prev, win):
            run = run + halo_sc[POOL_HALO - k:POOL_HALO - k + ts, :]
        prev = win
        in_group = (lane >= g * POOL_GROUP_DIM) & (lane < (g + 1) * POOL_GROUP_DIM)
        pooled = jnp.where(in_group, run / jnp.minimum(tpos, float(win)), pooled)
    centred = pooled - u

    d = x_ref.shape[-1]
    for r0 in range(0, ts, MERGE_ROWS):
        rs = slice(r0, r0 + MERGE_ROWS)
        x = x_ref[0, rs, :]
        h = _rms(x, gpre_ref[...]) * (1.0 + sc_ref[0]) + sh_ref[0]
        y_pool = _mm(centred[rs], wp_ref[...]) * ps_ref[...]
        merged = (_sigmoid(_mm(h, wg_ref[:, :d])) * _mm(ya_ref[0, rs, :], pa_ref[...])
                  + _sigmoid(_mm(h, wg_ref[:, d:2 * d])) * _mm(ys_ref[rs, :], pb_ref[...])
                  + _sigmoid(_mm(h, wg_ref[:, 2 * d:])) * _mm(y_pool, pc_ref[...]))
        y = _mm(merged, wo_ref[...])
        o_ref[0, rs, :] = x + gt_ref[0] * _rms(y, gpost_ref[...])


def _merge_call(x, g_pre, sc, sh, gt, g_post, w_gate, y_attn, y_ssm_tm, u_pool, w_pool_bd, pool_scale,
                p_a, p_b, p_c, w_out, ts):
    B, S, D = x.shape
    row = lambda b, i: (b, i, 0)
    per_b = lambda b, i: (b, 0, 0)
    const2 = lambda b, i: (0, 0)
    hb = ts // POOL_HALO
    return pl.pallas_call(
        _merge_kernel,
        grid=(B, S // ts),
        in_specs=[pl.BlockSpec((1, ts, D), row),
                  pl.BlockSpec((1, D), const2),
                  pl.BlockSpec((1, 1, D), per_b),
                  pl.BlockSpec((1, 1, D), per_b),
                  pl.BlockSpec((1, 1, D), per_b),
                  pl.BlockSpec((1, D), const2),
                  pl.BlockSpec((D, N_BRANCH * D), const2),
                  pl.BlockSpec((1, ts, W_ATTN), row),
                  pl.BlockSpec((ts, W_SSM), lambda b, i: (i, b)),
                  pl.BlockSpec((1, ts, W_POOL), row),
                  pl.BlockSpec((1, POOL_HALO, W_POOL), lambda b, i: (b, jnp.maximum(i * hb - 1, 0), 0)),
                  pl.BlockSpec((W_POOL, W_POOL), const2),
                  pl.BlockSpec((1, W_POOL), const2),
                  pl.BlockSpec((W_ATTN, D), const2),
                  pl.BlockSpec((W_SSM, D), const2),
                  pl.BlockSpec((W_POOL, D), const2),
                  pl.BlockSpec((D, D), const2)],
        out_specs=pl.BlockSpec((1, ts, D), row),
        out_shape=jax.ShapeDtypeStruct((B, S, D), F32),
        scratch_shapes=[pltpu.VMEM((POOL_HALO + ts, W_POOL), F32)],
        compiler_params=_params("arbitrary", "arbitrary"),
        name="mixer_merge",
    )(x, g_pre, sc, sh, gt, g_post, w_gate, y_attn, y_ssm_tm, u_pool, u_pool, w_pool_bd, pool_scale,
      p_a, p_b, p_c, w_out)


def _ffn_kernel(x_ref, xh_ref, gpre_ref, sc_ref, sh_ref, gt_ref, gpost_ref, wup_ref, cw_ref, cb_ref,
                wdn_ref, o_ref, h_sc, u_sc, a_sc, *, cols, down_cols, ahead):
    i = pl.program_id(1)
    ts = x_ref.shape[1]
    dff = wdn_ref.shape[0]
    x = x_ref[0]
    adaln = lambda v: _rms(v, gpre_ref[...]) * (1.0 + sc_ref[0]) + sh_ref[0]
    h_sc[:CONV_HALO, :] = adaln(xh_ref[0]).astype(h_sc.dtype)
    h_sc[CONV_HALO:, :] = adaln(x).astype(h_sc.dtype)
    keep = jnp.where(i > 0, 1.0, 0.0)
    nchunk = dff // cols
    per_down = down_cols // cols
    col = lambda c, half: slice(half * dff + c * cols, half * dff + (c + 1) * cols)

    def up_project(c):
        for half in range(2):
            up = jnp.dot(h_sc[...], wup_ref[:, col(c, half)], preferred_element_type=F32)
            u_sc[half, :CONV_HALO, col(c, 0)] = up[:CONV_HALO] * keep
            u_sc[half, CONV_HALO:, col(c, 0)] = up[CONV_HALO:]

    def conv(c, half):
        acc = cb_ref[:, col(c, half)]
        for k in range(CONV_WIDTH):
            r0 = CONV_HALO - (CONV_WIDTH - 1) + k
            acc = acc + cw_ref[k:k + 1, col(c, half)] * u_sc[half, r0:r0 + ts, col(c, 0)]
        return acc

    y = jnp.zeros((ts, x.shape[-1]), F32)
    for c in range(min(ahead, nchunk)):
        up_project(c)
    for c in range(nchunk):
        if c + ahead < nchunk:
            up_project(c + ahead)
        a_sc[:, c * cols:(c + 1) * cols] = (_gelu(conv(c, 0)) * conv(c, 1)).astype(a_sc.dtype)
        if (c + 1) % per_down == 0:
            rows = slice((c + 1) * cols - down_cols, (c + 1) * cols)
            y = y + jnp.dot(a_sc[:, rows], wdn_ref[rows, :], preferred_element_type=F32)
    o_ref[0] = x + gt_ref[0] * _rms(y, gpost_ref[...])


def _ffn_call(x, g_pre, sc, sh, gt, g_post, w_up, conv_w, conv_b, w_down, ts, cols=256, down_cols=512, ahead=3):
    B, S, D = x.shape
    dff = w_down.shape[0]
    row = lambda b, i: (b, i, 0)
    per_b = lambda b, i: (b, 0, 0)
    const2 = lambda b, i: (0, 0)
    hb = ts // CONV_HALO
    return pl.pallas_call(
        functools.partial(_ffn_kernel, cols=cols, down_cols=down_cols, ahead=ahead),
        grid=(B, S // ts),
        in_specs=[pl.BlockSpec((1, ts, D), row),
                  pl.BlockSpec((1, CONV_HALO, D), lambda b, i: (b, jnp.maximum(i * hb - 1, 0), 0)),
                  pl.BlockSpec((1, D), const2),
                  pl.BlockSpec((1, 1, D), per_b),
                  pl.BlockSpec((1, 1, D), per_b),
                  pl.BlockSpec((1, 1, D), per_b),
                  pl.BlockSpec((1, D), const2),
                  pl.BlockSpec((D, 2 * dff), const2, pipeline_mode=pl.Buffered(1)),
                  pl.BlockSpec((CONV_WIDTH, 2 * dff), const2),
                  pl.BlockSpec((1, 2 * dff), const2),
                  pl.BlockSpec((dff, D), const2, pipeline_mode=pl.Buffered(1))],
        out_specs=pl.BlockSpec((1, ts, D), row),
        out_shape=jax.ShapeDtypeStruct((B, S, D), F32),
        scratch_shapes=[pltpu.VMEM((CONV_HALO + ts, D), _MXU_DTYPE),
                        pltpu.VMEM((2, CONV_HALO + ts, dff), F32),
                        pltpu.VMEM((ts, dff), _MXU_DTYPE)],
        compiler_params=_params("arbitrary", "arbitrary"),
        name="conv_gated_ffn",
    )(x, x, g_pre, sc, sh, gt, g_post, w_up, conv_w, conv_b, w_down)


def _pick(n, pref):
    t = min(n, pref)
    assert n % t == 0, (n, t)
    return t


def kernel(x, c, mod_w, mod_b, mix_pre_g, mix_post_g, ffn_pre_g, ffn_post_g, w_in, g_cq, w_uq, w_qi, g_ckv, w_uv, a_re, a_im, b_re, b_im, c_re, c_im, d_skip, log_step, w_glu, b_glu, w_pool, pool_scale, p_a, p_b, p_c, w_out, w_up, conv_w, conv_b, w_down):
    B, S, D = x.shape
    depth = mod_w.shape[0]
    assert S % QB == 0 and QB % CHUNK == 0 and B % SUBLANES == 0
    assert _pick(S, INPROJ_TILE) % INPROJ_ROWS == 0 and _pick(S, MERGE_TILE) % MERGE_ROWS == 0
    topk = min(DSA_TOPK_MAX, S // 4)
    tc = _pick(S, SSM_TIME_CHUNK)
    cast = lambda w: w.astype(_MXU_DTYPE)
    row = lambda v: v.reshape(1, -1)

    mod = _mod_call(c, mod_w, mod_b)
    cuts = [0]
    for wdt in IN_SPLITS:
        cuts.append(cuts[-1] + wdt)
    eye_p = jnp.eye(POOL_GROUPS, dtype=F32)
    for l in range(depth):
        sh_m, sc_m, gt_m, sh_f, sc_f, gt_f = [mod[l][:, None, k * D:(k + 1) * D] for k in range(6)]
        wl = w_in[l]
        pad = jnp.zeros((D, MISC_W - IDX_DIM - IDX_HEADS), F32)
        w_small = cast(jnp.concatenate(
            [wl[:, cuts[0]:cuts[2]], wl[:, cuts[2]:cuts[4]], pad, wl[:, cuts[4]:cuts[6]]], axis=1))
        w_gate = cast(wl[:, cuts[6]:])
        cqn, ckvn, misc, u_ssm, u_pool = _inproj_call(
            x, row(mix_pre_g[l]), sc_m, sh_m, w_small, row(g_cq[l]), row(g_ckv[l]), _pick(S, INPROJ_TILE))

        wqi = jnp.transpose(w_qi[l], (1, 2, 0))
        wqi = jnp.pad(wqi, ((0, 0), (0, LANES - IDX_DIM), (0, 0))).reshape(IDX_HEADS * LANES, D_QLAT)
        y_attn = _attn_call(cqn, ckvn, misc, cast(wqi), cast(jnp.transpose(w_uq[l], (1, 2, 0))),
                            cast(jnp.transpose(w_uv[l], (0, 2, 1))), topk)

        bd, lr, li, cd = _ssm_weights(a_re[l], a_im[l], b_re[l], b_im[l], c_re[l], c_im[l], log_step[l])
        y_ssm = _ssm_call(u_ssm.reshape(S * B, W_SSM), bd, lr, li, cd, row(d_skip[l]), cast(w_glu[l]),
                          row(b_glu[l]), B, tc).reshape(S, B * W_SSM)

        w_pool_bd = cast(jnp.einsum('gcd,gh->gchd', w_pool[l], eye_p).reshape(W_POOL, W_POOL))
        x = _merge_call(x, row(mix_pre_g[l]), sc_m, sh_m, gt_m, row(mix_post_g[l]), w_gate, y_attn, y_ssm,
                        u_pool, w_pool_bd, row(pool_scale[l]), cast(p_a[l]), cast(p_b[l]), cast(p_c[l]),
                        cast(w_out[l]), _pick(S, MERGE_TILE))
        x = _ffn_call(x, row(ffn_pre_g[l]), sc_f, sh_f, gt_f, row(ffn_post_g[l]), cast(w_up[l]), conv_w[l],
                      row(conv_b[l]), cast(w_down[l]), _pick(S, FFN_TILE))
    return x
```

```python
import functools
import math

import jax
import jax.numpy as jnp
from jax import lax
from jax.experimental import pallas as pl
from jax.experimental.pallas import tpu as pltpu

F32 = jnp.float32
_MXU_DTYPE = jnp.bfloat16

CHUNK = 64
N_HEADS = 8
D_QLAT = 256
D_LAT = 128
D_VHEAD = 64
IDX_HEADS = 8
IDX_DIM = 32
DSA_TOPK_MAX = 256
SSM_GROUPS = 16
SSM_GROUP_DIM = 16
SSM_STATE = 64
W_SSM = SSM_GROUPS * SSM_GROUP_DIM
N_STATE = SSM_GROUPS * SSM_STATE
POOL_WINDOWS = (2, 4, 8, 16)
POOL_GROUPS = 4
POOL_GROUP_DIM = 64
W_POOL = POOL_GROUPS * POOL_GROUP_DIM
W_ATTN = N_HEADS * D_VHEAD
N_BRANCH = 3
CONV_WIDTH = 3
RMS_EPS = 1e-6
NEG_INF = -1e30
BIG = 3.0e38
ATTN_SCALE = D_LAT ** -0.5
IDX_SCALE = IDX_DIM ** -0.5
IDX_HEAD_SCALE = IDX_HEADS ** -0.5
IN_SPLITS = (D_QLAT, D_LAT, IDX_DIM, IDX_HEADS, W_SSM, W_POOL)

LANES = 128
SUBLANES = 8
MISC_W = LANES
W_SMALL = D_QLAT + D_LAT + MISC_W + W_SSM + W_POOL
VMEM_LIMIT = 56 * 1024 * 1024

QB = 256
BISECT_PASSES = 15
SLOPE_PARTS = 3
DENOM_ROWS = 16
LOG2E = math.log2(math.e)
COUNT_ROWS = 4 * SUBLANES
WALK_ROWS = 2 * SUBLANES
WALK_UNTESTED = 2
POOL_HALO = 16
INPROJ_TILE = 2048
MERGE_TILE = 1024
FFN_TILE = 512
SSM_TIME_CHUNK = 128
MERGE_ROWS = 256
INPROJ_ROWS = 512
SSM_ROW_GROUPS = 2
CONV_HALO = 16


def _mm(a, b):
    return jnp.dot(a.astype(_MXU_DTYPE), b.astype(_MXU_DTYPE), preferred_element_type=F32)


def _rms(x, g):
    return x * lax.rsqrt(jnp.mean(x * x, axis=-1, keepdims=True) + RMS_EPS) * g


def _gelu(x):
    return 0.5 * x * (1.0 + jnp.tanh(math.sqrt(2.0 / math.pi) * (x + 0.044715 * (x * x * x))))


def _sigmoid(x):
    return 1.0 / (1.0 + jnp.exp(-x))


def _params(*sem):
    return pltpu.CompilerParams(dimension_semantics=sem, vmem_limit_bytes=VMEM_LIMIT)


def _mod_kernel(c_ref, w_ref, b_ref, o_ref):
    c = c_ref[...]
    cond = c * _sigmoid(c)
    o_ref[0] = _mm(cond, w_ref[0]) + b_ref[0]


def _mod_call(c, mod_w, mod_b):
    L, D, D6 = mod_w.shape
    B = c.shape[0]
    nt = D6 // D
    return pl.pallas_call(
        _mod_kernel,
        grid=(L, nt),
        in_specs=[pl.BlockSpec((B, D), lambda l, n: (0, 0)),
                  pl.BlockSpec((1, D, D), lambda l, n: (l, 0, n)),
                  pl.BlockSpec((1, 1, D), lambda l, n: (l, 0, n))],
        out_specs=pl.BlockSpec((1, B, D), lambda l, n: (l, 0, n)),
        out_shape=jax.ShapeDtypeStruct((L, B, D6), F32),
        compiler_params=_params("arbitrary", "arbitrary"),
        name="adaln_mod",
    )(c, mod_w.astype(_MXU_DTYPE), mod_b.reshape(L, 1, D6))


def _inproj_kernel(x_ref, g_ref, sc_ref, sh_ref, w_ref, gcq_ref, gckv_ref,
                   cq_o, ckv_o, misc_o, ussm_o, upool_o):
    o0, o1, o2, o3 = D_QLAT, D_QLAT + D_LAT, D_QLAT + D_LAT + MISC_W, D_QLAT + D_LAT + MISC_W + W_SSM
    for r0 in range(0, x_ref.shape[1], INPROJ_ROWS):
        rs = slice(r0, r0 + INPROJ_ROWS)
        h = _rms(x_ref[0, rs, :], g_ref[...]) * (1.0 + sc_ref[0]) + sh_ref[0]
        z = _mm(h, w_ref[...])
        cq_o[0, rs, :] = _rms(z[:, :o0], gcq_ref[...]).astype(cq_o.dtype)
        ckv_o[0, rs, :] = _rms(z[:, o0:o1], gckv_ref[...]).astype(ckv_o.dtype)
        misc_o[0, rs, :] = z[:, o1:o2]
        ussm_o[rs, :] = z[:, o2:o3]
        upool_o[0, rs, :] = z[:, o3:]


def _inproj_call(x, g_pre, sc, sh, w_small, g_cq, g_ckv, ts):
    B, S, D = x.shape
    row = lambda b, i: (b, i, 0)
    per_b = lambda b, i: (b, 0, 0)
    const2 = lambda b, i: (0, 0)
    return pl.pallas_call(
        _inproj_kernel,
        grid=(B, S // ts),
        in_specs=[pl.BlockSpec((1, ts, D), row),
                  pl.BlockSpec((1, D), const2),
                  pl.BlockSpec((1, 1, D), per_b),
                  pl.BlockSpec((1, 1, D), per_b),
                  pl.BlockSpec((D, W_SMALL), const2),
                  pl.BlockSpec((1, D_QLAT), const2),
                  pl.BlockSpec((1, D_LAT), const2)],
        out_specs=[pl.BlockSpec((1, ts, D_QLAT), row),
                   pl.BlockSpec((1, ts, D_LAT), row),
                   pl.BlockSpec((1, ts, MISC_W), row),
                   pl.BlockSpec((ts, W_SSM), lambda b, i: (i, b)),
                   pl.BlockSpec((1, ts, W_POOL), row)],
        out_shape=[jax.ShapeDtypeStruct((B, S, D_QLAT), _MXU_DTYPE),
                   jax.ShapeDtypeStruct((B, S, D_LAT), _MXU_DTYPE),
                   jax.ShapeDtypeStruct((B, S, MISC_W), F32),
                   jax.ShapeDtypeStruct((S, B * W_SSM), F32),
                   jax.ShapeDtypeStruct((B, S, W_POOL), F32)],
        compiler_params=_params("arbitrary", "arbitrary"),
        name="in_projection",
    )(x, g_pre, sc, sh, w_small, g_cq, g_ckv)


def _sortable(i):
    return i ^ ((i >> 31) & jnp.int32(0x7FFFFFFF))


def _attn_kernel(slope_ref, cq_ref, ckv_ref, misc_ref, wqi_ref, wuq_ref, wuv_ref, o_ref,
                 kaug_sc, ckvt_sc, qi_sc, qt_sc, score_sc, lg_sc, tmax_sc, m_sc, acc_sc, yt_sc, *, topk):
    j = pl.program_id(1)
    nkt = ckvt_sc.shape[0]
    q0 = pl.multiple_of(j * QB, QB)

    @pl.when((pl.program_id(0) == 0) & (j == 0))
    def _():
        lane = lax.broadcasted_iota(jnp.int32, (QB, D_LAT), 1)
        s_in = lax.broadcasted_iota(jnp.int32, (QB, D_LAT), 0).astype(F32)
        for kt in range(nkt):
            pos = jnp.where(lane < SLOPE_PARTS, s_in, jnp.where(lane < 2 * SLOPE_PARTS, float(kt), 0.0))
            kaug_sc[kt * QB:(kt + 1) * QB, D_LAT:] = pos.astype(kaug_sc.dtype)
        sub = lax.broadcasted_iota(jnp.int32, (D_LAT, QB), 0)
        for h in range(N_HEADS):
            rows = jnp.zeros((D_LAT, QB), F32)
            for part in range(SLOPE_PARTS):
                rows = jnp.where(sub == part, slope_ref[h, part], rows)
                rows = jnp.where(sub == SLOPE_PARTS + part, slope_ref[h, part] * QB, rows)
            qt_sc[h, D_LAT:, :] = rows.astype(qt_sc.dtype)
        ones_row = jnp.where(lax.broadcasted_iota(jnp.int32, (DENOM_ROWS, QB), 0) == 0, 1.0, 0.0)
        for kt in range(nkt):
            ckvt_sc[kt, D_LAT:, :] = ones_row.astype(ckvt_sc.dtype)

    @pl.when(j == 0)
    def _():
        for kt in range(nkt):
            kv = ckv_ref[0, kt * QB:(kt + 1) * QB, :]
            ckvt_sc[kt, :D_LAT, :] = kv.astype(F32).T.astype(ckvt_sc.dtype)
            kaug_sc[kt * QB:(kt + 1) * QB, :D_LAT] = kv

    cqt = cq_ref[0].astype(F32).T.astype(_MXU_DTYPE)
    misct = misc_ref[0, pl.ds(q0, QB), :].T
    qi_sc[...] = _mm(wqi_ref[...], cqt).astype(qi_sc.dtype)
    for h in range(N_HEADS):
        qt_sc[h, :D_LAT, :] = (_mm(wuq_ref[h], cqt) * (ATTN_SCALE * LOG2E)).astype(qt_sc.dtype)

    kl = lax.broadcasted_iota(jnp.int32, (QB, QB), 0)
    ql = lax.broadcasted_iota(jnp.int32, (QB, QB), 1)
    diag_ok = kl < (ql // CHUNK + 1) * CHUNK
    ahead = 2.0 * jnp.maximum(kl - ql, 0).astype(F32)

    def score_tile(kt):
        k0 = pl.multiple_of(kt * QB, QB)
        kmat = misc_ref[0, pl.ds(k0, QB), :].astype(_MXU_DTYPE)
        acc = jnp.zeros((QB, QB), F32)
        for h in range(IDX_HEADS):
            lg = jnp.dot(kmat, qi_sc[h * LANES:(h + 1) * LANES, :], preferred_element_type=F32)
            wq = misct[IDX_DIM + h:IDX_DIM + h + 1, :] * (IDX_SCALE * IDX_HEAD_SCALE)
            acc = acc + jnp.maximum(lg, 0.0) * wq
        return acc

    def score_body(kt, carry):
        cmin, cmax = carry
        s = score_tile(kt)
        score_sc[pl.ds(pl.multiple_of(kt * QB, QB), QB), :] = s
        return (jnp.minimum(cmin, jnp.min(s, axis=0, keepdims=True)),
                jnp.maximum(cmax, jnp.max(s, axis=0, keepdims=True)))

    big = jnp.full((1, QB), BIG, F32)
    cmin, cmax = lax.fori_loop(0, j, score_body, (big, -big))
    s = score_tile(j)
    cmin = jnp.minimum(cmin, jnp.min(jnp.where(diag_ok, s, BIG), axis=0, keepdims=True))
    cmax = jnp.maximum(cmax, jnp.max(jnp.where(diag_ok, s, -BIG), axis=0, keepdims=True))
    score_sc[pl.ds(q0, QB), :] = jnp.where(diag_ok, s, -jnp.inf)

    def rows_reduce(x, op, rows):
        return op(x.reshape(QB // rows, rows, QB), axis=0)

    def count(pred, thr):
        def body(kt, acc):
            t = score_sc[pl.ds(pl.multiple_of(kt * QB, QB), QB), :]
            return acc + rows_reduce(jnp.where(pred(t, thr), 1.0, 0.0), jnp.sum, COUNT_ROWS)
        acc = lax.fori_loop(0, j + 1, body, jnp.zeros((COUNT_ROWS, QB), F32))
        return jnp.sum(acc, axis=0, keepdims=True)

    def count_and_next(v):
        def body(kt, carry):
            acc, below = carry
            t = score_sc[pl.ds(pl.multiple_of(kt * QB, QB), QB), :]
            hit = t >= v
            return (acc + rows_reduce(jnp.where(hit, 1.0, 0.0), jnp.sum, WALK_ROWS),
                    jnp.maximum(below, rows_reduce(jnp.where(hit, -jnp.inf, t), jnp.max, WALK_ROWS)))
        init = (jnp.zeros((WALK_ROWS, QB), F32), jnp.full((WALK_ROWS, QB), -jnp.inf, F32))
        acc, below = lax.fori_loop(0, j + 1, body, init)
        return jnp.sum(acc, axis=0, keepdims=True), jnp.max(below, axis=0, keepdims=True)

    def max_below(v):
        def body(kt, below):
            t = score_sc[pl.ds(pl.multiple_of(kt * QB, QB), QB), :]
            return jnp.maximum(below, rows_reduce(jnp.where(t < v, t, -jnp.inf), jnp.max, COUNT_ROWS))
        below = lax.fori_loop(0, j + 1, body, jnp.full((COUNT_ROWS, QB), -jnp.inf, F32))
        return jnp.max(below, axis=0, keepdims=True)

    ge = lambda t, thr: t >= thr
    gt = lambda t, thr: t > thr

    @pl.when((j + 1) * QB > topk)
    def _():
        kf = jnp.float32(topk)

        def any_lane(flag):
            f = jnp.where(flag, 1.0, 0.0)
            parts = [f[:, k * LANES:(k + 1) * LANES] for k in range(QB // LANES)]
            return jnp.max(functools.reduce(jnp.maximum, parts)) > 0.0

        above_max = pltpu.bitcast(_sortable(_sortable(pltpu.bitcast(cmax, jnp.int32)) + 1), F32)

        def halve(_, c):
            lo, hi = c
            mid = 0.5 * lo + 0.5 * hi
            keep_low = count(ge, mid) >= kf
            return jnp.where(keep_low, mid, lo), jnp.where(keep_low, hi, mid)

        _, hi = lax.fori_loop(0, BISECT_PASSES, halve, (cmin, above_max))

        def unresolved(c):
            return any_lane(c[1] == 0)

        def step(c):
            v, done, thr, cnt_ge = c
            cnt, below = count_and_next(v)
            hit = (done == 0) & ((cnt >= kf) | (v <= cmin))
            thr = jnp.where(hit, v, thr)
            cnt_ge = jnp.where(hit, cnt, cnt_ge)
            done = jnp.where(hit, 1, done)
            return jnp.where(done > 0, v, jnp.maximum(below, cmin)), done, thr, cnt_ge

        zero = jnp.zeros((1, QB), F32)
        first = jnp.maximum(max_below(hi), cmin)
        state = (first, jnp.zeros((1, QB), jnp.int32), zero, zero)
        for _ in range(WALK_UNTESTED):
            state = step(state)
        _, _, thr, cnt_ge = lax.while_loop(unresolved, step, state)
        ties = any_lane(cnt_ge > kf)

        @pl.when(jnp.logical_not(ties))
        def _():
            def body(kt, _):
                r = pl.ds(pl.multiple_of(kt * QB, QB), QB)
                score_sc[r, :] = jnp.where(score_sc[r, :] >= thr, 0.0, NEG_INF)
                return 0
            lax.fori_loop(0, j + 1, body, 0)

        @pl.when(ties)
        def _():
            need = kf - count(gt, thr)
            tri = (lax.broadcasted_iota(jnp.int32, (QB, QB), 0)
                   >= lax.broadcasted_iota(jnp.int32, (QB, QB), 1)).astype(_MXU_DTYPE)

            def body(kt, seen):
                r = pl.ds(pl.multiple_of(kt * QB, QB), QB)
                t = score_sc[r, :]
                eq = jnp.where(t == thr, 1.0, 0.0)
                rank = seen + jnp.dot(tri, eq.astype(_MXU_DTYPE), preferred_element_type=F32)
                sel = (t > thr) | ((t == thr) & (rank <= need))
                score_sc[r, :] = jnp.where(sel, 0.0, NEG_INF)
                return seen + jnp.sum(eq, axis=0, keepdims=True)
            lax.fori_loop(0, j + 1, body, jnp.zeros((1, QB), F32))

    @pl.when((j + 1) * QB <= topk)
    def _():
        def body(kt, _):
            r = pl.ds(pl.multiple_of(kt * QB, QB), QB)
            score_sc[r, :] = jnp.where(score_sc[r, :] > -jnp.inf, 0.0, NEG_INF)
            return 0
        lax.fori_loop(0, j + 1, body, 0)

    m_sc[...] = jnp.full(m_sc.shape, -BIG, F32)
    acc_sc[...] = jnp.zeros(acc_sc.shape, F32)

    def qk_stage(kt, h, ahead_of_query):
        k0 = pl.multiple_of(kt * QB, QB)
        lg = score_sc[pl.ds(k0, QB), :] + jnp.dot(kaug_sc[pl.ds(k0, QB), :], qt_sc[h], preferred_element_type=F32)
        if ahead_of_query is not None:
            lg = lg - slope_ref[h, SLOPE_PARTS] * ahead_of_query
        lg_sc[h] = lg
        tmax_sc[h] = jnp.max(lg, axis=0, keepdims=True)

    def pv_stage(kt, h):
        m = m_sc[h]
        m_new = jnp.maximum(m, tmax_sc[h])
        alpha = jnp.exp2(m - m_new)
        m_sc[h] = m_new
        p = jnp.exp2(lg_sc[h] - m_new)
        acc_sc[h] = alpha * acc_sc[h] + jnp.dot(ckvt_sc[kt], p.astype(_MXU_DTYPE), preferred_element_type=F32)

    @pl.when(j == 0)
    def _():
        for h in range(N_HEADS):
            qk_stage(j, h, ahead)

    @pl.when(j > 0)
    def _():
        for h in range(N_HEADS):
            qk_stage(0, h, None)

        def steady(kt, _):
            for h in range(N_HEADS):
                pv_stage(kt, h)
                qk_stage(kt + 1, h, None)
            return 0

        lax.fori_loop(0, j - 1, steady, 0)
        for h in range(N_HEADS):
            pv_stage(j - 1, h)
            qk_stage(j, h, ahead)

    for h in range(N_HEADS):
        pv_stage(j, h)
    for h in range(N_HEADS):
        out = acc_sc[h, :D_LAT, :] / acc_sc[h, D_LAT:D_LAT + 1, :]
        yt_sc[h * D_VHEAD:(h + 1) * D_VHEAD, :] = _mm(wuv_ref[h], out)
    o_ref[0] = yt_sc[...].T.astype(o_ref.dtype)


def _attn_call(cqn, ckvn, misc, wqi, wuq, wuv, topk):
    B, S, _ = cqn.shape
    nkt = S // QB
    slope = jnp.exp2(-8.0 * jnp.arange(1, N_HEADS + 1, dtype=F32) / N_HEADS) * LOG2E
    parts, rest = [], slope
    for _ in range(SLOPE_PARTS):
        piece = rest.astype(_MXU_DTYPE).astype(F32)
        parts.append(piece)
        rest = rest - piece
    slopes = jnp.stack(parts + [slope], axis=1)
    per_b = lambda b, j: (b, 0, 0)
    const2 = lambda b, j: (0, 0)
    const3 = lambda b, j: (0, 0, 0)
    return pl.pallas_call(
        functools.partial(_attn_kernel, topk=topk),
        grid=(B, nkt),
        in_specs=[pl.BlockSpec(memory_space=pltpu.SMEM),
                  pl.BlockSpec((1, QB, D_QLAT), lambda b, j: (b, j, 0)),
                  pl.BlockSpec((1, S, D_LAT), per_b),
                  pl.BlockSpec((1, S, MISC_W), per_b),
                  pl.BlockSpec((IDX_HEADS * LANES, D_QLAT), const2),
                  pl.BlockSpec((N_HEADS, D_LAT, D_QLAT), const3),
                  pl.BlockSpec((N_HEADS, D_VHEAD, D_LAT), const3)],
        out_specs=pl.BlockSpec((1, QB, W_ATTN), lambda b, j: (b, j, 0)),
        out_shape=jax.ShapeDtypeStruct((B, S, W_ATTN), _MXU_DTYPE),
        scratch_shapes=[pltpu.VMEM((S, 2 * D_LAT), _MXU_DTYPE),
                        pltpu.VMEM((nkt, D_LAT + DENOM_ROWS, QB), _MXU_DTYPE),
                        pltpu.VMEM((IDX_HEADS * LANES, QB), _MXU_DTYPE),
                        pltpu.VMEM((N_HEADS, 2 * D_LAT, QB), _MXU_DTYPE),
                        pltpu.VMEM((S, QB), F32),
                        pltpu.VMEM((N_HEADS, QB, QB), F32),
                        pltpu.VMEM((N_HEADS, 1, QB), F32),
                        pltpu.VMEM((N_HEADS, 1, QB), F32),
                        pltpu.VMEM((N_HEADS, D_LAT + DENOM_ROWS, QB), F32),
                        pltpu.VMEM((W_ATTN, QB), F32)],
        compiler_params=_params("arbitrary", "arbitrary"),
        name="dsa_attention",
    )(slopes, cqn, ckvn, misc, wqi, wuq, wuv)


def _ssm_kernel(u_ref, bd_ref, lr_ref, li_ref, cd_ref, d_ref, wg_ref, bg_ref, o_ref, x_sc, bu_sc, *, nb, cw):
    tc = u_ref.shape[0] // nb

    @pl.when(pl.program_id(0) == 0)
    def _():
        x_sc[...] = jnp.zeros_like(x_sc)

    nrow = u_ref.shape[0]
    groups = [slice(g * nrow // SSM_ROW_GROUPS, (g + 1) * nrow // SSM_ROW_GROUPS) for g in range(SSM_ROW_GROUPS)]
    for rs in groups:
        bu_sc[rs, :] = _mm(u_ref[rs, :], bd_ref[...])
    for c in range(N_STATE // cw):
        re = slice(c * cw, (c + 1) * cw)
        im = slice(N_STATE + c * cw, N_STATE + (c + 1) * cw)
        lr = lr_ref[:, re]
        li = li_ref[:, re]

        def step(t, carry, re=re, im=im, lr=lr, li=li):
            xr, xi = carry
            rows = pl.ds(pl.multiple_of(t * nb, nb), nb)
            nr = lr * xr - li * xi + bu_sc[rows, re]
            ni = lr * xi + li * xr + bu_sc[rows, im]
            bu_sc[rows, re] = nr
            bu_sc[rows, im] = ni
            return nr, ni

        xr, xi = lax.fori_loop(0, tc, step, (x_sc[:, re], x_sc[:, im]), unroll=16)
        x_sc[:, re] = xr
        x_sc[:, im] = xi
    n_tiles = W_SSM // LANES
    span = N_STATE // n_tiles
    for rs in groups:
        cols = []
        for k in range(n_tiles):
            out = slice(k * LANES, (k + 1) * LANES)
            re = slice(k * span, (k + 1) * span)
            im = slice(N_STATE + k * span, N_STATE + (k + 1) * span)
            cols.append(_mm(bu_sc[rs, re], cd_ref[re, out]) + _mm(bu_sc[rs, im], cd_ref[im, out]))
        y = jnp.concatenate(cols, axis=1) + d_ref[...] * u_ref[rs, :]
        z = _gelu(y)
        o_ref[rs, :] = (z * _sigmoid(_mm(z, wg_ref[...]) + bg_ref[...])).astype(o_ref.dtype)


def _ssm_call(u_tm, bd, lr, li, cd, dskip, w_glu, b_glu, nb, tc):
    rows = u_tm.shape[0]
    const2 = lambda i: (0, 0)
    return pl.pallas_call(
        functools.partial(_ssm_kernel, nb=nb, cw=256),
        grid=(rows // (tc * nb),),
        in_specs=[pl.BlockSpec((tc * nb, W_SSM), lambda i: (i, 0)),
                  pl.BlockSpec((W_SSM, 2 * N_STATE), const2),
                  pl.BlockSpec((1, N_STATE), const2),
                  pl.BlockSpec((1, N_STATE), const2),
                  pl.BlockSpec((2 * N_STATE, W_SSM), const2),
                  pl.BlockSpec((1, W_SSM), const2),
                  pl.BlockSpec((W_SSM, W_SSM), const2),
                  pl.BlockSpec((1, W_SSM), const2)],
        out_specs=pl.BlockSpec((tc * nb, W_SSM), lambda i: (i, 0)),
        out_shape=jax.ShapeDtypeStruct((rows, W_SSM), _MXU_DTYPE),
        scratch_shapes=[pltpu.VMEM((nb, 2 * N_STATE), F32),
                        pltpu.VMEM((tc * nb, 2 * N_STATE), F32)],
        compiler_params=_params("arbitrary"),
        name="s5_ssm",
    )(u_tm, bd, lr, li, cd, dskip, w_glu, b_glu)


def _ssm_weights(a_re, a_im, b_re, b_im, c_re, c_im, log_step):
    step = jnp.exp(log_step)[:, None]
    er = jnp.exp(a_re * step)
    ang = a_im * step
    lr, li = er * jnp.cos(ang), er * jnp.sin(ang)
    den = a_re * a_re + a_im * a_im
    fr = ((lr - 1.0) * a_re + li * a_im) / den
    fi = (li * a_re - (lr - 1.0) * a_im) / den
    br = fr[:, :, None] * b_re - fi[:, :, None] * b_im
    bi = fr[:, :, None] * b_im + fi[:, :, None] * b_re
    eye = jnp.eye(SSM_GROUPS, dtype=F32)
    pack_b = lambda m: jnp.einsum('gnp,gh->gphn', m, eye).reshape(W_SSM, N_STATE)
    pack_c = lambda m: jnp.einsum('gpn,gh->gnhp', m, eye).reshape(N_STATE, W_SSM)
    bd = jnp.concatenate([pack_b(br), pack_b(bi)], axis=1)
    cd = jnp.concatenate([pack_c(c_re), pack_c(-c_im)], axis=0)
    return (bd.astype(_MXU_DTYPE), lr.reshape(1, N_STATE), li.reshape(1, N_STATE), cd.astype(_MXU_DTYPE))


def _merge_kernel(x_ref, gpre_ref, sc_ref, sh_ref, gt_ref, gpost_ref, wg_ref, ya_ref, ys_ref,
                  up_ref, uh_ref, wp_ref, ps_ref, pa_ref, pb_ref, pc_ref, wo_ref, o_ref, halo_sc):
    i = pl.program_id(1)
    ts = x_ref.shape[1]

    u = up_ref[0]
    halo_sc[:POOL_HALO, :] = jnp.where(i > 0, uh_ref[0], 0.0)
    halo_sc[POOL_HALO:, :] = u
    lane = lax.broadcasted_iota(jnp.int32, (ts, W_POOL), 1)
    tpos = (i * ts + lax.broadcasted_iota(jnp.int32, (ts, W_POOL), 0) + 1).astype(F32)
    run = u
    pooled = jnp.zeros_like(u)
    prev = 1
    for g, win in enumerate(POOL_WINDOWS):
        for k in range(prev, win):
            run = run + halo_sc[POOL_HALO - k:POOL_HALO - k + ts, :]
        prev = win
        in_group = (lane >= g * POOL_GROUP_DIM) & (lane < (g + 1) * POOL_GROUP_DIM)
        pooled = jnp.where(in_group, run / jnp.minimum(tpos, float(win)), pooled)
    centred = pooled - u

    d = x_ref.shape[-1]
    for r0 in range(0, ts, MERGE_ROWS):
        rs = slice(r0, r0 + MERGE_ROWS)
        x = x_ref[0, rs, :]
        h = _rms(x, gpre_ref[...]) * (1.0 + sc_ref[0]) + sh_ref[0]
        y_pool = _mm(centred[rs], wp_ref[...]) * ps_ref[...]
        merged = (_sigmoid(_mm(h, wg_ref[:, :d])) * _mm(ya_ref[0, rs, :], pa_ref[...])
                  + _sigmoid(_mm(h, wg_ref[:, d:2 * d])) * _mm(ys_ref[rs, :], pb_ref[...])
                  + _sigmoid(_mm(h, wg_ref[:, 2 * d:])) * _mm(y_pool, pc_ref[...]))
        y = _mm(merged, wo_ref[...])
        o_ref[0, rs, :] = x + gt_ref[0] * _rms(y, gpost_ref[...])


def _merge_call(x, g_pre, sc, sh, gt, g_post, w_gate, y_attn, y_ssm_tm, u_pool, w_pool_bd, pool_scale,
                p_a, p_b, p_c, w_out, ts):
    B, S, D = x.shape
    row = lambda b, i: (b, i, 0)
    per_b = lambda b, i: (b, 0, 0)
    const2 = lambda b, i: (0, 0)
    hb = ts // POOL_HALO
    return pl.pallas_call(
        _merge_kernel,
        grid=(B, S // ts),
        in_specs=[pl.BlockSpec((1, ts, D), row),
                  pl.BlockSpec((1, D), const2),
                  pl.BlockSpec((1, 1, D), per_b),
                  pl.BlockSpec((1, 1, D), per_b),
                  pl.BlockSpec((1, 1, D), per_b),
                  pl.BlockSpec((1, D), const2),
                  pl.BlockSpec((D, N_BRANCH * D), const2),
                  pl.BlockSpec((1, ts, W_ATTN), row),
                  pl.BlockSpec((ts, W_SSM), lambda b, i: (i, b)),
                  pl.BlockSpec((1, ts, W_POOL), row),
                  pl.BlockSpec((1, POOL_HALO, W_POOL), lambda b, i: (b, jnp.maximum(i * hb - 1, 0), 0)),
                  pl.BlockSpec((W_POOL, W_POOL), const2),
                  pl.BlockSpec((1, W_POOL), const2),
                  pl.BlockSpec((W_ATTN, D), const2),
                  pl.BlockSpec((W_SSM, D), const2),
                  pl.BlockSpec((W_POOL, D), const2),
                  pl.BlockSpec((D, D), const2)],
        out_specs=pl.BlockSpec((1, ts, D), row),
        out_shape=jax.ShapeDtypeStruct((B, S, D), F32),
        scratch_shapes=[pltpu.VMEM((POOL_HALO + ts, W_POOL), F32)],
        compiler_params=_params("arbitrary", "arbitrary"),
        name="mixer_merge",
    )(x, g_pre, sc, sh, gt, g_post, w_gate, y_attn, y_ssm_tm, u_pool, u_pool, w_pool_bd, pool_scale,
      p_a, p_b, p_c, w_out)


def _ffn_kernel(x_ref, xh_ref, gpre_ref, sc_ref, sh_ref, gt_ref, gpost_ref, wup_ref, cw_ref, cb_ref,
                wdn_ref, o_ref, h_sc, u_sc, a_sc, *, cols, down_cols, ahead):
    i = pl.program_id(1)
    ts = x_ref.shape[1]
    dff = wdn_ref.shape[0]
    x = x_ref[0]
    adaln = lambda v: _rms(v, gpre_ref[...]) * (1.0 + sc_ref[0]) + sh_ref[0]
    h_sc[:CONV_HALO, :] = adaln(xh_ref[0]).astype(h_sc.dtype)
    h_sc[CONV_HALO:, :] = adaln(x).astype(h_sc.dtype)
    keep = jnp.where(i > 0, 1.0, 0.0)
    nchunk = dff // cols
    per_down = down_cols // cols
    col = lambda c, half: slice(half * dff + c * cols, half * dff + (c + 1) * cols)

    def up_project(c):
        for half in range(2):
            up = jnp.dot(h_sc[...], wup_ref[:, col(c, half)], preferred_element_type=F32)
            u_sc[half, :CONV_HALO, col(c, 0)] = up[:CONV_HALO] * keep
            u_sc[half, CONV_HALO:, col(c, 0)] = up[CONV_HALO:]

    def conv(c, half):
        acc = cb_ref[:, col(c, half)]
        for k in range(CONV_WIDTH):
            r0 = CONV_HALO - (CONV_WIDTH - 1) + k
            acc = acc + cw_ref[k:k + 1, col(c, half)] * u_sc[half, r0:r0 + ts, col(c, 0)]
        return acc

    y = jnp.zeros((ts, x.shape[-1]), F32)
    for c in range(min(ahead, nchunk)):
        up_project(c)
    for c in range(nchunk):
        if c + ahead < nchunk:
            up_project(c + ahead)
        a_sc[:, c * cols:(c + 1) * cols] = (_gelu(conv(c, 0)) * conv(c, 1)).astype(a_sc.dtype)
        if (c + 1) % per_down == 0:
            rows = slice((c + 1) * cols - down_cols, (c + 1) * cols)
            y = y + jnp.dot(a_sc[:, rows], wdn_ref[rows, :], preferred_element_type=F32)
    o_ref[0] = x + gt_ref[0] * _rms(y, gpost_ref[...])


def _ffn_call(x, g_pre, sc, sh, gt, g_post, w_up, conv_w, conv_b, w_down, ts, cols=256, down_cols=512, ahead=3):
    B, S, D = x.shape
    dff = w_down.shape[0]
    row = lambda b, i: (b, i, 0)
    per_b = lambda b, i: (b, 0, 0)
    const2 = lambda b, i: (0, 0)
    hb = ts // CONV_HALO
    return pl.pallas_call(
        functools.partial(_ffn_kernel, cols=cols, down_cols=down_cols, ahead=ahead),
        grid=(B, S // ts),
        in_specs=[pl.BlockSpec((1, ts, D), row),
                  pl.BlockSpec((1, CONV_HALO, D), lambda b, i: (b, jnp.maximum(i * hb - 1, 0), 0)),
                  pl.BlockSpec((1, D), const2),
                  pl.BlockSpec((1, 1, D), per_b),
                  pl.BlockSpec((1, 1, D), per_b),
                  pl.BlockSpec((1, 1, D), per_b),
                  pl.BlockSpec((1, D), const2),
                  pl.BlockSpec((D, 2 * dff), const2, pipeline_mode=pl.Buffered(1)),
                  pl.BlockSpec((CONV_WIDTH, 2 * dff), const2),
                  pl.BlockSpec((1, 2 * dff), const2),
                  pl.BlockSpec((dff, D), const2, pipeline_mode=pl.Buffered(1))],
        out_specs=pl.BlockSpec((1, ts, D), row),
        out_shape=jax.ShapeDtypeStruct((B, S, D), F32),
        scratch_shapes=[pltpu.VMEM((CONV_HALO + ts, D), _MXU_DTYPE),
                        pltpu.VMEM((2, CONV_HALO + ts, dff), F32),
                        pltpu.VMEM((ts, dff), _MXU_DTYPE)],
        compiler_params=_params("arbitrary", "arbitrary"),
        name="conv_gated_ffn",
    )(x, x, g_pre, sc, sh, gt, g_post, w_up, conv_w, conv_b, w_down)


def _pick(n, pref):
    t = min(n, pref)
    assert n % t == 0, (n, t)
    return t


def kernel(x, c, mod_w, mod_b, mix_pre_g, mix_post_g, ffn_pre_g, ffn_post_g, w_in, g_cq, w_uq, w_qi, g_ckv, w_uv, a_re, a_im, b_re, b_im, c_re, c_im, d_skip, log_step, w_glu, b_glu, w_pool, pool_scale, p_a, p_b, p_c, w_out, w_up, conv_w, conv_b, w_down):
    B, S, D = x.shape
    depth = mod_w.shape[0]
    assert S % QB == 0 and QB % CHUNK == 0 and B % SUBLANES == 0
    assert _pick(S, INPROJ_TILE) % INPROJ_ROWS == 0 and _pick(S, MERGE_TILE) % MERGE_ROWS == 0
    topk = min(DSA_TOPK_MAX, S // 4)
    tc = _pick(S, SSM_TIME_CHUNK)
    cast = lambda w: w.astype(_MXU_DTYPE)
    row = lambda v: v.reshape(1, -1)

    mod = _mod_call(c, mod_w, mod_b)
    cuts = [0]
    for wdt in IN_SPLITS:
        cuts.append(cuts[-1] + wdt)
    eye_p = jnp.eye(POOL_GROUPS, dtype=F32)
    for l in range(depth):
        sh_m, sc_m, gt_m, sh_f, sc_f, gt_f = [mod[l][:, None, k * D:(k + 1) * D] for k in range(6)]
        wl = w_in[l]
        pad = jnp.zeros((D, MISC_W - IDX_DIM - IDX_HEADS), F32)
        w_small = cast(jnp.concatenate(
            [wl[:, cuts[0]:cuts[2]], wl[:, cuts[2]:cuts[4]], pad, wl[:, cuts[4]:cuts[6]]], axis=1))
        w_gate = cast(wl[:, cuts[6]:])
        cqn, ckvn, misc, u_ssm, u_pool = _inproj_call(
            x, row(mix_pre_g[l]), sc_m, sh_m, w_small, row(g_cq[l]), row(g_ckv[l]), _pick(S, INPROJ_TILE))

        wqi = jnp.transpose(w_qi[l], (1, 2, 0))
        wqi = jnp.pad(wqi, ((0, 0), (0, LANES - IDX_DIM), (0, 0))).reshape(IDX_HEADS * LANES, D_QLAT)
        y_attn = _attn_call(cqn, ckvn, misc, cast(wqi), cast(jnp.transpose(w_uq[l], (1, 2, 0))),
                            cast(jnp.transpose(w_uv[l], (0, 2, 1))), topk)

        bd, lr, li, cd = _ssm_weights(a_re[l], a_im[l], b_re[l], b_im[l], c_re[l], c_im[l], log_step[l])
        y_ssm = _ssm_call(u_ssm.reshape(S * B, W_SSM), bd, lr, li, cd, row(d_skip[l]), cast(w_glu[l]),
                          row(b_glu[l]), B, tc).reshape(S, B * W_SSM)

        w_pool_bd = cast(jnp.einsum('gcd,gh->gchd', w_pool[l], eye_p).reshape(W_POOL, W_POOL))
        x = _merge_call(x, row(mix_pre_g[l]), sc_m, sh_m, gt_m, row(mix_post_g[l]), w_gate, y_attn, y_ssm,
                        u_pool, w_pool_bd, row(pool_scale[l]), cast(p_a[l]), cast(p_b[l]), cast(p_c[l]),
                        cast(w_out[l]), _pick(S, MERGE_TILE))
        x = _ffn_call(x, row(ffn_pre_g[l]), sc_f, sh_f, gt_f, row(ffn_post_g[l]), cast(w_up[l]), conv_w[l],
                      row(conv_b[l]), cast(w_down[l]), _pick(S, FFN_TILE))
    return x
```
